```python
import math
import jax, jax.numpy as jnp
from jax import lax
import numpy as np

D_MODEL = 4096
BATCH = 1
SEQ = 8192
DEPTH = 1

MIX_WIDTH = D_MODEL
HEAD_DIM = 128
ATTN_WIDTH = MIX_WIDTH // 2
N_HEADS = ATTN_WIDTH // HEAD_DIM
CONV_WIDTH = MIX_WIDTH - ATTN_WIDTH
CONV_KERNEL = 31
DILATED_BRANCHES = ((128, 1), (512, 4), (2048, 16))
ATTN_BLOCK = 128
N_GROUPS = 8
EXPERTS_PER_GROUP = 8
N_EXPERTS = N_GROUPS * EXPERTS_PER_GROUP
TOP_K = 2
D_FF_EXPERT = 1024
MOE_BLOCK = 128
EPS = 1e-6
NEG_INF = -1e30
IN_COLS = 3 * ATTN_WIDTH + 2 * CONV_WIDTH

kernel_name = "hybrid_dilated_attn_conformer_conv_hmoe"


def rms_norm(x, w):
    xf = x.astype(jnp.float32)
    y = xf * lax.rsqrt(jnp.mean(xf * xf, axis=-1, keepdims=True) + EPS)
    return (y * w.astype(jnp.float32)).astype(x.dtype)


def layer_norm(x, w, b):
    xf = x.astype(jnp.float32)
    mu = jnp.mean(xf, axis=-1, keepdims=True)
    var = jnp.mean(jnp.square(xf - mu), axis=-1, keepdims=True)
    y = (xf - mu) * lax.rsqrt(var + EPS)
    return (y * w.astype(jnp.float32) + b.astype(jnp.float32)).astype(x.dtype)


def alibi_slopes(n_heads):
    return jnp.exp2(-8.0 * jnp.arange(1, n_heads + 1, dtype=jnp.float32) / n_heads)


def dilated_branch(q, k, v, slopes, window, dilation):
    B, H, S, E = q.shape
    n_keys = window // dilation
    span = dilation * ATTN_BLOCK
    s_pad = -(-S // span) * span
    n_sub = s_pad // dilation
    nb = n_sub // ATTN_BLOCK

    def to_sub(t):
        t = jnp.pad(t, ((0, 0), (0, 0), (0, s_pad - S), (0, 0)))
        t = t.reshape(B, H, n_sub, dilation, E).transpose(0, 1, 3, 2, 4)
        return t.reshape(B, H, dilation, nb, ATTN_BLOCK, E)

    qs, ks, vs = to_sub(q), to_sub(k), to_sub(v)
    blk_pad = ((0, 0), (0, 0), (0, 0), (1, 0), (0, 0), (0, 0))
    kk = jnp.concatenate([jnp.pad(ks, blk_pad)[:, :, :, :-1], ks], axis=4)
    vv = jnp.concatenate([jnp.pad(vs, blk_pad)[:, :, :, :-1], vs], axis=4)

    scores = jnp.einsum('bhrnqe,bhrnke->bhrnqk', qs, kk) * (1.0 / math.sqrt(E))
    qi = jnp.arange(ATTN_BLOCK)[:, None]
    kj = jnp.arange(2 * ATTN_BLOCK)[None, :]
    steps = qi + ATTN_BLOCK - kj
    valid = (steps >= 0) & (steps <= n_keys)
    before_start = (jnp.arange(nb)[:, None, None] == 0) & (kj[None] < ATTN_BLOCK)
    mask = valid[None] & ~before_start
    dist = (steps * dilation).astype(jnp.float32)
    bias = -slopes[:, None, None, None, None] * dist
    scores = jnp.where(mask[None, None, None], scores + bias[None], NEG_INF)
    m = jnp.max(scores, axis=-1, keepdims=True)
    p = jnp.exp(scores - m)
    den = jnp.sum(p, axis=-1)
    o = jnp.einsum('bhrnqk,bhrnke->bhrnqe', p, vv) / den[..., None]
    lse = m[..., 0] + jnp.log(den)
    o = o.reshape(B, H, dilation, n_sub, E).transpose(0, 1, 3, 2, 4).reshape(B, H, s_pad, E)[:, :, :S]
    lse = lse.reshape(B, H, dilation, n_sub).transpose(0, 1, 3, 2).reshape(B, H, s_pad)[:, :, :S]
    return o, lse


def dilated_attention(q, k, v):
    slopes = alibi_slopes(q.shape[1])
    outs, lses = [], []
    for window, dilation in DILATED_BRANCHES:
        o, l = dilated_branch(q, k, v, slopes, window, dilation)
        outs.append(o)
        lses.append(l)
    alpha = jax.nn.softmax(jnp.stack(lses, axis=0), axis=0)
    return jnp.sum(alpha[..., None] * jnp.stack(outs, axis=0), axis=0)


def conformer_conv(a, gate, conv_w, conv_b, ln_w, ln_b):
    u = a * jax.nn.sigmoid(gate)
    C = u.shape[-1]
    u = lax.conv_general_dilated(
        u, conv_w[:, None, :].astype(u.dtype), window_strides=(1,),
        padding=[(CONV_KERNEL - 1, 0)], dimension_numbers=('NWC', 'WIO', 'NWC'),
        feature_group_count=C) + conv_b
    return jax.nn.silu(layer_norm(u, ln_w, ln_b))


def token_mixer(hn, w_in, q_norm_w, k_norm_w, conv_w, conv_b, conv_ln_w, conv_ln_b, w_out):
    B, S, _ = hn.shape
    z = hn @ w_in
    cuts = np.cumsum([ATTN_WIDTH, ATTN_WIDTH, ATTN_WIDTH, CONV_WIDTH])
    q, k, v, a, g = jnp.split(z, [int(c) for c in cuts], axis=-1)

    def heads(t):
        return t.reshape(B, S, N_HEADS, HEAD_DIM).transpose(0, 2, 1, 3).astype(jnp.float32)

    q = heads(rms_norm(q.reshape(B, S, N_HEADS, HEAD_DIM), q_norm_w).reshape(B, S, ATTN_WIDTH))
    k = heads(rms_norm(k.reshape(B, S, N_HEADS, HEAD_DIM), k_norm_w).reshape(B, S, ATTN_WIDTH))
    v = heads(v)
    attn = dilated_attention(q, k, v)
    attn = attn.transpose(0, 2, 1, 3).reshape(B, S, ATTN_WIDTH).astype(hn.dtype)
    conv = conformer_conv(a, g, conv_w, conv_b, conv_ln_w, conv_ln_b).astype(hn.dtype)
    return jnp.concatenate([attn, conv], axis=-1) @ w_out


def hierarchical_moe(hn, w_group_router, b_group_router, w_expert_router, b_expert_router,
                     w_gate, w_up, w_down):
    B, S, D = hn.shape
    N = B * S
    xt = hn.reshape(N, D)
    g_logits = (xt @ w_group_router).astype(jnp.float32) + b_group_router.astype(jnp.float32)
    g_prob = jax.nn.softmax(g_logits, axis=-1)
    g_top = jnp.argmax(g_prob, axis=-1)
    g_w = jnp.take_along_axis(g_prob, g_top[:, None], axis=-1)
    e_logits = (xt @ w_expert_router).astype(jnp.float32) + b_expert_router.astype(jnp.float32)
    e_logits = e_logits.reshape(N, N_GROUPS, EXPERTS_PER_GROUP)
    e_logits = jnp.take_along_axis(e_logits, g_top[:, None, None], axis=1)[:, 0]
    e_prob = jax.nn.softmax(e_logits, axis=-1)
    top_v, top_i = lax.top_k(e_prob, TOP_K)
    gates = g_w * top_v / jnp.sum(top_v, axis=-1, keepdims=True)
    expert_id = (g_top[:, None] * EXPERTS_PER_GROUP + top_i).reshape(-1).astype(jnp.int32)
    token_id = jnp.repeat(jnp.arange(N, dtype=jnp.int32), TOP_K)
    gate_w = gates.reshape(-1)

    nk = N * TOP_K
    order = jnp.argsort(expert_id)
    e_s, t_s, w_s = expert_id[order], token_id[order], gate_w[order]
    counts = jnp.bincount(expert_id, length=N_EXPERTS)
    starts = jnp.cumsum(counts) - counts
    padded = (counts + MOE_BLOCK - 1) // MOE_BLOCK * MOE_BLOCK
    p_ends = jnp.cumsum(padded)
    p_starts = p_ends - padded
    dest = p_starts[e_s] + (jnp.arange(nk) - starts[e_s])
    n_rows = nk + N_EXPERTS * MOE_BLOCK
    n_blocks = n_rows // MOE_BLOCK
    row_tok = jnp.full((n_rows,), N, jnp.int32).at[dest].set(t_s)
    row_w = jnp.zeros((n_rows,), jnp.float32).at[dest].set(w_s)
    block_e = jnp.clip(jnp.searchsorted(p_ends, jnp.arange(n_blocks) * MOE_BLOCK, side='right'),
                       0, N_EXPERTS - 1)
    x_pad = jnp.concatenate([xt, jnp.zeros((1, D), xt.dtype)], axis=0)

    def expert_block(args):
        toks, e = args
        xb = x_pad[toks]
        hid = jax.nn.silu(xb @ w_gate[e]) * (xb @ w_up[e])
        return hid @ w_down[e]

    y = lax.map(expert_block, (row_tok.reshape(n_blocks, MOE_BLOCK), block_e))
    y = y.reshape(n_rows, D).astype(jnp.float32) * row_w[:, None]
    out = jax.ops.segment_sum(y, row_tok, num_segments=N + 1)[:N]
    return out.astype(hn.dtype).reshape(B, S, D)


def setup_inputs(seed: int = 0) -> dict:
    key = jax.random.key(seed)
    ks = jax.random.split(key, 18)
    f32 = jnp.float32
    L = DEPTH

    def nrm(k, shape, scale):
        return jax.random.normal(k, shape, f32) * scale

    return {
        'x': nrm(ks[0], (BATCH, SEQ, D_MODEL), 1.0),
        'norm_mix_w': 1.0 + nrm(ks[1], (L, D_MODEL), 0.02),
        'w_in': nrm(ks[2], (L, D_MODEL, IN_COLS), D_MODEL ** -0.5),
        'q_norm_w': 1.0 + nrm(ks[3], (L, HEAD_DIM), 0.02),
        'k_norm_w': 1.0 + nrm(ks[4], (L, HEAD_DIM), 0.02),
        'conv_w': nrm(ks[5], (L, CONV_KERNEL, CONV_WIDTH), CONV_KERNEL ** -0.5),
        'conv_b': nrm(ks[6], (L, CONV_WIDTH), 0.02),
        'conv_ln_w': 1.0 + nrm(ks[7], (L, CONV_WIDTH), 0.02),
        'conv_ln_b': nrm(ks[8], (L, CONV_WIDTH), 0.02),
        'w_out': nrm(ks[9], (L, MIX_WIDTH, D_MODEL), MIX_WIDTH ** -0.5),
        'norm_ffn_w': 1.0 + nrm(ks[10], (L, D_MODEL), 0.02),
        'w_group_router': nrm(ks[11], (L, D_MODEL, N_GROUPS), D_MODEL ** -0.5),
        'b_group_router': nrm(ks[12], (L, N_GROUPS), 0.01),
        'w_expert_router': nrm(ks[13], (L, D_MODEL, N_EXPERTS), D_MODEL ** -0.5),
        'b_expert_router': nrm(ks[14], (L, N_EXPERTS), 0.01),
        'w_gate': nrm(ks[15], (L, N_EXPERTS, D_MODEL, D_FF_EXPERT), D_MODEL ** -0.5),
        'w_up': nrm(ks[16], (L, N_EXPERTS, D_MODEL, D_FF_EXPERT), D_MODEL ** -0.5),
        'w_down': nrm(ks[17], (L, N_EXPERTS, D_FF_EXPERT, D_MODEL), D_FF_EXPERT ** -0.5),
    }


def reference(x, norm_mix_w, w_in, q_norm_w, k_norm_w, conv_w, conv_b, conv_ln_w, conv_ln_b,
              w_out, norm_ffn_w, w_group_router, b_group_router, w_expert_router,
              b_expert_router, w_gate, w_up, w_down):
    h = x
    for l in range(DEPTH):
        hn = rms_norm(h, norm_mix_w[l])
        h = h + token_mixer(hn, w_in[l], q_norm_w[l], k_norm_w[l], conv_w[l], conv_b[l],
                            conv_ln_w[l], conv_ln_b[l], w_out[l])
        hn = rms_norm(h, norm_ffn_w[l])
        h = h + hierarchical_moe(hn, w_group_router[l], b_group_router[l], w_expert_router[l],
                                 b_expert_router[l], w_gate[l], w_up[l], w_down[l])
    return h
```

```python
import functools
import math

import jax
import jax.numpy as jnp
from jax import lax
from jax.experimental import pallas as pl
from jax.experimental.pallas import tpu as pltpu

F32 = jnp.float32
BF16 = jnp.bfloat16

D_MODEL = 4096
SEQ = 8192
HEAD_DIM = 128
N_HEADS = 16
ATTN_WIDTH = N_HEADS * HEAD_DIM
CONV_WIDTH = D_MODEL - ATTN_WIDTH
CONV_KERNEL = 31
IN_COLS = 3 * ATTN_WIDTH + 2 * CONV_WIDTH
DILATIONS = (1, 4, 16)
ATTN_BLOCK = 128
N_KEYS = 128
N_GROUPS = 8
EXPERTS_PER_GROUP = 8
N_EXPERTS = N_GROUPS * EXPERTS_PER_GROUP
TOP_K = 2
D_FF = 1024
EPS = 1e-6
NEG_INF = -1e30

LANES = 128
VMEM_LIMIT = 56 * 1024 * 1024

SPAN = DILATIONS[-1] * ATTN_BLOCK

MOE_ROWS = 256
N_ASSIGN = SEQ * TOP_K
N_MOE_ROWS = N_ASSIGN + N_EXPERTS * MOE_ROWS
N_MOE_BLOCKS = N_MOE_ROWS // MOE_ROWS
UP_TILE = 512
DOWN_TILE = 2048
ROUTER_ROWS = 256
COMBINE_ROWS = 128


def _params(n_axes):
    return pltpu.CompilerParams(dimension_semantics=("arbitrary",) * n_axes,
                                vmem_limit_bytes=VMEM_LIMIT)


def _rmsnorm_kernel(x_ref, w_ref, o_ref):
    x = x_ref[...]
    ms = jnp.mean(x * x, axis=-1, keepdims=True)
    o_ref[...] = (x * lax.rsqrt(ms + EPS) * w_ref[...]).astype(o_ref.dtype)


def _rmsnorm(x, w, rows=256):
    n, d = x.shape
    return pl.pallas_call(
        _rmsnorm_kernel,
        grid=(n // rows,),
        in_specs=[pl.BlockSpec((rows, d), lambda i: (i, 0)),
                  pl.BlockSpec((1, d), lambda i: (0, 0))],
        out_specs=pl.BlockSpec((rows, d), lambda i: (i, 0)),
        out_shape=jax.ShapeDtypeStruct((n, d), BF16),
        compiler_params=_params(1),
        name="rmsnorm",
    )(x, w)


IN_TM = 1024
IN_TN = 512
QK_TILES = 2 * ATTN_WIDTH // IN_TN


def _inproj_kernel(hn_ref, w_ref, nw_ref, o_ref):
    j = pl.program_id(1)
    acc = jnp.dot(hn_ref[...], w_ref[...], preferred_element_type=F32)

    @pl.when(j < QK_TILES)
    def _():
        for c in range(IN_TN // HEAD_DIM):
            sl = slice(c * HEAD_DIM, (c + 1) * HEAD_DIM)
            t = acc[:, sl]
            ms = jnp.mean(t * t, axis=-1, keepdims=True)
            o_ref[:, sl] = (t * lax.rsqrt(ms + EPS) * nw_ref[:, sl]).astype(o_ref.dtype)

    @pl.when(j >= QK_TILES)
    def _():
        o_ref[...] = acc.astype(o_ref.dtype)


def _inproj(hn, w_bf, qk_norm_w):
    n = hn.shape[0]
    return pl.pallas_call(
        _inproj_kernel,
        grid=(n // IN_TM, IN_COLS // IN_TN),
        in_specs=[pl.BlockSpec((IN_TM, D_MODEL), lambda i, j: (i, 0)),
                  pl.BlockSpec((D_MODEL, IN_TN), lambda i, j: (0, j)),
                  pl.BlockSpec((1, IN_TN), lambda i, j: (0, jnp.minimum(j, QK_TILES - 1)))],
        out_specs=pl.BlockSpec((IN_TM, IN_TN), lambda i, j: (i, j)),
        out_shape=jax.ShapeDtypeStruct((n, IN_COLS), BF16),
        compiler_params=_params(2),
        name="inproj",
    )(hn, w_bf, qk_norm_w)


def _attn_kernel(slopes_ref, q_ref, kc_ref, kp_ref, vc_ref, vp_ref, o_ref, qf, kf, vf, ob, lb):
    h = pl.program_id(0)
    first_span = pl.program_id(1) == 0
    slope = slopes_ref[h]

    qf[...] = q_ref[...].astype(F32)
    kf[0:SPAN, :] = kp_ref[...].astype(F32)
    kf[SPAN:2 * SPAN, :] = kc_ref[...].astype(F32)
    vf[0:SPAN, :] = vp_ref[...].astype(F32)
    vf[SPAN:2 * SPAN, :] = vc_ref[...].astype(F32)

    qi = lax.broadcasted_iota(jnp.int32, (ATTN_BLOCK, 2 * ATTN_BLOCK), 0)
    kj = lax.broadcasted_iota(jnp.int32, (ATTN_BLOCK, 2 * ATTN_BLOCK), 1)
    steps = qi + ATTN_BLOCK - kj
    valid = (steps >= 0) & (steps <= N_KEYS)
    valid_first = valid & (kj >= jnp.where(first_span, ATTN_BLOCK, 0))
    stepsf = steps.astype(F32)
    scale = 1.0 / math.sqrt(HEAD_DIM)

    for b, d in enumerate(DILATIONS):
        bias = (-slope * float(d)) * stepsf
        for r in range(d):
            for n in range(SPAN // (ATTN_BLOCK * d)):
                q0 = r + d * ATTN_BLOCK * n
                k0 = SPAN + r + d * ATTN_BLOCK * (n - 1)
                q = qf[pl.ds(q0, ATTN_BLOCK, stride=d), :].astype(BF16)
                k = kf[pl.ds(k0, 2 * ATTN_BLOCK, stride=d), :].astype(BF16)
                v = vf[pl.ds(k0, 2 * ATTN_BLOCK, stride=d), :].astype(BF16)
                s = lax.dot_general(q, k, (((1,), (1,)), ((), ())),
                                    preferred_element_type=F32) * scale
                s = jnp.where(valid_first if n == 0 else valid, s + bias, NEG_INF)
                m = jnp.max(s, axis=-1, keepdims=True)
                p = jnp.exp(s - m)
                den = jnp.sum(p, axis=-1, keepdims=True)
                o = jnp.dot(p.astype(BF16), v, preferred_element_type=F32) / den
                lse = m + jnp.log(den)
                ob[b, pl.ds(q0, ATTN_BLOCK, stride=d), :] = o
                lb[b, pl.ds(q0, ATTN_BLOCK, stride=d), :] = jnp.broadcast_to(
                    lse, (ATTN_BLOCK, HEAD_DIM))

    chunk = 256
    for c in range(SPAN // chunk):
        rows = slice(c * chunk, (c + 1) * chunk)
        l0, l1, l2 = lb[0, rows, :], lb[1, rows, :], lb[2, rows, :]
        mx = jnp.maximum(jnp.maximum(l0, l1), l2)
        w0, w1, w2 = jnp.exp(l0 - mx), jnp.exp(l1 - mx), jnp.exp(l2 - mx)
        num = w0 * ob[0, rows, :] + w1 * ob[1, rows, :] + w2 * ob[2, rows, :]
        o_ref[rows, :] = (num / (w0 + w1 + w2)).astype(o_ref.dtype)


def _attention(z, slopes):
    n_spans = SEQ // SPAN
    k_col, v_col = N_HEADS, 2 * N_HEADS
    blk = (SPAN, HEAD_DIM)
    prev = lambda s: jnp.maximum(s - 1, 0)
    return pl.pallas_call(
        _attn_kernel,
        grid_spec=pltpu.PrefetchScalarGridSpec(
            num_scalar_prefetch=1,
            grid=(N_HEADS, n_spans),
            in_specs=[pl.BlockSpec(blk, lambda h, s, sl: (s, h)),
                      pl.BlockSpec(blk, lambda h, s, sl: (s, k_col + h)),
                      pl.BlockSpec(blk, lambda h, s, sl: (prev(s), k_col + h)),
                      pl.BlockSpec(blk, lambda h, s, sl: (s, v_col + h)),
                      pl.BlockSpec(blk, lambda h, s, sl: (prev(s), v_col + h))],
            out_specs=pl.BlockSpec(blk, lambda h, s, sl: (s, h)),
            scratch_shapes=[pltpu.VMEM((SPAN, HEAD_DIM), F32),
                            pltpu.VMEM((2 * SPAN, HEAD_DIM), F32),
                            pltpu.VMEM((2 * SPAN, HEAD_DIM), F32),
                            pltpu.VMEM((3, SPAN, HEAD_DIM), F32),
                            pltpu.VMEM((3, SPAN, HEAD_DIM), F32)]),
        out_shape=jax.ShapeDtypeStruct((SEQ, ATTN_WIDTH), BF16),
        compiler_params=_params(2),
        name="dilated_attention",
    )(slopes, z, z, z, z, z)


CONV_ROWS = 256
CONV_HALO = 32
CONV_CHUNK = 64


def _conv_kernel(a_ref, g_ref, ah_ref, gh_ref, cw_ref, cb_ref, lw_ref, lb_ref, o_ref, u_ext, c_scr):
    i = pl.program_id(0)
    u_ext[CONV_HALO:CONV_HALO + CONV_ROWS, :] = (
        a_ref[...].astype(F32) * jax.nn.sigmoid(g_ref[...].astype(F32)))
    uh = ah_ref[...].astype(F32) * jax.nn.sigmoid(gh_ref[...].astype(F32))
    u_ext[0:CONV_HALO, :] = jnp.where(i == 0, 0.0, uh)

    shift = CONV_HALO - (CONV_KERNEL - 1)
    for ct in range(CONV_WIDTH // LANES):
        cols = slice(ct * LANES, (ct + 1) * LANES)
        for rc in range(CONV_ROWS // CONV_CHUNK):
            base = rc * CONV_CHUNK
            acc = jnp.broadcast_to(cb_ref[:, cols], (CONV_CHUNK, LANES))
            for j in range(CONV_KERNEL):
                acc = acc + cw_ref[j:j + 1, cols] * u_ext[base + shift + j:
                                                           base + shift + j + CONV_CHUNK, cols]
            c_scr[base:base + CONV_CHUNK, cols] = acc

    c = c_scr[...]
    mu = jnp.mean(c, axis=-1, keepdims=True)
    var = jnp.mean(jnp.square(c - mu), axis=-1, keepdims=True)
    y = (c - mu) * lax.rsqrt(var + EPS) * lw_ref[...] + lb_ref[...]
    o_ref[...] = (y * jax.nn.sigmoid(y)).astype(o_ref.dtype)


def _conformer_conv(z, conv_w, conv_b, ln_w, ln_b):
    a_col = 3 * ATTN_WIDTH // CONV_WIDTH
    g_col = a_col + 1
    halo_blocks = CONV_ROWS // CONV_HALO
    halo = lambda i: jnp.maximum(i * halo_blocks - 1, 0)
    vec = pl.BlockSpec((1, CONV_WIDTH), lambda i: (0, 0))
    return pl.pallas_call(
        _conv_kernel,
        grid=(SEQ // CONV_ROWS,),
        in_specs=[pl.BlockSpec((CONV_ROWS, CONV_WIDTH), lambda i: (i, a_col)),
                  pl.BlockSpec((CONV_ROWS, CONV_WIDTH), lambda i: (i, g_col)),
                  pl.BlockSpec((CONV_HALO, CONV_WIDTH), lambda i: (halo(i), a_col)),
                  pl.BlockSpec((CONV_HALO, CONV_WIDTH), lambda i: (halo(i), g_col)),
                  pl.BlockSpec((CONV_KERNEL, CONV_WIDTH), lambda i: (0, 0)),
                  vec, vec, vec],
        out_specs=pl.BlockSpec((CONV_ROWS, CONV_WIDTH), lambda i: (i, 0)),
        out_shape=jax.ShapeDtypeStruct((SEQ, CONV_WIDTH), BF16),
        scratch_shapes=[pltpu.VMEM((CONV_HALO + CONV_ROWS, CONV_WIDTH), F32),
                        pltpu.VMEM((CONV_ROWS, CONV_WIDTH), F32)],
        compiler_params=_params(1),
        name="conformer_conv",
    )(z, z, z, z, conv_w, conv_b, ln_w, ln_b)


OUT_TM = 1024
OUT_TN = 512


def _outproj_kernel(attn_ref, conv_ref, wa_ref, wc_ref, x_ref, o_ref):
    acc = jnp.dot(attn_ref[...], wa_ref[...], preferred_element_type=F32)
    acc = acc + jnp.dot(conv_ref[...], wc_ref[...], preferred_element_type=F32)
    o_ref[...] = x_ref[...] + acc


def _outproj(attn, conv, w_bf, x):
    n = x.shape[0]
    conv_blk = ATTN_WIDTH // CONV_WIDTH
    return pl.pallas_call(
        _outproj_kernel,
        grid=(n // OUT_TM, D_MODEL // OUT_TN),
        in_specs=[pl.BlockSpec((OUT_TM, ATTN_WIDTH), lambda i, j: (i, 0)),
                  pl.BlockSpec((OUT_TM, CONV_WIDTH), lambda i, j: (i, 0)),
                  pl.BlockSpec((ATTN_WIDTH, OUT_TN), lambda i, j: (0, j)),
                  pl.BlockSpec((CONV_WIDTH, OUT_TN), lambda i, j: (conv_blk, j)),
                  pl.BlockSpec((OUT_TM, OUT_TN), lambda i, j: (i, j))],
        out_specs=pl.BlockSpec((OUT_TM, OUT_TN), lambda i, j: (i, j)),
        out_shape=jax.ShapeDtypeStruct((n, D_MODEL), F32),
        compiler_params=_params(2),
        name="outproj",
    )(attn, conv, w_bf, w_bf, x)


def _router_kernel(h_ref, nw_ref, wr_ref, br_ref, hn_ref, ids_ref, gates_ref):
    i = pl.program_id(0)
    half = D_MODEL // 2

    @pl.when(i < SEQ // ROUTER_ROWS)
    def _():
        x = h_ref[...]
        ms = jnp.mean(x * x, axis=-1, keepdims=True)
        hn = x * lax.rsqrt(ms + EPS) * nw_ref[...]
        lo = pltpu.bitcast(hn[:, :half].astype(BF16).astype(F32), jnp.uint32)
        hi = pltpu.bitcast(hn[:, half:].astype(BF16).astype(F32), jnp.uint32)
        hn_ref[...] = (hi & jnp.uint32(0xFFFF0000)) | (lo >> 16)

        logits = jnp.dot(hn, wr_ref[...], preferred_element_type=F32,
                         precision=lax.Precision.HIGHEST) + br_ref[...]
        lane = lax.broadcasted_iota(jnp.int32, logits.shape, 1)
        big = jnp.int32(1 << 20)

        is_g = lane < N_GROUPS
        gl = jnp.where(is_g, logits, -jnp.inf)
        ge = jnp.exp(gl - jnp.max(gl, axis=-1, keepdims=True))
        g_prob = ge / jnp.sum(ge, axis=-1, keepdims=True)
        g_w = jnp.max(g_prob, axis=-1, keepdims=True)
        g_top = jnp.min(jnp.where(is_g & (g_prob == g_w), lane, big), axis=-1, keepdims=True)

        eidx = lane - N_GROUPS
        grp_shift = EXPERTS_PER_GROUP.bit_length() - 1
        in_grp = (eidx >= 0) & (eidx < N_EXPERTS) & ((eidx >> grp_shift) == g_top)
        el = jnp.where(in_grp, logits, -jnp.inf)
        ee = jnp.exp(el - jnp.max(el, axis=-1, keepdims=True))
        e_prob = ee / jnp.sum(ee, axis=-1, keepdims=True)
        v1 = jnp.max(jnp.where(in_grp, e_prob, -1.0), axis=-1, keepdims=True)
        i1 = jnp.min(jnp.where(in_grp & (e_prob == v1), lane, big), axis=-1, keepdims=True)
        rest = in_grp & (lane != i1)
        v2 = jnp.max(jnp.where(rest, e_prob, -1.0), axis=-1, keepdims=True)
        i2 = jnp.min(jnp.where(rest & (e_prob == v2), lane, big), axis=-1, keepdims=True)
        tot = v1 + v2
        ids_ref[...] = jnp.where(lane == 0, i1 - N_GROUPS, jnp.where(lane == 1, i2 - N_GROUPS, 0))
        gates_ref[...] = jnp.where(lane == 0, g_w * v1 / tot,
                                   jnp.where(lane == 1, g_w * v2 / tot, 0.0))

    @pl.when(i >= SEQ // ROUTER_ROWS)
    def _():
        hn_ref[...] = jnp.zeros_like(hn_ref)


def _router(h, norm_w, w_router, b_router):
    n_tiles = SEQ // ROUTER_ROWS
    last = lambda i: jnp.minimum(i, n_tiles - 1)
    return pl.pallas_call(
        _router_kernel,
        grid=(n_tiles + 1,),
        in_specs=[pl.BlockSpec((ROUTER_ROWS, D_MODEL), lambda i: (last(i), 0)),
                  pl.BlockSpec((1, D_MODEL), lambda i: (0, 0)),
                  pl.BlockSpec((D_MODEL, LANES), lambda i: (0, 0)),
                  pl.BlockSpec((1, LANES), lambda i: (0, 0))],
        out_specs=[pl.BlockSpec((ROUTER_ROWS, D_MODEL // 2), lambda i: (i, 0)),
                   pl.BlockSpec((ROUTER_ROWS, LANES), lambda i: (last(i), 0)),
                   pl.BlockSpec((ROUTER_ROWS, LANES), lambda i: (last(i), 0))],
        out_shape=[jax.ShapeDtypeStruct((SEQ + ROUTER_ROWS, D_MODEL // 2), jnp.uint32),
                   jax.ShapeDtypeStruct((SEQ, LANES), jnp.int32),
                   jax.ShapeDtypeStruct((SEQ, LANES), F32)],
        compiler_params=_params(1),
        name="ffn_norm_router",
    )(h, norm_w, w_router, b_router)


def _row_copy(src_hbm, src_row, dst_ref, dst_row, sem):
    return pltpu.make_async_copy(src_hbm.at[pl.ds(src_row, 1), :],
                                 dst_ref.at[pl.ds(dst_row, 1), :], sem)


def _gather_kernel(tok_ref, hn_hbm, o_ref, sem):
    def start(r, carry):
        _row_copy(hn_hbm, tok_ref[0, 0, r], o_ref, r, sem).start()
        return carry

    def wait(r, carry):
        _row_copy(hn_hbm, 0, o_ref, r, sem).wait()
        return carry

    lax.fori_loop(0, MOE_ROWS, start, 0)
    lax.fori_loop(0, MOE_ROWS, wait, 0)


def _gather_rows(row_tok, hn_words):
    width = hn_words.shape[1]
    return pl.pallas_call(
        _gather_kernel,
        grid=(N_MOE_BLOCKS,),
        in_specs=[pl.BlockSpec((1, 1, MOE_ROWS), lambda i: (i, 0, 0), memory_space=pltpu.SMEM),
                  pl.BlockSpec(memory_space=pl.ANY)],
        out_specs=pl.BlockSpec((MOE_ROWS, width), lambda i: (i, 0)),
        out_shape=jax.ShapeDtypeStruct((N_MOE_ROWS, width), jnp.uint32),
        scratch_shapes=[pltpu.SemaphoreType.DMA(())],
        compiler_params=_params(1),
        name="moe_gather",
    )(row_tok.reshape(N_MOE_BLOCKS, 1, MOE_ROWS), hn_words)


def _unpack_rows(words):
    lo = pltpu.bitcast(words << 16, F32).astype(BF16)
    hi = pltpu.bitcast(words & jnp.uint32(0xFFFF0000), F32).astype(BF16)
    return jnp.concatenate([lo, hi], axis=-1)


def _moe_up_kernel(e_ref, wt_ref, b_ref, ot_ref, valid_ref, first_ref,
                   xs_ref, wg_ref, wu_ref, hid_ref, wg_bf, wu_bf):
    w = pl.program_id(0)

    @pl.when(valid_ref[w] == 1)
    def _():
        @pl.when(first_ref[w] == 1)
        def _():
            wg_bf[...] = wg_ref[0].astype(BF16)
            wu_bf[...] = wu_ref[0].astype(BF16)

        x = _unpack_rows(xs_ref[...])
        g = jnp.dot(x, wg_bf[...], preferred_element_type=F32)
        u = jnp.dot(x, wu_bf[...], preferred_element_type=F32)
        hid_ref[...] = (g * jax.nn.sigmoid(g) * u).astype(hid_ref.dtype)

    @pl.when(valid_ref[w] == 0)
    def _():
        hid_ref[...] = jnp.zeros_like(hid_ref)


def _moe_up(items, xs, w_gate, w_up):
    n_items = N_MOE_BLOCKS * (D_FF // UP_TILE)
    wspec = pl.BlockSpec((1, D_MODEL, UP_TILE), lambda w, e, wt, b, ot, v, f: (e[w], 0, wt[w]))
    return pl.pallas_call(
        _moe_up_kernel,
        grid_spec=pltpu.PrefetchScalarGridSpec(
            num_scalar_prefetch=6,
            grid=(n_items,),
            in_specs=[pl.BlockSpec((MOE_ROWS, D_MODEL // 2),
                                   lambda w, e, wt, b, ot, v, f: (b[w], 0)),
                      wspec, wspec],
            out_specs=pl.BlockSpec((MOE_ROWS, UP_TILE),
                                   lambda w, e, wt, b, ot, v, f: (b[w], ot[w])),
            scratch_shapes=[pltpu.VMEM((D_MODEL, UP_TILE), BF16),
                            pltpu.VMEM((D_MODEL, UP_TILE), BF16)]),
        out_shape=jax.ShapeDtypeStruct((N_MOE_ROWS, D_FF), BF16),
        compiler_params=_params(1),
        name="moe_gate_up",
    )(*items, xs, w_gate, w_up)


def _moe_down_kernel(e_ref, wt_ref, b_ref, ot_ref, valid_ref, first_ref,
                     hid_ref, wd_ref, rw_ref, y_ref, wd_bf):
    w = pl.program_id(0)

    @pl.when(valid_ref[w] == 1)
    def _():
        @pl.when(first_ref[w] == 1)
        def _():
            wd_bf[...] = wd_ref[0].astype(BF16)

        y = jnp.dot(hid_ref[...], wd_bf[...], preferred_element_type=F32)
        y_ref[...] = y * rw_ref[...]

    @pl.when(valid_ref[w] == 0)
    def _():
        y_ref[...] = jnp.zeros_like(y_ref)


def _moe_down(items, hid, w_down, row_w):
    n_items = N_MOE_BLOCKS * (D_MODEL // DOWN_TILE)
    return pl.pallas_call(
        _moe_down_kernel,
        grid_spec=pltpu.PrefetchScalarGridSpec(
            num_scalar_prefetch=6,
            grid=(n_items,),
            in_specs=[pl.BlockSpec((MOE_ROWS, D_FF), lambda w, e, wt, b, ot, v, f: (b[w], 0)),
                      pl.BlockSpec((1, D_FF, DOWN_TILE),
                                   lambda w, e, wt, b, ot, v, f: (e[w], 0, wt[w])),
                      pl.BlockSpec((MOE_ROWS, 1), lambda w, e, wt, b, ot, v, f: (b[w], 0))],
            out_specs=pl.BlockSpec((MOE_ROWS, DOWN_TILE),
                                   lambda w, e, wt, b, ot, v, f: (b[w], ot[w])),
            scratch_shapes=[pltpu.VMEM((D_FF, DOWN_TILE), BF16)]),
        out_shape=jax.ShapeDtypeStruct((N_MOE_ROWS, D_MODEL), F32),
        compiler_params=_params(1),
        name="moe_down",
    )(*items, hid, w_down, row_w)


def _combine_kernel(dest_ref, h_ref, y_hbm, o_ref, buf, sem):
    def start(a, carry):
        _row_copy(y_hbm, dest_ref[0, 0, a], buf.at[a % TOP_K], a // TOP_K, sem).start()
        return carry

    def wait(a, carry):
        _row_copy(y_hbm, 0, buf.at[a % TOP_K], a // TOP_K, sem).wait()
        return carry

    lax.fori_loop(0, COMBINE_ROWS * TOP_K, start, 0)
    lax.fori_loop(0, COMBINE_ROWS * TOP_K, wait, 0)
    o_ref[...] = h_ref[...] + (buf[0] + buf[1])


def _combine(dest, h, y):
    n_tiles = SEQ // COMBINE_ROWS
    per_tile = COMBINE_ROWS * TOP_K
    return pl.pallas_call(
        _combine_kernel,
        grid=(n_tiles,),
        in_specs=[pl.BlockSpec((1, 1, per_tile), lambda i: (i, 0, 0), memory_space=pltpu.SMEM),
                  pl.BlockSpec((COMBINE_ROWS, D_MODEL), lambda i: (i, 0)),
                  pl.BlockSpec(memory_space=pl.ANY)],
        out_specs=pl.BlockSpec((COMBINE_ROWS, D_MODEL), lambda i: (i, 0)),
        out_shape=jax.ShapeDtypeStruct((SEQ, D_MODEL), F32),
        scratch_shapes=[pltpu.VMEM((TOP_K, COMBINE_ROWS, D_MODEL), F32),
                        pltpu.SemaphoreType.DMA(())],
        compiler_params=_params(1),
        name="moe_combine",
    )(dest.reshape(n_tiles, 1, per_tile), h, y)


def _work_items(n_blocks_e, block_start_e, tiles):
    n_items = N_MOE_BLOCKS * tiles
    per_e = n_blocks_e * tiles
    ends = jnp.cumsum(per_e)
    total = ends[-1]
    idx = jnp.arange(n_items, dtype=jnp.int32)
    valid = idx < total
    w = jnp.minimum(idx, total - 1)
    e = jnp.searchsorted(ends, w, side='right').astype(jnp.int32)
    local = w - (ends - per_e)[e]
    nb = n_blocks_e[e]
    spare = idx - total
    w_tile = local // nb
    o_tile = jnp.where(valid, w_tile, spare % tiles)
    blk = jnp.where(valid, block_start_e[e] + local % nb, total // tiles + spare // tiles)
    first = valid & (local % nb == 0)
    i32 = lambda a: a.astype(jnp.int32)
    return i32(e), i32(w_tile), i32(blk), i32(o_tile), i32(valid), i32(first)


def _dispatch_plan(ids, gates):
    expert = ids[:, :TOP_K].reshape(-1)
    gate_w = gates[:, :TOP_K].reshape(-1)
    token = jnp.repeat(jnp.arange(SEQ, dtype=jnp.int32), TOP_K)
    onehot = (expert[:, None] == jnp.arange(N_EXPERTS, dtype=jnp.int32)[None, :]).astype(jnp.int32)
    running = jnp.cumsum(onehot, axis=0)
    rank = jnp.take_along_axis(running, expert[:, None], axis=1)[:, 0] - 1
    counts = running[-1]
    n_blocks_e = (counts + MOE_ROWS - 1) // MOE_ROWS
    block_start_e = jnp.cumsum(n_blocks_e) - n_blocks_e
    dest = (block_start_e[expert] * MOE_ROWS + rank).astype(jnp.int32)
    row_tok = jnp.full((N_MOE_ROWS,), SEQ, jnp.int32).at[dest].set(token)
    row_w = jnp.zeros((N_MOE_ROWS,), F32).at[dest].set(gate_w)
    return dest, row_tok, row_w, n_blocks_e.astype(jnp.int32), block_start_e.astype(jnp.int32)


def kernel(x, norm_mix_w, w_in, q_norm_w, k_norm_w, conv_w, conv_b, conv_ln_w, conv_ln_b, w_out,
           norm_ffn_w, w_group_router, b_group_router, w_expert_router, b_expert_router,
           w_gate, w_up, w_down):
    h = x.reshape(SEQ, D_MODEL)
    slopes = jnp.exp2(-8.0 * jnp.arange(1, N_HEADS + 1, dtype=F32) / N_HEADS)
    for l in range(norm_mix_w.shape[0]):
        hn = _rmsnorm(h, norm_mix_w[l][None, :])
        qk_norm = jnp.concatenate([jnp.tile(q_norm_w[l], N_HEADS), jnp.tile(k_norm_w[l], N_HEADS)])
        z = _inproj(hn, w_in[l].astype(BF16), qk_norm[None, :])
        attn = _attention(z, slopes)
        conv = _conformer_conv(z, conv_w[l], conv_b[l][None, :], conv_ln_w[l][None, :],
                               conv_ln_b[l][None, :])
        h = _outproj(attn, conv, w_out[l].astype(BF16), h)

        pad = LANES - N_GROUPS - N_EXPERTS
        w_router = jnp.concatenate([w_group_router[l], w_expert_router[l],
                                    jnp.zeros((D_MODEL, pad), F32)], axis=1)
        b_router = jnp.concatenate([b_group_router[l], b_expert_router[l], jnp.zeros((pad,), F32)])
        hn_words, ids, gates = _router(h, norm_ffn_w[l][None, :], w_router, b_router[None, :])
        dest, row_tok, row_w, n_blocks_e, block_start_e = _dispatch_plan(ids, gates)
        xs = _gather_rows(row_tok, hn_words)
        hid = _moe_up(_work_items(n_blocks_e, block_start_e, D_FF // UP_TILE), xs, w_gate[l], w_up[l])
        y = _moe_down(_work_items(n_blocks_e, block_start_e, D_MODEL // DOWN_TILE), hid, w_down[l],
                      row_w[:, None])
        h = _combine(dest, h, y)
    return h.reshape(x.shape)
```

```python
import math

import jax
import jax.numpy as jnp
from jax import lax
from jax.experimental import pallas as pl
from jax.experimental.pallas import tpu as pltpu

F32 = jnp.float32
BF16 = jnp.bfloat16

D_MODEL = 4096
SEQ = 8192
HEAD_DIM = 128
N_HEADS = 16
ATTN_WIDTH = N_HEADS * HEAD_DIM
CONV_WIDTH = D_MODEL - ATTN_WIDTH
CONV_KERNEL = 31
IN_COLS = 3 * ATTN_WIDTH + 2 * CONV_WIDTH
DILATIONS = (1, 4, 16)
ATTN_BLOCK = 128
N_KEYS = 128
N_GROUPS = 8
EXPERTS_PER_GROUP = 8
N_EXPERTS = N_GROUPS * EXPERTS_PER_GROUP
TOP_K = 2
D_FF = 1024
EPS = 1e-6
NEG_INF = -1e30

LANES = 128
VMEM_LIMIT = 56 * 1024 * 1024

SPAN = DILATIONS[-1] * ATTN_BLOCK

MOE_ROWS = 256
N_ASSIGN = SEQ * TOP_K
N_MOE_ROWS = N_ASSIGN + N_EXPERTS * MOE_ROWS
N_MOE_BLOCKS = N_MOE_ROWS // MOE_ROWS
UP_TILE = 512
DOWN_TILE = 2048
WEIGHT_RING = 2
ROUTER_ROWS = 256
COMBINE_ROWS = 128
X_ROW_TILES = D_MODEL // 2 // LANES
Y_ROW_TILES = D_MODEL // LANES
COMBINE_PITCH = Y_ROW_TILES + 4


def _params(n_axes):
    return pltpu.CompilerParams(dimension_semantics=("arbitrary",) * n_axes,
                                vmem_limit_bytes=VMEM_LIMIT)


def _rmsnorm_kernel(x_ref, w_ref, o_ref):
    x = x_ref[...]
    ms = jnp.mean(x * x, axis=-1, keepdims=True)
    o_ref[...] = (x * lax.rsqrt(ms + EPS) * w_ref[...]).astype(o_ref.dtype)


def _rmsnorm(x, w, rows=256):
    n, d = x.shape
    return pl.pallas_call(
        _rmsnorm_kernel,
        grid=(n // rows,),
        in_specs=[pl.BlockSpec((rows, d), lambda i: (i, 0)),
                  pl.BlockSpec((1, d), lambda i: (0, 0))],
        out_specs=pl.BlockSpec((rows, d), lambda i: (i, 0)),
        out_shape=jax.ShapeDtypeStruct((n, d), BF16),
        compiler_params=_params(1),
        name="rmsnorm",
    )(x, w)


IN_TM = 1024
IN_TN = 512
QK_TILES = 2 * ATTN_WIDTH // IN_TN


def _inproj_kernel(hn_ref, w_ref, nw_ref, o_ref):
    j = pl.program_id(1)
    acc = jnp.dot(hn_ref[...], w_ref[...], preferred_element_type=F32)

    @pl.when(j < QK_TILES)
    def _():
        for c in range(IN_TN // HEAD_DIM):
            sl = slice(c * HEAD_DIM, (c + 1) * HEAD_DIM)
            t = acc[:, sl]
            ms = jnp.mean(t * t, axis=-1, keepdims=True)
            o_ref[:, sl] = (t * lax.rsqrt(ms + EPS) * nw_ref[:, sl]).astype(o_ref.dtype)

    @pl.when(j >= QK_TILES)
    def _():
        o_ref[...] = acc.astype(o_ref.dtype)


def _inproj(hn, w_bf, qk_norm_w):
    n = hn.shape[0]
    return pl.pallas_call(
        _inproj_kernel,
        grid=(n // IN_TM, IN_COLS // IN_TN),
        in_specs=[pl.BlockSpec((IN_TM, D_MODEL), lambda i, j: (i, 0)),
                  pl.BlockSpec((D_MODEL, IN_TN), lambda i, j: (0, j)),
                  pl.BlockSpec((1, IN_TN), lambda i, j: (0, jnp.minimum(j, QK_TILES - 1)))],
        out_specs=pl.BlockSpec((IN_TM, IN_TN), lambda i, j: (i, j)),
        out_shape=jax.ShapeDtypeStruct((n, IN_COLS), BF16),
        compiler_params=_params(2),
        name="inproj",
    )(hn, w_bf, qk_norm_w)


def _attn_kernel(slopes_ref, q_ref, kc_ref, kp_ref, vc_ref, vp_ref, o_ref, qf, kf, vf, ob, lb):
    h = pl.program_id(0)
    first_span = pl.program_id(1) == 0
    slope = slopes_ref[h]

    qf[...] = q_ref[...].astype(F32)
    kf[0:SPAN, :] = kp_ref[...].astype(F32)
    kf[SPAN:2 * SPAN, :] = kc_ref[...].astype(F32)
    vf[0:SPAN, :] = vp_ref[...].astype(F32)
    vf[SPAN:2 * SPAN, :] = vc_ref[...].astype(F32)

    qi = lax.broadcasted_iota(jnp.int32, (ATTN_BLOCK, 2 * ATTN_BLOCK), 0)
    kj = lax.broadcasted_iota(jnp.int32, (ATTN_BLOCK, 2 * ATTN_BLOCK), 1)
    steps = qi + ATTN_BLOCK - kj
    valid = (steps >= 0) & (steps <= N_KEYS)
    valid_first = valid & (kj >= jnp.where(first_span, ATTN_BLOCK, 0))
    stepsf = steps.astype(F32)
    scale = 1.0 / math.sqrt(HEAD_DIM)

    for b, d in enumerate(DILATIONS):
        bias = (-slope * float(d)) * stepsf
        for r in range(d):
            for n in range(SPAN // (ATTN_BLOCK * d)):
                q0 = r + d * ATTN_BLOCK * n
                k0 = SPAN + r + d * ATTN_BLOCK * (n - 1)
                q = qf[pl.ds(q0, ATTN_BLOCK, stride=d), :].astype(BF16)
                k = kf[pl.ds(k0, 2 * ATTN_BLOCK, stride=d), :].astype(BF16)
                v = vf[pl.ds(k0, 2 * ATTN_BLOCK, stride=d), :].astype(BF16)
                s = lax.dot_general(q, k, (((1,), (1,)), ((), ())),
                                    preferred_element_type=F32) * scale
                s = jnp.where(valid_first if n == 0 else valid, s + bias, NEG_INF)
                m = jnp.max(s, axis=-1, keepdims=True)
                p = jnp.exp(s - m)
                den = jnp.sum(p, axis=-1, keepdims=True)
                o = jnp.dot(p.astype(BF16), v, preferred_element_type=F32) / den
                lse = m + jnp.log(den)
                ob[b, pl.ds(q0, ATTN_BLOCK, stride=d), :] = o
                lb[b, pl.ds(q0, ATTN_BLOCK, stride=d), :] = jnp.broadcast_to(
                    lse, (ATTN_BLOCK, HEAD_DIM))

    chunk = 256
    for c in range(SPAN // chunk):
        rows = slice(c * chunk, (c + 1) * chunk)
        l0, l1, l2 = lb[0, rows, :], lb[1, rows, :], lb[2, rows, :]
        mx = jnp.maximum(jnp.maximum(l0, l1), l2)
        w0, w1, w2 = jnp.exp(l0 - mx), jnp.exp(l1 - mx), jnp.exp(l2 - mx)
        num = w0 * ob[0, rows, :] + w1 * ob[1, rows, :] + w2 * ob[2, rows, :]
        o_ref[rows, :] = (num / (w0 + w1 + w2)).astype(o_ref.dtype)


def _attention(z, slopes):
    n_spans = SEQ // SPAN
    k_col, v_col = N_HEADS, 2 * N_HEADS
    blk = (SPAN, HEAD_DIM)
    prev = lambda s: jnp.maximum(s - 1, 0)
    return pl.pallas_call(
        _attn_kernel,
        grid_spec=pltpu.PrefetchScalarGridSpec(
            num_scalar_prefetch=1,
            grid=(N_HEADS, n_spans),
            in_specs=[pl.BlockSpec(blk, lambda h, s, sl: (s, h)),
                      pl.BlockSpec(blk, lambda h, s, sl: (s, k_col + h)),
                      pl.BlockSpec(blk, lambda h, s, sl: (prev(s), k_col + h)),
                      pl.BlockSpec(blk, lambda h, s, sl: (s, v_col + h)),
                      pl.BlockSpec(blk, lambda h, s, sl: (prev(s), v_col + h))],
            out_specs=pl.BlockSpec(blk, lambda h, s, sl: (s, h)),
            scratch_shapes=[pltpu.VMEM((SPAN, HEAD_DIM), F32),
                            pltpu.VMEM((2 * SPAN, HEAD_DIM), F32),
                            pltpu.VMEM((2 * SPAN, HEAD_DIM), F32),
                            pltpu.VMEM((3, SPAN, HEAD_DIM), F32),
                            pltpu.VMEM((3, SPAN, HEAD_DIM), F32)]),
        out_shape=jax.ShapeDtypeStruct((SEQ, ATTN_WIDTH), BF16),
        compiler_params=_params(2),
        name="dilated_attention",
    )(slopes, z, z, z, z, z)


CONV_ROWS = 256
CONV_HALO = 32
CONV_CHUNK = 64


def _conv_kernel(a_ref, g_ref, ah_ref, gh_ref, cw_ref, cb_ref, lw_ref, lb_ref, o_ref, u_ext, c_scr):
    i = pl.program_id(0)
    u_ext[CONV_HALO:CONV_HALO + CONV_ROWS, :] = (
        a_ref[...].astype(F32) * jax.nn.sigmoid(g_ref[...].astype(F32)))
    uh = ah_ref[...].astype(F32) * jax.nn.sigmoid(gh_ref[...].astype(F32))
    u_ext[0:CONV_HALO, :] = jnp.where(i == 0, 0.0, uh)

    shift = CONV_HALO - (CONV_KERNEL - 1)
    for ct in range(CONV_WIDTH // LANES):
        cols = slice(ct * LANES, (ct + 1) * LANES)
        for rc in range(CONV_ROWS // CONV_CHUNK):
            base = rc * CONV_CHUNK
            acc = jnp.broadcast_to(cb_ref[:, cols], (CONV_CHUNK, LANES))
            for j in range(CONV_KERNEL):
                acc = acc + cw_ref[j:j + 1, cols] * u_ext[base + shift + j:
                                                           base + shift + j + CONV_CHUNK, cols]
            c_scr[base:base + CONV_CHUNK, cols] = acc

    c = c_scr[...]
    mu = jnp.mean(c, axis=-1, keepdims=True)
    var = jnp.mean(jnp.square(c - mu), axis=-1, keepdims=True)
    y = (c - mu) * lax.rsqrt(var + EPS) * lw_ref[...] + lb_ref[...]
    o_ref[...] = (y * jax.nn.sigmoid(y)).astype(o_ref.dtype)


def _conformer_conv(z, conv_w, conv_b, ln_w, ln_b):
    a_col = 3 * ATTN_WIDTH // CONV_WIDTH
    g_col = a_col + 1
    halo_blocks = CONV_ROWS // CONV_HALO
    halo = lambda i: jnp.maximum(i * halo_blocks - 1, 0)
    vec = pl.BlockSpec((1, CONV_WIDTH), lambda i: (0, 0))
    return pl.pallas_call(
        _conv_kernel,
        grid=(SEQ // CONV_ROWS,),
        in_specs=[pl.BlockSpec((CONV_ROWS, CONV_WIDTH), lambda i: (i, a_col)),
                  pl.BlockSpec((CONV_ROWS, CONV_WIDTH), lambda i: (i, g_col)),
                  pl.BlockSpec((CONV_HALO, CONV_WIDTH), lambda i: (halo(i), a_col)),
                  pl.BlockSpec((CONV_HALO, CONV_WIDTH), lambda i: (halo(i), g_col)),
                  pl.BlockSpec((CONV_KERNEL, CONV_WIDTH), lambda i: (0, 0)),
                  vec, vec, vec],
        out_specs=pl.BlockSpec((CONV_ROWS, CONV_WIDTH), lambda i: (i, 0)),
        out_shape=jax.ShapeDtypeStruct((SEQ, CONV_WIDTH), BF16),
        scratch_shapes=[pltpu.VMEM((CONV_HALO + CONV_ROWS, CONV_WIDTH), F32),
                        pltpu.VMEM((CONV_ROWS, CONV_WIDTH), F32)],
        compiler_params=_params(1),
        name="conformer_conv",
    )(z, z, z, z, conv_w, conv_b, ln_w, ln_b)


OUT_TM = 1024
OUT_TN = 512


def _outproj_kernel(attn_ref, conv_ref, wa_ref, wc_ref, x_ref, o_ref):
    acc = jnp.dot(attn_ref[...], wa_ref[...], preferred_element_type=F32)
    acc = acc + jnp.dot(conv_ref[...], wc_ref[...], preferred_element_type=F32)
    o_ref[...] = x_ref[...] + acc


def _outproj(attn, conv, w_bf, x):
    n = x.shape[0]
    conv_blk = ATTN_WIDTH // CONV_WIDTH
    return pl.pallas_call(
        _outproj_kernel,
        grid=(n // OUT_TM, D_MODEL // OUT_TN),
        in_specs=[pl.BlockSpec((OUT_TM, ATTN_WIDTH), lambda i, j: (i, 0)),
                  pl.BlockSpec((OUT_TM, CONV_WIDTH), lambda i, j: (i, 0)),
                  pl.BlockSpec((ATTN_WIDTH, OUT_TN), lambda i, j: (0, j)),
                  pl.BlockSpec((CONV_WIDTH, OUT_TN), lambda i, j: (conv_blk, j)),
                  pl.BlockSpec((OUT_TM, OUT_TN), lambda i, j: (i, j))],
        out_specs=pl.BlockSpec((OUT_TM, OUT_TN), lambda i, j: (i, j)),
        out_shape=jax.ShapeDtypeStruct((n, D_MODEL), F32),
        compiler_params=_params(2),
        name="outproj",
    )(attn, conv, w_bf, w_bf, x)


def _router_kernel(h_ref, nw_ref, wr_ref, br_ref, hn_ref, ids_ref, gates_ref):
    half = D_MODEL // 2
    x = h_ref[...]
    ms = jnp.mean(x * x, axis=-1, keepdims=True)
    hn = x * lax.rsqrt(ms + EPS) * nw_ref[...]
    lo = pltpu.bitcast(hn[:, :half].astype(BF16).astype(F32), jnp.uint32)
    hi = pltpu.bitcast(hn[:, half:].astype(BF16).astype(F32), jnp.uint32)
    words = (hi & jnp.uint32(0xFFFF0000)) | (lo >> 16)
    for s in range(X_ROW_TILES):
        hn_ref[:, s, :] = words[:, s * LANES:(s + 1) * LANES]

    logits = jnp.dot(hn, wr_ref[...], preferred_element_type=F32,
                     precision=lax.Precision.HIGHEST) + br_ref[...]
    lane = lax.broadcasted_iota(jnp.int32, logits.shape, 1)
    big = jnp.int32(1 << 20)

    is_g = lane < N_GROUPS
    gl = jnp.where(is_g, logits, -jnp.inf)
    ge = jnp.exp(gl - jnp.max(gl, axis=-1, keepdims=True))
    g_prob = ge / jnp.sum(ge, axis=-1, keepdims=True)
    g_w = jnp.max(g_prob, axis=-1, keepdims=True)
    g_top = jnp.min(jnp.where(is_g & (g_prob == g_w), lane, big), axis=-1, keepdims=True)

    eidx = lane - N_GROUPS
    grp_shift = EXPERTS_PER_GROUP.bit_length() - 1
    in_grp = (eidx >= 0) & (eidx < N_EXPERTS) & ((eidx >> grp_shift) == g_top)
    el = jnp.where(in_grp, logits, -jnp.inf)
    ee = jnp.exp(el - jnp.max(el, axis=-1, keepdims=True))
    e_prob = ee / jnp.sum(ee, axis=-1, keepdims=True)
    v1 = jnp.max(jnp.where(in_grp, e_prob, -1.0), axis=-1, keepdims=True)
    i1 = jnp.min(jnp.where(in_grp & (e_prob == v1), lane, big), axis=-1, keepdims=True)
    rest = in_grp & (lane != i1)
    v2 = jnp.max(jnp.where(rest, e_prob, -1.0), axis=-1, keepdims=True)
    i2 = jnp.min(jnp.where(rest & (e_prob == v2), lane, big), axis=-1, keepdims=True)
    tot = v1 + v2
    ids_ref[...] = jnp.where(lane == 0, i1 - N_GROUPS, jnp.where(lane == 1, i2 - N_GROUPS, 0))
    gates_ref[...] = jnp.where(lane == 0, g_w * v1 / tot,
                               jnp.where(lane == 1, g_w * v2 / tot, 0.0))


def _router(h, norm_w, w_router, b_router):
    return pl.pallas_call(
        _router_kernel,
        grid=(SEQ // ROUTER_ROWS,),
        in_specs=[pl.BlockSpec((ROUTER_ROWS, D_MODEL), lambda i: (i, 0)),
                  pl.BlockSpec((1, D_MODEL), lambda i: (0, 0)),
                  pl.BlockSpec((D_MODEL, LANES), lambda i: (0, 0)),
                  pl.BlockSpec((1, LANES), lambda i: (0, 0))],
        out_specs=[pl.BlockSpec((ROUTER_ROWS, X_ROW_TILES, LANES), lambda i: (i, 0, 0)),
                   pl.BlockSpec((ROUTER_ROWS, LANES), lambda i: (i, 0)),
                   pl.BlockSpec((ROUTER_ROWS, LANES), lambda i: (i, 0))],
        out_shape=[jax.ShapeDtypeStruct((SEQ, X_ROW_TILES, LANES), jnp.uint32),
                   jax.ShapeDtypeStruct((SEQ, LANES), jnp.int32),
                   jax.ShapeDtypeStruct((SEQ, LANES), F32)],
        compiler_params=_params(1),
        name="ffn_norm_router",
    )(h, norm_w, w_router, b_router)


def _row_copy(src_hbm, src_row, dst_ref, dst_row, sem):
    return pltpu.make_async_copy(src_hbm.at[pl.ds(src_row, 1)], dst_ref.at[pl.ds(dst_row, 1)], sem)


def _gather_kernel(nv_ref, tok_ref, hn_hbm, o_ref, sem):
    nv = nv_ref[pl.program_id(0)]

    def start(r, carry):
        _row_copy(hn_hbm, tok_ref[0, 0, r], o_ref, r, sem).start()
        return carry

    def zero(r, carry):
        o_ref[r] = jnp.zeros((X_ROW_TILES, LANES), o_ref.dtype)
        return carry

    def wait(r, carry):
        _row_copy(hn_hbm, 0, o_ref, r, sem).wait()
        return carry

    lax.fori_loop(0, nv, start, 0)
    lax.fori_loop(nv, MOE_ROWS, zero, 0)
    lax.fori_loop(0, nv, wait, 0)


def _gather_rows(n_valid, row_tok, hn_words):
    return pl.pallas_call(
        _gather_kernel,
        grid_spec=pltpu.PrefetchScalarGridSpec(
            num_scalar_prefetch=1,
            grid=(N_MOE_BLOCKS,),
            in_specs=[pl.BlockSpec((1, 1, MOE_ROWS), lambda i, nv: (i, 0, 0),
                                   memory_space=pltpu.SMEM),
                      pl.BlockSpec(memory_space=pl.ANY)],
            out_specs=pl.BlockSpec((MOE_ROWS, X_ROW_TILES, LANES), lambda i, nv: (i, 0, 0)),
            scratch_shapes=[pltpu.SemaphoreType.DMA(())]),
        out_shape=jax.ShapeDtypeStruct((N_MOE_ROWS, X_ROW_TILES, LANES), jnp.uint32),
        compiler_params=_params(1),
        name="moe_gather",
    )(n_valid, row_tok.reshape(N_MOE_BLOCKS, 1, MOE_ROWS), hn_words)


def _unpack_rows(xs_ref):
    words = jnp.concatenate([xs_ref[:, s, :] for s in range(X_ROW_TILES)], axis=-1)
    lo = pltpu.bitcast(words << 16, F32).astype(BF16)
    hi = pltpu.bitcast(words & jnp.uint32(0xFFFF0000), F32).astype(BF16)
    return jnp.concatenate([lo, hi], axis=-1)


def _stream_weight_tile(w, first_ref, seq_ref, n_tiles_ref, tile_copies, consume):
    n_tiles = n_tiles_ref[0]

    @pl.when(w == 0)
    def _():
        for k in range(WEIGHT_RING):
            @pl.when(k < n_tiles)
            def _():
                for c in tile_copies(k, k):
                    c.start()

    @pl.when(first_ref[w] == 1)
    def _():
        k = seq_ref[w]
        slot = k % WEIGHT_RING
        for c in tile_copies(k, slot):
            c.wait()
        consume(slot)

        @pl.when(k + WEIGHT_RING < n_tiles)
        def _():
            for c in tile_copies(k + WEIGHT_RING, slot):
                c.start()


def _moe_up_kernel(b_ref, ot_ref, valid_ref, first_ref, seq_ref, te_ref, tt_ref, nt_ref,
                   xs_ref, wg_hbm, wu_hbm, hid_ref, ring_g, ring_u, wg_bf, wu_bf, sem):
    w = pl.program_id(0)

    def tile_copies(k, slot):
        e = te_ref[k]
        col = pl.multiple_of(tt_ref[k] * UP_TILE, UP_TILE)
        return (pltpu.make_async_copy(wg_hbm.at[e, :, pl.ds(col, UP_TILE)], ring_g.at[slot],
                                      sem.at[0, slot]),
                pltpu.make_async_copy(wu_hbm.at[e, :, pl.ds(col, UP_TILE)], ring_u.at[slot],
                                      sem.at[1, slot]))

    def consume(slot):
        wg_bf[...] = ring_g[slot].astype(BF16)
        wu_bf[...] = ring_u[slot].astype(BF16)

    _stream_weight_tile(w, first_ref, seq_ref, nt_ref, tile_copies, consume)

    @pl.when(valid_ref[w] == 1)
    def _():
        x = _unpack_rows(xs_ref)
        g = jnp.dot(x, wg_bf[...], preferred_element_type=F32)
        u = jnp.dot(x, wu_bf[...], preferred_element_type=F32)
        hid_ref[...] = (g * jax.nn.sigmoid(g) * u).astype(hid_ref.dtype)

    @pl.when(valid_ref[w] == 0)
    def _():
        hid_ref[...] = jnp.zeros_like(hid_ref)


def _moe_up(items, xs, w_gate, w_up):
    n_items = N_MOE_BLOCKS * (D_FF // UP_TILE)
    any_spec = pl.BlockSpec(memory_space=pl.ANY)
    return pl.pallas_call(
        _moe_up_kernel,
        grid_spec=pltpu.PrefetchScalarGridSpec(
            num_scalar_prefetch=8,
            grid=(n_items,),
            in_specs=[pl.BlockSpec((MOE_ROWS, X_ROW_TILES, LANES),
                                   lambda w, b, ot, *_: (b[w], 0, 0)),
                      any_spec, any_spec],
            out_specs=pl.BlockSpec((MOE_ROWS, UP_TILE), lambda w, b, ot, *_: (b[w], ot[w])),
            scratch_shapes=[pltpu.VMEM((WEIGHT_RING, D_MODEL, UP_TILE), F32),
                            pltpu.VMEM((WEIGHT_RING, D_MODEL, UP_TILE), F32),
                            pltpu.VMEM((D_MODEL, UP_TILE), BF16),
                            pltpu.VMEM((D_MODEL, UP_TILE), BF16),
                            pltpu.SemaphoreType.DMA((2, WEIGHT_RING))]),
        out_shape=jax.ShapeDtypeStruct((N_MOE_ROWS, D_FF), BF16),
        compiler_params=_params(1),
        name="moe_gate_up",
    )(*items, xs, w_gate, w_up)


def _moe_down_kernel(b_ref, ot_ref, valid_ref, first_ref, seq_ref, te_ref, tt_ref, nt_ref,
                     hid_ref, wd_hbm, y_ref, ring, wd_bf, sem):
    w = pl.program_id(0)

    def tile_copies(k, slot):
        col = pl.multiple_of(tt_ref[k] * DOWN_TILE, DOWN_TILE)
        return (pltpu.make_async_copy(wd_hbm.at[te_ref[k], :, pl.ds(col, DOWN_TILE)],
                                      ring.at[slot], sem.at[slot]),)

    def consume(slot):
        wd_bf[...] = ring[slot].astype(BF16)

    _stream_weight_tile(w, first_ref, seq_ref, nt_ref, tile_copies, consume)

    @pl.when(valid_ref[w] == 1)
    def _():
        y = jnp.dot(hid_ref[...], wd_bf[...], preferred_element_type=F32)
        for s in range(DOWN_TILE // LANES):
            y_ref[:, s, :] = y[:, s * LANES:(s + 1) * LANES]

    @pl.when(valid_ref[w] == 0)
    def _():
        y_ref[...] = jnp.zeros_like(y_ref)


def _moe_down(items, hid, w_down):
    n_items = N_MOE_BLOCKS * (D_MODEL // DOWN_TILE)
    tile_rows = DOWN_TILE // LANES
    return pl.pallas_call(
        _moe_down_kernel,
        grid_spec=pltpu.PrefetchScalarGridSpec(
            num_scalar_prefetch=8,
            grid=(n_items,),
            in_specs=[pl.BlockSpec((MOE_ROWS, D_FF), lambda w, b, ot, *_: (b[w], 0)),
                      pl.BlockSpec(memory_space=pl.ANY)],
            out_specs=pl.BlockSpec((MOE_ROWS, tile_rows, LANES),
                                   lambda w, b, ot, *_: (b[w], ot[w], 0)),
            scratch_shapes=[pltpu.VMEM((WEIGHT_RING, D_FF, DOWN_TILE), F32),
                            pltpu.VMEM((D_FF, DOWN_TILE), BF16),
                            pltpu.SemaphoreType.DMA((WEIGHT_RING,))]),
        out_shape=jax.ShapeDtypeStruct((N_MOE_ROWS, Y_ROW_TILES, LANES), F32),
        compiler_params=_params(1),
        name="moe_down",
    )(*items, hid, w_down)


def _combine_kernel(dest_ref, h_ref, gate_ref, y_hbm, o_ref, buf, sem):
    def copy(a, src_row):
        dst = buf.at[a % TOP_K, pl.ds((a // TOP_K) * COMBINE_PITCH, Y_ROW_TILES)]
        return pltpu.make_async_copy(y_hbm.at[src_row], dst, sem)

    def start(a, carry):
        copy(a, dest_ref[0, 0, a]).start()
        return carry

    def wait(a, carry):
        copy(a, 0).wait()
        return carry

    lax.fori_loop(0, COMBINE_ROWS * TOP_K, start, 0)
    lax.fori_loop(0, COMBINE_ROWS * TOP_K, wait, 0)
    g0 = gate_ref[:, 0:1]
    g1 = gate_ref[:, 1:2]
    for s in range(Y_ROW_TILES):
        cols = slice(s * LANES, (s + 1) * LANES)
        y0 = buf[0, pl.ds(s, COMBINE_ROWS, stride=COMBINE_PITCH), :]
        y1 = buf[1, pl.ds(s, COMBINE_ROWS, stride=COMBINE_PITCH), :]
        o_ref[:, cols] = h_ref[:, cols] + (y0 * g0 + y1 * g1)


def _combine(dest, h, gates, y):
    n_tiles = SEQ // COMBINE_ROWS
    per_tile = COMBINE_ROWS * TOP_K
    return pl.pallas_call(
        _combine_kernel,
        grid=(n_tiles,),
        in_specs=[pl.BlockSpec((1, 1, per_tile), lambda i: (i, 0, 0), memory_space=pltpu.SMEM),
                  pl.BlockSpec((COMBINE_ROWS, D_MODEL), lambda i: (i, 0)),
                  pl.BlockSpec((COMBINE_ROWS, LANES), lambda i: (i, 0)),
                  pl.BlockSpec(memory_space=pl.ANY)],
        out_specs=pl.BlockSpec((COMBINE_ROWS, D_MODEL), lambda i: (i, 0)),
        out_shape=jax.ShapeDtypeStruct((SEQ, D_MODEL), F32),
        scratch_shapes=[pltpu.VMEM((TOP_K, COMBINE_ROWS * COMBINE_PITCH, LANES), F32),
                        pltpu.SemaphoreType.DMA(())],
        compiler_params=_params(1),
        name="moe_combine",
    )(dest.reshape(n_tiles, 1, per_tile), h, gates, y)


def _work_items(n_blocks_e, block_start_e, tiles):
    i32 = lambda a: a.astype(jnp.int32)
    n_items = N_MOE_BLOCKS * tiles
    per_e = n_blocks_e * tiles
    ends = jnp.cumsum(per_e)
    total = ends[-1]
    idx = jnp.arange(n_items, dtype=jnp.int32)
    valid = idx < total
    w = jnp.minimum(idx, total - 1)
    e = i32(jnp.searchsorted(ends, w, side='right'))
    local = w - (ends - per_e)[e]
    nb = n_blocks_e[e]
    spare = idx - total
    w_tile = local // nb
    o_tile = jnp.where(valid, w_tile, spare % tiles)
    blk = jnp.where(valid, block_start_e[e] + local % nb, total // tiles + spare // tiles)
    first = valid & (local % nb == 0)
    active = n_blocks_e > 0
    active_rank = jnp.cumsum(active) - active
    seq = active_rank[e] * tiles + w_tile
    active_experts = jnp.argsort(~active, stable=True)
    k = jnp.arange(N_EXPERTS * tiles, dtype=jnp.int32)
    tile_e = active_experts[k // tiles]
    tile_t = k % tiles
    n_tiles = (jnp.sum(active) * tiles).reshape(1)
    return (i32(blk), i32(o_tile), i32(valid), i32(first), i32(seq), i32(tile_e), i32(tile_t),
            i32(n_tiles))


def _dispatch_plan(ids):
    expert = ids[:, :TOP_K].reshape(-1)
    token = jnp.repeat(jnp.arange(SEQ, dtype=jnp.int32), TOP_K)
    onehot = (expert[:, None] == jnp.arange(N_EXPERTS, dtype=jnp.int32)[None, :]).astype(jnp.int32)
    running = jnp.cumsum(onehot, axis=0)
    rank = jnp.take_along_axis(running, expert[:, None], axis=1)[:, 0] - 1
    counts = running[-1]
    n_blocks_e = (counts + MOE_ROWS - 1) // MOE_ROWS
    block_ends = jnp.cumsum(n_blocks_e)
    block_start_e = block_ends - n_blocks_e
    dest = (block_start_e[expert] * MOE_ROWS + rank).astype(jnp.int32)
    row_tok = jnp.zeros((N_MOE_ROWS,), jnp.int32).at[dest].set(token)
    blk = jnp.arange(N_MOE_BLOCKS, dtype=jnp.int32)
    blk_e = jnp.minimum(jnp.searchsorted(block_ends, blk, side='right'), N_EXPERTS - 1)
    n_valid = jnp.clip(counts[blk_e] - (blk - block_start_e[blk_e]) * MOE_ROWS, 0, MOE_ROWS)
    n_valid = jnp.where(blk < block_ends[-1], n_valid, 0).astype(jnp.int32)
    return dest, row_tok, n_valid, n_blocks_e.astype(jnp.int32), block_start_e.astype(jnp.int32)


def kernel(x, norm_mix_w, w_in, q_norm_w, k_norm_w, conv_w, conv_b, conv_ln_w, conv_ln_b, w_out,
           norm_ffn_w, w_group_router, b_group_router, w_expert_router, b_expert_router,
           w_gate, w_up, w_down):
    h = x.reshape(SEQ, D_MODEL)
    slopes = jnp.exp2(-8.0 * jnp.arange(1, N_HEADS + 1, dtype=F32) / N_HEADS)
    for l in range(norm_mix_w.shape[0]):
        hn = _rmsnorm(h, norm_mix_w[l][None, :])
        qk_norm = jnp.concatenate([jnp.tile(q_norm_w[l], N_HEADS), jnp.tile(k_norm_w[l], N_HEADS)])
        z = _inproj(hn, w_in[l].astype(BF16), qk_norm[None, :])
        attn = _attention(z, slopes)
        conv = _conformer_conv(z, conv_w[l], conv_b[l][None, :], conv_ln_w[l][None, :],
                               conv_ln_b[l][None, :])
        h = _outproj(attn, conv, w_out[l].astype(BF16), h)

        pad = LANES - N_GROUPS - N_EXPERTS
        w_router = jnp.concatenate([w_group_router[l], w_expert_router[l],
                                    jnp.zeros((D_MODEL, pad), F32)], axis=1)
        b_router = jnp.concatenate([b_group_router[l], b_expert_router[l], jnp.zeros((pad,), F32)])
        hn_words, ids, gates = _router(h, norm_ffn_w[l][None, :], w_router, b_router[None, :])
        dest, row_tok, n_valid, n_blocks_e, block_start_e = _dispatch_plan(ids)
        xs = _gather_rows(n_valid, row_tok, hn_words)
        hid = _moe_up(_work_items(n_blocks_e, block_start_e, D_FF // UP_TILE), xs, w_gate[l], w_up[l])
        y = _moe_down(_work_items(n_blocks_e, block_start_e, D_MODEL // DOWN_TILE), hid, w_down[l])
        h = _combine(dest, h, gates, y)
    return h.reshape(x.shape)
```

```python
import math

import jax
import jax.numpy as jnp
from jax import lax
from jax.experimental import pallas as pl
from jax.experimental.pallas import tpu as pltpu

F32 = jnp.float32
BF16 = jnp.bfloat16

D_MODEL = 4096
SEQ = 8192
HEAD_DIM = 128
N_HEADS = 16
ATTN_WIDTH = N_HEADS * HEAD_DIM
CONV_WIDTH = D_MODEL - ATTN_WIDTH
CONV_KERNEL = 31
IN_COLS = 3 * ATTN_WIDTH + 2 * CONV_WIDTH
DILATIONS = (1, 4, 16)
ATTN_BLOCK = 128
N_KEYS = 128
N_GROUPS = 8
EXPERTS_PER_GROUP = 8
N_EXPERTS = N_GROUPS * EXPERTS_PER_GROUP
TOP_K = 2
D_FF = 1024
EPS = 1e-6
NEG_INF = -1e30

LANES = 128
VMEM_LIMIT = 56 * 1024 * 1024

SPAN = DILATIONS[-1] * ATTN_BLOCK

MOE_ROWS = 256
N_ASSIGN = SEQ * TOP_K
N_MOE_ROWS = N_ASSIGN + N_EXPERTS * MOE_ROWS
N_MOE_BLOCKS = N_MOE_ROWS // MOE_ROWS
UP_TILE = 512
DOWN_TILE = 2048
WEIGHT_RING = 2
ROUTER_ROWS = 256
COMBINE_ROWS = 128
X_ROW_TILES = D_MODEL // 2 // LANES
Y_ROW_TILES = D_MODEL // LANES
COMBINE_PITCH = Y_ROW_TILES + 4
GATHER_PITCH = X_ROW_TILES + 8


def _params(n_axes):
    return pltpu.CompilerParams(dimension_semantics=("arbitrary",) * n_axes,
                                vmem_limit_bytes=VMEM_LIMIT)


def _rmsnorm_kernel(x_ref, w_ref, o_ref):
    x = x_ref[...]
    ms = jnp.mean(x * x, axis=-1, keepdims=True)
    o_ref[...] = (x * lax.rsqrt(ms + EPS) * w_ref[...]).astype(o_ref.dtype)


def _rmsnorm(x, w, rows=256):
    n, d = x.shape
    return pl.pallas_call(
        _rmsnorm_kernel,
        grid=(n // rows,),
        in_specs=[pl.BlockSpec((rows, d), lambda i: (i, 0)),
                  pl.BlockSpec((1, d), lambda i: (0, 0))],
        out_specs=pl.BlockSpec((rows, d), lambda i: (i, 0)),
        out_shape=jax.ShapeDtypeStruct((n, d), BF16),
        compiler_params=_params(1),
        name="rmsnorm",
    )(x, w)


IN_TM = 1024
IN_TN = 512
QK_TILES = 2 * ATTN_WIDTH // IN_TN


def _inproj_kernel(hn_ref, w_ref, nw_ref, o_ref, w_bf):
    j = pl.program_id(0)

    @pl.when(pl.program_id(1) == 0)
    def _():
        w_bf[...] = w_ref[...].astype(BF16)

    acc = jnp.dot(hn_ref[...], w_bf[...], preferred_element_type=F32)

    @pl.when(j < QK_TILES)
    def _():
        for c in range(IN_TN // HEAD_DIM):
            sl = slice(c * HEAD_DIM, (c + 1) * HEAD_DIM)
            t = acc[:, sl]
            ms = jnp.mean(t * t, axis=-1, keepdims=True)
            o_ref[:, sl] = (t * lax.rsqrt(ms + EPS) * nw_ref[:, sl]).astype(o_ref.dtype)

    @pl.when(j >= QK_TILES)
    def _():
        o_ref[...] = acc.astype(o_ref.dtype)


def _inproj(hn, w, qk_norm_w):
    n = hn.shape[0]
    return pl.pallas_call(
        _inproj_kernel,
        grid=(IN_COLS // IN_TN, n // IN_TM),
        in_specs=[pl.BlockSpec((IN_TM, D_MODEL), lambda j, i: (i, 0)),
                  pl.BlockSpec((D_MODEL, IN_TN), lambda j, i: (0, j)),
                  pl.BlockSpec((1, IN_TN), lambda j, i: (0, jnp.minimum(j, QK_TILES - 1)))],
        out_specs=pl.BlockSpec((IN_TM, IN_TN), lambda j, i: (i, j)),
        out_shape=jax.ShapeDtypeStruct((n, IN_COLS), BF16),
        scratch_shapes=[pltpu.VMEM((D_MODEL, IN_TN), BF16)],
        compiler_params=_params(2),
        name="inproj",
    )(hn, w, qk_norm_w)


def _attn_kernel(slopes_ref, q_ref, kc_ref, kp_ref, vc_ref, vp_ref, o_ref, qf, kf, vf, ob, lb):
    h = pl.program_id(0)
    first_span = pl.program_id(1) == 0
    slope = slopes_ref[h]

    qf[...] = q_ref[...].astype(F32)
    kf[0:SPAN, :] = kp_ref[...].astype(F32)
    kf[SPAN:2 * SPAN, :] = kc_ref[...].astype(F32)
    vf[0:SPAN, :] = vp_ref[...].astype(F32)
    vf[SPAN:2 * SPAN, :] = vc_ref[...].astype(F32)

    qi = lax.broadcasted_iota(jnp.int32, (ATTN_BLOCK, 2 * ATTN_BLOCK), 0)
    kj = lax.broadcasted_iota(jnp.int32, (ATTN_BLOCK, 2 * ATTN_BLOCK), 1)
    steps = qi + ATTN_BLOCK - kj
    valid = (steps >= 0) & (steps <= N_KEYS)
    valid_first = valid & (kj >= jnp.where(first_span, ATTN_BLOCK, 0))
    stepsf = steps.astype(F32)
    scale = 1.0 / math.sqrt(HEAD_DIM)

    for b, d in enumerate(DILATIONS):
        bias = (-slope * float(d)) * stepsf
        for r in range(d):
            for n in range(SPAN // (ATTN_BLOCK * d)):
                q0 = r + d * ATTN_BLOCK * n
                k0 = SPAN + r + d * ATTN_BLOCK * (n - 1)
                q = qf[pl.ds(q0, ATTN_BLOCK, stride=d), :].astype(BF16)
                k = kf[pl.ds(k0, 2 * ATTN_BLOCK, stride=d), :].astype(BF16)
                v = vf[pl.ds(k0, 2 * ATTN_BLOCK, stride=d), :].astype(BF16)
                s = lax.dot_general(q, k, (((1,), (1,)), ((), ())),
                                    preferred_element_type=F32) * scale
                s = jnp.where(valid_first if n == 0 else valid, s + bias, NEG_INF)
                m = jnp.max(s, axis=-1, keepdims=True)
                p = jnp.exp(s - m)
                den = jnp.sum(p, axis=-1, keepdims=True)
                o = jnp.dot(p.astype(BF16), v, preferred_element_type=F32) / den
                lse = m + jnp.log(den)
                ob[b, pl.ds(q0, ATTN_BLOCK, stride=d), :] = o
                lb[b, pl.ds(q0, ATTN_BLOCK, stride=d), :] = jnp.broadcast_to(
                    lse, (ATTN_BLOCK, HEAD_DIM))

    chunk = 256
    for c in range(SPAN // chunk):
        rows = slice(c * chunk, (c + 1) * chunk)
        l0, l1, l2 = lb[0, rows, :], lb[1, rows, :], lb[2, rows, :]
        mx = jnp.maximum(jnp.maximum(l0, l1), l2)
        w0, w1, w2 = jnp.exp(l0 - mx), jnp.exp(l1 - mx), jnp.exp(l2 - mx)
        num = w0 * ob[0, rows, :] + w1 * ob[1, rows, :] + w2 * ob[2, rows, :]
        o_ref[rows, :] = (num / (w0 + w1 + w2)).astype(o_ref.dtype)


def _attention(z, slopes):
    n_spans = SEQ // SPAN
    k_col, v_col = N_HEADS, 2 * N_HEADS
    blk = (SPAN, HEAD_DIM)
    prev = lambda s: jnp.maximum(s - 1, 0)
    return pl.pallas_call(
        _attn_kernel,
        grid_spec=pltpu.PrefetchScalarGridSpec(
            num_scalar_prefetch=1,
            grid=(N_HEADS, n_spans),
            in_specs=[pl.BlockSpec(blk, lambda h, s, sl: (s, h)),
                      pl.BlockSpec(blk, lambda h, s, sl: (s, k_col + h)),
                      pl.BlockSpec(blk, lambda h, s, sl: (prev(s), k_col + h)),
                      pl.BlockSpec(blk, lambda h, s, sl: (s, v_col + h)),
                      pl.BlockSpec(blk, lambda h, s, sl: (prev(s), v_col + h))],
            out_specs=pl.BlockSpec(blk, lambda h, s, sl: (s, h)),
            scratch_shapes=[pltpu.VMEM((SPAN, HEAD_DIM), F32),
                            pltpu.VMEM((2 * SPAN, HEAD_DIM), F32),
                            pltpu.VMEM((2 * SPAN, HEAD_DIM), F32),
                            pltpu.VMEM((3, SPAN, HEAD_DIM), F32),
                            pltpu.VMEM((3, SPAN, HEAD_DIM), F32)]),
        out_shape=jax.ShapeDtypeStruct((SEQ, ATTN_WIDTH), BF16),
        compiler_params=_params(2),
        name="dilated_attention",
    )(slopes, z, z, z, z, z)


CONV_ROWS = 256
CONV_HALO = 32
CONV_CHUNK = 64


SUBLANES = 8
CONV_EXT = CONV_HALO + CONV_ROWS
CONV_SHIFTED = CONV_EXT - SUBLANES


def _conv_kernel(a_ref, g_ref, ah_ref, gh_ref, cw_ref, cb_ref, lw_ref, lb_ref, o_ref,
                 u_ext, shifted, c_scr):
    i = pl.program_id(0)
    n_ct = CONV_WIDTH // LANES
    for ct in range(n_ct):
        cols = slice(ct * LANES, (ct + 1) * LANES)
        u_ext[ct, CONV_HALO:, :] = (a_ref[:, cols].astype(F32)
                                    * jax.nn.sigmoid(g_ref[:, cols].astype(F32)))
        uh = ah_ref[:, cols].astype(F32) * jax.nn.sigmoid(gh_ref[:, cols].astype(F32))
        u_ext[ct, 0:CONV_HALO, :] = jnp.where(i == 0, 0.0, uh)

    first_tap = CONV_HALO - (CONV_KERNEL - 1)

    def slab(ct, carry):
        for k in range(1, SUBLANES):
            shifted[k - 1] = u_ext[ct, k:k + CONV_SHIFTED, :]
        for rc in range(CONV_ROWS // CONV_CHUNK):
            base = rc * CONV_CHUNK
            acc = jnp.broadcast_to(cb_ref[pl.ds(ct, 1), :], (CONV_CHUNK, LANES))
            for j in range(CONV_KERNEL):
                k = (first_tap + j) % SUBLANES
                row = base + (first_tap + j) - k
                if k == 0:
                    src = u_ext[ct, row:row + CONV_CHUNK, :]
                else:
                    src = shifted[k - 1, row:row + CONV_CHUNK, :]
                acc = acc + cw_ref[j, pl.ds(ct, 1), :] * src
            c_scr[ct, base:base + CONV_CHUNK, :] = acc
        return carry

    lax.fori_loop(0, n_ct, slab, 0)

    total = c_scr[0]
    for ct in range(1, n_ct):
        total = total + c_scr[ct]
    mu = jnp.sum(total, axis=-1, keepdims=True) * (1.0 / CONV_WIDTH)
    sq = jnp.square(c_scr[0] - mu)
    for ct in range(1, n_ct):
        sq = sq + jnp.square(c_scr[ct] - mu)
    var = jnp.sum(sq, axis=-1, keepdims=True) * (1.0 / CONV_WIDTH)
    rstd = lax.rsqrt(var + EPS)
    for ct in range(n_ct):
        cols = slice(ct * LANES, (ct + 1) * LANES)
        y = (c_scr[ct] - mu) * rstd * lw_ref[:, cols] + lb_ref[:, cols]
        o_ref[:, cols] = (y * jax.nn.sigmoid(y)).astype(o_ref.dtype)


def _conformer_conv(z, conv_w, conv_b, ln_w, ln_b):
    a_col = 3 * ATTN_WIDTH // CONV_WIDTH
    g_col = a_col + 1
    n_ct = CONV_WIDTH // LANES
    halo_blocks = CONV_ROWS // CONV_HALO
    halo = lambda i: jnp.maximum(i * halo_blocks - 1, 0)
    vec = pl.BlockSpec((1, CONV_WIDTH), lambda i: (0, 0))
    return pl.pallas_call(
        _conv_kernel,
        grid=(SEQ // CONV_ROWS,),
        in_specs=[pl.BlockSpec((CONV_ROWS, CONV_WIDTH), lambda i: (i, a_col)),
                  pl.BlockSpec((CONV_ROWS, CONV_WIDTH), lambda i: (i, g_col)),
                  pl.BlockSpec((CONV_HALO, CONV_WIDTH), lambda i: (halo(i), a_col)),
                  pl.BlockSpec((CONV_HALO, CONV_WIDTH), lambda i: (halo(i), g_col)),
                  pl.BlockSpec((CONV_KERNEL, n_ct, LANES), lambda i: (0, 0, 0)),
                  pl.BlockSpec((n_ct, LANES), lambda i: (0, 0)),
                  vec, vec],
        out_specs=pl.BlockSpec((CONV_ROWS, CONV_WIDTH), lambda i: (i, 0)),
        out_shape=jax.ShapeDtypeStruct((SEQ, CONV_WIDTH), BF16),
        scratch_shapes=[pltpu.VMEM((n_ct, CONV_EXT, LANES), F32),
                        pltpu.VMEM((SUBLANES - 1, CONV_SHIFTED, LANES), F32),
                        pltpu.VMEM((n_ct, CONV_ROWS, LANES), F32)],
        compiler_params=_params(1),
        name="conformer_conv",
    )(z, z, z, z, conv_w, conv_b, ln_w, ln_b)


OUT_TM = 1024
OUT_TN = 512


def _outproj_kernel(attn_ref, conv_ref, wa_ref, wc_ref, x_ref, o_ref, wa_bf, wc_bf):
    @pl.when(pl.program_id(1) == 0)
    def _():
        wa_bf[...] = wa_ref[...].astype(BF16)
        wc_bf[...] = wc_ref[...].astype(BF16)

    acc = jnp.dot(attn_ref[...], wa_bf[...], preferred_element_type=F32)
    acc = acc + jnp.dot(conv_ref[...], wc_bf[...], preferred_element_type=F32)
    o_ref[...] = x_ref[...] + acc


def _outproj(attn, conv, w, x):
    n = x.shape[0]
    conv_blk = ATTN_WIDTH // CONV_WIDTH
    return pl.pallas_call(
        _outproj_kernel,
        grid=(D_MODEL // OUT_TN, n // OUT_TM),
        in_specs=[pl.BlockSpec((OUT_TM, ATTN_WIDTH), lambda j, i: (i, 0)),
                  pl.BlockSpec((OUT_TM, CONV_WIDTH), lambda j, i: (i, 0)),
                  pl.BlockSpec((ATTN_WIDTH, OUT_TN), lambda j, i: (0, j)),
                  pl.BlockSpec((CONV_WIDTH, OUT_TN), lambda j, i: (conv_blk, j)),
                  pl.BlockSpec((OUT_TM, OUT_TN), lambda j, i: (i, j))],
        out_specs=pl.BlockSpec((OUT_TM, OUT_TN), lambda j, i: (i, j)),
        out_shape=jax.ShapeDtypeStruct((n, D_MODEL), F32),
        scratch_shapes=[pltpu.VMEM((ATTN_WIDTH, OUT_TN), BF16),
                        pltpu.VMEM((CONV_WIDTH, OUT_TN), BF16)],
        compiler_params=_params(2),
        name="outproj",
    )(attn, conv, w, w, x)


def _router_kernel(h_ref, nw_ref, wr_ref, br_ref, hn_ref, ids_ref, gates_ref):
    half = D_MODEL // 2
    x = h_ref[...]
    ms = jnp.mean(x * x, axis=-1, keepdims=True)
    hn = x * lax.rsqrt(ms + EPS) * nw_ref[...]
    lo = pltpu.bitcast(hn[:, :half].astype(BF16).astype(F32), jnp.uint32)
    hi = pltpu.bitcast(hn[:, half:].astype(BF16).astype(F32), jnp.uint32)
    words = (hi & jnp.uint32(0xFFFF0000)) | (lo >> 16)
    for s in range(X_ROW_TILES):
        hn_ref[:, s, :] = words[:, s * LANES:(s + 1) * LANES]

    logits = jnp.dot(hn, wr_ref[...], preferred_element_type=F32,
                     precision=lax.Precision.HIGHEST) + br_ref[...]
    lane = lax.broadcasted_iota(jnp.int32, logits.shape, 1)
    big = jnp.int32(1 << 20)

    is_g = lane < N_GROUPS
    gl = jnp.where(is_g, logits, -jnp.inf)
    ge = jnp.exp(gl - jnp.max(gl, axis=-1, keepdims=True))
    g_prob = ge / jnp.sum(ge, axis=-1, keepdims=True)
    g_w = jnp.max(g_prob, axis=-1, keepdims=True)
    g_top = jnp.min(jnp.where(is_g & (g_prob == g_w), lane, big), axis=-1, keepdims=True)

    eidx = lane - N_GROUPS
    grp_shift = EXPERTS_PER_GROUP.bit_length() - 1
    in_grp = (eidx >= 0) & (eidx < N_EXPERTS) & ((eidx >> grp_shift) == g_top)
    el = jnp.where(in_grp, logits, -jnp.inf)
    ee = jnp.exp(el - jnp.max(el, axis=-1, keepdims=True))
    e_prob = ee / jnp.sum(ee, axis=-1, keepdims=True)
    v1 = jnp.max(jnp.where(in_grp, e_prob, -1.0), axis=-1, keepdims=True)
    i1 = jnp.min(jnp.where(in_grp & (e_prob == v1), lane, big), axis=-1, keepdims=True)
    rest = in_grp & (lane != i1)
    v2 = jnp.max(jnp.where(rest, e_prob, -1.0), axis=-1, keepdims=True)
    i2 = jnp.min(jnp.where(rest & (e_prob == v2), lane, big), axis=-1, keepdims=True)
    tot = v1 + v2
    ids_ref[...] = jnp.where(lane == 0, i1 - N_GROUPS, jnp.where(lane == 1, i2 - N_GROUPS, 0))
    gates_ref[...] = jnp.where(lane == 0, g_w * v1 / tot,
                               jnp.where(lane == 1, g_w * v2 / tot, 0.0))


def _router(h, norm_w, w_router, b_router):
    return pl.pallas_call(
        _router_kernel,
        grid=(SEQ // ROUTER_ROWS,),
        in_specs=[pl.BlockSpec((ROUTER_ROWS, D_MODEL), lambda i: (i, 0)),
                  pl.BlockSpec((1, D_MODEL), lambda i: (0, 0)),
                  pl.BlockSpec((D_MODEL, LANES), lambda i: (0, 0)),
                  pl.BlockSpec((1, LANES), lambda i: (0, 0))],
        out_specs=[pl.BlockSpec((ROUTER_ROWS, X_ROW_TILES, LANES), lambda i: (i, 0, 0)),
                   pl.BlockSpec((ROUTER_ROWS, LANES), lambda i: (i, 0)),
                   pl.BlockSpec((ROUTER_ROWS, LANES), lambda i: (i, 0))],
        out_shape=[jax.ShapeDtypeStruct((SEQ, X_ROW_TILES, LANES), jnp.uint32),
                   jax.ShapeDtypeStruct((SEQ, LANES), jnp.int32),
                   jax.ShapeDtypeStruct((SEQ, LANES), F32)],
        compiler_params=_params(1),
        name="ffn_norm_router",
    )(h, norm_w, w_router, b_router)


def _gather_kernel(nv_ref, tok_ref, tok_next_ref, hn_hbm, o_ref, stage, sem):
    i = pl.program_id(0)
    n_blocks = pl.num_programs(0)

    def token_rows(slot, r):
        return stage.at[slot, pl.ds(pl.multiple_of(r * GATHER_PITCH, SUBLANES), X_ROW_TILES)]

    def issue(block, toks, slot):
        nv = nv_ref[block]

        def start(r, carry):
            pltpu.make_async_copy(hn_hbm.at[toks[0, 0, r]], token_rows(slot, r), sem.at[slot]).start()
            return carry

        def zero(r, carry):
            token_rows(slot, r)[...] = jnp.zeros((X_ROW_TILES, LANES), stage.dtype)
            return carry

        lax.fori_loop(0, nv, start, 0)
        lax.fori_loop(nv, MOE_ROWS, zero, 0)

    @pl.when(i == 0)
    def _():
        issue(0, tok_ref, 0)

    @pl.when(i + 1 < n_blocks)
    def _():
        issue(i + 1, tok_next_ref, (i + 1) % 2)

    slot = i % 2

    def wait(r, carry):
        pltpu.make_async_copy(hn_hbm.at[0], token_rows(slot, r), sem.at[slot]).wait()
        return carry

    lax.fori_loop(0, nv_ref[i], wait, 0)
    for s in range(X_ROW_TILES):
        o_ref[:, s * LANES:(s + 1) * LANES] = stage[slot, pl.ds(s, MOE_ROWS, stride=GATHER_PITCH), :]


def _gather_rows(n_valid, row_tok, hn_words):
    toks = row_tok.reshape(N_MOE_BLOCKS, 1, MOE_ROWS)
    nxt = lambda i, nv: (jnp.minimum(i + 1, N_MOE_BLOCKS - 1), 0, 0)
    return pl.pallas_call(
        _gather_kernel,
        grid_spec=pltpu.PrefetchScalarGridSpec(
            num_scalar_prefetch=1,
            grid=(N_MOE_BLOCKS,),
            in_specs=[pl.BlockSpec((1, 1, MOE_ROWS), lambda i, nv: (i, 0, 0),
                                   memory_space=pltpu.SMEM),
                      pl.BlockSpec((1, 1, MOE_ROWS), nxt, memory_space=pltpu.SMEM),
                      pl.BlockSpec(memory_space=pl.ANY)],
            out_specs=pl.BlockSpec((MOE_ROWS, D_MODEL // 2), lambda i, nv: (i, 0)),
            scratch_shapes=[pltpu.VMEM((2, MOE_ROWS * GATHER_PITCH, LANES), jnp.uint32),
                            pltpu.SemaphoreType.DMA((2,))]),
        out_shape=jax.ShapeDtypeStruct((N_MOE_ROWS, D_MODEL // 2), jnp.uint32),
        compiler_params=_params(1),
        name="moe_gather",
    )(n_valid, toks, toks, hn_words)


def _unpack_rows(words):
    lo =pltpu.bitcast(words << 16, F32).astype(BF16)
    hi = pltpu.bitcast(words & jnp.uint32(0xFFFF0000), F32).astype(BF16)
    return jnp.concatenate([lo, hi], axis=-1)


def _stream_weight_tile(w, first_ref, seq_ref, n_tiles_ref, tile_copies, consume):
    n_tiles = n_tiles_ref[0]

    @pl.when(w == 0)
    def _():
        for k in range(WEIGHT_RING):
            @pl.when(k < n_tiles)
            def _():
                for c in tile_copies(k, k):
                    c.start()

    @pl.when(first_ref[w] == 1)
    def _():
        k = seq_ref[w]
        slot = k % WEIGHT_RING
        for c in tile_copies(k, slot):
            c.wait()
        consume(slot)

        @pl.when(k + WEIGHT_RING < n_tiles)
        def _():
            for c in tile_copies(k + WEIGHT_RING, slot):
                c.start()


def _moe_up_kernel(b_ref, ot_ref, valid_ref, first_ref, seq_ref, te_ref, tt_ref, nt_ref,
                   xs_ref, wg_hbm, wu_hbm, hid_ref, ring_g, ring_u, wg_bf, wu_bf, sem):
    w = pl.program_id(0)

    def tile_copies(k, slot):
        e = te_ref[k]
        col = pl.multiple_of(tt_ref[k] * UP_TILE, UP_TILE)
        return (pltpu.make_async_copy(wg_hbm.at[e, :, pl.ds(col, UP_TILE)], ring_g.at[slot],
                                      sem.at[0, slot]),
                pltpu.make_async_copy(wu_hbm.at[e, :, pl.ds(col, UP_TILE)], ring_u.at[slot],
                                      sem.at[1, slot]))

    def consume(slot):
        wg_bf[...] = ring_g[slot].astype(BF16)
        wu_bf[...] = ring_u[slot].astype(BF16)

    _stream_weight_tile(w, first_ref, seq_ref, nt_ref, tile_copies, consume)

    @pl.when(valid_ref[w] == 1)
    def _():
        x = _unpack_rows(xs_ref[...])
        g = jnp.dot(x, wg_bf[...], preferred_element_type=F32)
        u = jnp.dot(x, wu_bf[...], preferred_element_type=F32)
        hid_ref[...] = (g * jax.nn.sigmoid(g) * u).astype(hid_ref.dtype)

    @pl.when(valid_ref[w] == 0)
    def _():
        hid_ref[...] = jnp.zeros_like(hid_ref)


def _moe_up(items, xs, w_gate, w_up):
    n_items = N_MOE_BLOCKS * (D_FF // UP_TILE)
    any_spec = pl.BlockSpec(memory_space=pl.ANY)
    return pl.pallas_call(
        _moe_up_kernel,
        grid_spec=pltpu.PrefetchScalarGridSpec(
            num_scalar_prefetch=8,
            grid=(n_items,),
            in_specs=[pl.BlockSpec((MOE_ROWS, D_MODEL // 2), lambda w, b, ot, *_: (b[w], 0)),
                      any_spec, any_spec],
            out_specs=pl.BlockSpec((MOE_ROWS, UP_TILE), lambda w, b, ot, *_: (b[w], ot[w])),
            scratch_shapes=[pltpu.VMEM((WEIGHT_RING, D_MODEL, UP_TILE), F32),
                            pltpu.VMEM((WEIGHT_RING, D_MODEL, UP_TILE), F32),
                            pltpu.VMEM((D_MODEL, UP_TILE), BF16),
                            pltpu.VMEM((D_MODEL, UP_TILE), BF16),
                            pltpu.SemaphoreType.DMA((2, WEIGHT_RING))]),
        out_shape=jax.ShapeDtypeStruct((N_MOE_ROWS, D_FF), BF16),
        compiler_params=_params(1),
        name="moe_gate_up",
    )(*items, xs, w_gate, w_up)


def _moe_down_kernel(b_ref, ot_ref, valid_ref, first_ref, seq_ref, te_ref, tt_ref, nt_ref,
                     hid_ref, wd_hbm, y_ref, ring, wd_bf, sem):
    w = pl.program_id(0)

    def tile_copies(k, slot):
        col = pl.multiple_of(tt_ref[k] * DOWN_TILE, DOWN_TILE)
        return (pltpu.make_async_copy(wd_hbm.at[te_ref[k], :, pl.ds(col, DOWN_TILE)],
                                      ring.at[slot], sem.at[slot]),)

    def consume(slot):
        wd_bf[...] = ring[slot].astype(BF16)

    _stream_weight_tile(w, first_ref, seq_ref, nt_ref, tile_copies, consume)

    @pl.when(valid_ref[w] == 1)
    def _():
        y = jnp.dot(hid_ref[...], wd_bf[...], preferred_element_type=F32)
        for s in range(DOWN_TILE // LANES):
            y_ref[:, s, :] = y[:, s * LANES:(s + 1) * LANES]

    @pl.when(valid_ref[w] == 0)
    def _():
        y_ref[...] = jnp.zeros_like(y_ref)


def _moe_down(items, hid, w_down):
    n_items = N_MOE_BLOCKS * (D_MODEL // DOWN_TILE)
    tile_rows = DOWN_TILE // LANES
    return pl.pallas_call(
        _moe_down_kernel,
        grid_spec=pltpu.PrefetchScalarGridSpec(
            num_scalar_prefetch=8,
            grid=(n_items,),
            in_specs=[pl.BlockSpec((MOE_ROWS, D_FF), lambda w, b, ot, *_: (b[w], 0)),
                      pl.BlockSpec(memory_space=pl.ANY)],
            out_specs=pl.BlockSpec((MOE_ROWS, tile_rows, LANES),
                                   lambda w, b, ot, *_: (b[w], ot[w], 0)),
            scratch_shapes=[pltpu.VMEM((WEIGHT_RING, D_FF, DOWN_TILE), F32),
                            pltpu.VMEM((D_FF, DOWN_TILE), BF16),
                            pltpu.SemaphoreType.DMA((WEIGHT_RING,))]),
        out_shape=jax.ShapeDtypeStruct((N_MOE_ROWS, Y_ROW_TILES, LANES), F32),
        compiler_params=_params(1),
        name="moe_down",
    )(*items, hid, w_down)


def _combine_kernel(dest_ref, dest_next_ref, h_ref, gate_ref, y_hbm, o_ref, buf, sem):
    i = pl.program_id(0)
    n_assign = COMBINE_ROWS * TOP_K

    def copy(slot, a, src_row):
        dst = buf.at[slot, a % TOP_K, pl.ds((a // TOP_K) * COMBINE_PITCH, Y_ROW_TILES)]
        return pltpu.make_async_copy(y_hbm.at[src_row], dst, sem.at[slot])

    def issue(dests, slot):
        def start(a, carry):
            copy(slot, a, dests[0, 0, a]).start()
            return carry
        lax.fori_loop(0, n_assign, start, 0, unroll=8)

    @pl.when(i == 0)
    def _():
        issue(dest_ref, 0)

    @pl.when(i + 1 < pl.num_programs(0))
    def _():
        issue(dest_next_ref, (i + 1) % 2)

    slot = i % 2

    def wait(a, carry):
        copy(slot, a, 0).wait()
        return carry

    lax.fori_loop(0, n_assign, wait, 0, unroll=8)
    g0 = gate_ref[:, 0:1]
    g1 = gate_ref[:, 1:2]
    for s in range(Y_ROW_TILES):
        cols = slice(s * LANES, (s + 1) * LANES)
        y0 = buf[slot, 0, pl.ds(s, COMBINE_ROWS, stride=COMBINE_PITCH), :]
        y1 = buf[slot, 1, pl.ds(s, COMBINE_ROWS, stride=COMBINE_PITCH), :]
        o_ref[:, cols] = h_ref[:, cols] + (y0 * g0 + y1 * g1)


def _combine(dest, h, gates, y):
    n_tiles = SEQ // COMBINE_ROWS
    per_tile = COMBINE_ROWS * TOP_K
    dests = dest.reshape(n_tiles, 1, per_tile)
    return pl.pallas_call(
        _combine_kernel,
        grid=(n_tiles,),
        in_specs=[pl.BlockSpec((1, 1, per_tile), lambda i: (i, 0, 0), memory_space=pltpu.SMEM),
                  pl.BlockSpec((1, 1, per_tile), lambda i: (jnp.minimum(i + 1, n_tiles - 1), 0, 0),
                               memory_space=pltpu.SMEM),
                  pl.BlockSpec((COMBINE_ROWS, D_MODEL), lambda i: (i, 0)),
                  pl.BlockSpec((COMBINE_ROWS, LANES), lambda i: (i, 0)),
                  pl.BlockSpec(memory_space=pl.ANY)],
        out_specs=pl.BlockSpec((COMBINE_ROWS, D_MODEL), lambda i: (i, 0)),
        out_shape=jax.ShapeDtypeStruct((SEQ, D_MODEL), F32),
        scratch_shapes=[pltpu.VMEM((2, TOP_K, COMBINE_ROWS * COMBINE_PITCH, LANES), F32),
                        pltpu.SemaphoreType.DMA((2,))],
        compiler_params=_params(1),
        name="moe_combine",
    )(dests, dests, h, gates, y)


def _work_items(n_blocks_e, block_start_e, tiles):
    i32 = lambda a: a.astype(jnp.int32)
    n_items = N_MOE_BLOCKS * tiles
    per_e = n_blocks_e * tiles
    ends = jnp.cumsum(per_e)
    total = ends[-1]
    idx = jnp.arange(n_items, dtype=jnp.int32)
    valid = idx < total
    w = jnp.minimum(idx, total - 1)
    e = i32(jnp.searchsorted(ends, w, side='right'))
    local = w - (ends - per_e)[e]
    nb = n_blocks_e[e]
    spare = idx - total
    w_tile = local // nb
    o_tile = jnp.where(valid, w_tile, spare % tiles)
    blk = jnp.where(valid, block_start_e[e] + local % nb, total // tiles + spare // tiles)
    first = valid & (local % nb == 0)
    active = n_blocks_e > 0
    active_rank = jnp.cumsum(active) - active
    seq = active_rank[e] * tiles + w_tile
    active_experts = jnp.argsort(~active, stable=True)
    k = jnp.arange(N_EXPERTS * tiles, dtype=jnp.int32)
    tile_e = active_experts[k // tiles]
    tile_t = k % tiles
    n_tiles = (jnp.sum(active) * tiles).reshape(1)
    return (i32(blk), i32(o_tile), i32(valid), i32(first), i32(seq), i32(tile_e), i32(tile_t),
            i32(n_tiles))


def _dispatch_plan(ids):
    expert = ids[:, :TOP_K].reshape(-1)
    token = jnp.repeat(jnp.arange(SEQ, dtype=jnp.int32), TOP_K)
    onehot = (expert[:, None] == jnp.arange(N_EXPERTS, dtype=jnp.int32)[None, :]).astype(jnp.int32)
    running = jnp.cumsum(onehot, axis=0)
    rank = jnp.take_along_axis(running, expert[:, None], axis=1)[:, 0] - 1
    counts = running[-1]
    n_blocks_e = (counts + MOE_ROWS - 1) // MOE_ROWS
    block_ends = jnp.cumsum(n_blocks_e)
    block_start_e = block_ends - n_blocks_e
    dest = (block_start_e[expert] * MOE_ROWS + rank).astype(jnp.int32)
    row_tok = jnp.zeros((N_MOE_ROWS,), jnp.int32).at[dest].set(token)
    blk = jnp.arange(N_MOE_BLOCKS, dtype=jnp.int32)
    blk_e = jnp.minimum(jnp.searchsorted(block_ends, blk, side='right'), N_EXPERTS - 1)
    n_valid = jnp.clip(counts[blk_e] - (blk - block_start_e[blk_e]) * MOE_ROWS, 0, MOE_ROWS)
    n_valid = jnp.where(blk < block_ends[-1], n_valid, 0).astype(jnp.int32)
    return dest, row_tok, n_valid, n_blocks_e.astype(jnp.int32), block_start_e.astype(jnp.int32)


def kernel(x, norm_mix_w, w_in, q_norm_w, k_norm_w, conv_w, conv_b, conv_ln_w, conv_ln_b, w_out,
           norm_ffn_w, w_group_router, b_group_router, w_expert_router, b_expert_router,
           w_gate, w_up, w_down):
    h = x.reshape(SEQ, D_MODEL)
    slopes = jnp.exp2(-8.0 * jnp.arange(1, N_HEADS + 1, dtype=F32) / N_HEADS)
    for l in range(norm_mix_w.shape[0]):
        hn = _rmsnorm(h, norm_mix_w[l][None, :])
        qk_norm = jnp.concatenate([jnp.tile(q_norm_w[l], N_HEADS), jnp.tile(k_norm_w[l], N_HEADS)])
        z = _inproj(hn, w_in[l], qk_norm[None, :])
        attn = _attention(z, slopes)
        n_ct = CONV_WIDTH // LANES
        conv = _conformer_conv(z, conv_w[l].reshape(CONV_KERNEL, n_ct, LANES),
                               conv_b[l].reshape(n_ct, LANES), conv_ln_w[l][None, :],
                               conv_ln_b[l][None, :])
        h = _outproj(attn, conv, w_out[l], h)

        pad = LANES - N_GROUPS - N_EXPERTS
        w_router = jnp.concatenate([w_group_router[l], w_expert_router[l],
                                    jnp.zeros((D_MODEL, pad), F32)], axis=1)
        b_router = jnp.concatenate([b_group_router[l], b_expert_router[l], jnp.zeros((pad,), F32)])
        hn_words, ids, gates = _router(h, norm_ffn_w[l][None, :], w_router, b_router[None, :])
        dest, row_tok, n_valid, n_blocks_e, block_start_e = _dispatch_plan(ids)
        xs = _gather_rows(n_valid, row_tok, hn_words)
        hid = _moe_up(_work_items(n_blocks_e, block_start_e, D_FF // UP_TILE), xs, w_gate[l], w_up[l])
        y = _moe_down(_work_items(n_blocks_e, block_start_e, D_MODEL // DOWN_TILE), hid, w_down[l])
        h = _combine(dest, h, gates, y)
    return h.reshape(x.shape)
```

```python
import math

import jax
import jax.numpy as jnp
from jax import lax
from jax.experimental import pallas as pl
from jax.experimental.pallas import tpu as pltpu

F32 = jnp.float32
BF16 = jnp.bfloat16

D_MODEL = 4096
SEQ = 8192
HEAD_DIM = 128
N_HEADS = 16
ATTN_WIDTH = N_HEADS * HEAD_DIM
CONV_WIDTH = D_MODEL - ATTN_WIDTH
CONV_KERNEL = 31
IN_COLS = 3 * ATTN_WIDTH + 2 * CONV_WIDTH
DILATIONS = (1, 4, 16)
ATTN_BLOCK = 128
N_KEYS = 128
N_GROUPS = 8
EXPERTS_PER_GROUP = 8
N_EXPERTS = N_GROUPS * EXPERTS_PER_GROUP
TOP_K = 2
D_FF = 1024
EPS = 1e-6
NEG_INF = -1e30

LANES = 128
VMEM_LIMIT = 56 * 1024 * 1024

SPAN = DILATIONS[-1] * ATTN_BLOCK

MOE_ROWS = 256
N_ASSIGN = SEQ * TOP_K
N_MOE_ROWS = N_ASSIGN + N_EXPERTS * MOE_ROWS
N_MOE_BLOCKS = N_MOE_ROWS // MOE_ROWS
UP_TILE = 512
DOWN_TILE = 2048
WEIGHT_RING = 2
ROUTER_ROWS = 256
COMBINE_ROWS = 128
X_ROW_TILES = D_MODEL // 2 // LANES
Y_ROW_TILES = D_MODEL // 2 // LANES
COMBINE_PITCH = Y_ROW_TILES + 4
GATHER_PITCH = X_ROW_TILES + 8
DMA_UNROLL = 8
WAIT_ROWS = 32


def _params(n_axes):
    return pltpu.CompilerParams(dimension_semantics=("arbitrary",) * n_axes,
                                vmem_limit_bytes=VMEM_LIMIT)


def _pack_rows(x):
    half = x.shape[1] // 2
    lo = pltpu.bitcast(x[:, :half].astype(BF16).astype(F32), jnp.uint32)
    hi = pltpu.bitcast(x[:, half:].astype(BF16).astype(F32), jnp.uint32)
    return (hi & jnp.uint32(0xFFFF0000)) | (lo >> 16)


def _unpack_words(words):
    return (pltpu.bitcast(words << 16, F32),
            pltpu.bitcast(words & jnp.uint32(0xFFFF0000), F32))


def _rmsnorm_kernel(x_ref, w_ref, o_ref):
    x = x_ref[...]
    ms = jnp.mean(x * x, axis=-1, keepdims=True)
    o_ref[...] = (x * lax.rsqrt(ms + EPS) * w_ref[...]).astype(o_ref.dtype)


def _rmsnorm(x, w, rows=256):
    n, d = x.shape
    return pl.pallas_call(
        _rmsnorm_kernel,
        grid=(n // rows,),
        in_specs=[pl.BlockSpec((rows, d), lambda i: (i, 0)),
                  pl.BlockSpec((1, d), lambda i: (0, 0))],
        out_specs=pl.BlockSpec((rows, d), lambda i: (i, 0)),
        out_shape=jax.ShapeDtypeStruct((n, d), BF16),
        compiler_params=_params(1),
        name="rmsnorm",
    )(x, w)


IN_TM = 1024
IN_TN = 512
IN_CHUNK = 256
QK_TILES = 2 * ATTN_WIDTH // IN_TN


def _inproj_kernel(hn_ref, w_ref, nw_ref, o_ref, w_bf):
    j = pl.program_id(0)

    @pl.when(pl.program_id(1) == 0)
    def _():
        w_bf[...] = w_ref[...].astype(BF16)

    is_qk = j < QK_TILES
    for c in range(IN_TM // IN_CHUNK):
        rows = slice(c * IN_CHUNK, (c + 1) * IN_CHUNK)
        acc = jnp.dot(hn_ref[rows, :], w_bf[...], preferred_element_type=F32)
        for hd in range(IN_TN // HEAD_DIM):
            sl = slice(hd * HEAD_DIM, (hd + 1) * HEAD_DIM)
            t = acc[:, sl]
            ms = jnp.mean(t * t, axis=-1, keepdims=True)
            normed = t * lax.rsqrt(ms + EPS) * nw_ref[:, sl]
            o_ref[rows, sl] = jnp.where(is_qk, normed, t).astype(o_ref.dtype)


def _inproj(hn, w, qk_norm_w):
    n = hn.shape[0]
    return pl.pallas_call(
        _inproj_kernel,
        grid=(IN_COLS // IN_TN, n // IN_TM),
        in_specs=[pl.BlockSpec((IN_TM, D_MODEL), lambda j, i: (i, 0)),
                  pl.BlockSpec((D_MODEL, IN_TN), lambda j, i: (0, j)),
                  pl.BlockSpec((1, IN_TN), lambda j, i: (0, jnp.minimum(j, QK_TILES - 1)))],
        out_specs=pl.BlockSpec((IN_TM, IN_TN), lambda j, i: (i, j)),
        out_shape=jax.ShapeDtypeStruct((n, IN_COLS), BF16),
        scratch_shapes=[pltpu.VMEM((D_MODEL, IN_TN), BF16)],
        compiler_params=_params(2),
        name="inproj",
    )(hn, w, qk_norm_w)


def _attn_kernel(slopes_ref, q_ref, kc_ref, kp_ref, vc_ref, vp_ref, o_ref, qf, kf, vf, ob, lb):
    h = pl.program_id(0)
    first_span = pl.program_id(1) == 0
    slope = slopes_ref[h]

    qf[...] = q_ref[...].astype(F32)
    kf[0:SPAN, :] = kp_ref[...].astype(F32)
    kf[SPAN:2 * SPAN, :] = kc_ref[...].astype(F32)
    vf[0:SPAN, :] = vp_ref[...].astype(F32)
    vf[SPAN:2 * SPAN, :] = vc_ref[...].astype(F32)

    qi = lax.broadcasted_iota(jnp.int32, (ATTN_BLOCK, 2 * ATTN_BLOCK), 0)
    kj = lax.broadcasted_iota(jnp.int32, (ATTN_BLOCK, 2 * ATTN_BLOCK), 1)
    steps = qi + ATTN_BLOCK - kj
    valid = (steps >= 0) & (steps <= N_KEYS)
    valid_first = valid & (kj >= jnp.where(first_span, ATTN_BLOCK, 0))
    stepsf = steps.astype(F32)
    scale = 1.0 / math.sqrt(HEAD_DIM)

    for b, d in enumerate(DILATIONS):
        bias = (-slope * float(d)) * stepsf
        for r in range(d):
            for n in range(SPAN // (ATTN_BLOCK * d)):
                q0 = r + d * ATTN_BLOCK * n
                k0 = SPAN + r + d * ATTN_BLOCK * (n - 1)
                q = qf[pl.ds(q0, ATTN_BLOCK, stride=d), :].astype(BF16)
                k = kf[pl.ds(k0, 2 * ATTN_BLOCK, stride=d), :].astype(BF16)
                v = vf[pl.ds(k0, 2 * ATTN_BLOCK, stride=d), :].astype(BF16)
                s = lax.dot_general(q, k, (((1,), (1,)), ((), ())),
                                    preferred_element_type=F32) * scale
                s = jnp.where(valid_first if n == 0 else valid, s + bias, NEG_INF)
                m = jnp.max(s, axis=-1, keepdims=True)
                p = jnp.exp(s - m)
                den = jnp.sum(p, axis=-1, keepdims=True)
                o = jnp.dot(p.astype(BF16), v, preferred_element_type=F32) / den
                lse = m + jnp.log(den)
                ob[b, pl.ds(q0, ATTN_BLOCK, stride=d), :] = o
                lb[b, pl.ds(q0, ATTN_BLOCK, stride=d), :] = jnp.broadcast_to(
                    lse, (ATTN_BLOCK, HEAD_DIM))

    chunk = 256
    for c in range(SPAN // chunk):
        rows = slice(c * chunk, (c + 1) * chunk)
        l0, l1, l2 = lb[0, rows, :], lb[1, rows, :], lb[2, rows, :]
        mx = jnp.maximum(jnp.maximum(l0, l1), l2)
        w0, w1, w2 = jnp.exp(l0 - mx), jnp.exp(l1 - mx), jnp.exp(l2 - mx)
        num = w0 * ob[0, rows, :] + w1 * ob[1, rows, :] + w2 * ob[2, rows, :]
        o_ref[rows, :] = (num / (w0 + w1 + w2)).astype(o_ref.dtype)


def _attention(z, slopes):
    n_spans = SEQ // SPAN
    k_col, v_col = N_HEADS, 2 * N_HEADS
    blk = (SPAN, HEAD_DIM)
    prev = lambda s: jnp.maximum(s - 1, 0)
    return pl.pallas_call(
        _attn_kernel,
        grid_spec=pltpu.PrefetchScalarGridSpec(
            num_scalar_prefetch=1,
            grid=(N_HEADS, n_spans),
            in_specs=[pl.BlockSpec(blk, lambda h, s, sl: (s, h)),
                      pl.BlockSpec(blk, lambda h, s, sl: (s, k_col + h)),
                      pl.BlockSpec(blk, lambda h, s, sl: (prev(s), k_col + h)),
                      pl.BlockSpec(blk, lambda h, s, sl: (s, v_col + h)),
                      pl.BlockSpec(blk, lambda h, s, sl: (prev(s), v_col + h))],
            out_specs=pl.BlockSpec(blk, lambda h, s, sl: (s, h)),
            scratch_shapes=[pltpu.VMEM((SPAN, HEAD_DIM), F32),
                            pltpu.VMEM((2 * SPAN, HEAD_DIM), F32),
                            pltpu.VMEM((2 * SPAN, HEAD_DIM), F32),
                            pltpu.VMEM((3, SPAN, HEAD_DIM), F32),
                            pltpu.VMEM((3, SPAN, HEAD_DIM), F32)]),
        out_shape=jax.ShapeDtypeStruct((SEQ, ATTN_WIDTH), BF16),
        compiler_params=_params(2),
        name="dilated_attention",
    )(slopes, z, z, z, z, z)


CONV_ROWS = 256
CONV_HALO = 32
CONV_CHUNK = 64


SUBLANES = 8
CONV_EXT = CONV_HALO + CONV_ROWS
CONV_SHIFTED = CONV_EXT - SUBLANES


def _conv_kernel(a_ref, g_ref, ah_ref, gh_ref, cw_ref, cb_ref, lw_ref, lb_ref, o_ref,
                 u_ext, shifted, c_scr):
    i = pl.program_id(0)
    n_ct = CONV_WIDTH // LANES
    for ct in range(n_ct):
        cols = slice(ct * LANES, (ct + 1) * LANES)
        u_ext[ct, CONV_HALO:, :] = (a_ref[:, cols].astype(F32)
                                    * jax.nn.sigmoid(g_ref[:, cols].astype(F32)))
        uh = ah_ref[:, cols].astype(F32) * jax.nn.sigmoid(gh_ref[:, cols].astype(F32))
        u_ext[ct, 0:CONV_HALO, :] = jnp.where(i == 0, 0.0, uh)

    first_tap = CONV_HALO - (CONV_KERNEL - 1)

    def slab(ct, carry):
        for k in range(1, SUBLANES):
            shifted[k - 1] = u_ext[ct, k:k + CONV_SHIFTED, :]
        for rc in range(CONV_ROWS // CONV_CHUNK):
            base = rc * CONV_CHUNK
            acc = jnp.broadcast_to(cb_ref[pl.ds(ct, 1), :], (CONV_CHUNK, LANES))
            for j in range(CONV_KERNEL):
                k = (first_tap + j) % SUBLANES
                row = base + (first_tap + j) - k
                if k == 0:
                    src = u_ext[ct, row:row + CONV_CHUNK, :]
                else:
                    src = shifted[k - 1, row:row + CONV_CHUNK, :]
                acc = acc + cw_ref[j, pl.ds(ct, 1), :] * src
            c_scr[ct, base:base + CONV_CHUNK, :] = acc
        return carry

    lax.fori_loop(0, n_ct, slab, 0)

    total = c_scr[0]
    for ct in range(1, n_ct):
        total = total + c_scr[ct]
    mu = jnp.sum(total, axis=-1, keepdims=True) * (1.0 / CONV_WIDTH)
    sq = jnp.square(c_scr[0] - mu)
    for ct in range(1, n_ct):
        sq = sq + jnp.square(c_scr[ct] - mu)
    var = jnp.sum(sq, axis=-1, keepdims=True) * (1.0 / CONV_WIDTH)
    rstd = lax.rsqrt(var + EPS)
    for ct in range(n_ct):
        cols = slice(ct * LANES, (ct + 1) * LANES)
        y = (c_scr[ct] - mu) * rstd * lw_ref[:, cols] + lb_ref[:, cols]
        o_ref[:, cols] = (y * jax.nn.sigmoid(y)).astype(o_ref.dtype)


def _conformer_conv(z, conv_w, conv_b, ln_w, ln_b):
    a_col = 3 * ATTN_WIDTH // CONV_WIDTH
    g_col = a_col + 1
    n_ct = CONV_WIDTH // LANES
    halo_blocks = CONV_ROWS // CONV_HALO
    halo = lambda i: jnp.maximum(i * halo_blocks - 1, 0)
    vec = pl.BlockSpec((1, CONV_WIDTH), lambda i: (0, 0))
    return pl.pallas_call(
        _conv_kernel,
        grid=(SEQ // CONV_ROWS,),
        in_specs=[pl.BlockSpec((CONV_ROWS, CONV_WIDTH), lambda i: (i, a_col)),
                  pl.BlockSpec((CONV_ROWS, CONV_WIDTH), lambda i: (i, g_col)),
                  pl.BlockSpec((CONV_HALO, CONV_WIDTH), lambda i: (halo(i), a_col)),
                  pl.BlockSpec((CONV_HALO, CONV_WIDTH), lambda i: (halo(i), g_col)),
                  pl.BlockSpec((CONV_KERNEL, n_ct, LANES), lambda i: (0, 0, 0)),
                  pl.BlockSpec((n_ct, LANES), lambda i: (0, 0)),
                  vec, vec],
        out_specs=pl.BlockSpec((CONV_ROWS, CONV_WIDTH), lambda i: (i, 0)),
        out_shape=jax.ShapeDtypeStruct((SEQ, CONV_WIDTH), BF16),
        scratch_shapes=[pltpu.VMEM((n_ct, CONV_EXT, LANES), F32),
                        pltpu.VMEM((SUBLANES - 1, CONV_SHIFTED, LANES), F32),
                        pltpu.VMEM((n_ct, CONV_ROWS, LANES), F32)],
        compiler_params=_params(1),
        name="conformer_conv",
    )(z, z, z, z, conv_w, conv_b, ln_w, ln_b)


OUT_TM = 1024
OUT_TN = 512


def _outproj_kernel(attn_ref, conv_ref, wa_ref, wc_ref, x_ref, o_ref, wa_bf, wc_bf):
    @pl.when(pl.program_id(1) == 0)
    def _():
        wa_bf[...] = wa_ref[...].astype(BF16)
        wc_bf[...] = wc_ref[...].astype(BF16)

    acc = jnp.dot(attn_ref[...], wa_bf[...], preferred_element_type=F32)
    acc = acc + jnp.dot(conv_ref[...], wc_bf[...], preferred_element_type=F32)
    o_ref[...] = x_ref[...] + acc


def _outproj(attn, conv, w, x):
    n = x.shape[0]
    conv_blk = ATTN_WIDTH // CONV_WIDTH
    return pl.pallas_call(
        _outproj_kernel,
        grid=(D_MODEL // OUT_TN, n // OUT_TM),
        in_specs=[pl.BlockSpec((OUT_TM, ATTN_WIDTH), lambda j, i: (i, 0)),
                  pl.BlockSpec((OUT_TM, CONV_WIDTH), lambda j, i: (i, 0)),
                  pl.BlockSpec((ATTN_WIDTH, OUT_TN), lambda j, i: (0, j)),
                  pl.BlockSpec((CONV_WIDTH, OUT_TN), lambda j, i: (conv_blk, j)),
                  pl.BlockSpec((OUT_TM, OUT_TN), lambda j, i: (i, j))],
        out_specs=pl.BlockSpec((OUT_TM, OUT_TN), lambda j, i: (i, j)),
        out_shape=jax.ShapeDtypeStruct((n, D_MODEL), F32),
        scratch_shapes=[pltpu.VMEM((ATTN_WIDTH, OUT_TN), BF16),
                        pltpu.VMEM((CONV_WIDTH, OUT_TN), BF16)],
        compiler_params=_params(2),
        name="outproj",
    )(attn, conv, w, w, x)


def _router_kernel(h_ref, nw_ref, wr_ref, br_ref, hn_ref, ids_ref, gates_ref):
    x = h_ref[...]
    ms = jnp.mean(x * x, axis=-1, keepdims=True)
    hn = x * lax.rsqrt(ms + EPS) * nw_ref[...]
    words = _pack_rows(hn)
    for s in range(X_ROW_TILES):
        hn_ref[:, s, :] = words[:, s * LANES:(s + 1) * LANES]

    h_hi = hn.astype(BF16)
    h_lo = (hn - h_hi.astype(F32)).astype(BF16)
    hi_both = jnp.dot(h_hi, wr_ref[...], preferred_element_type=F32)
    lo_hi = jnp.dot(h_lo, wr_ref[:, :LANES], preferred_element_type=F32)
    logits = hi_both[:, :LANES] + (hi_both[:, LANES:] + lo_hi) + br_ref[...]
    lane = lax.broadcasted_iota(jnp.int32, logits.shape, 1)
    big = jnp.int32(1 << 20)

    is_g = lane < N_GROUPS
    gl = jnp.where(is_g, logits, -jnp.inf)
    ge = jnp.exp(gl - jnp.max(gl, axis=-1, keepdims=True))
    g_prob = ge / jnp.sum(ge, axis=-1, keepdims=True)
    g_w = jnp.max(g_prob, axis=-1, keepdims=True)
    g_top = jnp.min(jnp.where(is_g & (g_prob == g_w), lane, big), axis=-1, keepdims=True)

    eidx = lane - N_GROUPS
    grp_shift = EXPERTS_PER_GROUP.bit_length() - 1
    in_grp = (eidx >= 0) & (eidx < N_EXPERTS) & ((eidx >> grp_shift) == g_top)
    el = jnp.where(in_grp, logits, -jnp.inf)
    ee = jnp.exp(el - jnp.max(el, axis=-1, keepdims=True))
    e_prob = ee / jnp.sum(ee, axis=-1, keepdims=True)
    v1 = jnp.max(jnp.where(in_grp, e_prob, -1.0), axis=-1, keepdims=True)
    i1 = jnp.min(jnp.where(in_grp & (e_prob == v1), lane, big), axis=-1, keepdims=True)
    rest = in_grp & (lane != i1)
    v2 = jnp.max(jnp.where(rest, e_prob, -1.0), axis=-1, keepdims=True)
    i2 = jnp.min(jnp.where(rest & (e_prob == v2), lane, big), axis=-1, keepdims=True)
    tot = v1 + v2
    ids_ref[...] = jnp.where(lane == 0, i1 - N_GROUPS, jnp.where(lane == 1, i2 - N_GROUPS, 0))
    gates_ref[...] = jnp.where(lane == 0, g_w * v1 / tot,
                               jnp.where(lane == 1, g_w * v2 / tot, 0.0))


def _router(h, norm_w, w_router, b_router):
    return pl.pallas_call(
        _router_kernel,
        grid=(SEQ // ROUTER_ROWS,),
        in_specs=[pl.BlockSpec((ROUTER_ROWS, D_MODEL), lambda i: (i, 0)),
                  pl.BlockSpec((1, D_MODEL), lambda i: (0, 0)),
                  pl.BlockSpec((D_MODEL, 2 * LANES), lambda i: (0, 0)),
                  pl.BlockSpec((1, LANES), lambda i: (0, 0))],
        out_specs=[pl.BlockSpec((ROUTER_ROWS, X_ROW_TILES, LANES), lambda i: (i, 0, 0)),
                   pl.BlockSpec((ROUTER_ROWS, LANES), lambda i: (i, 0)),
                   pl.BlockSpec((ROUTER_ROWS, LANES), lambda i: (i, 0))],
        out_shape=[jax.ShapeDtypeStruct((SEQ, X_ROW_TILES, LANES), jnp.uint32),
                   jax.ShapeDtypeStruct((SEQ, LANES), jnp.int32),
                   jax.ShapeDtypeStruct((SEQ, LANES), F32)],
        compiler_params=_params(1),
        name="ffn_norm_router",
    )(h, norm_w, w_router, b_router)


def _gather_kernel(nv_ref, tok_ref, tok_next_ref, hn_hbm, o_ref, stage, sem):
    i = pl.program_id(0)
    n_blocks = pl.num_programs(0)

    def token_rows(slot, r):
        return stage.at[slot, pl.ds(pl.multiple_of(r * GATHER_PITCH, SUBLANES), X_ROW_TILES)]

    def issue(block, toks, slot):
        nv = nv_ref[block]

        def start(r, carry):
            pltpu.make_async_copy(hn_hbm.at[toks[0, 0, r]], token_rows(slot, r), sem.at[slot]).start()
            return carry

        def start_group(g, carry):
            for u in range(DMA_UNROLL):
                start(g * DMA_UNROLL + u, carry)
            return carry

        def zero(r, carry):
            token_rows(slot, r)[...] = jnp.zeros((X_ROW_TILES, LANES), stage.dtype)
            return carry

        groups = nv // DMA_UNROLL
        lax.fori_loop(0, groups, start_group, 0)
        lax.fori_loop(groups * DMA_UNROLL, nv, start, 0)
        lax.fori_loop(nv, MOE_ROWS, zero, 0)

    @pl.when(i == 0)
    def _():
        issue(0, tok_ref, 0)

    @pl.when(i + 1 < n_blocks)
    def _():
        issue(i + 1, tok_next_ref, (i + 1) % 2)

    slot = i % 2

    def wait_rows(n_tokens):
        span = stage.at[slot, pl.ds(0, n_tokens * X_ROW_TILES)]
        pltpu.make_async_copy(span, span, sem.at[slot]).wait()

    def wait_many(c, carry):
        wait_rows(WAIT_ROWS)
        return carry

    def wait_one(c, carry):
        wait_rows(1)
        return carry

    nv = nv_ref[i]
    lax.fori_loop(0, nv // WAIT_ROWS, wait_many, 0)
    lax.fori_loop(0, nv % WAIT_ROWS, wait_one, 0)
    for s in range(X_ROW_TILES):
        o_ref[:, s * LANES:(s + 1) * LANES] = stage[slot, pl.ds(s, MOE_ROWS, stride=GATHER_PITCH), :]


def _gather_rows(n_valid, row_tok, hn_words):
    toks = row_tok.reshape(N_MOE_BLOCKS, 1, MOE_ROWS)
    nxt = lambda i, nv: (jnp.minimum(i + 1, N_MOE_BLOCKS - 1), 0, 0)
    return pl.pallas_call(
        _gather_kernel,
        grid_spec=pltpu.PrefetchScalarGridSpec(
            num_scalar_prefetch=1,
            grid=(N_MOE_BLOCKS,),
            in_specs=[pl.BlockSpec((1, 1, MOE_ROWS), lambda i, nv: (i, 0, 0),
                                   memory_space=pltpu.SMEM),
                      pl.BlockSpec((1, 1, MOE_ROWS), nxt, memory_space=pltpu.SMEM),
                      pl.BlockSpec(memory_space=pl.ANY)],
            out_specs=pl.BlockSpec((MOE_ROWS, D_MODEL // 2), lambda i, nv: (i, 0)),
            scratch_shapes=[pltpu.VMEM((2, MOE_ROWS * GATHER_PITCH, LANES), jnp.uint32),
                            pltpu.SemaphoreType.DMA((2,))]),
        out_shape=jax.ShapeDtypeStruct((N_MOE_ROWS, D_MODEL // 2), jnp.uint32),
        compiler_params=_params(1),
        name="moe_gather",
    )(n_valid, toks, toks, hn_words)


def _unpack_rows(words):
    lo, hi = _unpack_words(words)
    return jnp.concatenate([lo.astype(BF16), hi.astype(BF16)], axis=-1)


def _stream_weight_tile(w, first_ref, seq_ref, n_tiles_ref, tile_copies, consume):
    n_tiles = n_tiles_ref[0]

    @pl.when(w == 0)
    def _():
        for k in range(WEIGHT_RING):
            @pl.when(k < n_tiles)
            def _():
                for c in tile_copies(k, k):
                    c.start()

    @pl.when(first_ref[w] == 1)
    def _():
        k = seq_ref[w]
        slot = k % WEIGHT_RING
        for c in tile_copies(k, slot):
            c.wait()
        consume(slot)

        @pl.when(k + WEIGHT_RING < n_tiles)
        def _():
            for c in tile_copies(k + WEIGHT_RING, slot):
                c.start()


def _moe_up_kernel(ib_ref, b_ref, ot_ref, valid_ref, first_ref, seq_ref, te_ref, tt_ref, nt_ref,
                   xs_ref, wg_hbm, wu_hbm, hid_ref, ring_g, ring_u, wg_bf, wu_bf, sem):
    w = pl.program_id(0)

    def tile_copies(k, slot):
        e = te_ref[k]
        col = pl.multiple_of(tt_ref[k] * UP_TILE, UP_TILE)
        return (pltpu.make_async_copy(wg_hbm.at[e, :, pl.ds(col, UP_TILE)], ring_g.at[slot],
                                      sem.at[0, slot]),
                pltpu.make_async_copy(wu_hbm.at[e, :, pl.ds(col, UP_TILE)], ring_u.at[slot],
                                      sem.at[1, slot]))

    def consume(slot):
        wg_bf[...] = ring_g[slot].astype(BF16)
        wu_bf[...] = ring_u[slot].astype(BF16)

    _stream_weight_tile(w, first_ref, seq_ref, nt_ref, tile_copies, consume)

    @pl.when(valid_ref[w] == 1)
    def _():
        x = _unpack_rows(xs_ref[...])
        g = jnp.dot(x, wg_bf[...], preferred_element_type=F32)
        u = jnp.dot(x, wu_bf[...], preferred_element_type=F32)
        hid_ref[...] = (g * jax.nn.sigmoid(g) * u).astype(hid_ref.dtype)

    @pl.when(valid_ref[w] == 0)
    def _():
        hid_ref[...] = jnp.zeros_like(hid_ref)


def _moe_up(items, xs, w_gate, w_up):
    n_items = N_MOE_BLOCKS * (D_FF // UP_TILE)
    any_spec = pl.BlockSpec(memory_space=pl.ANY)
    return pl.pallas_call(
        _moe_up_kernel,
        grid_spec=pltpu.PrefetchScalarGridSpec(
            num_scalar_prefetch=9,
            grid=(n_items,),
            in_specs=[pl.BlockSpec((MOE_ROWS, D_MODEL // 2), lambda w, ib, *_: (ib[w], 0)),
                      any_spec, any_spec],
            out_specs=pl.BlockSpec((MOE_ROWS, UP_TILE), lambda w, ib, b, ot, *_: (b[w], ot[w])),
            scratch_shapes=[pltpu.VMEM((WEIGHT_RING, D_MODEL, UP_TILE), F32),
                            pltpu.VMEM((WEIGHT_RING, D_MODEL, UP_TILE), F32),
                            pltpu.VMEM((D_MODEL, UP_TILE), BF16),
                            pltpu.VMEM((D_MODEL, UP_TILE), BF16),
                            pltpu.SemaphoreType.DMA((2, WEIGHT_RING))]),
        out_shape=jax.ShapeDtypeStruct((N_MOE_ROWS, D_FF), BF16),
        compiler_params=_params(1),
        name="moe_gate_up",
    )(*items, xs, w_gate, w_up)


def _moe_down_kernel(ib_ref, b_ref, ot_ref, valid_ref, first_ref, seq_ref, te_ref, tt_ref, nt_ref,
                     hid_ref, wd_hbm, y_ref, ring, wd_bf, sem):
    w = pl.program_id(0)

    def tile_copies(k, slot):
        col = pl.multiple_of(tt_ref[k] * DOWN_TILE, DOWN_TILE)
        return (pltpu.make_async_copy(wd_hbm.at[te_ref[k], :, pl.ds(col, DOWN_TILE)],
                                      ring.at[slot], sem.at[slot]),)

    def consume(slot):
        wd_bf[...] = ring[slot].astype(BF16)

    _stream_weight_tile(w, first_ref, seq_ref, nt_ref, tile_copies, consume)

    @pl.when(valid_ref[w] == 1)
    def _():
        y = jnp.dot(hid_ref[...], wd_bf[...], preferred_element_type=F32)
        words = _pack_rows(y)
        for s in range(DOWN_TILE // 2 // LANES):
            y_ref[:, s, :] = words[:, s * LANES:(s + 1) * LANES]

    @pl.when(valid_ref[w] == 0)
    def _():
        y_ref[...] = jnp.zeros_like(y_ref)


def _moe_down(items, hid, w_down):
    n_items = N_MOE_BLOCKS * (D_MODEL // DOWN_TILE)
    tile_rows = DOWN_TILE // 2 // LANES
    return pl.pallas_call(
        _moe_down_kernel,
        grid_spec=pltpu.PrefetchScalarGridSpec(
            num_scalar_prefetch=9,
            grid=(n_items,),
            in_specs=[pl.BlockSpec((MOE_ROWS, D_FF), lambda w, ib, *_: (ib[w], 0)),
                      pl.BlockSpec(memory_space=pl.ANY)],
            out_specs=pl.BlockSpec((MOE_ROWS, tile_rows, LANES),
                                   lambda w, ib, b, ot, *_: (b[w], ot[w], 0)),
            scratch_shapes=[pltpu.VMEM((WEIGHT_RING, D_FF, DOWN_TILE), F32),
                            pltpu.VMEM((D_FF, DOWN_TILE), BF16),
                            pltpu.SemaphoreType.DMA((WEIGHT_RING,))]),
        out_shape=jax.ShapeDtypeStruct((N_MOE_ROWS, Y_ROW_TILES, LANES), jnp.uint32),
        compiler_params=_params(1),
        name="moe_down",
    )(*items, hid, w_down)


def _combine_kernel(dest_ref, dest_next_ref, h_ref, gate_ref, y_hbm, o_ref, buf, sem):
    i = pl.program_id(0)
    n_assign = COMBINE_ROWS * TOP_K

    def copy(slot, a, src_row):
        dst = buf.at[slot, a % TOP_K, pl.ds((a // TOP_K) * COMBINE_PITCH, Y_ROW_TILES)]
        return pltpu.make_async_copy(y_hbm.at[src_row], dst, sem.at[slot])

    def issue(dests, slot):
        def start(a, carry):
            copy(slot, a, dests[0, 0, a]).start()
            return carry
        lax.fori_loop(0, n_assign, start, 0, unroll=8)

    @pl.when(i == 0)
    def _():
        issue(dest_ref, 0)

    @pl.when(i + 1 < pl.num_programs(0))
    def _():
        issue(dest_next_ref, (i + 1) % 2)

    slot = i % 2

    for k in range(TOP_K):
        for c in range(COMBINE_ROWS // WAIT_ROWS):
            span = buf.at[slot, k, pl.ds(c * WAIT_ROWS * Y_ROW_TILES, WAIT_ROWS * Y_ROW_TILES)]
            pltpu.make_async_copy(span, span, sem.at[slot]).wait()

    g0 = gate_ref[:, 0:1]
    g1 = gate_ref[:, 1:2]
    tiles_per_half = DOWN_TILE // 2 // LANES
    for s in range(Y_ROW_TILES):
        lo_col = (s // tiles_per_half) * DOWN_TILE + (s % tiles_per_half) * LANES
        lo0, hi0 = _unpack_words(buf[slot, 0, pl.ds(s, COMBINE_ROWS, stride=COMBINE_PITCH), :])
        lo1, hi1 = _unpack_words(buf[slot, 1, pl.ds(s, COMBINE_ROWS, stride=COMBINE_PITCH), :])
        for col, y0, y1 in ((lo_col, lo0, lo1), (lo_col + DOWN_TILE // 2, hi0, hi1)):
            cols = slice(col, col + LANES)
            o_ref[:, cols] = h_ref[:, cols] + (y0 * g0 + y1 * g1)


def _combine(dest, h, gates, y):
    n_tiles = SEQ // COMBINE_ROWS
    per_tile = COMBINE_ROWS * TOP_K
    dests = dest.reshape(n_tiles, 1, per_tile)
    return pl.pallas_call(
        _combine_kernel,
        grid=(n_tiles,),
        in_specs=[pl.BlockSpec((1, 1, per_tile), lambda i: (i, 0, 0), memory_space=pltpu.SMEM),
                  pl.BlockSpec((1, 1, per_tile), lambda i: (jnp.minimum(i + 1, n_tiles - 1), 0, 0),
                               memory_space=pltpu.SMEM),
                  pl.BlockSpec((COMBINE_ROWS, D_MODEL), lambda i: (i, 0)),
                  pl.BlockSpec((COMBINE_ROWS, LANES), lambda i: (i, 0)),
                  pl.BlockSpec(memory_space=pl.ANY)],
        out_specs=pl.BlockSpec((COMBINE_ROWS, D_MODEL), lambda i: (i, 0)),
        out_shape=jax.ShapeDtypeStruct((SEQ, D_MODEL), F32),
        scratch_shapes=[pltpu.VMEM((2, TOP_K, COMBINE_ROWS * COMBINE_PITCH, LANES), jnp.uint32),
                        pltpu.SemaphoreType.DMA((2,))],
        compiler_params=_params(1),
        name="moe_combine",
    )(dests, dests, h, gates, y)


def _work_items(n_blocks_e, block_start_e, tiles):
    i32 = lambda a: a.astype(jnp.int32)
    n_items = N_MOE_BLOCKS * tiles
    per_e = n_blocks_e * tiles
    ends = jnp.cumsum(per_e)
    total = ends[-1]
    idx = jnp.arange(n_items, dtype=jnp.int32)
    valid = idx < total
    w = jnp.minimum(idx, total - 1)
    e = i32(jnp.searchsorted(ends, w, side='right'))
    local = w - (ends - per_e)[e]
    nb = n_blocks_e[e]
    spare = idx - total
    w_tile = local // nb
    o_tile = jnp.where(valid, w_tile, spare % tiles)
    blk = jnp.where(valid, block_start_e[e] + local % nb, total // tiles + spare // tiles)
    first = valid & (local % nb == 0)
    active = n_blocks_e > 0
    active_rank = jnp.cumsum(active) - active
    seq = active_rank[e] * tiles + w_tile
    active_experts = jnp.argsort(~active, stable=True)
    k = jnp.arange(N_EXPERTS * tiles, dtype=jnp.int32)
    tile_e = active_experts[k // tiles]
    tile_t = k % tiles
    n_tiles = (jnp.sum(active) * tiles).reshape(1)
    in_blk = block_start_e[e] + local % nb
    return (i32(in_blk), i32(blk), i32(o_tile), i32(valid), i32(first), i32(seq), i32(tile_e),
            i32(tile_t), i32(n_tiles))


def _dispatch_plan(ids):
    expert = ids[:, :TOP_K].reshape(-1)
    token = jnp.repeat(jnp.arange(SEQ, dtype=jnp.int32), TOP_K)
    onehot = (expert[:, None] == jnp.arange(N_EXPERTS, dtype=jnp.int32)[None, :]).astype(jnp.int32)
    running = jnp.cumsum(onehot, axis=0)
    rank = jnp.take_along_axis(running, expert[:, None], axis=1)[:, 0] - 1
    counts = running[-1]
    n_blocks_e = (counts + MOE_ROWS - 1) // MOE_ROWS
    block_ends = jnp.cumsum(n_blocks_e)
    block_start_e = block_ends - n_blocks_e
    dest = (block_start_e[expert] * MOE_ROWS + rank).astype(jnp.int32)
    row_tok = jnp.zeros((N_MOE_ROWS,), jnp.int32).at[dest].set(token)
    blk = jnp.arange(N_MOE_BLOCKS, dtype=jnp.int32)
    blk_e = jnp.minimum(jnp.searchsorted(block_ends, blk, side='right'), N_EXPERTS - 1)
    n_valid = jnp.clip(counts[blk_e] - (blk - block_start_e[blk_e]) * MOE_ROWS, 0, MOE_ROWS)
    n_valid = jnp.where(blk < block_ends[-1], n_valid, 0).astype(jnp.int32)
    return dest, row_tok, n_valid, n_blocks_e.astype(jnp.int32), block_start_e.astype(jnp.int32)


def kernel(x, norm_mix_w, w_in, q_norm_w, k_norm_w, conv_w, conv_b, conv_ln_w, conv_ln_b, w_out,
           norm_ffn_w, w_group_router, b_group_router, w_expert_router, b_expert_router,
           w_gate, w_up, w_down):
    h = x.reshape(SEQ, D_MODEL)
    slopes = jnp.exp2(-8.0 * jnp.arange(1, N_HEADS + 1, dtype=F32) / N_HEADS)
    for l in range(norm_mix_w.shape[0]):
        hn = _rmsnorm(h, norm_mix_w[l][None, :])
        qk_norm = jnp.concatenate([jnp.tile(q_norm_w[l], N_HEADS), jnp.tile(k_norm_w[l], N_HEADS)])
        z = _inproj(hn, w_in[l], qk_norm[None, :])
        attn = _attention(z, slopes)
        n_ct = CONV_WIDTH // LANES
        conv = _conformer_conv(z, conv_w[l].reshape(CONV_KERNEL, n_ct, LANES),
                               conv_b[l].reshape(n_ct, LANES), conv_ln_w[l][None, :],
                               conv_ln_b[l][None, :])
        h = _outproj(attn, conv, w_out[l], h)

        pad = LANES - N_GROUPS - N_EXPERTS
        w_router = jnp.concatenate([w_group_router[l], w_expert_router[l],
                                    jnp.zeros((D_MODEL, pad), F32)], axis=1)
        b_router = jnp.concatenate([b_group_router[l], b_expert_router[l], jnp.zeros((pad,), F32)])
        w_router_hi = w_router.astype(BF16)
        w_router_lo = (w_router - w_router_hi.astype(F32)).astype(BF16)
        hn_words, ids, gates = _router(h, norm_ffn_w[l][None, :],
                                       jnp.concatenate([w_router_hi, w_router_lo], axis=1),
                                       b_router[None, :])
        dest, row_tok, n_valid, n_blocks_e, block_start_e = _dispatch_plan(ids)
        xs = _gather_rows(n_valid, row_tok, hn_words)
        hid = _moe_up(_work_items(n_blocks_e, block_start_e, D_FF // UP_TILE), xs, w_gate[l], w_up[l])
        y = _moe_down(_work_items(n_blocks_e, block_start_e, D_MODEL // DOWN_TILE), hid, w_down[l])
        h = _combine(dest, h, gates, y)
    return h.reshape(x.shape)
```

```python
import math

import jax
import jax.numpy as jnp
from jax import lax
from jax.experimental import pallas as pl
from jax.experimental.pallas import tpu as pltpu

F32 = jnp.float32
BF16 = jnp.bfloat16

D_MODEL = 4096
SEQ = 8192
HEAD_DIM = 128
N_HEADS = 16
ATTN_WIDTH = N_HEADS * HEAD_DIM
CONV_WIDTH = D_MODEL - ATTN_WIDTH
CONV_KERNEL = 31
IN_COLS = 3 * ATTN_WIDTH + 2 * CONV_WIDTH
DILATIONS = (1, 4, 16)
ATTN_BLOCK = 128
N_KEYS = 128
N_GROUPS = 8
EXPERTS_PER_GROUP = 8
N_EXPERTS = N_GROUPS * EXPERTS_PER_GROUP
TOP_K = 2
D_FF = 1024
EPS = 1e-6
NEG_INF = -1e30

LANES = 128
VMEM_LIMIT = 56 * 1024 * 1024

SPAN = DILATIONS[-1] * ATTN_BLOCK

MOE_ROWS = 256
N_ASSIGN = SEQ * TOP_K
N_MOE_ROWS = N_ASSIGN + N_EXPERTS * MOE_ROWS
N_MOE_BLOCKS = N_MOE_ROWS // MOE_ROWS
UP_TILE = 512
DOWN_TILE = 2048
WEIGHT_RING = 2
WEIGHT_DMA_PRIORITY = 1
ROUTER_ROWS = 256
COMBINE_ROWS = 128
X_ROW_TILES = D_MODEL // 2 // LANES
Y_ROW_TILES = D_MODEL // 2 // LANES
COMBINE_PITCH = Y_ROW_TILES + 4
GATHER_PITCH = X_ROW_TILES + 8
DMA_UNROLL = 8
WAIT_ROWS = 32


def _params(n_axes):
    return pltpu.CompilerParams(dimension_semantics=("arbitrary",) * n_axes,
                                vmem_limit_bytes=VMEM_LIMIT)


def _pack_rows(x):
    half = x.shape[1] // 2
    lo = pltpu.bitcast(x[:, :half].astype(BF16).astype(F32), jnp.uint32)
    hi = pltpu.bitcast(x[:, half:].astype(BF16).astype(F32), jnp.uint32)
    return (hi & jnp.uint32(0xFFFF0000)) | (lo >> 16)


def _unpack_words(words):
    return (pltpu.bitcast(words << 16, F32),
            pltpu.bitcast(words & jnp.uint32(0xFFFF0000), F32))


def _rmsnorm_kernel(x_ref, w_ref, o_ref):
    x = x_ref[...]
    ms = jnp.mean(x * x, axis=-1, keepdims=True)
    o_ref[...] = (x * lax.rsqrt(ms + EPS) * w_ref[...]).astype(o_ref.dtype)


def _rmsnorm(x, w, rows=256):
    n, d = x.shape
    return pl.pallas_call(
        _rmsnorm_kernel,
        grid=(n // rows,),
        in_specs=[pl.BlockSpec((rows, d), lambda i: (i, 0)),
                  pl.BlockSpec((1, d), lambda i: (0, 0))],
        out_specs=pl.BlockSpec((rows, d), lambda i: (i, 0)),
        out_shape=jax.ShapeDtypeStruct((n, d), BF16),
        compiler_params=_params(1),
        name="rmsnorm",
    )(x, w)


IN_TM = 1024
IN_TN = 512
IN_CHUNK = 256
QK_TILES = 2 * ATTN_WIDTH // IN_TN


def _inproj_kernel(hn_ref, w_ref, nw_ref, o_ref, w_bf):
    j = pl.program_id(0)

    @pl.when(pl.program_id(1) == 0)
    def _():
        w_bf[...] = w_ref[...].astype(BF16)

    is_qk = j < QK_TILES
    for c in range(IN_TM // IN_CHUNK):
        rows = slice(c * IN_CHUNK, (c + 1) * IN_CHUNK)
        acc = jnp.dot(hn_ref[rows, :], w_bf[...], preferred_element_type=F32)
        for hd in range(IN_TN // HEAD_DIM):
            sl = slice(hd * HEAD_DIM, (hd + 1) * HEAD_DIM)
            t = acc[:, sl]
            ms = jnp.mean(t * t, axis=-1, keepdims=True)
            normed = t * lax.rsqrt(ms + EPS) * nw_ref[:, sl]
            o_ref[rows, sl] = jnp.where(is_qk, normed, t).astype(o_ref.dtype)


def _inproj(hn, w, qk_norm_w):
    n = hn.shape[0]
    return pl.pallas_call(
        _inproj_kernel,
        grid=(IN_COLS // IN_TN, n // IN_TM),
        in_specs=[pl.BlockSpec((IN_TM, D_MODEL), lambda j, i: (i, 0)),
                  pl.BlockSpec((D_MODEL, IN_TN), lambda j, i: (0, j)),
                  pl.BlockSpec((1, IN_TN), lambda j, i: (0, jnp.minimum(j, QK_TILES - 1)))],
        out_specs=pl.BlockSpec((IN_TM, IN_TN), lambda j, i: (i, j)),
        out_shape=jax.ShapeDtypeStruct((n, IN_COLS), BF16),
        scratch_shapes=[pltpu.VMEM((D_MODEL, IN_TN), BF16)],
        compiler_params=_params(2),
        name="inproj",
    )(hn, w, qk_norm_w)


def _attn_kernel(slopes_ref, q_ref, kc_ref, kp_ref, vc_ref, vp_ref, o_ref, qf, kf, vf, ob, lb):
    h = pl.program_id(0)
    first_span = pl.program_id(1) == 0
    slope = slopes_ref[h]

    qf[...] = q_ref[...].astype(F32)
    kf[0:SPAN, :] = kp_ref[...].astype(F32)
    kf[SPAN:2 * SPAN, :] = kc_ref[...].astype(F32)
    vf[0:SPAN, :] = vp_ref[...].astype(F32)
    vf[SPAN:2 * SPAN, :] = vc_ref[...].astype(F32)

    qi = lax.broadcasted_iota(jnp.int32, (ATTN_BLOCK, 2 * ATTN_BLOCK), 0)
    kj = lax.broadcasted_iota(jnp.int32, (ATTN_BLOCK, 2 * ATTN_BLOCK), 1)
    steps = qi + ATTN_BLOCK - kj
    valid = (steps >= 0) & (steps <= N_KEYS)
    valid_first = valid & (kj >= jnp.where(first_span, ATTN_BLOCK, 0))
    stepsf = steps.astype(F32)
    scale = 1.0 / math.sqrt(HEAD_DIM)

    for b, d in enumerate(DILATIONS):
        bias = (-slope * float(d)) * stepsf
        for r in range(d):
            for n in range(SPAN // (ATTN_BLOCK * d)):
                q0 = r + d * ATTN_BLOCK * n
                k0 = SPAN + r + d * ATTN_BLOCK * (n - 1)
                q = qf[pl.ds(q0, ATTN_BLOCK, stride=d), :].astype(BF16)
                k = kf[pl.ds(k0, 2 * ATTN_BLOCK, stride=d), :].astype(BF16)
                v = vf[pl.ds(k0, 2 * ATTN_BLOCK, stride=d), :].astype(BF16)
                s = lax.dot_general(q, k, (((1,), (1,)), ((), ())),
                                    preferred_element_type=F32) * scale
                s = jnp.where(valid_first if n == 0 else valid, s + bias, NEG_INF)
                m = jnp.max(s, axis=-1, keepdims=True)
                p = jnp.exp(s - m)
                den = jnp.sum(p, axis=-1, keepdims=True)
                o = jnp.dot(p.astype(BF16), v, preferred_element_type=F32) / den
                lse = m + jnp.log(den)
                ob[b, pl.ds(q0, ATTN_BLOCK, stride=d), :] = o
                lb[b, pl.ds(q0, ATTN_BLOCK, stride=d), :] = jnp.broadcast_to(
                    lse, (ATTN_BLOCK, HEAD_DIM))

    chunk = 256
    for c in range(SPAN // chunk):
        rows = slice(c * chunk, (c + 1) * chunk)
        l0, l1, l2 = lb[0, rows, :], lb[1, rows, :], lb[2, rows, :]
        mx = jnp.maximum(jnp.maximum(l0, l1), l2)
        w0, w1, w2 = jnp.exp(l0 - mx), jnp.exp(l1 - mx), jnp.exp(l2 - mx)
        num = w0 * ob[0, rows, :] + w1 * ob[1, rows, :] + w2 * ob[2, rows, :]
        o_ref[rows, :] = (num / (w0 + w1 + w2)).astype(o_ref.dtype)


def _attention(z, slopes):
    n_spans = SEQ // SPAN
    k_col, v_col = N_HEADS, 2 * N_HEADS
    blk = (SPAN, HEAD_DIM)
    prev = lambda s: jnp.maximum(s - 1, 0)
    return pl.pallas_call(
        _attn_kernel,
        grid_spec=pltpu.PrefetchScalarGridSpec(
            num_scalar_prefetch=1,
            grid=(N_HEADS, n_spans),
            in_specs=[pl.BlockSpec(blk, lambda h, s, sl: (s, h)),
                      pl.BlockSpec(blk, lambda h, s, sl: (s, k_col + h)),
                      pl.BlockSpec(blk, lambda h, s, sl: (prev(s), k_col + h)),
                      pl.BlockSpec(blk, lambda h, s, sl: (s, v_col + h)),
                      pl.BlockSpec(blk, lambda h, s, sl: (prev(s), v_col + h))],
            out_specs=pl.BlockSpec(blk, lambda h, s, sl: (s, h)),
            scratch_shapes=[pltpu.VMEM((SPAN, HEAD_DIM), F32),
                            pltpu.VMEM((2 * SPAN, HEAD_DIM), F32),
                            pltpu.VMEM((2 * SPAN, HEAD_DIM), F32),
                            pltpu.VMEM((3, SPAN, HEAD_DIM), F32),
                            pltpu.VMEM((3, SPAN, HEAD_DIM), F32)]),
        out_shape=jax.ShapeDtypeStruct((SEQ, ATTN_WIDTH), BF16),
        compiler_params=_params(2),
        name="dilated_attention",
    )(slopes, z, z, z, z, z)


CONV_ROWS = 256
CONV_HALO = 32
CONV_CHUNK = 64


SUBLANES = 8
CONV_EXT = CONV_HALO + CONV_ROWS
CONV_SHIFTED = CONV_EXT - SUBLANES


def _conv_kernel(a_ref, g_ref, ah_ref, gh_ref, cw_ref, cb_ref, lw_ref, lb_ref, o_ref,
                 u_ext, shifted, c_scr):
    i = pl.program_id(0)
    n_ct = CONV_WIDTH // LANES
    for ct in range(n_ct):
        cols = slice(ct * LANES, (ct + 1) * LANES)
        u_ext[ct, CONV_HALO:, :] = (a_ref[:, cols].astype(F32)
                                    * jax.nn.sigmoid(g_ref[:, cols].astype(F32)))
        uh = ah_ref[:, cols].astype(F32) * jax.nn.sigmoid(gh_ref[:, cols].astype(F32))
        u_ext[ct, 0:CONV_HALO, :] = jnp.where(i == 0, 0.0, uh)

    first_tap = CONV_HALO - (CONV_KERNEL - 1)

    def slab(ct, carry):
        for k in range(1, SUBLANES):
            shifted[k - 1] = u_ext[ct, k:k + CONV_SHIFTED, :]
        for rc in range(CONV_ROWS // CONV_CHUNK):
            base = rc * CONV_CHUNK
            acc = jnp.broadcast_to(cb_ref[pl.ds(ct, 1), :], (CONV_CHUNK, LANES))
            for j in range(CONV_KERNEL):
                k = (first_tap + j) % SUBLANES
                row = base + (first_tap + j) - k
                if k == 0:
                    src = u_ext[ct, row:row + CONV_CHUNK, :]
                else:
                    src = shifted[k - 1, row:row + CONV_CHUNK, :]
                acc = acc + cw_ref[j, pl.ds(ct, 1), :] * src
            c_scr[ct, base:base + CONV_CHUNK, :] = acc
        return carry

    lax.fori_loop(0, n_ct, slab, 0)

    total = c_scr[0]
    for ct in range(1, n_ct):
        total = total + c_scr[ct]
    mu = jnp.sum(total, axis=-1, keepdims=True) * (1.0 / CONV_WIDTH)
    sq = jnp.square(c_scr[0] - mu)
    for ct in range(1, n_ct):
        sq = sq + jnp.square(c_scr[ct] - mu)
    var = jnp.sum(sq, axis=-1, keepdims=True) * (1.0 / CONV_WIDTH)
    rstd = lax.rsqrt(var + EPS)
    for ct in range(n_ct):
        cols = slice(ct * LANES, (ct + 1) * LANES)
        y = (c_scr[ct] - mu) * rstd * lw_ref[:, cols] + lb_ref[:, cols]
        o_ref[:, cols] = (y * jax.nn.sigmoid(y)).astype(o_ref.dtype)


def _conformer_conv(z, conv_w, conv_b, ln_w, ln_b):
    a_col = 3 * ATTN_WIDTH // CONV_WIDTH
    g_col = a_col + 1
    n_ct = CONV_WIDTH // LANES
    halo_blocks = CONV_ROWS // CONV_HALO
    halo = lambda i: jnp.maximum(i * halo_blocks - 1, 0)
    vec = pl.BlockSpec((1, CONV_WIDTH), lambda i: (0, 0))
    return pl.pallas_call(
        _conv_kernel,
        grid=(SEQ // CONV_ROWS,),
        in_specs=[pl.BlockSpec((CONV_ROWS, CONV_WIDTH), lambda i: (i, a_col)),
                  pl.BlockSpec((CONV_ROWS, CONV_WIDTH), lambda i: (i, g_col)),
                  pl.BlockSpec((CONV_HALO, CONV_WIDTH), lambda i: (halo(i), a_col)),
                  pl.BlockSpec((CONV_HALO, CONV_WIDTH), lambda i: (halo(i), g_col)),
                  pl.BlockSpec((CONV_KERNEL, n_ct, LANES), lambda i: (0, 0, 0)),
                  pl.BlockSpec((n_ct, LANES), lambda i: (0, 0)),
                  vec, vec],
        out_specs=pl.BlockSpec((CONV_ROWS, CONV_WIDTH), lambda i: (i, 0)),
        out_shape=jax.ShapeDtypeStruct((SEQ, CONV_WIDTH), BF16),
        scratch_shapes=[pltpu.VMEM((n_ct, CONV_EXT, LANES), F32),
                        pltpu.VMEM((SUBLANES - 1, CONV_SHIFTED, LANES), F32),
                        pltpu.VMEM((n_ct, CONV_ROWS, LANES), F32)],
        compiler_params=_params(1),
        name="conformer_conv",
    )(z, z, z, z, conv_w, conv_b, ln_w, ln_b)


OUT_TM = 1024
OUT_TN = 512


def _outproj_kernel(attn_ref, conv_ref, wa_ref, wc_ref, x_ref, o_ref, wa_bf, wc_bf):
    @pl.when(pl.program_id(1) == 0)
    def _():
        wa_bf[...] = wa_ref[...].astype(BF16)
        wc_bf[...] = wc_ref[...].astype(BF16)

    acc = jnp.dot(attn_ref[...], wa_bf[...], preferred_element_type=F32)
    acc = acc + jnp.dot(conv_ref[...], wc_bf[...], preferred_element_type=F32)
    o_ref[...] = x_ref[...] + acc


def _outproj(attn, conv, w, x):
    n = x.shape[0]
    conv_blk = ATTN_WIDTH // CONV_WIDTH
    return pl.pallas_call(
        _outproj_kernel,
        grid=(D_MODEL // OUT_TN, n // OUT_TM),
        in_specs=[pl.BlockSpec((OUT_TM, ATTN_WIDTH), lambda j, i: (i, 0)),
                  pl.BlockSpec((OUT_TM, CONV_WIDTH), lambda j, i: (i, 0)),
                  pl.BlockSpec((ATTN_WIDTH, OUT_TN), lambda j, i: (0, j)),
                  pl.BlockSpec((CONV_WIDTH, OUT_TN), lambda j, i: (conv_blk, j)),
                  pl.BlockSpec((OUT_TM, OUT_TN), lambda j, i: (i, j))],
        out_specs=pl.BlockSpec((OUT_TM, OUT_TN), lambda j, i: (i, j)),
        out_shape=jax.ShapeDtypeStruct((n, D_MODEL), F32),
        scratch_shapes=[pltpu.VMEM((ATTN_WIDTH, OUT_TN), BF16),
                        pltpu.VMEM((CONV_WIDTH, OUT_TN), BF16)],
        compiler_params=_params(2),
        name="outproj",
    )(attn, conv, w, w, x)


def _router_kernel(h_ref, nw_ref, wr_ref, br_ref, hn_ref, ids_ref, gates_ref):
    x = h_ref[...]
    ms = jnp.mean(x * x, axis=-1, keepdims=True)
    hn = x * lax.rsqrt(ms + EPS) * nw_ref[...]
    words = _pack_rows(hn)
    for s in range(X_ROW_TILES):
        hn_ref[:, s, :] = words[:, s * LANES:(s + 1) * LANES]

    h_hi = hn.astype(BF16)
    h_lo = (hn - h_hi.astype(F32)).astype(BF16)
    hi_both = jnp.dot(h_hi, wr_ref[...], preferred_element_type=F32)
    lo_hi = jnp.dot(h_lo, wr_ref[:, :LANES], preferred_element_type=F32)
    logits = hi_both[:, :LANES] + (hi_both[:, LANES:] + lo_hi) + br_ref[...]
    lane = lax.broadcasted_iota(jnp.int32, logits.shape, 1)
    big = jnp.int32(1 << 20)

    is_g = lane < N_GROUPS
    gl = jnp.where(is_g, logits, -jnp.inf)
    ge = jnp.exp(gl - jnp.max(gl, axis=-1, keepdims=True))
    g_prob = ge / jnp.sum(ge, axis=-1, keepdims=True)
    g_w = jnp.max(g_prob, axis=-1, keepdims=True)
    g_top = jnp.min(jnp.where(is_g & (g_prob == g_w), lane, big), axis=-1, keepdims=True)

    eidx = lane - N_GROUPS
    grp_shift = EXPERTS_PER_GROUP.bit_length() - 1
    in_grp = (eidx >= 0) & (eidx < N_EXPERTS) & ((eidx >> grp_shift) == g_top)
    el = jnp.where(in_grp, logits, -jnp.inf)
    ee = jnp.exp(el - jnp.max(el, axis=-1, keepdims=True))
    e_prob = ee / jnp.sum(ee, axis=-1, keepdims=True)
    v1 = jnp.max(jnp.where(in_grp, e_prob, -1.0), axis=-1, keepdims=True)
    i1 = jnp.min(jnp.where(in_grp & (e_prob == v1), lane, big), axis=-1, keepdims=True)
    rest = in_grp & (lane != i1)
    v2 = jnp.max(jnp.where(rest, e_prob, -1.0), axis=-1, keepdims=True)
    i2 = jnp.min(jnp.where(rest & (e_prob == v2), lane, big), axis=-1, keepdims=True)
    tot = v1 + v2
    ids_ref[...] = jnp.where(lane == 0, i1 - N_GROUPS, jnp.where(lane == 1, i2 - N_GROUPS, 0))
    gates_ref[...] = jnp.where(lane == 0, g_w * v1 / tot,
                               jnp.where(lane == 1, g_w * v2 / tot, 0.0))


def _router(h, norm_w, w_router, b_router):
    return pl.pallas_call(
        _router_kernel,
        grid=(SEQ // ROUTER_ROWS,),
        in_specs=[pl.BlockSpec((ROUTER_ROWS, D_MODEL), lambda i: (i, 0)),
                  pl.BlockSpec((1, D_MODEL), lambda i: (0, 0)),
                  pl.BlockSpec((D_MODEL, 2 * LANES), lambda i: (0, 0)),
                  pl.BlockSpec((1, LANES), lambda i: (0, 0))],
        out_specs=[pl.BlockSpec((ROUTER_ROWS, X_ROW_TILES, LANES), lambda i: (i, 0, 0)),
                   pl.BlockSpec((ROUTER_ROWS, LANES), lambda i: (i, 0)),
                   pl.BlockSpec((ROUTER_ROWS, LANES), lambda i: (i, 0))],
        out_shape=[jax.ShapeDtypeStruct((SEQ, X_ROW_TILES, LANES), jnp.uint32),
                   jax.ShapeDtypeStruct((SEQ, LANES), jnp.int32),
                   jax.ShapeDtypeStruct((SEQ, LANES), F32)],
        compiler_params=_params(1),
        name="ffn_norm_router",
    )(h, norm_w, w_router, b_router)


def _staged_token(stage, slot, r):
    return stage.at[slot, pl.ds(pl.multiple_of(r * GATHER_PITCH, SUBLANES), X_ROW_TILES)]


def _issue_token_rows(toks, nv, hn_hbm, stage, slot, sem):
    def start(r, carry):
        pltpu.make_async_copy(hn_hbm.at[toks[0, 0, r]], _staged_token(stage, slot, r),
                              sem.at[slot]).start()
        return carry

    def start_group(g, carry):
        for u in range(DMA_UNROLL):
            start(g * DMA_UNROLL + u, carry)
        return carry

    def zero(r, carry):
        _staged_token(stage, slot, r)[...] = jnp.zeros((X_ROW_TILES, LANES), stage.dtype)
        return carry

    groups = nv // DMA_UNROLL
    lax.fori_loop(0, groups, start_group, 0)
    lax.fori_loop(groups * DMA_UNROLL, nv, start, 0)
    lax.fori_loop(nv, MOE_ROWS, zero, 0)


def _wait_token_rows(nv, stage, slot, sem):
    def wait_rows(n_tokens):
        span = stage.at[slot, pl.ds(0, n_tokens * X_ROW_TILES)]
        pltpu.make_async_copy(span, span, sem.at[slot]).wait()

    def wait_many(c, carry):
        wait_rows(WAIT_ROWS)
        return carry

    def wait_one(c, carry):
        wait_rows(1)
        return carry

    lax.fori_loop(0, nv // WAIT_ROWS, wait_many, 0)
    lax.fori_loop(0, nv % WAIT_ROWS, wait_one, 0)


def _staged_block(stage, slot):
    return jnp.concatenate([stage[slot, pl.ds(s, MOE_ROWS, stride=GATHER_PITCH), :]
                            for s in range(X_ROW_TILES)], axis=-1)


def _unpack_rows(words):
    lo, hi = _unpack_words(words)
    return jnp.concatenate([lo.astype(BF16), hi.astype(BF16)], axis=-1)


def _stream_weight_tile(w, first_ref, seq_ref, n_tiles_ref, tile_copies, consume):
    n_tiles = n_tiles_ref[0]

    @pl.when(w == 0)
    def _():
        for k in range(WEIGHT_RING):
            @pl.when(k < n_tiles)
            def _():
                for c in tile_copies(k, k):
                    c.start(priority=WEIGHT_DMA_PRIORITY)

    @pl.when(first_ref[w] == 1)
    def _():
        k = seq_ref[w]
        slot = k % WEIGHT_RING
        for c in tile_copies(k, slot):
            c.wait()
        consume(slot)

        @pl.when(k + WEIGHT_RING < n_tiles)
        def _():
            for c in tile_copies(k + WEIGHT_RING, slot):
                c.start(priority=WEIGHT_DMA_PRIORITY)


def _moe_up_kernel(ib_ref, nv_ref, b_ref, ot_ref, valid_ref, first_ref, seq_ref, te_ref, tt_ref,
                   nt_ref, tok_ref, tok_next_ref, hn_hbm, wg_hbm, wu_hbm, hid_ref,
                   stage, ring_g, ring_u, wg_bf, wu_bf, row_sem, sem):
    w = pl.program_id(0)
    n_items = pl.num_programs(0)
    nxt = jnp.minimum(w + 1, n_items - 1)

    @pl.when(w == 0)
    def _():
        _issue_token_rows(tok_ref, nv_ref[0], hn_hbm, stage, 0, row_sem)

    @pl.when((w + 1 < n_items) & (valid_ref[nxt] == 1))
    def _():
        _issue_token_rows(tok_next_ref, nv_ref[nxt], hn_hbm, stage, (w + 1) % 2, row_sem)

    def tile_copies(k, slot):
        e = te_ref[k]
        col = pl.multiple_of(tt_ref[k] * UP_TILE, UP_TILE)
        return (pltpu.make_async_copy(wg_hbm.at[e, :, pl.ds(col, UP_TILE)], ring_g.at[slot],
                                      sem.at[0, slot]),
                pltpu.make_async_copy(wu_hbm.at[e, :, pl.ds(col, UP_TILE)], ring_u.at[slot],
                                      sem.at[1, slot]))

    def consume(slot):
        wg_bf[...] = ring_g[slot].astype(BF16)
        wu_bf[...] = ring_u[slot].astype(BF16)

    _stream_weight_tile(w, first_ref, seq_ref, nt_ref, tile_copies, consume)

    @pl.when(valid_ref[w] == 1)
    def _():
        slot = w % 2
        _wait_token_rows(nv_ref[w], stage, slot, row_sem)
        x = _unpack_rows(_staged_block(stage, slot))
        g = jnp.dot(x, wg_bf[...], preferred_element_type=F32)
        u = jnp.dot(x, wu_bf[...], preferred_element_type=F32)
        hid_ref[...] = (g * jax.nn.sigmoid(g) * u).astype(hid_ref.dtype)

    @pl.when(valid_ref[w] == 0)
    def _():
        hid_ref[...] = jnp.zeros_like(hid_ref)


def _moe_up(items, n_valid, row_tok, hn_words, w_gate, w_up):
    n_items = N_MOE_BLOCKS * (D_FF // UP_TILE)
    in_blk = items[0]
    toks = row_tok.reshape(N_MOE_BLOCKS, 1, MOE_ROWS)
    tok_spec = lambda shift: pl.BlockSpec(
        (1, 1, MOE_ROWS), lambda w, ib, *_: (ib[jnp.minimum(w + shift, n_items - 1)], 0, 0),
        memory_space=pltpu.SMEM)
    any_spec = pl.BlockSpec(memory_space=pl.ANY)
    return pl.pallas_call(
        _moe_up_kernel,
        grid_spec=pltpu.PrefetchScalarGridSpec(
            num_scalar_prefetch=10,
            grid=(n_items,),
            in_specs=[tok_spec(0), tok_spec(1), any_spec, any_spec, any_spec],
            out_specs=pl.BlockSpec((MOE_ROWS, UP_TILE),
                                   lambda w, ib, nv, b, ot, *_: (b[w], ot[w])),
            scratch_shapes=[pltpu.VMEM((2, MOE_ROWS * GATHER_PITCH, LANES), jnp.uint32),
                            pltpu.VMEM((WEIGHT_RING, D_MODEL, UP_TILE), F32),
                            pltpu.VMEM((WEIGHT_RING, D_MODEL, UP_TILE), F32),
                            pltpu.VMEM((D_MODEL, UP_TILE), BF16),
                            pltpu.VMEM((D_MODEL, UP_TILE), BF16),
                            pltpu.SemaphoreType.DMA((2,)),
                            pltpu.SemaphoreType.DMA((2, WEIGHT_RING))]),
        out_shape=jax.ShapeDtypeStruct((N_MOE_ROWS, D_FF), BF16),
        compiler_params=_params(1),
        name="moe_gate_up",
    )(in_blk, n_valid[in_blk], *items[1:], toks, toks, hn_words, w_gate, w_up)


def _moe_down_kernel(ib_ref, b_ref, ot_ref, valid_ref, first_ref, seq_ref, te_ref, tt_ref, nt_ref,
                     hid_ref, wd_hbm, y_ref, ring, wd_bf, sem):
    w = pl.program_id(0)

    def tile_copies(k, slot):
        col = pl.multiple_of(tt_ref[k] * DOWN_TILE, DOWN_TILE)
        return (pltpu.make_async_copy(wd_hbm.at[te_ref[k], :, pl.ds(col, DOWN_TILE)],
                                      ring.at[slot], sem.at[slot]),)

    def consume(slot):
        wd_bf[...] = ring[slot].astype(BF16)

    _stream_weight_tile(w, first_ref, seq_ref, nt_ref, tile_copies, consume)

    @pl.when(valid_ref[w] == 1)
    def _():
        y = jnp.dot(hid_ref[...], wd_bf[...], preferred_element_type=F32)
        words = _pack_rows(y)
        for s in range(DOWN_TILE // 2 // LANES):
            y_ref[:, s, :] = words[:, s * LANES:(s + 1) * LANES]

    @pl.when(valid_ref[w] == 0)
    def _():
        y_ref[...] = jnp.zeros_like(y_ref)


def _moe_down(items, hid, w_down):
    n_items = N_MOE_BLOCKS * (D_MODEL // DOWN_TILE)
    tile_rows = DOWN_TILE // 2 // LANES
    return pl.pallas_call(
        _moe_down_kernel,
        grid_spec=pltpu.PrefetchScalarGridSpec(
            num_scalar_prefetch=9,
            grid=(n_items,),
            in_specs=[pl.BlockSpec((MOE_ROWS, D_FF), lambda w, ib, *_: (ib[w], 0)),
                      pl.BlockSpec(memory_space=pl.ANY)],
            out_specs=pl.BlockSpec((MOE_ROWS, tile_rows, LANES),
                                   lambda w, ib, b, ot, *_: (b[w], ot[w], 0)),
            scratch_shapes=[pltpu.VMEM((WEIGHT_RING, D_FF, DOWN_TILE), F32),
                            pltpu.VMEM((D_FF, DOWN_TILE), BF16),
                            pltpu.SemaphoreType.DMA((WEIGHT_RING,))]),
        out_shape=jax.ShapeDtypeStruct((N_MOE_ROWS, Y_ROW_TILES, LANES), jnp.uint32),
        compiler_params=_params(1),
        name="moe_down",
    )(*items, hid, w_down)


def _combine_kernel(dest_ref, dest_next_ref, h_ref, gate_ref, y_hbm, o_ref, buf, sem):
    i = pl.program_id(0)
    n_assign = COMBINE_ROWS * TOP_K

    def copy(slot, a, src_row):
        dst = buf.at[slot, a % TOP_K, pl.ds((a // TOP_K) * COMBINE_PITCH, Y_ROW_TILES)]
        return pltpu.make_async_copy(y_hbm.at[src_row], dst, sem.at[slot])

    def issue(dests, slot):
        def start(a, carry):
            copy(slot, a, dests[0, 0, a]).start()
            return carry
        lax.fori_loop(0, n_assign, start, 0, unroll=8)

    @pl.when(i == 0)
    def _():
        issue(dest_ref, 0)

    @pl.when(i + 1 < pl.num_programs(0))
    def _():
        issue(dest_next_ref, (i + 1) % 2)

    slot = i % 2

    for k in range(TOP_K):
        for c in range(COMBINE_ROWS // WAIT_ROWS):
            span = buf.at[slot, k, pl.ds(c * WAIT_ROWS * Y_ROW_TILES, WAIT_ROWS * Y_ROW_TILES)]
            pltpu.make_async_copy(span, span, sem.at[slot]).wait()

    g0 = gate_ref[:, 0:1]
    g1 = gate_ref[:, 1:2]
    tiles_per_half = DOWN_TILE // 2 // LANES
    for s in range(Y_ROW_TILES):
        lo_col = (s // tiles_per_half) * DOWN_TILE + (s % tiles_per_half) * LANES
        lo0, hi0 = _unpack_words(buf[slot, 0, pl.ds(s, COMBINE_ROWS, stride=COMBINE_PITCH), :])
        lo1, hi1 = _unpack_words(buf[slot, 1, pl.ds(s, COMBINE_ROWS, stride=COMBINE_PITCH), :])
        for col, y0, y1 in ((lo_col, lo0, lo1), (lo_col + DOWN_TILE // 2, hi0, hi1)):
            cols = slice(col, col + LANES)
            o_ref[:, cols] = h_ref[:, cols] + (y0 * g0 + y1 * g1)


def _combine(dest, h, gates, y):
    n_tiles = SEQ // COMBINE_ROWS
    per_tile = COMBINE_ROWS * TOP_K
    dests = dest.reshape(n_tiles, 1, per_tile)
    return pl.pallas_call(
        _combine_kernel,
        grid=(n_tiles,),
        in_specs=[pl.BlockSpec((1, 1, per_tile), lambda i: (i, 0, 0), memory_space=pltpu.SMEM),
                  pl.BlockSpec((1, 1, per_tile), lambda i: (jnp.minimum(i + 1, n_tiles - 1), 0, 0),
                               memory_space=pltpu.SMEM),
                  pl.BlockSpec((COMBINE_ROWS, D_MODEL), lambda i: (i, 0)),
                  pl.BlockSpec((COMBINE_ROWS, LANES), lambda i: (i, 0)),
                  pl.BlockSpec(memory_space=pl.ANY)],
        out_specs=pl.BlockSpec((COMBINE_ROWS, D_MODEL), lambda i: (i, 0)),
        out_shape=jax.ShapeDtypeStruct((SEQ, D_MODEL), F32),
        scratch_shapes=[pltpu.VMEM((2, TOP_K, COMBINE_ROWS * COMBINE_PITCH, LANES), jnp.uint32),
                        pltpu.SemaphoreType.DMA((2,))],
        compiler_params=_params(1),
        name="moe_combine",
    )(dests, dests, h, gates, y)


def _work_items(n_blocks_e, block_start_e, tiles):
    i32 = lambda a: a.astype(jnp.int32)
    n_items = N_MOE_BLOCKS * tiles
    per_e = n_blocks_e * tiles
    ends = jnp.cumsum(per_e)
    total = ends[-1]
    idx = jnp.arange(n_items, dtype=jnp.int32)
    valid = idx < total
    w = jnp.minimum(idx, total - 1)
    e = i32(jnp.searchsorted(ends, w, side='right'))
    local = w - (ends - per_e)[e]
    nb = n_blocks_e[e]
    spare = idx - total
    w_tile = local // nb
    o_tile = jnp.where(valid, w_tile, spare % tiles)
    blk = jnp.where(valid, block_start_e[e] + local % nb, total // tiles + spare // tiles)
    first = valid & (local % nb == 0)
    active = n_blocks_e > 0
    active_rank = jnp.cumsum(active) - active
    seq = active_rank[e] * tiles + w_tile
    active_experts = jnp.argsort(~active, stable=True)
    k = jnp.arange(N_EXPERTS * tiles, dtype=jnp.int32)
    tile_e = active_experts[k // tiles]
    tile_t = k % tiles
    n_tiles = (jnp.sum(active) * tiles).reshape(1)
    in_blk = block_start_e[e] + local % nb
    return (i32(in_blk), i32(blk), i32(o_tile), i32(valid), i32(first), i32(seq), i32(tile_e),
            i32(tile_t), i32(n_tiles))


def _dispatch_plan(ids):
    expert = ids[:, :TOP_K].reshape(-1)
    token = jnp.repeat(jnp.arange(SEQ, dtype=jnp.int32), TOP_K)
    onehot = (expert[:, None] == jnp.arange(N_EXPERTS, dtype=jnp.int32)[None, :]).astype(jnp.int32)
    running = jnp.cumsum(onehot, axis=0)
    rank = jnp.take_along_axis(running, expert[:, None], axis=1)[:, 0] - 1
    counts = running[-1]
    n_blocks_e = (counts + MOE_ROWS - 1) // MOE_ROWS
    block_ends = jnp.cumsum(n_blocks_e)
    block_start_e = block_ends - n_blocks_e
    dest = (block_start_e[expert] * MOE_ROWS + rank).astype(jnp.int32)
    row_tok = jnp.zeros((N_MOE_ROWS,), jnp.int32).at[dest].set(token)
    blk = jnp.arange(N_MOE_BLOCKS, dtype=jnp.int32)
    blk_e = jnp.minimum(jnp.searchsorted(block_ends, blk, side='right'), N_EXPERTS - 1)
    n_valid = jnp.clip(counts[blk_e] - (blk - block_start_e[blk_e]) * MOE_ROWS, 0, MOE_ROWS)
    n_valid = jnp.where(blk < block_ends[-1], n_valid, 0).astype(jnp.int32)
    return dest, row_tok, n_valid, n_blocks_e.astype(jnp.int32), block_start_e.astype(jnp.int32)


def kernel(x, norm_mix_w, w_in, q_norm_w, k_norm_w, conv_w, conv_b, conv_ln_w, conv_ln_b, w_out,
           norm_ffn_w, w_group_router, b_group_router, w_expert_router, b_expert_router,
           w_gate, w_up, w_down):
    h = x.reshape(SEQ, D_MODEL)
    slopes = jnp.exp2(-8.0 * jnp.arange(1, N_HEADS + 1, dtype=F32) / N_HEADS)
    for l in range(norm_mix_w.shape[0]):
        hn = _rmsnorm(h, norm_mix_w[l][None, :])
        qk_norm = jnp.concatenate([jnp.tile(q_norm_w[l], N_HEADS), jnp.tile(k_norm_w[l], N_HEADS)])
        z = _inproj(hn, w_in[l], qk_norm[None, :])
        attn = _attention(z, slopes)
        n_ct = CONV_WIDTH // LANES
        conv = _conformer_conv(z, conv_w[l].reshape(CONV_KERNEL, n_ct, LANES),
                               conv_b[l].reshape(n_ct, LANES), conv_ln_w[l][None, :],
                               conv_ln_b[l][None, :])
        h = _outproj(attn, conv, w_out[l], h)

        pad = LANES - N_GROUPS - N_EXPERTS
        w_router = jnp.concatenate([w_group_router[l], w_expert_router[l],
                                    jnp.zeros((D_MODEL, pad), F32)], axis=1)
        b_router = jnp.concatenate([b_group_router[l], b_expert_router[l], jnp.zeros((pad,), F32)])
        w_router_hi = w_router.astype(BF16)
        w_router_lo = (w_router - w_router_hi.astype(F32)).astype(BF16)
        hn_words, ids, gates = _router(h, norm_ffn_w[l][None, :],
                                       jnp.concatenate([w_router_hi, w_router_lo], axis=1),
                                       b_router[None, :])
        dest, row_tok, n_valid, n_blocks_e, block_start_e = _dispatch_plan(ids)
        hid = _moe_up(_work_items(n_blocks_e, block_start_e, D_FF // UP_TILE), n_valid, row_tok,
                      hn_words, w_gate[l], w_up[l])
        y = _moe_down(_work_items(n_blocks_e, block_start_e, D_MODEL // DOWN_TILE), hid, w_down[l])
        h = _combine(dest, h, gates, y)
    return h.reshape(x.shape)
```

```python
import math

import jax
import jax.numpy as jnp
from jax import lax
from jax.experimental import pallas as pl
from jax.experimental.pallas import tpu as pltpu

F32 = jnp.float32
BF16 = jnp.bfloat16

D_MODEL = 4096
SEQ = 8192
HEAD_DIM = 128
N_HEADS = 16
ATTN_WIDTH = N_HEADS * HEAD_DIM
CONV_WIDTH = D_MODEL - ATTN_WIDTH
CONV_KERNEL = 31
IN_COLS = 3 * ATTN_WIDTH + 2 * CONV_WIDTH
DILATIONS = (1, 4, 16)
ATTN_BLOCK = 128
N_KEYS = 128
N_GROUPS = 8
EXPERTS_PER_GROUP = 8
N_EXPERTS = N_GROUPS * EXPERTS_PER_GROUP
TOP_K = 2
D_FF = 1024
EPS = 1e-6
NEG_INF = -1e30

LANES = 128
VMEM_LIMIT = 56 * 1024 * 1024

SPAN = DILATIONS[-1] * ATTN_BLOCK

MOE_ROWS = 256
N_ASSIGN = SEQ * TOP_K
N_MOE_ROWS = N_ASSIGN + N_EXPERTS * MOE_ROWS
N_MOE_BLOCKS = N_MOE_ROWS // MOE_ROWS
UP_TILE = 512
DOWN_TILE = 4096
WEIGHT_RING = 2
WEIGHT_DMA_PRIORITY = 1
ROUTER_ROWS = 256
COMBINE_ROWS = 256
X_ROW_TILES = D_MODEL // 2 // LANES
Y_ROW_TILES = D_MODEL // 2 // LANES
COMBINE_PITCH = Y_ROW_TILES + 4
GATHER_PITCH = X_ROW_TILES + 8
DMA_UNROLL = 8
WAIT_ROWS = 32


def _params(n_axes):
    return pltpu.CompilerParams(dimension_semantics=("arbitrary",) * n_axes,
                                vmem_limit_bytes=VMEM_LIMIT)


def _pack_rows(x):
    half = x.shape[1] // 2
    lo = pltpu.bitcast(x[:, :half].astype(BF16).astype(F32), jnp.uint32)
    hi = pltpu.bitcast(x[:, half:].astype(BF16).astype(F32), jnp.uint32)
    return (hi & jnp.uint32(0xFFFF0000)) | (lo >> 16)


def _unpack_words(words):
    return (pltpu.bitcast(words << 16, F32),
            pltpu.bitcast(words & jnp.uint32(0xFFFF0000), F32))


def _rmsnorm_kernel(x_ref, w_ref, o_ref):
    x = x_ref[...]
    ms = jnp.mean(x * x, axis=-1, keepdims=True)
    o_ref[...] = (x * lax.rsqrt(ms + EPS) * w_ref[...]).astype(o_ref.dtype)


def _rmsnorm(x, w, rows=256):
    n, d = x.shape
    return pl.pallas_call(
        _rmsnorm_kernel,
        grid=(n // rows,),
        in_specs=[pl.BlockSpec((rows, d), lambda i: (i, 0)),
                  pl.BlockSpec((1, d), lambda i: (0, 0))],
        out_specs=pl.BlockSpec((rows, d), lambda i: (i, 0)),
        out_shape=jax.ShapeDtypeStruct((n, d), BF16),
        compiler_params=_params(1),
        name="rmsnorm",
    )(x, w)


IN_TM = 1024
IN_TN = 512
IN_CHUNK = 256
QK_TILES = 2 * ATTN_WIDTH // IN_TN


def _inproj_kernel(hn_ref, w_ref, nw_ref, o_ref, w_bf):
    j = pl.program_id(0)

    @pl.when(pl.program_id(1) == 0)
    def _():
        w_bf[...] = w_ref[...].astype(BF16)

    is_qk = j < QK_TILES
    for c in range(IN_TM // IN_CHUNK):
        rows = slice(c * IN_CHUNK, (c + 1) * IN_CHUNK)
        acc = jnp.dot(hn_ref[rows, :], w_bf[...], preferred_element_type=F32)
        for hd in range(IN_TN // HEAD_DIM):
            sl = slice(hd * HEAD_DIM, (hd + 1) * HEAD_DIM)
            t = acc[:, sl]
            ms = jnp.mean(t * t, axis=-1, keepdims=True)
            normed = t * lax.rsqrt(ms + EPS) * nw_ref[:, sl]
            o_ref[rows, sl] = jnp.where(is_qk, normed, t).astype(o_ref.dtype)


def _inproj(hn, w, qk_norm_w):
    n = hn.shape[0]
    return pl.pallas_call(
        _inproj_kernel,
        grid=(IN_COLS // IN_TN, n // IN_TM),
        in_specs=[pl.BlockSpec((IN_TM, D_MODEL), lambda j, i: (i, 0)),
                  pl.BlockSpec((D_MODEL, IN_TN), lambda j, i: (0, j)),
                  pl.BlockSpec((1, IN_TN), lambda j, i: (0, jnp.minimum(j, QK_TILES - 1)))],
        out_specs=pl.BlockSpec((IN_TM, IN_TN), lambda j, i: (i, j)),
        out_shape=jax.ShapeDtypeStruct((n, IN_COLS), BF16),
        scratch_shapes=[pltpu.VMEM((D_MODEL, IN_TN), BF16)],
        compiler_params=_params(2),
        name="inproj",
    )(hn, w, qk_norm_w)


def _attn_kernel(slopes_ref, q_ref, kc_ref, kp_ref, vc_ref, vp_ref, o_ref, qf, kf, vf, ob, lb):
    h = pl.program_id(0)
    first_span = pl.program_id(1) == 0
    slope = slopes_ref[h]

    qi = lax.broadcasted_iota(jnp.int32, (ATTN_BLOCK, 2 * ATTN_BLOCK), 0)
    kj = lax.broadcasted_iota(jnp.int32, (ATTN_BLOCK, 2 * ATTN_BLOCK), 1)
    steps = qi + ATTN_BLOCK - kj
    valid = (steps >= 0) & (steps <= N_KEYS)
    stepsf = steps.astype(F32)
    scale = 1.0 / math.sqrt(HEAD_DIM)

    def attend(b, d, bias, q0, q, k, v, no_prev_block):
        s = lax.dot_general(q, k, (((1,), (1,)), ((), ())),
                            preferred_element_type=F32) * scale + bias
        if no_prev_block is not None:
            s = jnp.where(kj >= jnp.where(no_prev_block, ATTN_BLOCK, 0), s, NEG_INF)
        m = jnp.max(s, axis=-1, keepdims=True)
        p = jnp.exp(s - m)
        den = jnp.sum(p, axis=-1, keepdims=True)
        o = jnp.dot(p.astype(BF16), v, preferred_element_type=F32) / den
        lse = m + jnp.log(den)
        ob[b, pl.ds(q0, ATTN_BLOCK, stride=d), :] = o
        lb[b, pl.ds(q0, ATTN_BLOCK, stride=d), :] = jnp.broadcast_to(lse, (ATTN_BLOCK, HEAD_DIM))

    def banded_bias(d):
        return jnp.where(valid, (-slope * float(d)) * stepsf, NEG_INF)

    def key_window(cur, prev, n):
        if n > 0:
            return cur[(n - 1) * ATTN_BLOCK:(n + 1) * ATTN_BLOCK, :]
        last = prev.shape[0] - ATTN_BLOCK
        return jnp.concatenate([prev[last:, :], cur[:ATTN_BLOCK, :]], axis=0)

    b, d = 0, DILATIONS[0]
    bias = banded_bias(d)
    for n in range(SPAN // ATTN_BLOCK):
        attend(b, d, bias, n * ATTN_BLOCK, q_ref[n * ATTN_BLOCK:(n + 1) * ATTN_BLOCK, :],
               key_window(kc_ref, kp_ref, n), key_window(vc_ref, vp_ref, n),
               first_span if n == 0 else None)

    qf[...] = q_ref[...].astype(F32)
    kf[0:SPAN, :] = kp_ref[...].astype(F32)
    kf[SPAN:2 * SPAN, :] = kc_ref[...].astype(F32)
    vf[0:SPAN, :] = vp_ref[...].astype(F32)
    vf[SPAN:2 * SPAN, :] = vc_ref[...].astype(F32)
    for b, d in list(enumerate(DILATIONS))[1:]:
        bias = banded_bias(d)
        for r in range(d):
            for n in range(SPAN // (ATTN_BLOCK * d)):
                q0 = r + d * ATTN_BLOCK * n
                k0 = SPAN + q0 - d * ATTN_BLOCK
                attend(b, d, bias, q0,
                       qf[pl.ds(q0, ATTN_BLOCK, stride=d), :].astype(BF16),
                       kf[pl.ds(k0, 2 * ATTN_BLOCK, stride=d), :].astype(BF16),
                       vf[pl.ds(k0, 2 * ATTN_BLOCK, stride=d), :].astype(BF16),
                       first_span if n == 0 else None)

    chunk = 256
    for c in range(SPAN // chunk):
        rows = slice(c * chunk, (c + 1) * chunk)
        l0, l1, l2 = lb[0, rows, :], lb[1, rows, :], lb[2, rows, :]
        mx = jnp.maximum(jnp.maximum(l0, l1), l2)
        w0, w1, w2 = jnp.exp(l0 - mx), jnp.exp(l1 - mx), jnp.exp(l2 - mx)
        num = w0 * ob[0, rows, :] + w1 * ob[1, rows, :] + w2 * ob[2, rows, :]
        o_ref[rows, :] = (num / (w0 + w1 + w2)).astype(o_ref.dtype)


def _attention(z, slopes):
    n_spans = SEQ // SPAN
    k_col, v_col = N_HEADS, 2 * N_HEADS
    blk = (SPAN, HEAD_DIM)
    prev = lambda s: jnp.maximum(s - 1, 0)
    return pl.pallas_call(
        _attn_kernel,
        grid_spec=pltpu.PrefetchScalarGridSpec(
            num_scalar_prefetch=1,
            grid=(N_HEADS, n_spans),
            in_specs=[pl.BlockSpec(blk, lambda h, s, sl: (s, h)),
                      pl.BlockSpec(blk, lambda h, s, sl: (s, k_col + h)),
                      pl.BlockSpec(blk, lambda h, s, sl: (prev(s), k_col + h)),
                      pl.BlockSpec(blk, lambda h, s, sl: (s, v_col + h)),
                      pl.BlockSpec(blk, lambda h, s, sl: (prev(s), v_col + h))],
            out_specs=pl.BlockSpec(blk, lambda h, s, sl: (s, h)),
            scratch_shapes=[pltpu.VMEM((SPAN, HEAD_DIM), F32),
                            pltpu.VMEM((2 * SPAN, HEAD_DIM), F32),
                            pltpu.VMEM((2 * SPAN, HEAD_DIM), F32),
                            pltpu.VMEM((3, SPAN, HEAD_DIM), F32),
                            pltpu.VMEM((3, SPAN, HEAD_DIM), F32)]),
        out_shape=jax.ShapeDtypeStruct((SEQ, ATTN_WIDTH), BF16),
        compiler_params=_params(2),
        name="dilated_attention",
    )(slopes, z, z, z, z, z)


CONV_ROWS = 256
CONV_HALO = 32
CONV_CHUNK = 64


SUBLANES = 8
CONV_EXT = CONV_HALO + CONV_ROWS
CONV_SHIFTED = CONV_EXT - SUBLANES


def _conv_kernel(a_ref, g_ref, ah_ref, gh_ref, cw_ref, cb_ref, lw_ref, lb_ref, o_ref,
                 u_ext, shifted, c_scr):
    i = pl.program_id(0)
    n_ct = CONV_WIDTH // LANES
    for ct in range(n_ct):
        cols = slice(ct * LANES, (ct + 1) * LANES)
        u_ext[ct, CONV_HALO:, :] = (a_ref[:, cols].astype(F32)
                                    * jax.nn.sigmoid(g_ref[:, cols].astype(F32)))
        uh = ah_ref[:, cols].astype(F32) * jax.nn.sigmoid(gh_ref[:, cols].astype(F32))
        u_ext[ct, 0:CONV_HALO, :] = jnp.where(i == 0, 0.0, uh)

    first_tap = CONV_HALO - (CONV_KERNEL - 1)

    def slab(ct, carry):
        for k in range(1, SUBLANES):
            shifted[k - 1] = u_ext[ct, k:k + CONV_SHIFTED, :]
        for rc in range(CONV_ROWS // CONV_CHUNK):
            base = rc * CONV_CHUNK
            acc = jnp.broadcast_to(cb_ref[pl.ds(ct, 1), :], (CONV_CHUNK, LANES))
            for j in range(CONV_KERNEL):
                k = (first_tap + j) % SUBLANES
                row = base + (first_tap + j) - k
                if k == 0:
                    src = u_ext[ct, row:row + CONV_CHUNK, :]
                else:
                    src = shifted[k - 1, row:row + CONV_CHUNK, :]
                acc = acc + cw_ref[j, pl.ds(ct, 1), :] * src
            c_scr[ct, base:base + CONV_CHUNK, :] = acc
        return carry

    lax.fori_loop(0, n_ct, slab, 0)

    total = c_scr[0]
    for ct in range(1, n_ct):
        total = total + c_scr[ct]
    mu = jnp.sum(total, axis=-1, keepdims=True) * (1.0 / CONV_WIDTH)
    sq = jnp.square(c_scr[0] - mu)
    for ct in range(1, n_ct):
        sq = sq + jnp.square(c_scr[ct] - mu)
    var = jnp.sum(sq, axis=-1, keepdims=True) * (1.0 / CONV_WIDTH)
    rstd = lax.rsqrt(var + EPS)
    for ct in range(n_ct):
        cols = slice(ct * LANES, (ct + 1) * LANES)
        y = (c_scr[ct] - mu) * rstd * lw_ref[:, cols] + lb_ref[:, cols]
        o_ref[:, cols] = (y * jax.nn.sigmoid(y)).astype(o_ref.dtype)


def _conformer_conv(z, conv_w, conv_b, ln_w, ln_b):
    a_col = 3 * ATTN_WIDTH // CONV_WIDTH
    g_col = a_col + 1
    n_ct = CONV_WIDTH // LANES
    halo_blocks = CONV_ROWS // CONV_HALO
    halo = lambda i: jnp.maximum(i * halo_blocks - 1, 0)
    vec = pl.BlockSpec((1, CONV_WIDTH), lambda i: (0, 0))
    return pl.pallas_call(
        _conv_kernel,
        grid=(SEQ // CONV_ROWS,),
        in_specs=[pl.BlockSpec((CONV_ROWS, CONV_WIDTH), lambda i: (i, a_col)),
                  pl.BlockSpec((CONV_ROWS, CONV_WIDTH), lambda i: (i, g_col)),
                  pl.BlockSpec((CONV_HALO, CONV_WIDTH), lambda i: (halo(i), a_col)),
                  pl.BlockSpec((CONV_HALO, CONV_WIDTH), lambda i: (halo(i), g_col)),
                  pl.BlockSpec((CONV_KERNEL, n_ct, LANES), lambda i: (0, 0, 0)),
                  pl.BlockSpec((n_ct, LANES), lambda i: (0, 0)),
                  vec, vec],
        out_specs=pl.BlockSpec((CONV_ROWS, CONV_WIDTH), lambda i: (i, 0)),
        out_shape=jax.ShapeDtypeStruct((SEQ, CONV_WIDTH), BF16),
        scratch_shapes=[pltpu.VMEM((n_ct, CONV_EXT, LANES), F32),
                        pltpu.VMEM((SUBLANES - 1, CONV_SHIFTED, LANES), F32),
                        pltpu.VMEM((n_ct, CONV_ROWS, LANES), F32)],
        compiler_params=_params(1),
        name="conformer_conv",
    )(z, z, z, z, conv_w, conv_b, ln_w, ln_b)


OUT_TM = 1024
OUT_TN = 512


def _outproj_kernel(attn_ref, conv_ref, wa_ref, wc_ref, x_ref, o_ref, wa_bf, wc_bf):
    @pl.when(pl.program_id(1) == 0)
    def _():
        wa_bf[...] = wa_ref[...].astype(BF16)
        wc_bf[...] = wc_ref[...].astype(BF16)

    acc = jnp.dot(attn_ref[...], wa_bf[...], preferred_element_type=F32)
    acc = acc + jnp.dot(conv_ref[...], wc_bf[...], preferred_element_type=F32)
    o_ref[...] = x_ref[...] + acc


def _outproj(attn, conv, w, x):
    n = x.shape[0]
    conv_blk = ATTN_WIDTH // CONV_WIDTH
    return pl.pallas_call(
        _outproj_kernel,
        grid=(D_MODEL // OUT_TN, n // OUT_TM),
        in_specs=[pl.BlockSpec((OUT_TM, ATTN_WIDTH), lambda j, i: (i, 0)),
                  pl.BlockSpec((OUT_TM, CONV_WIDTH), lambda j, i: (i, 0)),
                  pl.BlockSpec((ATTN_WIDTH, OUT_TN), lambda j, i: (0, j)),
                  pl.BlockSpec((CONV_WIDTH, OUT_TN), lambda j, i: (conv_blk, j)),
                  pl.BlockSpec((OUT_TM, OUT_TN), lambda j, i: (i, j))],
        out_specs=pl.BlockSpec((OUT_TM, OUT_TN), lambda j, i: (i, j)),
        out_shape=jax.ShapeDtypeStruct((n, D_MODEL), F32),
        scratch_shapes=[pltpu.VMEM((ATTN_WIDTH, OUT_TN), BF16),
                        pltpu.VMEM((CONV_WIDTH, OUT_TN), BF16)],
        compiler_params=_params(2),
        name="outproj",
    )(attn, conv, w, w, x)


def _router_kernel(h_ref, nw_ref, wr_ref, br_ref, hn_ref, ids_ref, gates_ref):
    x = h_ref[...]
    ms = jnp.mean(x * x, axis=-1, keepdims=True)
    hn = x * lax.rsqrt(ms + EPS) * nw_ref[...]
    words = _pack_rows(hn)
    for s in range(X_ROW_TILES):
        hn_ref[:, s, :] = words[:, s * LANES:(s + 1) * LANES]

    h_hi = hn.astype(BF16)
    h_lo = (hn - h_hi.astype(F32)).astype(BF16)
    hi_both = jnp.dot(h_hi, wr_ref[...], preferred_element_type=F32)
    lo_hi = jnp.dot(h_lo, wr_ref[:, :LANES], preferred_element_type=F32)
    logits = hi_both[:, :LANES] + (hi_both[:, LANES:] + lo_hi) + br_ref[...]
    lane = lax.broadcasted_iota(jnp.int32, logits.shape, 1)
    big = jnp.int32(1 << 20)

    is_g = lane < N_GROUPS
    gl = jnp.where(is_g, logits, -jnp.inf)
    ge = jnp.exp(gl - jnp.max(gl, axis=-1, keepdims=True))
    g_prob = ge / jnp.sum(ge, axis=-1, keepdims=True)
    g_w = jnp.max(g_prob, axis=-1, keepdims=True)
    g_top = jnp.min(jnp.where(is_g & (g_prob == g_w), lane, big), axis=-1, keepdims=True)

    eidx = lane - N_GROUPS
    grp_shift = EXPERTS_PER_GROUP.bit_length() - 1
    in_grp = (eidx >= 0) & (eidx < N_EXPERTS) & ((eidx >> grp_shift) == g_top)
    el = jnp.where(in_grp, logits, -jnp.inf)
    ee = jnp.exp(el - jnp.max(el, axis=-1, keepdims=True))
    e_prob = ee / jnp.sum(ee, axis=-1, keepdims=True)
    v1 = jnp.max(jnp.where(in_grp, e_prob, -1.0), axis=-1, keepdims=True)
    i1 = jnp.min(jnp.where(in_grp & (e_prob == v1), lane, big), axis=-1, keepdims=True)
    rest = in_grp & (lane != i1)
    v2 = jnp.max(jnp.where(rest, e_prob, -1.0), axis=-1, keepdims=True)
    i2 = jnp.min(jnp.where(rest & (e_prob == v2), lane, big), axis=-1, keepdims=True)
    tot = v1 + v2
    ids_ref[...] = jnp.where(lane == 0, i1 - N_GROUPS, jnp.where(lane == 1, i2 - N_GROUPS, 0))
    gates_ref[...] = jnp.where(lane == 0, g_w * v1 / tot,
                               jnp.where(lane == 1, g_w * v2 / tot, 0.0))


def _router(h, norm_w, w_router, b_router):
    return pl.pallas_call(
        _router_kernel,
        grid=(SEQ // ROUTER_ROWS,),
        in_specs=[pl.BlockSpec((ROUTER_ROWS, D_MODEL), lambda i: (i, 0)),
                  pl.BlockSpec((1, D_MODEL), lambda i: (0, 0)),
                  pl.BlockSpec((D_MODEL, 2 * LANES), lambda i: (0, 0)),
                  pl.BlockSpec((1, LANES), lambda i: (0, 0))],
        out_specs=[pl.BlockSpec((ROUTER_ROWS, X_ROW_TILES, LANES), lambda i: (i, 0, 0)),
                   pl.BlockSpec((ROUTER_ROWS, LANES), lambda i: (i, 0)),
                   pl.BlockSpec((ROUTER_ROWS, LANES), lambda i: (i, 0))],
        out_shape=[jax.ShapeDtypeStruct((SEQ, X_ROW_TILES, LANES), jnp.uint32),
                   jax.ShapeDtypeStruct((SEQ, LANES), jnp.int32),
                   jax.ShapeDtypeStruct((SEQ, LANES), F32)],
        compiler_params=_params(1),
        name="ffn_norm_router",
    )(h, norm_w, w_router, b_router)


def _staged_token(stage, slot, r):
    return stage.at[slot, pl.ds(pl.multiple_of(r * GATHER_PITCH, SUBLANES), X_ROW_TILES)]


def _issue_token_rows(toks, nv, hn_hbm, stage, slot, sem):
    def start(r, carry):
        pltpu.make_async_copy(hn_hbm.at[toks[0, 0, r]], _staged_token(stage, slot, r),
                              sem.at[slot]).start()
        return carry

    def start_group(g, carry):
        for u in range(DMA_UNROLL):
            start(g * DMA_UNROLL + u, carry)
        return carry

    def zero(r, carry):
        _staged_token(stage, slot, r)[...] = jnp.zeros((X_ROW_TILES, LANES), stage.dtype)
        return carry

    groups = nv // DMA_UNROLL
    lax.fori_loop(0, groups, start_group, 0)
    lax.fori_loop(groups * DMA_UNROLL, nv, start, 0)
    lax.fori_loop(nv, MOE_ROWS, zero, 0)


def _wait_token_rows(nv, stage, slot, sem):
    def wait_rows(n_tokens):
        span = stage.at[slot, pl.ds(0, n_tokens * X_ROW_TILES)]
        pltpu.make_async_copy(span, span, sem.at[slot]).wait()

    def wait_many(c, carry):
        wait_rows(WAIT_ROWS)
        return carry

    def wait_one(c, carry):
        wait_rows(1)
        return carry

    lax.fori_loop(0, nv // WAIT_ROWS, wait_many, 0)
    lax.fori_loop(0, nv % WAIT_ROWS, wait_one, 0)


def _staged_block(stage, slot):
    return jnp.concatenate([stage[slot, pl.ds(s, MOE_ROWS, stride=GATHER_PITCH), :]
                            for s in range(X_ROW_TILES)], axis=-1)


def _unpack_rows(words):
    lo, hi = _unpack_words(words)
    return jnp.concatenate([lo.astype(BF16), hi.astype(BF16)], axis=-1)


def _stream_weight_tile(w, first_ref, seq_ref, n_tiles_ref, tile_copies, consume):
    n_tiles = n_tiles_ref[0]

    @pl.when(w == 0)
    def _():
        for k in range(WEIGHT_RING):
            @pl.when(k < n_tiles)
            def _():
                for c in tile_copies(k, k):
                    c.start(priority=WEIGHT_DMA_PRIORITY)

    @pl.when(first_ref[w] == 1)
    def _():
        k = seq_ref[w]
        slot = k % WEIGHT_RING
        for c in tile_copies(k, slot):
            c.wait()
        consume(slot)

        @pl.when(k + WEIGHT_RING < n_tiles)
        def _():
            for c in tile_copies(k + WEIGHT_RING, slot):
                c.start(priority=WEIGHT_DMA_PRIORITY)


def _moe_up_kernel(ib_ref, nv_ref, b_ref, ot_ref, valid_ref, first_ref, seq_ref, te_ref, tt_ref,
                   nt_ref, tok_ref, tok_next_ref, hn_hbm, wg_hbm, wu_hbm, hid_ref,
                   stage, ring_g, ring_u, wg_bf, wu_bf, row_sem, sem):
    w = pl.program_id(0)
    n_items = pl.num_programs(0)
    nxt = jnp.minimum(w + 1, n_items - 1)

    @pl.when(w == 0)
    def _():
        _issue_token_rows(tok_ref, nv_ref[0], hn_hbm, stage, 0, row_sem)

    @pl.when((w + 1 < n_items) & (valid_ref[nxt] == 1))
    def _():
        _issue_token_rows(tok_next_ref, nv_ref[nxt], hn_hbm, stage, (w + 1) % 2, row_sem)

    def tile_copies(k, slot):
        e = te_ref[k]
        col = pl.multiple_of(tt_ref[k] * UP_TILE, UP_TILE)
        return (pltpu.make_async_copy(wg_hbm.at[e, :, pl.ds(col, UP_TILE)], ring_g.at[slot],
                                      sem.at[0, slot]),
                pltpu.make_async_copy(wu_hbm.at[e, :, pl.ds(col, UP_TILE)], ring_u.at[slot],
                                      sem.at[1, slot]))

    def consume(slot):
        wg_bf[...] = ring_g[slot].astype(BF16)
        wu_bf[...] = ring_u[slot].astype(BF16)

    _stream_weight_tile(w, first_ref, seq_ref, nt_ref, tile_copies, consume)

    @pl.when(valid_ref[w] == 1)
    def _():
        slot = w % 2
        _wait_token_rows(nv_ref[w], stage, slot, row_sem)
        x = _unpack_rows(_staged_block(stage, slot))
        g = jnp.dot(x, wg_bf[...], preferred_element_type=F32)
        u = jnp.dot(x, wu_bf[...], preferred_element_type=F32)
        hid_ref[...] = (g * jax.nn.sigmoid(g) * u).astype(hid_ref.dtype)

    @pl.when(valid_ref[w] == 0)
    def _():
        hid_ref[...] = jnp.zeros_like(hid_ref)


def _moe_up(items, n_valid, row_tok, hn_words, w_gate, w_up):
    n_items = N_MOE_BLOCKS * (D_FF // UP_TILE)
    in_blk = items[0]
    toks = row_tok.reshape(N_MOE_BLOCKS, 1, MOE_ROWS)
    tok_spec = lambda shift: pl.BlockSpec(
        (1, 1, MOE_ROWS), lambda w, ib, *_: (ib[jnp.minimum(w + shift, n_items - 1)], 0, 0),
        memory_space=pltpu.SMEM)
    any_spec = pl.BlockSpec(memory_space=pl.ANY)
    return pl.pallas_call(
        _moe_up_kernel,
        grid_spec=pltpu.PrefetchScalarGridSpec(
            num_scalar_prefetch=10,
            grid=(n_items,),
            in_specs=[tok_spec(0), tok_spec(1), any_spec, any_spec, any_spec],
            out_specs=pl.BlockSpec((MOE_ROWS, UP_TILE),
                                   lambda w, ib, nv, b, ot, *_: (b[w], ot[w])),
            scratch_shapes=[pltpu.VMEM((2, MOE_ROWS * GATHER_PITCH, LANES), jnp.uint32),
                            pltpu.VMEM((WEIGHT_RING, D_MODEL, UP_TILE), F32),
                            pltpu.VMEM((WEIGHT_RING, D_MODEL, UP_TILE), F32),
                            pltpu.VMEM((D_MODEL, UP_TILE), BF16),
                            pltpu.VMEM((D_MODEL, UP_TILE), BF16),
                            pltpu.SemaphoreType.DMA((2,)),
                            pltpu.SemaphoreType.DMA((2, WEIGHT_RING))]),
        out_shape=jax.ShapeDtypeStruct((N_MOE_ROWS, D_FF), BF16),
        compiler_params=_params(1),
        name="moe_gate_up",
    )(in_blk, n_valid[in_blk], *items[1:], toks, toks, hn_words, w_gate, w_up)


def _moe_down_kernel(ib_ref, b_ref, ot_ref, valid_ref, first_ref, seq_ref, te_ref, tt_ref, nt_ref,
                     hid_ref, wd_hbm, y_ref, ring, wd_bf, sem):
    w = pl.program_id(0)

    def tile_copies(k, slot):
        col = pl.multiple_of(tt_ref[k] * DOWN_TILE, DOWN_TILE)
        return (pltpu.make_async_copy(wd_hbm.at[te_ref[k], :, pl.ds(col, DOWN_TILE)],
                                      ring.at[slot], sem.at[slot]),)

    def consume(slot):
        wd_bf[...] = ring[slot].astype(BF16)

    _stream_weight_tile(w, first_ref, seq_ref, nt_ref, tile_copies, consume)

    @pl.when(valid_ref[w] == 1)
    def _():
        y = jnp.dot(hid_ref[...], wd_bf[...], preferred_element_type=F32)
        words = _pack_rows(y)
        for s in range(DOWN_TILE // 2 // LANES):
            y_ref[:, s, :] = words[:, s * LANES:(s + 1) * LANES]

    @pl.when(valid_ref[w] == 0)
    def _():
        y_ref[...] = jnp.zeros_like(y_ref)


def _moe_down(items, hid, w_down):
    n_items = N_MOE_BLOCKS * (D_MODEL // DOWN_TILE)
    tile_rows = DOWN_TILE // 2 // LANES
    return pl.pallas_call(
        _moe_down_kernel,
        grid_spec=pltpu.PrefetchScalarGridSpec(
            num_scalar_prefetch=9,
            grid=(n_items,),
            in_specs=[pl.BlockSpec((MOE_ROWS, D_FF), lambda w, ib, *_: (ib[w], 0)),
                      pl.BlockSpec(memory_space=pl.ANY)],
            out_specs=pl.BlockSpec((MOE_ROWS, tile_rows, LANES),
                                   lambda w, ib, b, ot, *_: (b[w], ot[w], 0)),
            scratch_shapes=[pltpu.VMEM((WEIGHT_RING, D_FF, DOWN_TILE), F32),
                            pltpu.VMEM((D_FF, DOWN_TILE), BF16),
                            pltpu.SemaphoreType.DMA((WEIGHT_RING,))]),
        out_shape=jax.ShapeDtypeStruct((N_MOE_ROWS, Y_ROW_TILES, LANES), jnp.uint32),
        compiler_params=_params(1),
        name="moe_down",
    )(*items, hid, w_down)


def _combine_kernel(dest_ref, dest_next_ref, h_ref, gate_ref, y_hbm, o_ref, buf, sem):
    i = pl.program_id(0)
    n_assign = COMBINE_ROWS * TOP_K

    def copy(slot, a, src_row):
        dst = buf.at[slot, a % TOP_K, pl.ds((a // TOP_K) * COMBINE_PITCH, Y_ROW_TILES)]
        return pltpu.make_async_copy(y_hbm.at[src_row], dst, sem.at[slot])

    def issue(dests, slot):
        def start(a, carry):
            copy(slot, a, dests[0, 0, a]).start()
            return carry
        lax.fori_loop(0, n_assign, start, 0, unroll=8)

    @pl.when(i == 0)
    def _():
        issue(dest_ref, 0)

    @pl.when(i + 1 < pl.num_programs(0))
    def _():
        issue(dest_next_ref, (i + 1) % 2)

    slot = i % 2

    for k in range(TOP_K):
        for c in range(COMBINE_ROWS // WAIT_ROWS):
            span = buf.at[slot, k, pl.ds(c * WAIT_ROWS * Y_ROW_TILES, WAIT_ROWS * Y_ROW_TILES)]
            pltpu.make_async_copy(span, span, sem.at[slot]).wait()

    g0 = gate_ref[:, 0:1]
    g1 = gate_ref[:, 1:2]
    tiles_per_half = DOWN_TILE // 2 // LANES
    for s in range(Y_ROW_TILES):
        lo_col = (s // tiles_per_half) * DOWN_TILE + (s % tiles_per_half) * LANES
        lo0, hi0 = _unpack_words(buf[slot, 0, pl.ds(s, COMBINE_ROWS, stride=COMBINE_PITCH), :])
        lo1, hi1 = _unpack_words(buf[slot, 1, pl.ds(s, COMBINE_ROWS, stride=COMBINE_PITCH), :])
        for col, y0, y1 in ((lo_col, lo0, lo1), (lo_col + DOWN_TILE // 2, hi0, hi1)):
            cols = slice(col, col + LANES)
            o_ref[:, cols] = h_ref[:, cols] + (y0 * g0 + y1 * g1)


def _combine(dest, h, gates, y):
    n_tiles = SEQ // COMBINE_ROWS
    per_tile = COMBINE_ROWS * TOP_K
    dests = dest.reshape(n_tiles, 1, per_tile)
    return pl.pallas_call(
        _combine_kernel,
        grid=(n_tiles,),
        in_specs=[pl.BlockSpec((1, 1, per_tile), lambda i: (i, 0, 0), memory_space=pltpu.SMEM),
                  pl.BlockSpec((1, 1, per_tile), lambda i: (jnp.minimum(i + 1, n_tiles - 1), 0, 0),
                               memory_space=pltpu.SMEM),
                  pl.BlockSpec((COMBINE_ROWS, D_MODEL), lambda i: (i, 0)),
                  pl.BlockSpec((COMBINE_ROWS, LANES), lambda i: (i, 0)),
                  pl.BlockSpec(memory_space=pl.ANY)],
        out_specs=pl.BlockSpec((COMBINE_ROWS, D_MODEL), lambda i: (i, 0)),
        out_shape=jax.ShapeDtypeStruct((SEQ, D_MODEL), F32),
        scratch_shapes=[pltpu.VMEM((2, TOP_K, COMBINE_ROWS * COMBINE_PITCH, LANES), jnp.uint32),
                        pltpu.SemaphoreType.DMA((2,))],
        compiler_params=_params(1),
        name="moe_combine",
    )(dests, dests, h, gates, y)


def _work_items(n_blocks_e, block_start_e, tiles):
    i32 = lambda a: a.astype(jnp.int32)
    n_items = N_MOE_BLOCKS * tiles
    per_e = n_blocks_e * tiles
    ends = jnp.cumsum(per_e)
    total = ends[-1]
    idx = jnp.arange(n_items, dtype=jnp.int32)
    valid = idx < total
    w = jnp.minimum(idx, total - 1)
    e = i32(jnp.searchsorted(ends, w, side='right'))
    local = w - (ends - per_e)[e]
    nb = n_blocks_e[e]
    spare = idx - total
    w_tile = local // nb
    o_tile = jnp.where(valid, w_tile, spare % tiles)
    blk = jnp.where(valid, block_start_e[e] + local % nb, total // tiles + spare // tiles)
    first = valid & (local % nb == 0)
    active = n_blocks_e > 0
    active_rank = jnp.cumsum(active) - active
    seq = active_rank[e] * tiles + w_tile
    active_experts = jnp.argsort(~active, stable=True)
    k = jnp.arange(N_EXPERTS * tiles, dtype=jnp.int32)
    tile_e = active_experts[k // tiles]
    tile_t = k % tiles
    n_tiles = (jnp.sum(active) * tiles).reshape(1)
    in_blk = block_start_e[e] + local % nb
    return (i32(in_blk), i32(blk), i32(o_tile), i32(valid), i32(first), i32(seq), i32(tile_e),
            i32(tile_t), i32(n_tiles))


def _dispatch_plan(ids):
    expert = ids[:, :TOP_K].reshape(-1)
    token = jnp.repeat(jnp.arange(SEQ, dtype=jnp.int32), TOP_K)
    onehot = (expert[:, None] == jnp.arange(N_EXPERTS, dtype=jnp.int32)[None, :]).astype(jnp.int32)
    running = jnp.cumsum(onehot, axis=0)
    rank = jnp.take_along_axis(running, expert[:, None], axis=1)[:, 0] - 1
    counts = running[-1]
    n_blocks_e = (counts + MOE_ROWS - 1) // MOE_ROWS
    block_ends = jnp.cumsum(n_blocks_e)
    block_start_e = block_ends - n_blocks_e
    dest = (block_start_e[expert] * MOE_ROWS + rank).astype(jnp.int32)
    row_tok = jnp.zeros((N_MOE_ROWS,), jnp.int32).at[dest].set(token)
    blk = jnp.arange(N_MOE_BLOCKS, dtype=jnp.int32)
    blk_e = jnp.minimum(jnp.searchsorted(block_ends, blk, side='right'), N_EXPERTS - 1)
    n_valid = jnp.clip(counts[blk_e] - (blk - block_start_e[blk_e]) * MOE_ROWS, 0, MOE_ROWS)
    n_valid = jnp.where(blk < block_ends[-1], n_valid, 0).astype(jnp.int32)
    return dest, row_tok, n_valid, n_blocks_e.astype(jnp.int32), block_start_e.astype(jnp.int32)


def kernel(x, norm_mix_w, w_in, q_norm_w, k_norm_w, conv_w, conv_b, conv_ln_w, conv_ln_b, w_out,
           norm_ffn_w, w_group_router, b_group_router, w_expert_router, b_expert_router,
           w_gate, w_up, w_down):
    h = x.reshape(SEQ, D_MODEL)
    slopes = jnp.exp2(-8.0 * jnp.arange(1, N_HEADS + 1, dtype=F32) / N_HEADS)
    for l in range(norm_mix_w.shape[0]):
        hn = _rmsnorm(h, norm_mix_w[l][None, :])
        qk_norm = jnp.concatenate([jnp.tile(q_norm_w[l], N_HEADS), jnp.tile(k_norm_w[l], N_HEADS)])
        z = _inproj(hn, w_in[l], qk_norm[None, :])
        attn = _attention(z, slopes)
        n_ct = CONV_WIDTH // LANES
        conv = _conformer_conv(z, conv_w[l].reshape(CONV_KERNEL, n_ct, LANES),
                               conv_b[l].reshape(n_ct, LANES), conv_ln_w[l][None, :],
                               conv_ln_b[l][None, :])
        h = _outproj(attn, conv, w_out[l], h)

        pad = LANES - N_GROUPS - N_EXPERTS
        w_router = jnp.concatenate([w_group_router[l], w_expert_router[l],
                                    jnp.zeros((D_MODEL, pad), F32)], axis=1)
        b_router = jnp.concatenate([b_group_router[l], b_expert_router[l], jnp.zeros((pad,), F32)])
        w_router_hi = w_router.astype(BF16)
        w_router_lo = (w_router - w_router_hi.astype(F32)).astype(BF16)
        hn_words, ids, gates = _router(h, norm_ffn_w[l][None, :],
                                       jnp.concatenate([w_router_hi, w_router_lo], axis=1),
                                       b_router[None, :])
        dest, row_tok, n_valid, n_blocks_e, block_start_e = _dispatch_plan(ids)
        hid = _moe_up(_work_items(n_blocks_e, block_start_e, D_FF // UP_TILE), n_valid, row_tok,
                      hn_words, w_gate[l], w_up[l])
        y = _moe_down(_work_items(n_blocks_e, block_start_e, D_MODEL // DOWN_TILE), hid, w_down[l])
        h = _combine(dest, h, gates, y)
    return h.reshape(x.shape)
```

```python
import math

import jax
import jax.numpy as jnp
from jax import lax
from jax.experimental import pallas as pl
from jax.experimental.pallas import tpu as pltpu

F32 = jnp.float32
BF16 = jnp.bfloat16

D_MODEL = 4096
SEQ = 8192
HEAD_DIM = 128
N_HEADS = 16
ATTN_WIDTH = N_HEADS * HEAD_DIM
CONV_WIDTH = D_MODEL - ATTN_WIDTH
CONV_KERNEL = 31
IN_COLS = 3 * ATTN_WIDTH + 2 * CONV_WIDTH
DILATIONS = (1, 4, 16)
ATTN_BLOCK = 128
N_KEYS = 128
N_GROUPS = 8
EXPERTS_PER_GROUP = 8
N_EXPERTS = N_GROUPS * EXPERTS_PER_GROUP
TOP_K = 2
D_FF = 1024
EPS = 1e-6
NEG_INF = -1e30

LANES = 128
VMEM_LIMIT = 56 * 1024 * 1024

SPAN = DILATIONS[-1] * ATTN_BLOCK

MOE_ROWS = 256
N_ASSIGN = SEQ * TOP_K
N_MOE_ROWS = N_ASSIGN + N_EXPERTS * MOE_ROWS
N_MOE_BLOCKS = N_MOE_ROWS // MOE_ROWS
UP_TILE = 512
DOWN_TILE = 4096
WEIGHT_RING = 2
WEIGHT_DMA_PRIORITY = 1
ROUTER_ROWS = 256
COMBINE_ROWS = 128
X_CACHE_SLOTS = 2
X_ROW_TILES = D_MODEL // 2 // LANES
Y_ROW_TILES = D_MODEL // 2 // LANES
COMBINE_PITCH = Y_ROW_TILES + 4
GATHER_PITCH = X_ROW_TILES + 8
DMA_UNROLL = 8
WAIT_ROWS = 32


def _params(n_axes):
    return pltpu.CompilerParams(dimension_semantics=("arbitrary",) * n_axes,
                                vmem_limit_bytes=VMEM_LIMIT)


def _pack_rows(x):
    half = x.shape[1] // 2
    lo = pltpu.bitcast(x[:, :half].astype(BF16).astype(F32), jnp.uint32)
    hi = pltpu.bitcast(x[:, half:].astype(BF16).astype(F32), jnp.uint32)
    return (hi & jnp.uint32(0xFFFF0000)) | (lo >> 16)


def _unpack_words(words):
    return (pltpu.bitcast(words << 16, F32),
            pltpu.bitcast(words & jnp.uint32(0xFFFF0000), F32))


def _rmsnorm_kernel(x_ref, w_ref, o_ref):
    x = x_ref[...]
    ms = jnp.mean(x * x, axis=-1, keepdims=True)
    o_ref[...] = (x * lax.rsqrt(ms + EPS) * w_ref[...]).astype(o_ref.dtype)


def _rmsnorm(x, w, rows=256):
    n, d = x.shape
    return pl.pallas_call(
        _rmsnorm_kernel,
        grid=(n // rows,),
        in_specs=[pl.BlockSpec((rows, d), lambda i: (i, 0)),
                  pl.BlockSpec((1, d), lambda i: (0, 0))],
        out_specs=pl.BlockSpec((rows, d), lambda i: (i, 0)),
        out_shape=jax.ShapeDtypeStruct((n, d), BF16),
        compiler_params=_params(1),
        name="rmsnorm",
    )(x, w)


IN_TM = 1024
IN_TN = 512
IN_CHUNK = 256
QK_TILES = 2 * ATTN_WIDTH // IN_TN


def _inproj_kernel(hn_ref, w_ref, nw_ref, o_ref, w_bf):
    j = pl.program_id(0)

    @pl.when(pl.program_id(1) == 0)
    def _():
        w_bf[...] = w_ref[...].astype(BF16)

    is_qk = j < QK_TILES
    for c in range(IN_TM // IN_CHUNK):
        rows = slice(c * IN_CHUNK, (c + 1) * IN_CHUNK)
        acc = jnp.dot(hn_ref[rows, :], w_bf[...], preferred_element_type=F32)
        for hd in range(IN_TN // HEAD_DIM):
            sl = slice(hd * HEAD_DIM, (hd + 1) * HEAD_DIM)
            t = acc[:, sl]
            ms = jnp.mean(t * t, axis=-1, keepdims=True)
            normed = t * lax.rsqrt(ms + EPS) * nw_ref[:, sl]
            o_ref[rows, sl] = jnp.where(is_qk, normed, t).astype(o_ref.dtype)


def _inproj(hn, w, qk_norm_w):
    n = hn.shape[0]
    return pl.pallas_call(
        _inproj_kernel,
        grid=(IN_COLS // IN_TN, n // IN_TM),
        in_specs=[pl.BlockSpec((IN_TM, D_MODEL), lambda j, i: (i, 0)),
                  pl.BlockSpec((D_MODEL, IN_TN), lambda j, i: (0, j)),
                  pl.BlockSpec((1, IN_TN), lambda j, i: (0, jnp.minimum(j, QK_TILES - 1)))],
        out_specs=pl.BlockSpec((IN_TM, IN_TN), lambda j, i: (i, j)),
        out_shape=jax.ShapeDtypeStruct((n, IN_COLS), BF16),
        scratch_shapes=[pltpu.VMEM((D_MODEL, IN_TN), BF16)],
        compiler_params=_params(2),
        name="inproj",
    )(hn, w, qk_norm_w)


def _attn_kernel(slopes_ref, q_ref, kc_ref, kp_ref, vc_ref, vp_ref, o_ref, qf, kf, vf, ob, lb):
    h = pl.program_id(0)
    first_span = pl.program_id(1) == 0
    slope = slopes_ref[h]

    qi = lax.broadcasted_iota(jnp.int32, (ATTN_BLOCK, 2 * ATTN_BLOCK), 0)
    kj = lax.broadcasted_iota(jnp.int32, (ATTN_BLOCK, 2 * ATTN_BLOCK), 1)
    steps = qi + ATTN_BLOCK - kj
    valid = (steps >= 0) & (steps <= N_KEYS)
    stepsf = steps.astype(F32)
    scale = 1.0 / math.sqrt(HEAD_DIM)

    def attend(b, d, bias, q0, q, k, v, no_prev_block):
        s = lax.dot_general(q, k, (((1,), (1,)), ((), ())),
                            preferred_element_type=F32) * scale + bias
        if no_prev_block is not None:
            s = jnp.where(kj >= jnp.where(no_prev_block, ATTN_BLOCK, 0), s, NEG_INF)
        m = jnp.max(s, axis=-1, keepdims=True)
        p = jnp.exp(s - m)
        den = jnp.sum(p, axis=-1, keepdims=True)
        o = jnp.dot(p.astype(BF16), v, preferred_element_type=F32) / den
        lse = m + jnp.log(den)
        ob[b, pl.ds(q0, ATTN_BLOCK, stride=d), :] = o
        lb[b, pl.ds(q0, ATTN_BLOCK, stride=d), :] = jnp.broadcast_to(lse, (ATTN_BLOCK, HEAD_DIM))

    def banded_bias(d):
        return jnp.where(valid, (-slope * float(d)) * stepsf, NEG_INF)

    def key_window(cur, prev, n):
        if n > 0:
            return cur[(n - 1) * ATTN_BLOCK:(n + 1) * ATTN_BLOCK, :]
        last = prev.shape[0] - ATTN_BLOCK
        return jnp.concatenate([prev[last:, :], cur[:ATTN_BLOCK, :]], axis=0)

    b, d = 0, DILATIONS[0]
    bias = banded_bias(d)
    for n in range(SPAN // ATTN_BLOCK):
        attend(b, d, bias, n * ATTN_BLOCK, q_ref[n * ATTN_BLOCK:(n + 1) * ATTN_BLOCK, :],
               key_window(kc_ref, kp_ref, n), key_window(vc_ref, vp_ref, n),
               first_span if n == 0 else None)

    qf[...] = q_ref[...].astype(F32)
    kf[0:SPAN, :] = kp_ref[...].astype(F32)
    kf[SPAN:2 * SPAN, :] = kc_ref[...].astype(F32)
    vf[0:SPAN, :] = vp_ref[...].astype(F32)
    vf[SPAN:2 * SPAN, :] = vc_ref[...].astype(F32)
    for b, d in list(enumerate(DILATIONS))[1:]:
        bias = banded_bias(d)
        for r in range(d):
            for n in range(SPAN // (ATTN_BLOCK * d)):
                q0 = r + d * ATTN_BLOCK * n
                k0 = SPAN + q0 - d * ATTN_BLOCK
                attend(b, d, bias, q0,
                       qf[pl.ds(q0, ATTN_BLOCK, stride=d), :].astype(BF16),
                       kf[pl.ds(k0, 2 * ATTN_BLOCK, stride=d), :].astype(BF16),
                       vf[pl.ds(k0, 2 * ATTN_BLOCK, stride=d), :].astype(BF16),
                       first_span if n == 0 else None)

    chunk = 256
    for c in range(SPAN // chunk):
        rows = slice(c * chunk, (c + 1) * chunk)
        l0, l1, l2 = lb[0, rows, :], lb[1, rows, :], lb[2, rows, :]
        mx = jnp.maximum(jnp.maximum(l0, l1), l2)
        w0, w1, w2 = jnp.exp(l0 - mx), jnp.exp(l1 - mx), jnp.exp(l2 - mx)
        num = w0 * ob[0, rows, :] + w1 * ob[1, rows, :] + w2 * ob[2, rows, :]
        o_ref[rows, :] = (num / (w0 + w1 + w2)).astype(o_ref.dtype)


def _attention(z, slopes):
    n_spans = SEQ // SPAN
    k_col, v_col = N_HEADS, 2 * N_HEADS
    blk = (SPAN, HEAD_DIM)
    prev = lambda s: jnp.maximum(s - 1, 0)
    return pl.pallas_call(
        _attn_kernel,
        grid_spec=pltpu.PrefetchScalarGridSpec(
            num_scalar_prefetch=1,
            grid=(N_HEADS, n_spans),
            in_specs=[pl.BlockSpec(blk, lambda h, s, sl: (s, h)),
                      pl.BlockSpec(blk, lambda h, s, sl: (s, k_col + h)),
                      pl.BlockSpec(blk, lambda h, s, sl: (prev(s), k_col + h)),
                      pl.BlockSpec(blk, lambda h, s, sl: (s, v_col + h)),
                      pl.BlockSpec(blk, lambda h, s, sl: (prev(s), v_col + h))],
            out_specs=pl.BlockSpec(blk, lambda h, s, sl: (s, h)),
            scratch_shapes=[pltpu.VMEM((SPAN, HEAD_DIM), F32),
                            pltpu.VMEM((2 * SPAN, HEAD_DIM), F32),
                            pltpu.VMEM((2 * SPAN, HEAD_DIM), F32),
                            pltpu.VMEM((3, SPAN, HEAD_DIM), F32),
                            pltpu.VMEM((3, SPAN, HEAD_DIM), F32)]),
        out_shape=jax.ShapeDtypeStruct((SEQ, ATTN_WIDTH), BF16),
        compiler_params=_params(2),
        name="dilated_attention",
    )(slopes, z, z, z, z, z)


CONV_ROWS = 256
CONV_HALO = 32
CONV_CHUNK = 64


SUBLANES = 8
CONV_EXT = CONV_HALO + CONV_ROWS
CONV_SHIFTED = CONV_EXT - SUBLANES


def _conv_kernel(a_ref, g_ref, ah_ref, gh_ref, cw_ref, cb_ref, lw_ref, lb_ref, o_ref,
                 u_ext, shifted, c_scr):
    i = pl.program_id(0)
    n_ct = CONV_WIDTH // LANES
    for ct in range(n_ct):
        cols = slice(ct * LANES, (ct + 1) * LANES)
        u_ext[ct, CONV_HALO:, :] = (a_ref[:, cols].astype(F32)
                                    * jax.nn.sigmoid(g_ref[:, cols].astype(F32)))
        uh = ah_ref[:, cols].astype(F32) * jax.nn.sigmoid(gh_ref[:, cols].astype(F32))
        u_ext[ct, 0:CONV_HALO, :] = jnp.where(i == 0, 0.0, uh)

    first_tap = CONV_HALO - (CONV_KERNEL - 1)

    def slab(ct, carry):
        for k in range(1, SUBLANES):
            shifted[k - 1] = u_ext[ct, k:k + CONV_SHIFTED, :]
        for rc in range(CONV_ROWS // CONV_CHUNK):
            base = rc * CONV_CHUNK
            acc = jnp.broadcast_to(cb_ref[pl.ds(ct, 1), :], (CONV_CHUNK, LANES))
            for j in range(CONV_KERNEL):
                k = (first_tap + j) % SUBLANES
                row = base + (first_tap + j) - k
                if k == 0:
                    src = u_ext[ct, row:row + CONV_CHUNK, :]
                else:
                    src = shifted[k - 1, row:row + CONV_CHUNK, :]
                acc = acc + cw_ref[j, pl.ds(ct, 1), :] * src
            c_scr[ct, base:base + CONV_CHUNK, :] = acc
        return carry

    lax.fori_loop(0, n_ct, slab, 0)

    total = c_scr[0]
    for ct in range(1, n_ct):
        total = total + c_scr[ct]
    mu = jnp.sum(total, axis=-1, keepdims=True) * (1.0 / CONV_WIDTH)
    sq = jnp.square(c_scr[0] - mu)
    for ct in range(1, n_ct):
        sq = sq + jnp.square(c_scr[ct] - mu)
    var = jnp.sum(sq, axis=-1, keepdims=True) * (1.0 / CONV_WIDTH)
    rstd = lax.rsqrt(var + EPS)
    for ct in range(n_ct):
        cols = slice(ct * LANES, (ct + 1) * LANES)
        y = (c_scr[ct] - mu) * rstd * lw_ref[:, cols] + lb_ref[:, cols]
        o_ref[:, cols] = (y * jax.nn.sigmoid(y)).astype(o_ref.dtype)


def _conformer_conv(z, conv_w, conv_b, ln_w, ln_b):
    a_col = 3 * ATTN_WIDTH // CONV_WIDTH
    g_col = a_col + 1
    n_ct = CONV_WIDTH // LANES
    halo_blocks = CONV_ROWS // CONV_HALO
    halo = lambda i: jnp.maximum(i * halo_blocks - 1, 0)
    vec = pl.BlockSpec((1, CONV_WIDTH), lambda i: (0, 0))
    return pl.pallas_call(
        _conv_kernel,
        grid=(SEQ // CONV_ROWS,),
        in_specs=[pl.BlockSpec((CONV_ROWS, CONV_WIDTH), lambda i: (i, a_col)),
                  pl.BlockSpec((CONV_ROWS, CONV_WIDTH), lambda i: (i, g_col)),
                  pl.BlockSpec((CONV_HALO, CONV_WIDTH), lambda i: (halo(i), a_col)),
                  pl.BlockSpec((CONV_HALO, CONV_WIDTH), lambda i: (halo(i), g_col)),
                  pl.BlockSpec((CONV_KERNEL, n_ct, LANES), lambda i: (0, 0, 0)),
                  pl.BlockSpec((n_ct, LANES), lambda i: (0, 0)),
                  vec, vec],
        out_specs=pl.BlockSpec((CONV_ROWS, CONV_WIDTH), lambda i: (i, 0)),
        out_shape=jax.ShapeDtypeStruct((SEQ, CONV_WIDTH), BF16),
        scratch_shapes=[pltpu.VMEM((n_ct, CONV_EXT, LANES), F32),
                        pltpu.VMEM((SUBLANES - 1, CONV_SHIFTED, LANES), F32),
                        pltpu.VMEM((n_ct, CONV_ROWS, LANES), F32)],
        compiler_params=_params(1),
        name="conformer_conv",
    )(z, z, z, z, conv_w, conv_b, ln_w, ln_b)


OUT_TM = 1024
OUT_TN = 512


def _outproj_kernel(attn_ref, conv_ref, wa_ref, wc_ref, x_ref, o_ref, wa_bf, wc_bf):
    @pl.when(pl.program_id(1) == 0)
    def _():
        wa_bf[...] = wa_ref[...].astype(BF16)
        wc_bf[...] = wc_ref[...].astype(BF16)

    acc = jnp.dot(attn_ref[...], wa_bf[...], preferred_element_type=F32)
    acc = acc + jnp.dot(conv_ref[...], wc_bf[...], preferred_element_type=F32)
    o_ref[...] = x_ref[...] + acc


def _outproj(attn, conv, w, x):
    n = x.shape[0]
    conv_blk = ATTN_WIDTH // CONV_WIDTH
    return pl.pallas_call(
        _outproj_kernel,
        grid=(D_MODEL // OUT_TN, n // OUT_TM),
        in_specs=[pl.BlockSpec((OUT_TM, ATTN_WIDTH), lambda j, i: (i, 0)),
                  pl.BlockSpec((OUT_TM, CONV_WIDTH), lambda j, i: (i, 0)),
                  pl.BlockSpec((ATTN_WIDTH, OUT_TN), lambda j, i: (0, j)),
                  pl.BlockSpec((CONV_WIDTH, OUT_TN), lambda j, i: (conv_blk, j)),
                  pl.BlockSpec((OUT_TM, OUT_TN), lambda j, i: (i, j))],
        out_specs=pl.BlockSpec((OUT_TM, OUT_TN), lambda j, i: (i, j)),
        out_shape=jax.ShapeDtypeStruct((n, D_MODEL), F32),
        scratch_shapes=[pltpu.VMEM((ATTN_WIDTH, OUT_TN), BF16),
                        pltpu.VMEM((CONV_WIDTH, OUT_TN), BF16)],
        compiler_params=_params(2),
        name="outproj",
    )(attn, conv, w, w, x)


def _router_kernel(h_ref, nw_ref, wr_ref, br_ref, hn_ref, ids_ref, gates_ref):
    x = h_ref[...]
    ms = jnp.mean(x * x, axis=-1, keepdims=True)
    hn = x * lax.rsqrt(ms + EPS) * nw_ref[...]
    words = _pack_rows(hn)
    for s in range(X_ROW_TILES):
        hn_ref[:, s, :] = words[:, s * LANES:(s + 1) * LANES]

    h_hi = hn.astype(BF16)
    h_lo = (hn - h_hi.astype(F32)).astype(BF16)
    hi_both = jnp.dot(h_hi, wr_ref[...], preferred_element_type=F32)
    lo_hi = jnp.dot(h_lo, wr_ref[:, :LANES], preferred_element_type=F32)
    logits = hi_both[:, :LANES] + (hi_both[:, LANES:] + lo_hi) + br_ref[...]
    lane = lax.broadcasted_iota(jnp.int32, logits.shape, 1)
    big = jnp.int32(1 << 20)

    is_g = lane < N_GROUPS
    gl = jnp.where(is_g, logits, -jnp.inf)
    ge = jnp.exp(gl - jnp.max(gl, axis=-1, keepdims=True))
    g_prob = ge / jnp.sum(ge, axis=-1, keepdims=True)
    g_w = jnp.max(g_prob, axis=-1, keepdims=True)
    g_top = jnp.min(jnp.where(is_g & (g_prob == g_w), lane, big), axis=-1, keepdims=True)

    eidx = lane - N_GROUPS
    grp_shift = EXPERTS_PER_GROUP.bit_length() - 1
    in_grp = (eidx >= 0) & (eidx < N_EXPERTS) & ((eidx >> grp_shift) == g_top)
    el = jnp.where(in_grp, logits, -jnp.inf)
    ee = jnp.exp(el - jnp.max(el, axis=-1, keepdims=True))
    e_prob = ee / jnp.sum(ee, axis=-1, keepdims=True)
    v1 = jnp.max(jnp.where(in_grp, e_prob, -1.0), axis=-1, keepdims=True)
    i1 = jnp.min(jnp.where(in_grp & (e_prob == v1), lane, big), axis=-1, keepdims=True)
    rest = in_grp & (lane != i1)
    v2 = jnp.max(jnp.where(rest, e_prob, -1.0), axis=-1, keepdims=True)
    i2 = jnp.min(jnp.where(rest & (e_prob == v2), lane, big), axis=-1, keepdims=True)
    tot = v1 + v2
    ids_ref[...] = jnp.where(lane == 0, i1 - N_GROUPS, jnp.where(lane == 1, i2 - N_GROUPS, 0))
    gates_ref[...] = jnp.where(lane == 0, g_w * v1 / tot,
                               jnp.where(lane == 1, g_w * v2 / tot, 0.0))


def _router(h, norm_w, w_router, b_router):
    return pl.pallas_call(
        _router_kernel,
        grid=(SEQ // ROUTER_ROWS,),
        in_specs=[pl.BlockSpec((ROUTER_ROWS, D_MODEL), lambda i: (i, 0)),
                  pl.BlockSpec((1, D_MODEL), lambda i: (0, 0)),
                  pl.BlockSpec((D_MODEL, 2 * LANES), lambda i: (0, 0)),
                  pl.BlockSpec((1, LANES), lambda i: (0, 0))],
        out_specs=[pl.BlockSpec((ROUTER_ROWS, X_ROW_TILES, LANES), lambda i: (i, 0, 0)),
                   pl.BlockSpec((ROUTER_ROWS, LANES), lambda i: (i, 0)),
                   pl.BlockSpec((ROUTER_ROWS, LANES), lambda i: (i, 0))],
        out_shape=[jax.ShapeDtypeStruct((SEQ, X_ROW_TILES, LANES), jnp.uint32),
                   jax.ShapeDtypeStruct((SEQ, LANES), jnp.int32),
                   jax.ShapeDtypeStruct((SEQ, LANES), F32)],
        compiler_params=_params(1),
        name="ffn_norm_router",
    )(h, norm_w, w_router, b_router)


def _staged_token(stage, slot, r):
    return stage.at[slot, pl.ds(pl.multiple_of(r * GATHER_PITCH, SUBLANES), X_ROW_TILES)]


def _issue_token_rows(toks, nv, hn_hbm, stage, slot, sem):
    def start(r, carry):
        pltpu.make_async_copy(hn_hbm.at[toks[0, 0, r]], _staged_token(stage, slot, r),
                              sem.at[slot]).start()
        return carry

    def start_group(g, carry):
        for u in range(DMA_UNROLL):
            start(g * DMA_UNROLL + u, carry)
        return carry

    def zero(r, carry):
        _staged_token(stage, slot, r)[...] = jnp.zeros((X_ROW_TILES, LANES), stage.dtype)
        return carry

    groups = nv // DMA_UNROLL
    lax.fori_loop(0, groups, start_group, 0)
    lax.fori_loop(groups * DMA_UNROLL, nv, start, 0)
    lax.fori_loop(nv, MOE_ROWS, zero, 0)


def _wait_token_rows(nv, stage, slot, sem):
    def wait_rows(n_tokens):
        span = stage.at[slot, pl.ds(0, n_tokens * X_ROW_TILES)]
        pltpu.make_async_copy(span, span, sem.at[slot]).wait()

    def wait_many(c, carry):
        wait_rows(WAIT_ROWS)
        return carry

    def wait_one(c, carry):
        wait_rows(1)
        return carry

    lax.fori_loop(0, nv // WAIT_ROWS, wait_many, 0)
    lax.fori_loop(0, nv % WAIT_ROWS, wait_one, 0)


def _staged_block(stage, slot):
    return jnp.concatenate([stage[slot, pl.ds(s, MOE_ROWS, stride=GATHER_PITCH), :]
                            for s in range(X_ROW_TILES)], axis=-1)


def _unpack_rows(words):
    lo, hi = _unpack_words(words)
    return jnp.concatenate([lo.astype(BF16), hi.astype(BF16)], axis=-1)


def _stream_weight_tile(w, first_ref, seq_ref, n_tiles_ref, tile_copies, consume):
    n_tiles = n_tiles_ref[0]

    @pl.when(w == 0)
    def _():
        for k in range(WEIGHT_RING):
            @pl.when(k < n_tiles)
            def _():
                for c in tile_copies(k, k):
                    c.start(priority=WEIGHT_DMA_PRIORITY)

    @pl.when(first_ref[w] == 1)
    def _():
        k = seq_ref[w]
        slot = k % WEIGHT_RING
        for c in tile_copies(k, slot):
            c.wait()
        consume(slot)

        @pl.when(k + WEIGHT_RING < n_tiles)
        def _():
            for c in tile_copies(k + WEIGHT_RING, slot):
                c.start(priority=WEIGHT_DMA_PRIORITY)


def _moe_up_kernel(ib_ref, nv_ref, b_ref, ot_ref, valid_ref, first_ref, seq_ref, te_ref, tt_ref,
                   nt_ref, fetch_ref, xslot_ref, sslot_ref, pf_ref, pfb_ref, pfnv_ref,
                   tok_ref, tok_next_ref, hn_hbm, wg_hbm, wu_hbm, hid_ref,
                   stage, x_cache, ring_g, ring_u, wg_bf, wu_bf, row_sem, sem):
    w = pl.program_id(0)

    @pl.when(w == 0)
    def _():
        _issue_token_rows(tok_ref, nv_ref[0], hn_hbm, stage, 0, row_sem)

    @pl.when(pf_ref[w] == 1)
    def _():
        _issue_token_rows(tok_next_ref, pfnv_ref[w], hn_hbm, stage, 1 - sslot_ref[w], row_sem)

    def tile_copies(k, slot):
        e = te_ref[k]
        col = pl.multiple_of(tt_ref[k] * UP_TILE, UP_TILE)
        return (pltpu.make_async_copy(wg_hbm.at[e, :, pl.ds(col, UP_TILE)], ring_g.at[slot],
                                      sem.at[0, slot]),
                pltpu.make_async_copy(wu_hbm.at[e, :, pl.ds(col, UP_TILE)], ring_u.at[slot],
                                      sem.at[1, slot]))

    def consume(slot):
        wg_bf[...] = ring_g[slot].astype(BF16)
        wu_bf[...] = ring_u[slot].astype(BF16)

    _stream_weight_tile(w, first_ref, seq_ref, nt_ref, tile_copies, consume)

    @pl.when(valid_ref[w] == 1)
    def _():
        x_slot = xslot_ref[w]

        @pl.when(fetch_ref[w] == 1)
        def _():
            slot = sslot_ref[w]
            _wait_token_rows(nv_ref[w], stage, slot, row_sem)
            x_cache[x_slot] = _unpack_rows(_staged_block(stage, slot))

        x = x_cache[x_slot]
        g = jnp.dot(x, wg_bf[...], preferred_element_type=F32)
        u = jnp.dot(x, wu_bf[...], preferred_element_type=F32)
        hid_ref[...] = (g * jax.nn.sigmoid(g) * u).astype(hid_ref.dtype)

    @pl.when(valid_ref[w] == 0)
    def _():
        hid_ref[...] = jnp.zeros_like(hid_ref)


def _moe_up(items, info, n_valid, row_tok, hn_words, w_gate, w_up):
    n_items = N_MOE_BLOCKS * (D_FF // UP_TILE)
    in_blk = items[0]
    schedule = _gather_schedule(in_blk, info, n_valid)
    n_prefetch = 1 + len(items) + len(schedule)
    toks = row_tok.reshape(N_MOE_BLOCKS, 1, MOE_ROWS)
    smem_block = lambda index_map: pl.BlockSpec((1, 1, MOE_ROWS), index_map,
                                                memory_space=pltpu.SMEM)
    any_spec = pl.BlockSpec(memory_space=pl.ANY)
    return pl.pallas_call(
        _moe_up_kernel,
        grid_spec=pltpu.PrefetchScalarGridSpec(
            num_scalar_prefetch=n_prefetch,
            grid=(n_items,),
            in_specs=[smem_block(lambda w, *refs: (refs[0][w], 0, 0)),
                      smem_block(lambda w, *refs: (refs[n_prefetch - 2][w], 0, 0)),
                      any_spec, any_spec, any_spec],
            out_specs=pl.BlockSpec((MOE_ROWS, UP_TILE),
                                   lambda w, ib, nv, b, ot, *_: (b[w], ot[w])),
            scratch_shapes=[pltpu.VMEM((2, MOE_ROWS * GATHER_PITCH, LANES), jnp.uint32),
                            pltpu.VMEM((X_CACHE_SLOTS, MOE_ROWS, D_MODEL), BF16),
                            pltpu.VMEM((WEIGHT_RING, D_MODEL, UP_TILE), F32),
                            pltpu.VMEM((WEIGHT_RING, D_MODEL, UP_TILE), F32),
                            pltpu.VMEM((D_MODEL, UP_TILE), BF16),
                            pltpu.VMEM((D_MODEL, UP_TILE), BF16),
                            pltpu.SemaphoreType.DMA((2,)),
                            pltpu.SemaphoreType.DMA((2, WEIGHT_RING))]),
        out_shape=jax.ShapeDtypeStruct((N_MOE_ROWS, D_FF), BF16),
        compiler_params=_params(1),
        name="moe_gate_up",
    )(in_blk, n_valid[in_blk], *items[1:], *schedule, toks, toks, hn_words, w_gate, w_up)


def _moe_down_kernel(ib_ref, b_ref, ot_ref, valid_ref, first_ref, seq_ref, te_ref, tt_ref, nt_ref,
                     hid_ref, wd_hbm, y_ref, ring, wd_bf, sem):
    w = pl.program_id(0)

    def tile_copies(k, slot):
        col = pl.multiple_of(tt_ref[k] * DOWN_TILE, DOWN_TILE)
        return (pltpu.make_async_copy(wd_hbm.at[te_ref[k], :, pl.ds(col, DOWN_TILE)],
                                      ring.at[slot], sem.at[slot]),)

    def consume(slot):
        wd_bf[...] = ring[slot].astype(BF16)

    _stream_weight_tile(w, first_ref, seq_ref, nt_ref, tile_copies, consume)

    @pl.when(valid_ref[w] == 1)
    def _():
        y = jnp.dot(hid_ref[...], wd_bf[...], preferred_element_type=F32)
        words = _pack_rows(y)
        for s in range(DOWN_TILE // 2 // LANES):
            y_ref[:, s, :] = words[:, s * LANES:(s + 1) * LANES]

    @pl.when(valid_ref[w] == 0)
    def _():
        y_ref[...] = jnp.zeros_like(y_ref)


def _moe_down(items, hid, w_down):
    n_items = N_MOE_BLOCKS * (D_MODEL // DOWN_TILE)
    tile_rows = DOWN_TILE // 2 // LANES
    return pl.pallas_call(
        _moe_down_kernel,
        grid_spec=pltpu.PrefetchScalarGridSpec(
            num_scalar_prefetch=9,
            grid=(n_items,),
            in_specs=[pl.BlockSpec((MOE_ROWS, D_FF), lambda w, ib, *_: (ib[w], 0)),
                      pl.BlockSpec(memory_space=pl.ANY)],
            out_specs=pl.BlockSpec((MOE_ROWS, tile_rows, LANES),
                                   lambda w, ib, b, ot, *_: (b[w], ot[w], 0)),
            scratch_shapes=[pltpu.VMEM((WEIGHT_RING, D_FF, DOWN_TILE), F32),
                            pltpu.VMEM((D_FF, DOWN_TILE), BF16),
                            pltpu.SemaphoreType.DMA((WEIGHT_RING,))]),
        out_shape=jax.ShapeDtypeStruct((N_MOE_ROWS, Y_ROW_TILES, LANES), jnp.uint32),
        compiler_params=_params(1),
        name="moe_down",
    )(*items, hid, w_down)


def _combine_kernel(dest_ref, dest_next_ref, h_ref, gate_ref, y_hbm, o_ref, buf, sem):
    i = pl.program_id(0)
    n_assign = COMBINE_ROWS * TOP_K

    def copy(slot, a, src_row):
        dst = buf.at[slot, a % TOP_K, pl.ds((a // TOP_K) * COMBINE_PITCH, Y_ROW_TILES)]
        return pltpu.make_async_copy(y_hbm.at[src_row], dst, sem.at[slot])

    def issue(dests, slot):
        def start(a, carry):
            copy(slot, a, dests[0, 0, a]).start()
            return carry
        lax.fori_loop(0, n_assign, start, 0, unroll=8)

    @pl.when(i == 0)
    def _():
        issue(dest_ref, 0)

    @pl.when(i + 1 < pl.num_programs(0))
    def _():
        issue(dest_next_ref, (i + 1) % 2)

    slot = i % 2

    for k in range(TOP_K):
        for c in range(COMBINE_ROWS // WAIT_ROWS):
            span = buf.at[slot, k, pl.ds(c * WAIT_ROWS * Y_ROW_TILES, WAIT_ROWS * Y_ROW_TILES)]
            pltpu.make_async_copy(span, span, sem.at[slot]).wait()

    g0 = gate_ref[:, 0:1]
    g1 = gate_ref[:, 1:2]
    tiles_per_half = DOWN_TILE // 2 // LANES
    for s in range(Y_ROW_TILES):
        lo_col = (s // tiles_per_half) * DOWN_TILE + (s % tiles_per_half) * LANES
        lo0, hi0 = _unpack_words(buf[slot, 0, pl.ds(s, COMBINE_ROWS, stride=COMBINE_PITCH), :])
        lo1, hi1 = _unpack_words(buf[slot, 1, pl.ds(s, COMBINE_ROWS, stride=COMBINE_PITCH), :])
        for col, y0, y1 in ((lo_col, lo0, lo1), (lo_col + DOWN_TILE // 2, hi0, hi1)):
            cols = slice(col, col + LANES)
            o_ref[:, cols] = h_ref[:, cols] + (y0 * g0 + y1 * g1)


def _combine(dest, h, gates, y):
    n_tiles = SEQ // COMBINE_ROWS
    per_tile = COMBINE_ROWS * TOP_K
    dests = dest.reshape(n_tiles, 1, per_tile)
    return pl.pallas_call(
        _combine_kernel,
        grid=(n_tiles,),
        in_specs=[pl.BlockSpec((1, 1, per_tile), lambda i: (i, 0, 0), memory_space=pltpu.SMEM),
                  pl.BlockSpec((1, 1, per_tile), lambda i: (jnp.minimum(i + 1, n_tiles - 1), 0, 0),
                               memory_space=pltpu.SMEM),
                  pl.BlockSpec((COMBINE_ROWS, D_MODEL), lambda i: (i, 0)),
                  pl.BlockSpec((COMBINE_ROWS, LANES), lambda i: (i, 0)),
                  pl.BlockSpec(memory_space=pl.ANY)],
        out_specs=pl.BlockSpec((COMBINE_ROWS, D_MODEL), lambda i: (i, 0)),
        out_shape=jax.ShapeDtypeStruct((SEQ, D_MODEL), F32),
        scratch_shapes=[pltpu.VMEM((2, TOP_K, COMBINE_ROWS * COMBINE_PITCH, LANES), jnp.uint32),
                        pltpu.SemaphoreType.DMA((2,))],
        compiler_params=_params(1),
        name="moe_combine",
    )(dests, dests, h, gates, y)


def _work_items(n_blocks_e, block_start_e, tiles):
    i32 = lambda a: a.astype(jnp.int32)
    n_items = N_MOE_BLOCKS * tiles
    per_e = n_blocks_e * tiles
    ends = jnp.cumsum(per_e)
    total = ends[-1]
    idx = jnp.arange(n_items, dtype=jnp.int32)
    valid = idx < total
    w = jnp.minimum(idx, total - 1)
    e = i32(jnp.searchsorted(ends, w, side='right'))
    local = w - (ends - per_e)[e]
    nb = n_blocks_e[e]
    spare = idx - total
    w_tile = local // nb
    o_tile = jnp.where(valid, w_tile, spare % tiles)
    blk = jnp.where(valid, block_start_e[e] + local % nb, total // tiles + spare // tiles)
    first = valid & (local % nb == 0)
    active = n_blocks_e > 0
    active_rank = jnp.cumsum(active) - active
    seq = active_rank[e] * tiles + w_tile
    active_experts = jnp.argsort(~active, stable=True)
    k = jnp.arange(N_EXPERTS * tiles, dtype=jnp.int32)
    tile_e = active_experts[k // tiles]
    tile_t = k % tiles
    n_tiles = (jnp.sum(active) * tiles).reshape(1)
    in_blk = block_start_e[e] + local % nb
    items = (i32(in_blk), i32(blk), i32(o_tile), i32(valid), i32(first), i32(seq), i32(tile_e),
             i32(tile_t), i32(n_tiles))
    return items, dict(blocks_of_expert=nb, local_block=local % nb, weight_tile=w_tile, valid=valid)


def _gather_schedule(in_blk, info, n_valid):
    i32 = lambda a: a.astype(jnp.int32)
    n = in_blk.shape[0]
    cached = (info['blocks_of_expert'] <= X_CACHE_SLOTS) & (info['weight_tile'] > 0)
    fetch = info['valid'] & ~cached
    x_slot = info['local_block'] % X_CACHE_SLOTS
    stage_slot = (jnp.cumsum(fetch) - 1) % 2
    idx = jnp.arange(n, dtype=jnp.int32)
    later = lax.cummin(jnp.where(fetch, idx, n), axis=0, reverse=True)
    next_fetch = jnp.concatenate([later[1:], jnp.full((1,), n, later.dtype)])
    prefetch = fetch & (next_fetch < n)
    pf_blk = jnp.where(prefetch, in_blk[jnp.minimum(next_fetch, n - 1)], in_blk)
    return (i32(fetch), i32(x_slot), i32(stage_slot), i32(prefetch), i32(pf_blk),
            i32(n_valid[pf_blk]))


def _dispatch_plan(ids):
    expert = ids[:, :TOP_K].reshape(-1)
    token = jnp.repeat(jnp.arange(SEQ, dtype=jnp.int32), TOP_K)
    onehot = (expert[:, None] == jnp.arange(N_EXPERTS, dtype=jnp.int32)[None, :]).astype(jnp.int32)
    running = jnp.cumsum(onehot, axis=0)
    rank = jnp.take_along_axis(running, expert[:, None], axis=1)[:, 0] - 1
    counts = running[-1]
    n_blocks_e = (counts + MOE_ROWS - 1) // MOE_ROWS
    block_ends = jnp.cumsum(n_blocks_e)
    block_start_e = block_ends - n_blocks_e
    dest = (block_start_e[expert] * MOE_ROWS + rank).astype(jnp.int32)
    row_tok = jnp.zeros((N_MOE_ROWS,), jnp.int32).at[dest].set(token)
    blk = jnp.arange(N_MOE_BLOCKS, dtype=jnp.int32)
    blk_e = jnp.minimum(jnp.searchsorted(block_ends, blk, side='right'), N_EXPERTS - 1)
    n_valid = jnp.clip(counts[blk_e] - (blk - block_start_e[blk_e]) * MOE_ROWS, 0, MOE_ROWS)
    n_valid = jnp.where(blk < block_ends[-1], n_valid, 0).astype(jnp.int32)
    return dest, row_tok, n_valid, n_blocks_e.astype(jnp.int32), block_start_e.astype(jnp.int32)


def kernel(x, norm_mix_w, w_in, q_norm_w, k_norm_w, conv_w, conv_b, conv_ln_w, conv_ln_b, w_out,
           norm_ffn_w, w_group_router, b_group_router, w_expert_router, b_expert_router,
           w_gate, w_up, w_down):
    h = x.reshape(SEQ, D_MODEL)
    slopes = jnp.exp2(-8.0 * jnp.arange(1, N_HEADS + 1, dtype=F32) / N_HEADS)
    for l in range(norm_mix_w.shape[0]):
        hn = _rmsnorm(h, norm_mix_w[l][None, :])
        qk_norm = jnp.concatenate([jnp.tile(q_norm_w[l], N_HEADS), jnp.tile(k_norm_w[l], N_HEADS)])
        z = _inproj(hn, w_in[l], qk_norm[None, :])
        attn = _attention(z, slopes)
        n_ct = CONV_WIDTH // LANES
        conv = _conformer_conv(z, conv_w[l].reshape(CONV_KERNEL, n_ct, LANES),
                               conv_b[l].reshape(n_ct, LANES), conv_ln_w[l][None, :],
                               conv_ln_b[l][None, :])
        h = _outproj(attn, conv, w_out[l], h)

        pad = LANES - N_GROUPS - N_EXPERTS
        w_router = jnp.concatenate([w_group_router[l], w_expert_router[l],
                                    jnp.zeros((D_MODEL, pad), F32)], axis=1)
        b_router = jnp.concatenate([b_group_router[l], b_expert_router[l], jnp.zeros((pad,), F32)])
        w_router_hi = w_router.astype(BF16)
        w_router_lo = (w_router - w_router_hi.astype(F32)).astype(BF16)
        hn_words, ids, gates = _router(h, norm_ffn_w[l][None, :],
                                       jnp.concatenate([w_router_hi, w_router_lo], axis=1),
                                       b_router[None, :])
        dest, row_tok, n_valid, n_blocks_e, block_start_e = _dispatch_plan(ids)
        up_items, up_info = _work_items(n_blocks_e, block_start_e, D_FF // UP_TILE)
        hid = _moe_up(up_items, up_info, n_valid, row_tok, hn_words, w_gate[l], w_up[l])
        down_items, _ = _work_items(n_blocks_e, block_start_e, D_MODEL // DOWN_TILE)
        y = _moe_down(down_items, hid, w_down[l])
        h = _combine(dest, h, gates, y)
    return h.reshape(x.shape)
```

```python
import math

import jax
import jax.numpy as jnp
from jax import lax
from jax.experimental import pallas as pl
from jax.experimental.pallas import tpu as pltpu

F32 = jnp.float32
BF16 = jnp.bfloat16

D_MODEL = 4096
SEQ = 8192
HEAD_DIM = 128
N_HEADS = 16
ATTN_WIDTH = N_HEADS * HEAD_DIM
CONV_WIDTH = D_MODEL - ATTN_WIDTH
CONV_KERNEL = 31
IN_COLS = 3 * ATTN_WIDTH + 2 * CONV_WIDTH
DILATIONS = (1, 4, 16)
ATTN_BLOCK = 128
N_KEYS = 128
N_GROUPS = 8
EXPERTS_PER_GROUP = 8
N_EXPERTS = N_GROUPS * EXPERTS_PER_GROUP
TOP_K = 2
D_FF = 1024
EPS = 1e-6
NEG_INF = -1e30

LANES = 128
VMEM_LIMIT = 56 * 1024 * 1024

SPAN = DILATIONS[-1] * ATTN_BLOCK

MOE_ROWS = 256
N_ASSIGN = SEQ * TOP_K
N_MOE_ROWS = N_ASSIGN + N_EXPERTS * MOE_ROWS
N_MOE_BLOCKS = N_MOE_ROWS // MOE_ROWS
UP_TILE = 512
DOWN_TILE = 4096
WEIGHT_RING = 2
WEIGHT_DMA_PRIORITY = 1
ROUTER_ROWS = 256
COMBINE_ROWS = 128
X_CACHE_SLOTS = 2
X_ROW_TILES = D_MODEL // 2 // LANES
Y_ROW_TILES = D_MODEL // 2 // LANES
COMBINE_PITCH = Y_ROW_TILES + 4
GATHER_PITCH = X_ROW_TILES + 8
DMA_UNROLL = 8
WAIT_ROWS = 32


def _params(n_axes):
    return pltpu.CompilerParams(dimension_semantics=("arbitrary",) * n_axes,
                                vmem_limit_bytes=VMEM_LIMIT)


def _pack_rows(x):
    half = x.shape[1] // 2
    lo = pltpu.bitcast(x[:, :half].astype(BF16).astype(F32), jnp.uint32)
    hi = pltpu.bitcast(x[:, half:].astype(BF16).astype(F32), jnp.uint32)
    return (hi & jnp.uint32(0xFFFF0000)) | (lo >> 16)


def _unpack_words(words):
    return (pltpu.bitcast(words << 16, F32),
            pltpu.bitcast(words & jnp.uint32(0xFFFF0000), F32))


def _rmsnorm_kernel(x_ref, w_ref, o_ref):
    x = x_ref[...]
    ms = jnp.mean(x * x, axis=-1, keepdims=True)
    o_ref[...] = (x * lax.rsqrt(ms + EPS) * w_ref[...]).astype(o_ref.dtype)


def _rmsnorm(x, w, rows=256):
    n, d = x.shape
    return pl.pallas_call(
        _rmsnorm_kernel,
        grid=(n // rows,),
        in_specs=[pl.BlockSpec((rows, d), lambda i: (i, 0)),
                  pl.BlockSpec((1, d), lambda i: (0, 0))],
        out_specs=pl.BlockSpec((rows, d), lambda i: (i, 0)),
        out_shape=jax.ShapeDtypeStruct((n, d), BF16),
        compiler_params=_params(1),
        name="rmsnorm",
    )(x, w)


IN_TM = 1024
IN_TN = 512
IN_CHUNK = 256
QK_TILES = 2 * ATTN_WIDTH // IN_TN


def _inproj_kernel(hn_ref, w_ref, nw_ref, o_ref, w_bf):
    j = pl.program_id(0)

    @pl.when(pl.program_id(1) == 0)
    def _():
        w_bf[...] = w_ref[...].astype(BF16)

    is_qk = j < QK_TILES
    for c in range(IN_TM // IN_CHUNK):
        rows = slice(c * IN_CHUNK, (c + 1) * IN_CHUNK)
        acc = jnp.dot(hn_ref[rows, :], w_bf[...], preferred_element_type=F32)
        for hd in range(IN_TN // HEAD_DIM):
            sl = slice(hd * HEAD_DIM, (hd + 1) * HEAD_DIM)
            t = acc[:, sl]
            ms = jnp.mean(t * t, axis=-1, keepdims=True)
            normed = t * lax.rsqrt(ms + EPS) * nw_ref[:, sl]
            o_ref[rows, sl] = jnp.where(is_qk, normed, t).astype(o_ref.dtype)


def _inproj(hn, w, qk_norm_w):
    n = hn.shape[0]
    return pl.pallas_call(
        _inproj_kernel,
        grid=(IN_COLS // IN_TN, n // IN_TM),
        in_specs=[pl.BlockSpec((IN_TM, D_MODEL), lambda j, i: (i, 0)),
                  pl.BlockSpec((D_MODEL, IN_TN), lambda j, i: (0, j)),
                  pl.BlockSpec((1, IN_TN), lambda j, i: (0, jnp.minimum(j, QK_TILES - 1)))],
        out_specs=pl.BlockSpec((IN_TM, IN_TN), lambda j, i: (i, j)),
        out_shape=jax.ShapeDtypeStruct((n, IN_COLS), BF16),
        scratch_shapes=[pltpu.VMEM((D_MODEL, IN_TN), BF16)],
        compiler_params=_params(2),
        name="inproj",
    )(hn, w, qk_norm_w)


def _attn_kernel(slopes_ref, q_ref, kc_ref, kp_ref, vc_ref, vp_ref, o_ref, qf, kf, vf, ob, lb):
    h = pl.program_id(0)
    first_span = pl.program_id(1) == 0
    slope = slopes_ref[h]

    qi = lax.broadcasted_iota(jnp.int32, (ATTN_BLOCK, 2 * ATTN_BLOCK), 0)
    kj = lax.broadcasted_iota(jnp.int32, (ATTN_BLOCK, 2 * ATTN_BLOCK), 1)
    steps = qi + ATTN_BLOCK - kj
    valid = (steps >= 0) & (steps <= N_KEYS)
    stepsf = steps.astype(F32)
    scale = 1.0 / math.sqrt(HEAD_DIM)

    def attend(b, d, bias, q0, q, k, v, no_prev_block):
        s = lax.dot_general(q, k, (((1,), (1,)), ((), ())),
                            preferred_element_type=F32) * scale + bias
        if no_prev_block is not None:
            s = jnp.where(kj >= jnp.where(no_prev_block, ATTN_BLOCK, 0), s, NEG_INF)
        m = jnp.max(s, axis=-1, keepdims=True)
        p = jnp.exp(s - m)
        den = jnp.sum(p, axis=-1, keepdims=True)
        o = jnp.dot(p.astype(BF16), v, preferred_element_type=F32) / den
        lse = m + jnp.log(den)
        ob[b, pl.ds(q0, ATTN_BLOCK, stride=d), :] = o
        lb[b, pl.ds(q0, ATTN_BLOCK, stride=d), :] = jnp.broadcast_to(lse, (ATTN_BLOCK, HEAD_DIM))

    def banded_bias(d):
        return jnp.where(valid, (-slope * float(d)) * stepsf, NEG_INF)

    def key_window(cur, prev, n):
        if n > 0:
            return cur[(n - 1) * ATTN_BLOCK:(n + 1) * ATTN_BLOCK, :]
        last = prev.shape[0] - ATTN_BLOCK
        return jnp.concatenate([prev[last:, :], cur[:ATTN_BLOCK, :]], axis=0)

    b, d = 0, DILATIONS[0]
    bias = banded_bias(d)
    for n in range(SPAN // ATTN_BLOCK):
        attend(b, d, bias, n * ATTN_BLOCK, q_ref[n * ATTN_BLOCK:(n + 1) * ATTN_BLOCK, :],
               key_window(kc_ref, kp_ref, n), key_window(vc_ref, vp_ref, n),
               first_span if n == 0 else None)

    qf[...] = q_ref[...].astype(F32)
    kf[0:SPAN, :] = kp_ref[...].astype(F32)
    kf[SPAN:2 * SPAN, :] = kc_ref[...].astype(F32)
    vf[0:SPAN, :] = vp_ref[...].astype(F32)
    vf[SPAN:2 * SPAN, :] = vc_ref[...].astype(F32)
    for b, d in list(enumerate(DILATIONS))[1:]:
        bias = banded_bias(d)
        for r in range(d):
            for n in range(SPAN // (ATTN_BLOCK * d)):
                q0 = r + d * ATTN_BLOCK * n
                k0 = SPAN + q0 - d * ATTN_BLOCK
                attend(b, d, bias, q0,
                       qf[pl.ds(q0, ATTN_BLOCK, stride=d), :].astype(BF16),
                       kf[pl.ds(k0, 2 * ATTN_BLOCK, stride=d), :].astype(BF16),
                       vf[pl.ds(k0, 2 * ATTN_BLOCK, stride=d), :].astype(BF16),
                       first_span if n == 0 else None)

    chunk = 256
    for c in range(SPAN // chunk):
        rows = slice(c * chunk, (c + 1) * chunk)
        l0, l1, l2 = lb[0, rows, :], lb[1, rows, :], lb[2, rows, :]
        mx = jnp.maximum(jnp.maximum(l0, l1), l2)
        w0, w1, w2 = jnp.exp(l0 - mx), jnp.exp(l1 - mx), jnp.exp(l2 - mx)
        num = w0 * ob[0, rows, :] + w1 * ob[1, rows, :] + w2 * ob[2, rows, :]
        o_ref[rows, :] = (num / (w0 + w1 + w2)).astype(o_ref.dtype)


def _attention(z, slopes):
    n_spans = SEQ // SPAN
    k_col, v_col = N_HEADS, 2 * N_HEADS
    blk = (SPAN, HEAD_DIM)
    prev = lambda s: jnp.maximum(s - 1, 0)
    return pl.pallas_call(
        _attn_kernel,
        grid_spec=pltpu.PrefetchScalarGridSpec(
            num_scalar_prefetch=1,
            grid=(N_HEADS, n_spans),
            in_specs=[pl.BlockSpec(blk, lambda h, s, sl: (s, h)),
                      pl.BlockSpec(blk, lambda h, s, sl: (s, k_col + h)),
                      pl.BlockSpec(blk, lambda h, s, sl: (prev(s), k_col + h)),
                      pl.BlockSpec(blk, lambda h, s, sl: (s, v_col + h)),
                      pl.BlockSpec(blk, lambda h, s, sl: (prev(s), v_col + h))],
            out_specs=pl.BlockSpec(blk, lambda h, s, sl: (s, h)),
            scratch_shapes=[pltpu.VMEM((SPAN, HEAD_DIM), F32),
                            pltpu.VMEM((2 * SPAN, HEAD_DIM), F32),
                            pltpu.VMEM((2 * SPAN, HEAD_DIM), F32),
                            pltpu.VMEM((3, SPAN, HEAD_DIM), F32),
                            pltpu.VMEM((3, SPAN, HEAD_DIM), F32)]),
        out_shape=jax.ShapeDtypeStruct((SEQ, ATTN_WIDTH), BF16),
        compiler_params=_params(2),
        name="dilated_attention",
    )(slopes, z, z, z, z, z)


CONV_ROWS = 256
CONV_HALO = 32
CONV_CHUNK = 64


SUBLANES = 8
CONV_EXT = CONV_HALO + CONV_ROWS
CONV_SHIFTED = CONV_EXT - SUBLANES


def _conv_kernel(a_ref, g_ref, ah_ref, gh_ref, cw_ref, cb_ref, lw_ref, lb_ref, o_ref,
                 u_ext, shifted, c_scr):
    i = pl.program_id(0)
    n_ct = CONV_WIDTH // LANES
    for ct in range(n_ct):
        cols = slice(ct * LANES, (ct + 1) * LANES)
        u_ext[ct, CONV_HALO:, :] = (a_ref[:, cols].astype(F32)
                                    * jax.nn.sigmoid(g_ref[:, cols].astype(F32)))
        uh = ah_ref[:, cols].astype(F32) * jax.nn.sigmoid(gh_ref[:, cols].astype(F32))
        u_ext[ct, 0:CONV_HALO, :] = jnp.where(i == 0, 0.0, uh)

    first_tap = CONV_HALO - (CONV_KERNEL - 1)

    def slab(ct, carry):
        for k in range(1, SUBLANES):
            shifted[k - 1] = u_ext[ct, k:k + CONV_SHIFTED, :]
        for rc in range(CONV_ROWS // CONV_CHUNK):
            base = rc * CONV_CHUNK
            acc = jnp.broadcast_to(cb_ref[pl.ds(ct, 1), :], (CONV_CHUNK, LANES))
            for j in range(CONV_KERNEL):
                k = (first_tap + j) % SUBLANES
                row = base + (first_tap + j) - k
                if k == 0:
                    src = u_ext[ct, row:row + CONV_CHUNK, :]
                else:
                    src = shifted[k - 1, row:row + CONV_CHUNK, :]
                acc = acc + cw_ref[j, pl.ds(ct, 1), :] * src
            c_scr[ct, base:base + CONV_CHUNK, :] = acc
        return carry

    lax.fori_loop(0, n_ct, slab, 0)

    total = c_scr[0]
    for ct in range(1, n_ct):
        total = total + c_scr[ct]
    mu = jnp.sum(total, axis=-1, keepdims=True) * (1.0 / CONV_WIDTH)
    sq = jnp.square(c_scr[0] - mu)
    for ct in range(1, n_ct):
        sq = sq + jnp.square(c_scr[ct] - mu)
    var = jnp.sum(sq, axis=-1, keepdims=True) * (1.0 / CONV_WIDTH)
    rstd = lax.rsqrt(var + EPS)
    for ct in range(n_ct):
        cols = slice(ct * LANES, (ct + 1) * LANES)
        y = (c_scr[ct] - mu) * rstd * lw_ref[:, cols] + lb_ref[:, cols]
        o_ref[:, cols] = (y * jax.nn.sigmoid(y)).astype(o_ref.dtype)


def _conformer_conv(z, conv_w, conv_b, ln_w, ln_b):
    a_col = 3 * ATTN_WIDTH // CONV_WIDTH
    g_col = a_col + 1
    n_ct = CONV_WIDTH // LANES
    halo_blocks = CONV_ROWS // CONV_HALO
    halo = lambda i: jnp.maximum(i * halo_blocks - 1, 0)
    vec = pl.BlockSpec((1, CONV_WIDTH), lambda i: (0, 0))
    return pl.pallas_call(
        _conv_kernel,
        grid=(SEQ // CONV_ROWS,),
        in_specs=[pl.BlockSpec((CONV_ROWS, CONV_WIDTH), lambda i: (i, a_col)),
                  pl.BlockSpec((CONV_ROWS, CONV_WIDTH), lambda i: (i, g_col)),
                  pl.BlockSpec((CONV_HALO, CONV_WIDTH), lambda i: (halo(i), a_col)),
                  pl.BlockSpec((CONV_HALO, CONV_WIDTH), lambda i: (halo(i), g_col)),
                  pl.BlockSpec((CONV_KERNEL, n_ct, LANES), lambda i: (0, 0, 0)),
                  pl.BlockSpec((n_ct, LANES), lambda i: (0, 0)),
                  vec, vec],
        out_specs=pl.BlockSpec((CONV_ROWS, CONV_WIDTH), lambda i: (i, 0)),
        out_shape=jax.ShapeDtypeStruct((SEQ, CONV_WIDTH), BF16),
        scratch_shapes=[pltpu.VMEM((n_ct, CONV_EXT, LANES), F32),
                        pltpu.VMEM((SUBLANES - 1, CONV_SHIFTED, LANES), F32),
                        pltpu.VMEM((n_ct, CONV_ROWS, LANES), F32)],
        compiler_params=_params(1),
        name="conformer_conv",
    )(z, z, z, z, conv_w, conv_b, ln_w, ln_b)


OUT_TM = 1024
OUT_TN = 512


def _outproj_kernel(attn_ref, conv_ref, wa_ref, wc_ref, x_ref, o_ref, wa_bf, wc_bf):
    @pl.when(pl.program_id(1) == 0)
    def _():
        wa_bf[...] = wa_ref[...].astype(BF16)
        wc_bf[...] = wc_ref[...].astype(BF16)

    acc = jnp.dot(attn_ref[...], wa_bf[...], preferred_element_type=F32)
    acc = acc + jnp.dot(conv_ref[...], wc_bf[...], preferred_element_type=F32)
    o_ref[...] = x_ref[...] + acc


def _outproj(attn, conv, w, x):
    n = x.shape[0]
    conv_blk = ATTN_WIDTH // CONV_WIDTH
    return pl.pallas_call(
        _outproj_kernel,
        grid=(D_MODEL // OUT_TN, n // OUT_TM),
        in_specs=[pl.BlockSpec((OUT_TM, ATTN_WIDTH), lambda j, i: (i, 0)),
                  pl.BlockSpec((OUT_TM, CONV_WIDTH), lambda j, i: (i, 0)),
                  pl.BlockSpec((ATTN_WIDTH, OUT_TN), lambda j, i: (0, j)),
                  pl.BlockSpec((CONV_WIDTH, OUT_TN), lambda j, i: (conv_blk, j)),
                  pl.BlockSpec((OUT_TM, OUT_TN), lambda j, i: (i, j))],
        out_specs=pl.BlockSpec((OUT_TM, OUT_TN), lambda j, i: (i, j)),
        out_shape=jax.ShapeDtypeStruct((n, D_MODEL), F32),
        scratch_shapes=[pltpu.VMEM((ATTN_WIDTH, OUT_TN), BF16),
                        pltpu.VMEM((CONV_WIDTH, OUT_TN), BF16)],
        compiler_params=_params(2),
        name="outproj",
    )(attn, conv, w, w, x)


def _router_kernel(h_ref, nw_ref, wr_ref, br_ref, hn_ref, ids_ref, gates_ref, counts_ref, seen):
    x = h_ref[...]
    ms = jnp.mean(x * x, axis=-1, keepdims=True)
    hn = x * lax.rsqrt(ms + EPS) * nw_ref[...]
    words = _pack_rows(hn)
    for s in range(X_ROW_TILES):
        hn_ref[:, s, :] = words[:, s * LANES:(s + 1) * LANES]

    h_hi = hn.astype(BF16)
    h_lo = (hn - h_hi.astype(F32)).astype(BF16)
    hi_both = jnp.dot(h_hi, wr_ref[...], preferred_element_type=F32)
    lo_hi = jnp.dot(h_lo, wr_ref[:, :LANES], preferred_element_type=F32)
    logits = hi_both[:, :LANES] + (hi_both[:, LANES:] + lo_hi) + br_ref[...]
    lane = lax.broadcasted_iota(jnp.int32, logits.shape, 1)
    big = jnp.int32(1 << 20)

    is_g = lane < N_GROUPS
    gl = jnp.where(is_g, logits, -jnp.inf)
    ge = jnp.exp(gl - jnp.max(gl, axis=-1, keepdims=True))
    g_prob = ge / jnp.sum(ge, axis=-1, keepdims=True)
    g_w = jnp.max(g_prob, axis=-1, keepdims=True)
    g_top = jnp.min(jnp.where(is_g & (g_prob == g_w), lane, big), axis=-1, keepdims=True)

    eidx = lane - N_GROUPS
    grp_shift = EXPERTS_PER_GROUP.bit_length() - 1
    in_grp = (eidx >= 0) & (eidx < N_EXPERTS) & ((eidx >> grp_shift) == g_top)
    el = jnp.where(in_grp, logits, -jnp.inf)
    ee = jnp.exp(el - jnp.max(el, axis=-1, keepdims=True))
    e_prob = ee / jnp.sum(ee, axis=-1, keepdims=True)
    v1 = jnp.max(jnp.where(in_grp, e_prob, -1.0), axis=-1, keepdims=True)
    i1 = jnp.min(jnp.where(in_grp & (e_prob == v1), lane, big), axis=-1, keepdims=True)
    rest = in_grp & (lane != i1)
    v2 = jnp.max(jnp.where(rest, e_prob, -1.0), axis=-1, keepdims=True)
    i2 = jnp.min(jnp.where(rest & (e_prob == v2), lane, big), axis=-1, keepdims=True)
    tot = v1 + v2
    gates_ref[...] = jnp.where(lane == 0, g_w * v1 / tot,
                               jnp.where(lane == 1, g_w * v2 / tot, 0.0))

    e1 = i1 - N_GROUPS
    e2 = i2 - N_GROUPS
    first_choice = lane == e1
    second_choice = lane == e2 + N_EXPERTS
    chosen = jnp.where(first_choice | second_choice, 1.0, 0.0)
    t_row = lax.broadcasted_iota(jnp.int32, (ROUTER_ROWS, ROUTER_ROWS), 0)
    t_col = lax.broadcasted_iota(jnp.int32, (ROUTER_ROWS, ROUTER_ROWS), 1)
    earlier = jnp.where(t_col < t_row, 1.0, 0.0).astype(BF16)
    chosen = chosen + pltpu.roll(chosen, N_EXPERTS, axis=1)
    before = jnp.dot(earlier, chosen.astype(BF16), preferred_element_type=F32)

    @pl.when(pl.program_id(0) == 0)
    def _():
        seen[...] = jnp.zeros_like(seen)

    before = before + seen[0:1, :]
    rank1 = jnp.sum(jnp.where(first_choice, before, 0.0), axis=-1, keepdims=True)
    rank2 = jnp.sum(jnp.where(second_choice, before, 0.0), axis=-1, keepdims=True)
    seen[0:1, :] = seen[0:1, :] + jnp.sum(chosen, axis=0, keepdims=True)
    counts_ref[...] = jnp.broadcast_to(seen[0:1, :], counts_ref.shape).astype(jnp.int32)

    ids_ref[...] = jnp.where(
        lane == 0, e1, jnp.where(
            lane == 1, e2, jnp.where(
                lane == 2, rank1.astype(jnp.int32), jnp.where(
                    lane == 3, rank2.astype(jnp.int32), 0))))


def _router(h, norm_w, w_router, b_router):
    return pl.pallas_call(
        _router_kernel,
        grid=(SEQ // ROUTER_ROWS,),
        in_specs=[pl.BlockSpec((ROUTER_ROWS, D_MODEL), lambda i: (i, 0)),
                  pl.BlockSpec((1, D_MODEL), lambda i: (0, 0)),
                  pl.BlockSpec((D_MODEL, 2 * LANES), lambda i: (0, 0)),
                  pl.BlockSpec((1, LANES), lambda i: (0, 0))],
        out_specs=[pl.BlockSpec((ROUTER_ROWS, X_ROW_TILES, LANES), lambda i: (i, 0, 0)),
                   pl.BlockSpec((ROUTER_ROWS, LANES), lambda i: (i, 0)),
                   pl.BlockSpec((ROUTER_ROWS, LANES), lambda i: (i, 0)),
                   pl.BlockSpec((SUBLANES, LANES), lambda i: (0, 0))],
        out_shape=[jax.ShapeDtypeStruct((SEQ, X_ROW_TILES, LANES), jnp.uint32),
                   jax.ShapeDtypeStruct((SEQ, LANES), jnp.int32),
                   jax.ShapeDtypeStruct((SEQ, LANES), F32),
                   jax.ShapeDtypeStruct((SUBLANES, LANES), jnp.int32)],
        scratch_shapes=[pltpu.VMEM((SUBLANES, LANES), F32)],
        compiler_params=_params(1),
        name="ffn_norm_router",
    )(h, norm_w, w_router, b_router)


def _staged_token(stage, slot, r):
    return stage.at[slot, pl.ds(pl.multiple_of(r * GATHER_PITCH, SUBLANES), X_ROW_TILES)]


def _issue_token_rows(toks, nv, hn_hbm, stage, slot, sem):
    def start(r, carry):
        pltpu.make_async_copy(hn_hbm.at[toks[0, 0, r]], _staged_token(stage, slot, r),
                              sem.at[slot]).start()
        return carry

    def start_group(g, carry):
        for u in range(DMA_UNROLL):
            start(g * DMA_UNROLL + u, carry)
        return carry

    def zero(r, carry):
        _staged_token(stage, slot, r)[...] = jnp.zeros((X_ROW_TILES, LANES), stage.dtype)
        return carry

    groups = nv // DMA_UNROLL
    lax.fori_loop(0, groups, start_group, 0)
    lax.fori_loop(groups * DMA_UNROLL, nv, start, 0)
    lax.fori_loop(nv, MOE_ROWS, zero, 0)


def _wait_token_rows(nv, stage, slot, sem):
    def wait_rows(n_tokens):
        span = stage.at[slot, pl.ds(0, n_tokens * X_ROW_TILES)]
        pltpu.make_async_copy(span, span, sem.at[slot]).wait()

    def wait_many(c, carry):
        wait_rows(WAIT_ROWS)
        return carry

    def wait_one(c, carry):
        wait_rows(1)
        return carry

    lax.fori_loop(0, nv // WAIT_ROWS, wait_many, 0)
    lax.fori_loop(0, nv % WAIT_ROWS, wait_one, 0)


def _staged_block(stage, slot):
    return jnp.concatenate([stage[slot, pl.ds(s, MOE_ROWS, stride=GATHER_PITCH), :]
                            for s in range(X_ROW_TILES)], axis=-1)


def _unpack_rows(words):
    lo, hi = _unpack_words(words)
    return jnp.concatenate([lo.astype(BF16), hi.astype(BF16)], axis=-1)


def _stream_weight_tile(w, first_ref, seq_ref, n_tiles_ref, tile_copies, consume):
    n_tiles = n_tiles_ref[0]

    @pl.when(w == 0)
    def _():
        for k in range(WEIGHT_RING):
            @pl.when(k < n_tiles)
            def _():
                for c in tile_copies(k, k):
                    c.start(priority=WEIGHT_DMA_PRIORITY)

    @pl.when(first_ref[w] == 1)
    def _():
        k = seq_ref[w]
        slot = k % WEIGHT_RING
        for c in tile_copies(k, slot):
            c.wait()
        consume(slot)

        @pl.when(k + WEIGHT_RING < n_tiles)
        def _():
            for c in tile_copies(k + WEIGHT_RING, slot):
                c.start(priority=WEIGHT_DMA_PRIORITY)


def _moe_up_kernel(ib_ref, nv_ref, b_ref, ot_ref, valid_ref, first_ref, seq_ref, te_ref, tt_ref,
                   nt_ref, fetch_ref, xslot_ref, sslot_ref, pf_ref, pfb_ref, pfnv_ref,
                   tok_ref, tok_next_ref, hn_hbm, wg_hbm, wu_hbm, hid_ref,
                   stage, x_cache, ring_g, ring_u, wg_bf, wu_bf, row_sem, sem):
    w = pl.program_id(0)

    @pl.when(w == 0)
    def _():
        _issue_token_rows(tok_ref, nv_ref[0], hn_hbm, stage, 0, row_sem)

    @pl.when(pf_ref[w] == 1)
    def _():
        _issue_token_rows(tok_next_ref, pfnv_ref[w], hn_hbm, stage, 1 - sslot_ref[w], row_sem)

    def tile_copies(k, slot):
        e = te_ref[k]
        col = pl.multiple_of(tt_ref[k] * UP_TILE, UP_TILE)
        return (pltpu.make_async_copy(wg_hbm.at[e, :, pl.ds(col, UP_TILE)], ring_g.at[slot],
                                      sem.at[0, slot]),
                pltpu.make_async_copy(wu_hbm.at[e, :, pl.ds(col, UP_TILE)], ring_u.at[slot],
                                      sem.at[1, slot]))

    def consume(slot):
        wg_bf[...] = ring_g[slot].astype(BF16)
        wu_bf[...] = ring_u[slot].astype(BF16)

    _stream_weight_tile(w, first_ref, seq_ref, nt_ref, tile_copies, consume)

    @pl.when(valid_ref[w] == 1)
    def _():
        x_slot = xslot_ref[w]

        @pl.when(fetch_ref[w] == 1)
        def _():
            slot = sslot_ref[w]
            _wait_token_rows(nv_ref[w], stage, slot, row_sem)
            x_cache[x_slot] = _unpack_rows(_staged_block(stage, slot))

        x = x_cache[x_slot]
        g = jnp.dot(x, wg_bf[...], preferred_element_type=F32)
        u = jnp.dot(x, wu_bf[...], preferred_element_type=F32)
        hid_ref[...] = (g * jax.nn.sigmoid(g) * u).astype(hid_ref.dtype)

    @pl.when(valid_ref[w] == 0)
    def _():
        hid_ref[...] = jnp.zeros_like(hid_ref)


def _moe_up(items, info, n_valid, row_tok, hn_words, w_gate, w_up):
    n_items = N_MOE_BLOCKS * (D_FF // UP_TILE)
    in_blk = items[0]
    schedule = _gather_schedule(in_blk, info, n_valid)
    n_prefetch = 1 + len(items) + len(schedule)
    toks = row_tok.reshape(N_MOE_BLOCKS, 1, MOE_ROWS)
    smem_block = lambda index_map: pl.BlockSpec((1, 1, MOE_ROWS), index_map,
                                                memory_space=pltpu.SMEM)
    any_spec = pl.BlockSpec(memory_space=pl.ANY)
    return pl.pallas_call(
        _moe_up_kernel,
        grid_spec=pltpu.PrefetchScalarGridSpec(
            num_scalar_prefetch=n_prefetch,
            grid=(n_items,),
            in_specs=[smem_block(lambda w, *refs: (refs[0][w], 0, 0)),
                      smem_block(lambda w, *refs: (refs[n_prefetch - 2][w], 0, 0)),
                      any_spec, any_spec, any_spec],
            out_specs=pl.BlockSpec((MOE_ROWS, UP_TILE),
                                   lambda w, ib, nv, b, ot, *_: (b[w], ot[w])),
            scratch_shapes=[pltpu.VMEM((2, MOE_ROWS * GATHER_PITCH, LANES), jnp.uint32),
                            pltpu.VMEM((X_CACHE_SLOTS, MOE_ROWS, D_MODEL), BF16),
                            pltpu.VMEM((WEIGHT_RING, D_MODEL, UP_TILE), F32),
                            pltpu.VMEM((WEIGHT_RING, D_MODEL, UP_TILE), F32),
                            pltpu.VMEM((D_MODEL, UP_TILE), BF16),
                            pltpu.VMEM((D_MODEL, UP_TILE), BF16),
                            pltpu.SemaphoreType.DMA((2,)),
                            pltpu.SemaphoreType.DMA((2, WEIGHT_RING))]),
        out_shape=jax.ShapeDtypeStruct((N_MOE_ROWS, D_FF), BF16),
        compiler_params=_params(1),
        name="moe_gate_up",
    )(in_blk, n_valid[in_blk], *items[1:], *schedule, toks, toks, hn_words, w_gate, w_up)


def _moe_down_kernel(ib_ref, b_ref, ot_ref, valid_ref, first_ref, seq_ref, te_ref, tt_ref, nt_ref,
                     hid_ref, wd_hbm, y_ref, ring, wd_bf, sem):
    w = pl.program_id(0)

    def tile_copies(k, slot):
        col = pl.multiple_of(tt_ref[k] * DOWN_TILE, DOWN_TILE)
        return (pltpu.make_async_copy(wd_hbm.at[te_ref[k], :, pl.ds(col, DOWN_TILE)],
                                      ring.at[slot], sem.at[slot]),)

    def consume(slot):
        wd_bf[...] = ring[slot].astype(BF16)

    _stream_weight_tile(w, first_ref, seq_ref, nt_ref, tile_copies, consume)

    @pl.when(valid_ref[w] == 1)
    def _():
        y = jnp.dot(hid_ref[...], wd_bf[...], preferred_element_type=F32)
        words = _pack_rows(y)
        for s in range(DOWN_TILE // 2 // LANES):
            y_ref[:, s, :] = words[:, s * LANES:(s + 1) * LANES]

    @pl.when(valid_ref[w] == 0)
    def _():
        y_ref[...] = jnp.zeros_like(y_ref)


def _moe_down(items, hid, w_down):
    n_items = N_MOE_BLOCKS * (D_MODEL // DOWN_TILE)
    tile_rows = DOWN_TILE // 2 // LANES
    return pl.pallas_call(
        _moe_down_kernel,
        grid_spec=pltpu.PrefetchScalarGridSpec(
            num_scalar_prefetch=9,
            grid=(n_items,),
            in_specs=[pl.BlockSpec((MOE_ROWS, D_FF), lambda w, ib, *_: (ib[w], 0)),
                      pl.BlockSpec(memory_space=pl.ANY)],
            out_specs=pl.BlockSpec((MOE_ROWS, tile_rows, LANES),
                                   lambda w, ib, b, ot, *_: (b[w], ot[w], 0)),
            scratch_shapes=[pltpu.VMEM((WEIGHT_RING, D_FF, DOWN_TILE), F32),
                            pltpu.VMEM((D_FF, DOWN_TILE), BF16),
                            pltpu.SemaphoreType.DMA((WEIGHT_RING,))]),
        out_shape=jax.ShapeDtypeStruct((N_MOE_ROWS, Y_ROW_TILES, LANES), jnp.uint32),
        compiler_params=_params(1),
        name="moe_down",
    )(*items, hid, w_down)


def _combine_kernel(dest_ref, dest_next_ref, h_ref, gate_ref, y_hbm, o_ref, buf, sem):
    i = pl.program_id(0)
    n_assign = COMBINE_ROWS * TOP_K

    def copy(slot, a, src_row):
        dst = buf.at[slot, a % TOP_K, pl.ds((a // TOP_K) * COMBINE_PITCH, Y_ROW_TILES)]
        return pltpu.make_async_copy(y_hbm.at[src_row], dst, sem.at[slot])

    def issue(dests, slot):
        def start(a, carry):
            copy(slot, a, dests[0, 0, a]).start()
            return carry
        lax.fori_loop(0, n_assign, start, 0, unroll=8)

    @pl.when(i == 0)
    def _():
        issue(dest_ref, 0)

    @pl.when(i + 1 < pl.num_programs(0))
    def _():
        issue(dest_next_ref, (i + 1) % 2)

    slot = i % 2

    for k in range(TOP_K):
        for c in range(COMBINE_ROWS // WAIT_ROWS):
            span = buf.at[slot, k, pl.ds(c * WAIT_ROWS * Y_ROW_TILES, WAIT_ROWS * Y_ROW_TILES)]
            pltpu.make_async_copy(span, span, sem.at[slot]).wait()

    g0 = gate_ref[:, 0:1]
    g1 = gate_ref[:, 1:2]
    tiles_per_half = DOWN_TILE // 2 // LANES
    for s in range(Y_ROW_TILES):
        lo_col = (s // tiles_per_half) * DOWN_TILE + (s % tiles_per_half) * LANES
        lo0, hi0 = _unpack_words(buf[slot, 0, pl.ds(s, COMBINE_ROWS, stride=COMBINE_PITCH), :])
        lo1, hi1 = _unpack_words(buf[slot, 1, pl.ds(s, COMBINE_ROWS, stride=COMBINE_PITCH), :])
        for col, y0, y1 in ((lo_col, lo0, lo1), (lo_col + DOWN_TILE // 2, hi0, hi1)):
            cols = slice(col, col + LANES)
            o_ref[:, cols] = h_ref[:, cols] + (y0 * g0 + y1 * g1)


def _combine(dest, h, gates, y):
    n_tiles = SEQ // COMBINE_ROWS
    per_tile = COMBINE_ROWS * TOP_K
    dests = dest.reshape(n_tiles, 1, per_tile)
    return pl.pallas_call(
        _combine_kernel,
        grid=(n_tiles,),
        in_specs=[pl.BlockSpec((1, 1, per_tile), lambda i: (i, 0, 0), memory_space=pltpu.SMEM),
                  pl.BlockSpec((1, 1, per_tile), lambda i: (jnp.minimum(i + 1, n_tiles - 1), 0, 0),
                               memory_space=pltpu.SMEM),
                  pl.BlockSpec((COMBINE_ROWS, D_MODEL), lambda i: (i, 0)),
                  pl.BlockSpec((COMBINE_ROWS, LANES), lambda i: (i, 0)),
                  pl.BlockSpec(memory_space=pl.ANY)],
        out_specs=pl.BlockSpec((COMBINE_ROWS, D_MODEL), lambda i: (i, 0)),
        out_shape=jax.ShapeDtypeStruct((SEQ, D_MODEL), F32),
        scratch_shapes=[pltpu.VMEM((2, TOP_K, COMBINE_ROWS * COMBINE_PITCH, LANES), jnp.uint32),
                        pltpu.SemaphoreType.DMA((2,))],
        compiler_params=_params(1),
        name="moe_combine",
    )(dests, dests, h, gates, y)


def _work_items(n_blocks_e, block_start_e, tiles):
    i32 = lambda a: a.astype(jnp.int32)
    n_items = N_MOE_BLOCKS * tiles
    per_e = n_blocks_e * tiles
    ends = jnp.cumsum(per_e)
    total = ends[-1]
    idx = jnp.arange(n_items, dtype=jnp.int32)
    valid = idx < total
    w = jnp.minimum(idx, total - 1)
    e = i32(jnp.searchsorted(ends, w, side='right'))
    local = w - (ends - per_e)[e]
    nb = n_blocks_e[e]
    spare = idx - total
    w_tile = local // nb
    o_tile = jnp.where(valid, w_tile, spare % tiles)
    blk = jnp.where(valid, block_start_e[e] + local % nb, total // tiles + spare // tiles)
    first = valid & (local % nb == 0)
    active = n_blocks_e > 0
    active_rank = jnp.cumsum(active) - active
    seq = active_rank[e] * tiles + w_tile
    active_experts = jnp.argsort(~active, stable=True)
    k = jnp.arange(N_EXPERTS * tiles, dtype=jnp.int32)
    tile_e = active_experts[k // tiles]
    tile_t = k % tiles
    n_tiles = (jnp.sum(active) * tiles).reshape(1)
    in_blk = block_start_e[e] + local % nb
    items = (i32(in_blk), i32(blk), i32(o_tile), i32(valid), i32(first), i32(seq), i32(tile_e),
             i32(tile_t), i32(n_tiles))
    return items, dict(blocks_of_expert=nb, local_block=local % nb, weight_tile=w_tile, valid=valid)


def _gather_schedule(in_blk, info, n_valid):
    i32 = lambda a: a.astype(jnp.int32)
    n = in_blk.shape[0]
    cached = (info['blocks_of_expert'] <= X_CACHE_SLOTS) & (info['weight_tile'] > 0)
    fetch = info['valid'] & ~cached
    x_slot = info['local_block'] % X_CACHE_SLOTS
    stage_slot = (jnp.cumsum(fetch) - 1) % 2
    idx = jnp.arange(n, dtype=jnp.int32)
    later = lax.cummin(jnp.where(fetch, idx, n), axis=0, reverse=True)
    next_fetch = jnp.concatenate([later[1:], jnp.full((1,), n, later.dtype)])
    prefetch = fetch & (next_fetch < n)
    pf_blk = jnp.where(prefetch, in_blk[jnp.minimum(next_fetch, n - 1)], in_blk)
    return (i32(fetch), i32(x_slot), i32(stage_slot), i32(prefetch), i32(pf_blk),
            i32(n_valid[pf_blk]))


def _dispatch_plan(ids, counts):
    expert = ids[:, :TOP_K].reshape(-1)
    rank = ids[:, TOP_K:2 * TOP_K].reshape(-1)
    token = jnp.repeat(jnp.arange(SEQ, dtype=jnp.int32), TOP_K)
    counts = counts[0, :N_EXPERTS]
    n_blocks_e = (counts + MOE_ROWS - 1) // MOE_ROWS
    block_ends = jnp.cumsum(n_blocks_e)
    block_start_e = block_ends - n_blocks_e
    dest = (block_start_e[expert] * MOE_ROWS + rank).astype(jnp.int32)
    row_tok = jnp.zeros((N_MOE_ROWS,), jnp.int32).at[dest].set(token)
    blk = jnp.arange(N_MOE_BLOCKS, dtype=jnp.int32)
    blk_e = jnp.minimum(jnp.searchsorted(block_ends, blk, side='right'), N_EXPERTS - 1)
    n_valid = jnp.clip(counts[blk_e] - (blk - block_start_e[blk_e]) * MOE_ROWS, 0, MOE_ROWS)
    n_valid = jnp.where(blk < block_ends[-1], n_valid, 0).astype(jnp.int32)
    return dest, row_tok, n_valid, n_blocks_e.astype(jnp.int32), block_start_e.astype(jnp.int32)


def kernel(x, norm_mix_w, w_in, q_norm_w, k_norm_w, conv_w, conv_b, conv_ln_w, conv_ln_b, w_out,
           norm_ffn_w, w_group_router, b_group_router, w_expert_router, b_expert_router,
           w_gate, w_up, w_down):
    h = x.reshape(SEQ, D_MODEL)
    slopes = jnp.exp2(-8.0 * jnp.arange(1, N_HEADS + 1, dtype=F32) / N_HEADS)
    for l in range(norm_mix_w.shape[0]):
        hn = _rmsnorm(h, norm_mix_w[l][None, :])
        qk_norm = jnp.concatenate([jnp.tile(q_norm_w[l], N_HEADS), jnp.tile(k_norm_w[l], N_HEADS)])
        z = _inproj(hn, w_in[l], qk_norm[None, :])
        attn = _attention(z, slopes)
        n_ct = CONV_WIDTH // LANES
        conv = _conformer_conv(z, conv_w[l].reshape(CONV_KERNEL, n_ct, LANES),
                               conv_b[l].reshape(n_ct, LANES), conv_ln_w[l][None, :],
                               conv_ln_b[l][None, :])
        h = _outproj(attn, conv, w_out[l], h)

        pad = LANES - N_GROUPS - N_EXPERTS
        w_router = jnp.concatenate([w_group_router[l], w_expert_router[l],
                                    jnp.zeros((D_MODEL, pad), F32)], axis=1)
        b_router = jnp.concatenate([b_group_router[l], b_expert_router[l], jnp.zeros((pad,), F32)])
        w_router_hi = w_router.astype(BF16)
        w_router_lo = (w_router - w_router_hi.astype(F32)).astype(BF16)
        hn_words, ids, gates, counts = _router(h, norm_ffn_w[l][None, :],
                                               jnp.concatenate([w_router_hi, w_router_lo], axis=1),
                                               b_router[None, :])
        dest, row_tok, n_valid, n_blocks_e, block_start_e = _dispatch_plan(ids, counts)
        up_items, up_info = _work_items(n_blocks_e, block_start_e, D_FF // UP_TILE)
        hid = _moe_up(up_items, up_info, n_valid, row_tok, hn_words, w_gate[l], w_up[l])
        down_items, _ = _work_items(n_blocks_e, block_start_e, D_MODEL // DOWN_TILE)
        y = _moe_down(down_items, hid, w_down[l])
        h = _combine(dest, h, gates, y)
    return h.reshape(x.shape)
```

```python
import math

import jax
import jax.numpy as jnp
from jax import lax
from jax.experimental import pallas as pl
from jax.experimental.pallas import tpu as pltpu

F32 = jnp.float32
BF16 = jnp.bfloat16

D_MODEL = 4096
SEQ = 8192
HEAD_DIM = 128
N_HEADS = 16
ATTN_WIDTH = N_HEADS * HEAD_DIM
CONV_WIDTH = D_MODEL - ATTN_WIDTH
CONV_KERNEL = 31
IN_COLS = 3 * ATTN_WIDTH + 2 * CONV_WIDTH
DILATIONS = (1, 4, 16)
ATTN_BLOCK = 128
N_KEYS = 128
N_GROUPS = 8
EXPERTS_PER_GROUP = 8
N_EXPERTS = N_GROUPS * EXPERTS_PER_GROUP
TOP_K = 2
D_FF = 1024
EPS = 1e-6
NEG_INF = -1e30

LANES = 128
VMEM_LIMIT = 56 * 1024 * 1024

SPAN = DILATIONS[-1] * ATTN_BLOCK

MOE_ROWS = 256
N_ASSIGN = SEQ * TOP_K
N_MOE_ROWS = N_ASSIGN + N_EXPERTS * MOE_ROWS
N_MOE_BLOCKS = N_MOE_ROWS // MOE_ROWS
UP_TILE = 512
DOWN_TILE = 4096
WEIGHT_RING = 2
WEIGHT_DMA_PRIORITY = 1
ROUTER_ROWS = 256
COMBINE_ROWS = 128
X_CACHE_SLOTS = 2
X_ROW_TILES = D_MODEL // 2 // LANES
Y_ROW_TILES = D_MODEL // 2 // LANES
COMBINE_PITCH = Y_ROW_TILES + 4
GATHER_PITCH = X_ROW_TILES + 8
DMA_UNROLL = 8
WAIT_ROWS = 32


def _params(n_axes):
    return pltpu.CompilerParams(dimension_semantics=("arbitrary",) * n_axes,
                                vmem_limit_bytes=VMEM_LIMIT)


def _pack_rows(x):
    half = x.shape[1] // 2
    lo = pltpu.bitcast(x[:, :half].astype(BF16).astype(F32), jnp.uint32)
    hi = pltpu.bitcast(x[:, half:].astype(BF16).astype(F32), jnp.uint32)
    return (hi & jnp.uint32(0xFFFF0000)) | (lo >> 16)


def _unpack_words(words):
    return (pltpu.bitcast(words << 16, F32),
            pltpu.bitcast(words & jnp.uint32(0xFFFF0000), F32))


def _rmsnorm_kernel(x_ref, w_ref, o_ref):
    x = x_ref[...]
    ms = jnp.mean(x * x, axis=-1, keepdims=True)
    o_ref[...] = (x * lax.rsqrt(ms + EPS) * w_ref[...]).astype(o_ref.dtype)


def _rmsnorm(x, w, rows=256):
    n, d = x.shape
    return pl.pallas_call(
        _rmsnorm_kernel,
        grid=(n // rows,),
        in_specs=[pl.BlockSpec((rows, d), lambda i: (i, 0)),
                  pl.BlockSpec((1, d), lambda i: (0, 0))],
        out_specs=pl.BlockSpec((rows, d), lambda i: (i, 0)),
        out_shape=jax.ShapeDtypeStruct((n, d), BF16),
        compiler_params=_params(1),
        name="rmsnorm",
    )(x, w)


IN_TM = 1024
IN_TN = 512
IN_CHUNK = 256
QK_TILES = 2 * ATTN_WIDTH // IN_TN


def _inproj_kernel(hn_ref, w_ref, nw_ref, o_ref, w_bf):
    j = pl.program_id(0)

    @pl.when(pl.program_id(1) == 0)
    def _():
        w_bf[...] = w_ref[...].astype(BF16)

    is_qk = j < QK_TILES
    for c in range(IN_TM // IN_CHUNK):
        rows = slice(c * IN_CHUNK, (c + 1) * IN_CHUNK)
        acc = jnp.dot(hn_ref[rows, :], w_bf[...], preferred_element_type=F32)
        for hd in range(IN_TN // HEAD_DIM):
            sl = slice(hd * HEAD_DIM, (hd + 1) * HEAD_DIM)
            t = acc[:, sl]
            ms = jnp.mean(t * t, axis=-1, keepdims=True)
            normed = t * lax.rsqrt(ms + EPS) * nw_ref[:, sl]
            o_ref[rows, sl] = jnp.where(is_qk, normed, t).astype(o_ref.dtype)


def _inproj(hn, w, qk_norm_w):
    n = hn.shape[0]
    return pl.pallas_call(
        _inproj_kernel,
        grid=(IN_COLS // IN_TN, n // IN_TM),
        in_specs=[pl.BlockSpec((IN_TM, D_MODEL), lambda j, i: (i, 0)),
                  pl.BlockSpec((D_MODEL, IN_TN), lambda j, i: (0, j)),
                  pl.BlockSpec((1, IN_TN), lambda j, i: (0, jnp.minimum(j, QK_TILES - 1)))],
        out_specs=pl.BlockSpec((IN_TM, IN_TN), lambda j, i: (i, j)),
        out_shape=jax.ShapeDtypeStruct((n, IN_COLS), BF16),
        scratch_shapes=[pltpu.VMEM((D_MODEL, IN_TN), BF16)],
        compiler_params=_params(2),
        name="inproj",
    )(hn, w, qk_norm_w)


def _attn_kernel(slopes_ref, q_ref, kc_ref, kp_ref, vc_ref, vp_ref, o_ref, qf, kf, vf, ob, lb):
    h = pl.program_id(0)
    first_span = pl.program_id(1) == 0
    slope = slopes_ref[h]

    qi = lax.broadcasted_iota(jnp.int32, (ATTN_BLOCK, 2 * ATTN_BLOCK), 0)
    kj = lax.broadcasted_iota(jnp.int32, (ATTN_BLOCK, 2 * ATTN_BLOCK), 1)
    steps = qi + ATTN_BLOCK - kj
    valid = (steps >= 0) & (steps <= N_KEYS)
    stepsf = steps.astype(F32)
    scale = 1.0 / math.sqrt(HEAD_DIM)

    def attend(b, d, bias, q0, q, k, v, no_prev_block):
        s = lax.dot_general(q, k, (((1,), (1,)), ((), ())),
                            preferred_element_type=F32) * scale + bias
        if no_prev_block is not None:
            s = jnp.where(kj >= jnp.where(no_prev_block, ATTN_BLOCK, 0), s, NEG_INF)
        m = jnp.max(s, axis=-1, keepdims=True)
        p = jnp.exp(s - m)
        den = jnp.sum(p, axis=-1, keepdims=True)
        o = jnp.dot(p.astype(BF16), v, preferred_element_type=F32) / den
        lse = m + jnp.log(den)
        ob[b, pl.ds(q0, ATTN_BLOCK, stride=d), :] = o
        lb[b, pl.ds(q0, ATTN_BLOCK, stride=d), :] = jnp.broadcast_to(lse, (ATTN_BLOCK, HEAD_DIM))

    def banded_bias(d):
        return jnp.where(valid, (-slope * float(d)) * stepsf, NEG_INF)

    def key_window(cur, prev, n):
        if n > 0:
            return cur[(n - 1) * ATTN_BLOCK:(n + 1) * ATTN_BLOCK, :]
        last = prev.shape[0] - ATTN_BLOCK
        return jnp.concatenate([prev[last:, :], cur[:ATTN_BLOCK, :]], axis=0)

    b, d = 0, DILATIONS[0]
    bias = banded_bias(d)
    for n in range(SPAN // ATTN_BLOCK):
        attend(b, d, bias, n * ATTN_BLOCK, q_ref[n * ATTN_BLOCK:(n + 1) * ATTN_BLOCK, :],
               key_window(kc_ref, kp_ref, n), key_window(vc_ref, vp_ref, n),
               first_span if n == 0 else None)

    qf[...] = q_ref[...].astype(F32)
    kf[0:SPAN, :] = kp_ref[...].astype(F32)
    kf[SPAN:2 * SPAN, :] = kc_ref[...].astype(F32)
    vf[0:SPAN, :] = vp_ref[...].astype(F32)
    vf[SPAN:2 * SPAN, :] = vc_ref[...].astype(F32)
    for b, d in list(enumerate(DILATIONS))[1:]:
        bias = banded_bias(d)
        for r in range(d):
            for n in range(SPAN // (ATTN_BLOCK * d)):
                q0 = r + d * ATTN_BLOCK * n
                k0 = SPAN + q0 - d * ATTN_BLOCK
                attend(b, d, bias, q0,
                       qf[pl.ds(q0, ATTN_BLOCK, stride=d), :].astype(BF16),
                       kf[pl.ds(k0, 2 * ATTN_BLOCK, stride=d), :].astype(BF16),
                       vf[pl.ds(k0, 2 * ATTN_BLOCK, stride=d), :].astype(BF16),
                       first_span if n == 0 else None)

    chunk = 256
    for c in range(SPAN // chunk):
        rows = slice(c * chunk, (c + 1) * chunk)
        l0, l1, l2 = lb[0, rows, :], lb[1, rows, :], lb[2, rows, :]
        mx = jnp.maximum(jnp.maximum(l0, l1), l2)
        w0, w1, w2 = jnp.exp(l0 - mx), jnp.exp(l1 - mx), jnp.exp(l2 - mx)
        num = w0 * ob[0, rows, :] + w1 * ob[1, rows, :] + w2 * ob[2, rows, :]
        o_ref[rows, :] = (num / (w0 + w1 + w2)).astype(o_ref.dtype)


def _attention(z, slopes):
    n_spans = SEQ // SPAN
    k_col, v_col = N_HEADS, 2 * N_HEADS
    blk = (SPAN, HEAD_DIM)
    prev = lambda s: jnp.maximum(s - 1, 0)
    return pl.pallas_call(
        _attn_kernel,
        grid_spec=pltpu.PrefetchScalarGridSpec(
            num_scalar_prefetch=1,
            grid=(N_HEADS, n_spans),
            in_specs=[pl.BlockSpec(blk, lambda h, s, sl: (s, h)),
                      pl.BlockSpec(blk, lambda h, s, sl: (s, k_col + h)),
                      pl.BlockSpec(blk, lambda h, s, sl: (prev(s), k_col + h)),
                      pl.BlockSpec(blk, lambda h, s, sl: (s, v_col + h)),
                      pl.BlockSpec(blk, lambda h, s, sl: (prev(s), v_col + h))],
            out_specs=pl.BlockSpec(blk, lambda h, s, sl: (s, h)),
            scratch_shapes=[pltpu.VMEM((SPAN, HEAD_DIM), F32),
                            pltpu.VMEM((2 * SPAN, HEAD_DIM), F32),
                            pltpu.VMEM((2 * SPAN, HEAD_DIM), F32),
                            pltpu.VMEM((3, SPAN, HEAD_DIM), F32),
                            pltpu.VMEM((3, SPAN, HEAD_DIM), F32)]),
        out_shape=jax.ShapeDtypeStruct((SEQ, ATTN_WIDTH), BF16),
        compiler_params=_params(2),
        name="dilated_attention",
    )(slopes, z, z, z, z, z)


CONV_ROWS = 256
CONV_HALO = 32
CONV_CHUNK = 64


SUBLANES = 8
CONV_EXT = CONV_HALO + CONV_ROWS
CONV_SHIFTED = CONV_EXT - SUBLANES


def _conv_kernel(a_ref, g_ref, ah_ref, gh_ref, cw_ref, cb_ref, lw_ref, lb_ref, o_ref,
                 u_ext, shifted, c_scr):
    i = pl.program_id(0)
    n_ct = CONV_WIDTH // LANES
    for ct in range(n_ct):
        cols = slice(ct * LANES, (ct + 1) * LANES)
        u_ext[ct, CONV_HALO:, :] = (a_ref[:, cols].astype(F32)
                                    * jax.nn.sigmoid(g_ref[:, cols].astype(F32)))
        uh = ah_ref[:, cols].astype(F32) * jax.nn.sigmoid(gh_ref[:, cols].astype(F32))
        u_ext[ct, 0:CONV_HALO, :] = jnp.where(i == 0, 0.0, uh)

    first_tap = CONV_HALO - (CONV_KERNEL - 1)

    def slab(ct, carry):
        for k in range(1, SUBLANES):
            shifted[k - 1] = u_ext[ct, k:k + CONV_SHIFTED, :]
        for rc in range(CONV_ROWS // CONV_CHUNK):
            base = rc * CONV_CHUNK
            acc = jnp.broadcast_to(cb_ref[pl.ds(ct, 1), :], (CONV_CHUNK, LANES))
            for j in range(CONV_KERNEL):
                k = (first_tap + j) % SUBLANES
                row = base + (first_tap + j) - k
                if k == 0:
                    src = u_ext[ct, row:row + CONV_CHUNK, :]
                else:
                    src = shifted[k - 1, row:row + CONV_CHUNK, :]
                acc = acc + cw_ref[j, pl.ds(ct, 1), :] * src
            c_scr[ct, base:base + CONV_CHUNK, :] = acc
        return carry

    lax.fori_loop(0, n_ct, slab, 0)

    total = c_scr[0]
    for ct in range(1, n_ct):
        total = total + c_scr[ct]
    mu = jnp.sum(total, axis=-1, keepdims=True) * (1.0 / CONV_WIDTH)
    sq = jnp.square(c_scr[0] - mu)
    for ct in range(1, n_ct):
        sq = sq + jnp.square(c_scr[ct] - mu)
    var = jnp.sum(sq, axis=-1, keepdims=True) * (1.0 / CONV_WIDTH)
    rstd = lax.rsqrt(var + EPS)
    for ct in range(n_ct):
        cols = slice(ct * LANES, (ct + 1) * LANES)
        y = (c_scr[ct] - mu) * rstd * lw_ref[:, cols] + lb_ref[:, cols]
        o_ref[:, cols] = (y * jax.nn.sigmoid(y)).astype(o_ref.dtype)


def _conformer_conv(z, conv_w, conv_b, ln_w, ln_b):
    a_col = 3 * ATTN_WIDTH // CONV_WIDTH
    g_col = a_col + 1
    n_ct = CONV_WIDTH // LANES
    halo_blocks = CONV_ROWS // CONV_HALO
    halo = lambda i: jnp.maximum(i * halo_blocks - 1, 0)
    vec = pl.BlockSpec((1, CONV_WIDTH), lambda i: (0, 0))
    return pl.pallas_call(
        _conv_kernel,
        grid=(SEQ // CONV_ROWS,),
        in_specs=[pl.BlockSpec((CONV_ROWS, CONV_WIDTH), lambda i: (i, a_col)),
                  pl.BlockSpec((CONV_ROWS, CONV_WIDTH), lambda i: (i, g_col)),
                  pl.BlockSpec((CONV_HALO, CONV_WIDTH), lambda i: (halo(i), a_col)),
                  pl.BlockSpec((CONV_HALO, CONV_WIDTH), lambda i: (halo(i), g_col)),
                  pl.BlockSpec((CONV_KERNEL, n_ct, LANES), lambda i: (0, 0, 0)),
                  pl.BlockSpec((n_ct, LANES), lambda i: (0, 0)),
                  vec, vec],
        out_specs=pl.BlockSpec((CONV_ROWS, CONV_WIDTH), lambda i: (i, 0)),
        out_shape=jax.ShapeDtypeStruct((SEQ, CONV_WIDTH), BF16),
        scratch_shapes=[pltpu.VMEM((n_ct, CONV_EXT, LANES), F32),
                        pltpu.VMEM((SUBLANES - 1, CONV_SHIFTED, LANES), F32),
                        pltpu.VMEM((n_ct, CONV_ROWS, LANES), F32)],
        compiler_params=_params(1),
        name="conformer_conv",
    )(z, z, z, z, conv_w, conv_b, ln_w, ln_b)


OUT_TM = 1024
OUT_TN = 512


def _outproj_kernel(attn_ref, conv_ref, wa_ref, wc_ref, x_ref, o_ref, wa_bf, wc_bf):
    @pl.when(pl.program_id(1) == 0)
    def _():
        wa_bf[...] = wa_ref[...].astype(BF16)
        wc_bf[...] = wc_ref[...].astype(BF16)

    acc = jnp.dot(attn_ref[...], wa_bf[...], preferred_element_type=F32)
    acc = acc + jnp.dot(conv_ref[...], wc_bf[...], preferred_element_type=F32)
    o_ref[...] = x_ref[...] + acc


def _outproj(attn, conv, w, x):
    n = x.shape[0]
    conv_blk = ATTN_WIDTH // CONV_WIDTH
    return pl.pallas_call(
        _outproj_kernel,
        grid=(D_MODEL // OUT_TN, n // OUT_TM),
        in_specs=[pl.BlockSpec((OUT_TM, ATTN_WIDTH), lambda j, i: (i, 0)),
                  pl.BlockSpec((OUT_TM, CONV_WIDTH), lambda j, i: (i, 0)),
                  pl.BlockSpec((ATTN_WIDTH, OUT_TN), lambda j, i: (0, j)),
                  pl.BlockSpec((CONV_WIDTH, OUT_TN), lambda j, i: (conv_blk, j)),
                  pl.BlockSpec((OUT_TM, OUT_TN), lambda j, i: (i, j))],
        out_specs=pl.BlockSpec((OUT_TM, OUT_TN), lambda j, i: (i, j)),
        out_shape=jax.ShapeDtypeStruct((n, D_MODEL), F32),
        scratch_shapes=[pltpu.VMEM((ATTN_WIDTH, OUT_TN), BF16),
                        pltpu.VMEM((CONV_WIDTH, OUT_TN), BF16)],
        compiler_params=_params(2),
        name="outproj",
    )(attn, conv, w, w, x)


def _router_kernel(h_ref, nw_ref, wr_ref, br_ref, hn_ref, ids_ref, gates_ref, counts_ref, seen):
    x = h_ref[...]
    ms = jnp.mean(x * x, axis=-1, keepdims=True)
    hn = x * lax.rsqrt(ms + EPS) * nw_ref[...]
    words = _pack_rows(hn)
    for s in range(X_ROW_TILES):
        hn_ref[:, s, :] = words[:, s * LANES:(s + 1) * LANES]

    h_hi = hn.astype(BF16)
    h_lo = (hn - h_hi.astype(F32)).astype(BF16)
    hi_both = jnp.dot(h_hi, wr_ref[...], preferred_element_type=F32)
    lo_hi = jnp.dot(h_lo, wr_ref[:, :LANES], preferred_element_type=F32)
    logits = hi_both[:, :LANES] + (hi_both[:, LANES:] + lo_hi) + br_ref[...]
    lane = lax.broadcasted_iota(jnp.int32, logits.shape, 1)
    big = jnp.int32(1 << 20)

    is_g = lane < N_GROUPS
    gl = jnp.where(is_g, logits, -jnp.inf)
    ge = jnp.exp(gl - jnp.max(gl, axis=-1, keepdims=True))
    g_prob = ge / jnp.sum(ge, axis=-1, keepdims=True)
    g_w = jnp.max(g_prob, axis=-1, keepdims=True)
    g_top = jnp.min(jnp.where(is_g & (g_prob == g_w), lane, big), axis=-1, keepdims=True)

    eidx = lane - N_GROUPS
    grp_shift = EXPERTS_PER_GROUP.bit_length() - 1
    in_grp = (eidx >= 0) & (eidx < N_EXPERTS) & ((eidx >> grp_shift) == g_top)
    el = jnp.where(in_grp, logits, -jnp.inf)
    ee = jnp.exp(el - jnp.max(el, axis=-1, keepdims=True))
    e_prob = ee / jnp.sum(ee, axis=-1, keepdims=True)
    v1 = jnp.max(jnp.where(in_grp, e_prob, -1.0), axis=-1, keepdims=True)
    i1 = jnp.min(jnp.where(in_grp & (e_prob == v1), lane, big), axis=-1, keepdims=True)
    rest = in_grp & (lane != i1)
    v2 = jnp.max(jnp.where(rest, e_prob, -1.0), axis=-1, keepdims=True)
    i2 = jnp.min(jnp.where(rest & (e_prob == v2), lane, big), axis=-1, keepdims=True)
    tot = v1 + v2
    gates_ref[...] = jnp.where(lane == 0, g_w * v1 / tot,
                               jnp.where(lane == 1, g_w * v2 / tot, 0.0))

    e1 = i1 - N_GROUPS
    e2 = i2 - N_GROUPS
    first_choice = lane == e1
    second_choice = lane == e2 + N_EXPERTS
    chosen = jnp.where(first_choice | second_choice, 1.0, 0.0)
    t_row = lax.broadcasted_iota(jnp.int32, (ROUTER_ROWS, ROUTER_ROWS), 0)
    t_col = lax.broadcasted_iota(jnp.int32, (ROUTER_ROWS, ROUTER_ROWS), 1)
    earlier = jnp.where(t_col < t_row, 1.0, 0.0).astype(BF16)
    chosen = chosen + pltpu.roll(chosen, N_EXPERTS, axis=1)
    before = jnp.dot(earlier, chosen.astype(BF16), preferred_element_type=F32)

    @pl.when(pl.program_id(0) == 0)
    def _():
        seen[...] = jnp.zeros_like(seen)

    before = before + seen[0:1, :]
    rank1 = jnp.sum(jnp.where(first_choice, before, 0.0), axis=-1, keepdims=True)
    rank2 = jnp.sum(jnp.where(second_choice, before, 0.0), axis=-1, keepdims=True)
    seen[0:1, :] = seen[0:1, :] + jnp.sum(chosen, axis=0, keepdims=True)
    counts_ref[...] = jnp.broadcast_to(seen[0:1, :], counts_ref.shape).astype(jnp.int32)

    ids_ref[...] = jnp.where(
        lane == 0, e1, jnp.where(
            lane == 1, e2, jnp.where(
                lane == 2, rank1.astype(jnp.int32), jnp.where(
                    lane == 3, rank2.astype(jnp.int32), 0))))


def _router(h, norm_w, w_router, b_router):
    return pl.pallas_call(
        _router_kernel,
        grid=(SEQ // ROUTER_ROWS,),
        in_specs=[pl.BlockSpec((ROUTER_ROWS, D_MODEL), lambda i: (i, 0)),
                  pl.BlockSpec((1, D_MODEL), lambda i: (0, 0)),
                  pl.BlockSpec((D_MODEL, 2 * LANES), lambda i: (0, 0)),
                  pl.BlockSpec((1, LANES), lambda i: (0, 0))],
        out_specs=[pl.BlockSpec((ROUTER_ROWS, X_ROW_TILES, LANES), lambda i: (i, 0, 0)),
                   pl.BlockSpec((ROUTER_ROWS, LANES), lambda i: (i, 0)),
                   pl.BlockSpec((ROUTER_ROWS, LANES), lambda i: (i, 0)),
                   pl.BlockSpec((SUBLANES, LANES), lambda i: (0, 0))],
        out_shape=[jax.ShapeDtypeStruct((SEQ, X_ROW_TILES, LANES), jnp.uint32),
                   jax.ShapeDtypeStruct((SEQ, LANES), jnp.int32),
                   jax.ShapeDtypeStruct((SEQ, LANES), F32),
                   jax.ShapeDtypeStruct((SUBLANES, LANES), jnp.int32)],
        scratch_shapes=[pltpu.VMEM((SUBLANES, LANES), F32)],
        compiler_params=_params(1),
        name="ffn_norm_router",
    )(h, norm_w, w_router, b_router)


def _staged_token(stage, slot, r):
    return stage.at[slot, pl.ds(pl.multiple_of(r * GATHER_PITCH, SUBLANES), X_ROW_TILES)]


def _issue_token_rows(toks, nv, hn_hbm, stage, slot, sem):
    def start(r, carry):
        pltpu.make_async_copy(hn_hbm.at[toks[0, 0, r]], _staged_token(stage, slot, r),
                              sem.at[slot]).start()
        return carry

    def start_group(g, carry):
        for u in range(DMA_UNROLL):
            start(g * DMA_UNROLL + u, carry)
        return carry

    def zero(r, carry):
        _staged_token(stage, slot, r)[...] = jnp.zeros((X_ROW_TILES, LANES), stage.dtype)
        return carry

    groups = nv // DMA_UNROLL
    lax.fori_loop(0, groups, start_group, 0)
    lax.fori_loop(groups * DMA_UNROLL, nv, start, 0)
    lax.fori_loop(nv, MOE_ROWS, zero, 0)


def _wait_token_rows(nv, stage, slot, sem):
    def wait_rows(n_tokens):
        span = stage.at[slot, pl.ds(0, n_tokens * X_ROW_TILES)]
        pltpu.make_async_copy(span, span, sem.at[slot]).wait()

    def wait_many(c, carry):
        wait_rows(WAIT_ROWS)
        return carry

    def wait_one(c, carry):
        wait_rows(1)
        return carry

    lax.fori_loop(0, nv // WAIT_ROWS, wait_many, 0)
    lax.fori_loop(0, nv % WAIT_ROWS, wait_one, 0)


def _staged_block(stage, slot):
    return jnp.concatenate([stage[slot, pl.ds(s, MOE_ROWS, stride=GATHER_PITCH), :]
                            for s in range(X_ROW_TILES)], axis=-1)


def _unpack_rows(words):
    lo, hi = _unpack_words(words)
    return jnp.concatenate([lo.astype(BF16), hi.astype(BF16)], axis=-1)


def _stream_weight_tile(w, first_ref, seq_ref, n_tiles_ref, tile_copies, consume):
    n_tiles = n_tiles_ref[0]

    @pl.when(w == 0)
    def _():
        for k in range(WEIGHT_RING):
            @pl.when(k < n_tiles)
            def _():
                for c in tile_copies(k, k):
                    c.start(priority=WEIGHT_DMA_PRIORITY)

    @pl.when(first_ref[w] == 1)
    def _():
        k = seq_ref[w]
        slot = k % WEIGHT_RING
        for c in tile_copies(k, slot):
            c.wait()
        consume(slot)

        @pl.when(k + WEIGHT_RING < n_tiles)
        def _():
            for c in tile_copies(k + WEIGHT_RING, slot):
                c.start(priority=WEIGHT_DMA_PRIORITY)


def _moe_up_kernel(ib_ref, nv_ref, b_ref, ot_ref, valid_ref, first_ref, seq_ref, te_ref, tt_ref,
                   nt_ref, fetch_ref, xslot_ref, sslot_ref, pf_ref, pfb_ref, pfnv_ref,
                   tok_ref, tok_next_ref, hn_hbm, wg_hbm, wu_hbm, hid_ref,
                   stage, x_cache, ring_g, ring_u, wg_bf, wu_bf, row_sem, sem):
    w = pl.program_id(0)

    @pl.when(w == 0)
    def _():
        _issue_token_rows(tok_ref, nv_ref[0], hn_hbm, stage, 0, row_sem)

    @pl.when(pf_ref[w] == 1)
    def _():
        _issue_token_rows(tok_next_ref, pfnv_ref[w], hn_hbm, stage, 1 - sslot_ref[w], row_sem)

    def tile_copies(k, slot):
        e = te_ref[k]
        col = pl.multiple_of(tt_ref[k] * UP_TILE, UP_TILE)
        return (pltpu.make_async_copy(wg_hbm.at[e, :, pl.ds(col, UP_TILE)], ring_g.at[slot],
                                      sem.at[0, slot]),
                pltpu.make_async_copy(wu_hbm.at[e, :, pl.ds(col, UP_TILE)], ring_u.at[slot],
                                      sem.at[1, slot]))

    def consume(slot):
        wg_bf[...] = ring_g[slot].astype(BF16)
        wu_bf[...] = ring_u[slot].astype(BF16)

    _stream_weight_tile(w, first_ref, seq_ref, nt_ref, tile_copies, consume)

    @pl.when(valid_ref[w] == 1)
    def _():
        x_slot = xslot_ref[w]

        @pl.when(fetch_ref[w] == 1)
        def _():
            slot = sslot_ref[w]
            _wait_token_rows(nv_ref[w], stage, slot, row_sem)
            x_cache[x_slot] = _unpack_rows(_staged_block(stage, slot))

        x = x_cache[x_slot]
        g = jnp.dot(x, wg_bf[...], preferred_element_type=F32)
        u = jnp.dot(x, wu_bf[...], preferred_element_type=F32)
        hid_ref[...] = (g * jax.nn.sigmoid(g) * u).astype(hid_ref.dtype)

    @pl.when(valid_ref[w] == 0)
    def _():
        hid_ref[...] = jnp.zeros_like(hid_ref)


def _moe_up(items, info, n_valid, row_tok, hn_words, w_gate, w_up):
    n_items = N_MOE_BLOCKS * (D_FF // UP_TILE)
    in_blk = items[0]
    schedule = _gather_schedule(in_blk, info, n_valid)
    n_prefetch = 1 + len(items) + len(schedule)
    toks = row_tok.reshape(N_MOE_BLOCKS, 1, MOE_ROWS)
    smem_block = lambda index_map: pl.BlockSpec((1, 1, MOE_ROWS), index_map,
                                                memory_space=pltpu.SMEM)
    any_spec = pl.BlockSpec(memory_space=pl.ANY)
    return pl.pallas_call(
        _moe_up_kernel,
        grid_spec=pltpu.PrefetchScalarGridSpec(
            num_scalar_prefetch=n_prefetch,
            grid=(n_items,),
            in_specs=[smem_block(lambda w, *refs: (refs[0][w], 0, 0)),
                      smem_block(lambda w, *refs: (refs[n_prefetch - 2][w], 0, 0)),
                      any_spec, any_spec, any_spec],
            out_specs=pl.BlockSpec((MOE_ROWS, UP_TILE),
                                   lambda w, ib, nv, b, ot, *_: (b[w], ot[w])),
            scratch_shapes=[pltpu.VMEM((2, MOE_ROWS * GATHER_PITCH, LANES), jnp.uint32),
                            pltpu.VMEM((X_CACHE_SLOTS, MOE_ROWS, D_MODEL), BF16),
                            pltpu.VMEM((WEIGHT_RING, D_MODEL, UP_TILE), F32),
                            pltpu.VMEM((WEIGHT_RING, D_MODEL, UP_TILE), F32),
                            pltpu.VMEM((D_MODEL, UP_TILE), BF16),
                            pltpu.VMEM((D_MODEL, UP_TILE), BF16),
                            pltpu.SemaphoreType.DMA((2,)),
                            pltpu.SemaphoreType.DMA((2, WEIGHT_RING))]),
        out_shape=jax.ShapeDtypeStruct((N_MOE_ROWS, D_FF), BF16),
        compiler_params=_params(1),
        name="moe_gate_up",
    )(in_blk, _lookup(n_valid, in_blk), *items[1:], *schedule, toks, toks, hn_words, w_gate, w_up)


def _moe_down_kernel(ib_ref, b_ref, ot_ref, valid_ref, first_ref, seq_ref, te_ref, tt_ref, nt_ref,
                     hid_ref, wd_hbm, y_ref, ring, wd_bf, sem):
    w = pl.program_id(0)

    def tile_copies(k, slot):
        col = pl.multiple_of(tt_ref[k] * DOWN_TILE, DOWN_TILE)
        return (pltpu.make_async_copy(wd_hbm.at[te_ref[k], :, pl.ds(col, DOWN_TILE)],
                                      ring.at[slot], sem.at[slot]),)

    def consume(slot):
        wd_bf[...] = ring[slot].astype(BF16)

    _stream_weight_tile(w, first_ref, seq_ref, nt_ref, tile_copies, consume)

    @pl.when(valid_ref[w] == 1)
    def _():
        y = jnp.dot(hid_ref[...], wd_bf[...], preferred_element_type=F32)
        words = _pack_rows(y)
        for s in range(DOWN_TILE // 2 // LANES):
            y_ref[:, s, :] = words[:, s * LANES:(s + 1) * LANES]

    @pl.when(valid_ref[w] == 0)
    def _():
        y_ref[...] = jnp.zeros_like(y_ref)


def _moe_down(items, hid, w_down):
    n_items = N_MOE_BLOCKS * (D_MODEL // DOWN_TILE)
    tile_rows = DOWN_TILE // 2 // LANES
    return pl.pallas_call(
        _moe_down_kernel,
        grid_spec=pltpu.PrefetchScalarGridSpec(
            num_scalar_prefetch=9,
            grid=(n_items,),
            in_specs=[pl.BlockSpec((MOE_ROWS, D_FF), lambda w, ib, *_: (ib[w], 0)),
                      pl.BlockSpec(memory_space=pl.ANY)],
            out_specs=pl.BlockSpec((MOE_ROWS, tile_rows, LANES),
                                   lambda w, ib, b, ot, *_: (b[w], ot[w], 0)),
            scratch_shapes=[pltpu.VMEM((WEIGHT_RING, D_FF, DOWN_TILE), F32),
                            pltpu.VMEM((D_FF, DOWN_TILE), BF16),
                            pltpu.SemaphoreType.DMA((WEIGHT_RING,))]),
        out_shape=jax.ShapeDtypeStruct((N_MOE_ROWS, Y_ROW_TILES, LANES), jnp.uint32),
        compiler_params=_params(1),
        name="moe_down",
    )(*items, hid, w_down)


def _combine_kernel(dest_ref, dest_next_ref, h_ref, gate_ref, y_hbm, o_ref, buf, sem):
    i = pl.program_id(0)

    def issue(dests, slot):
        def start(r, carry):
            for k in range(TOP_K):
                dst = buf.at[slot, k, pl.ds(r * COMBINE_PITCH, Y_ROW_TILES)]
                pltpu.make_async_copy(y_hbm.at[dests[0, k, r]], dst, sem.at[slot]).start()
            return carry
        lax.fori_loop(0, COMBINE_ROWS, start, 0, unroll=DMA_UNROLL // TOP_K)

    @pl.when(i == 0)
    def _():
        issue(dest_ref, 0)

    @pl.when(i + 1 < pl.num_programs(0))
    def _():
        issue(dest_next_ref, (i + 1) % 2)

    slot = i % 2

    for k in range(TOP_K):
        for c in range(COMBINE_ROWS // WAIT_ROWS):
            span = buf.at[slot, k, pl.ds(c * WAIT_ROWS * Y_ROW_TILES, WAIT_ROWS * Y_ROW_TILES)]
            pltpu.make_async_copy(span, span, sem.at[slot]).wait()

    g0 = gate_ref[:, 0:1]
    g1 = gate_ref[:, 1:2]
    tiles_per_half = DOWN_TILE // 2 // LANES
    for s in range(Y_ROW_TILES):
        lo_col = (s // tiles_per_half) * DOWN_TILE + (s % tiles_per_half) * LANES
        lo0, hi0 = _unpack_words(buf[slot, 0, pl.ds(s, COMBINE_ROWS, stride=COMBINE_PITCH), :])
        lo1, hi1 = _unpack_words(buf[slot, 1, pl.ds(s, COMBINE_ROWS, stride=COMBINE_PITCH), :])
        for col, y0, y1 in ((lo_col, lo0, lo1), (lo_col + DOWN_TILE // 2, hi0, hi1)):
            cols = slice(col, col + LANES)
            o_ref[:, cols] = h_ref[:, cols] + (y0 * g0 + y1 * g1)


def _combine(dest, h, gates, y):
    n_tiles = SEQ // COMBINE_ROWS
    dests = jnp.stack([d.reshape(n_tiles, COMBINE_ROWS) for d in dest], axis=1)
    blk = (1, TOP_K, COMBINE_ROWS)
    return pl.pallas_call(
        _combine_kernel,
        grid=(n_tiles,),
        in_specs=[pl.BlockSpec(blk, lambda i: (i, 0, 0), memory_space=pltpu.SMEM),
                  pl.BlockSpec(blk, lambda i: (jnp.minimum(i + 1, n_tiles - 1), 0, 0),
                               memory_space=pltpu.SMEM),
                  pl.BlockSpec((COMBINE_ROWS, D_MODEL), lambda i: (i, 0)),
                  pl.BlockSpec((COMBINE_ROWS, LANES), lambda i: (i, 0)),
                  pl.BlockSpec(memory_space=pl.ANY)],
        out_specs=pl.BlockSpec((COMBINE_ROWS, D_MODEL), lambda i: (i, 0)),
        out_shape=jax.ShapeDtypeStruct((SEQ, D_MODEL), F32),
        scratch_shapes=[pltpu.VMEM((2, TOP_K, COMBINE_ROWS * COMBINE_PITCH, LANES), jnp.uint32),
                        pltpu.SemaphoreType.DMA((2,))],
        compiler_params=_params(1),
        name="moe_combine",
    )(dests, dests, h, gates, y)


def _lookup(table, idx):
    pos = jnp.arange(table.shape[0], dtype=jnp.int32)
    return jnp.sum(jnp.where(idx[:, None] == pos[None, :], table[None, :], 0), axis=1)


def _count_le(ends, x):
    return jnp.sum((ends[None, :] <= x[:, None]).astype(jnp.int32), axis=1)


def _work_items(n_blocks_e, block_start_e, tiles):
    i32 = lambda a: a.astype(jnp.int32)
    n_items = N_MOE_BLOCKS * tiles
    per_e = n_blocks_e * tiles
    ends = jnp.cumsum(per_e)
    total = ends[-1]
    idx = jnp.arange(n_items, dtype=jnp.int32)
    valid = idx < total
    w = jnp.minimum(idx, total - 1)
    e = _count_le(ends, w)
    local = w - _lookup(ends - per_e, e)
    nb = _lookup(n_blocks_e, e)
    first_blk = _lookup(block_start_e, e)
    spare = idx - total
    w_tile = local // nb
    o_tile = jnp.where(valid, w_tile, spare % tiles)
    blk = jnp.where(valid, first_blk + local % nb, total // tiles + spare // tiles)
    first = valid & (local % nb == 0)
    active = n_blocks_e > 0
    active_rank = jnp.cumsum(active) - active
    seq = _lookup(active_rank, e) * tiles + w_tile
    k = jnp.arange(N_EXPERTS * tiles, dtype=jnp.int32)
    experts = jnp.arange(N_EXPERTS, dtype=jnp.int32)
    is_kth = active[None, :] & (active_rank[None, :] == (k // tiles)[:, None])
    tile_e = jnp.sum(jnp.where(is_kth, experts[None, :], 0), axis=1)
    tile_t = k % tiles
    n_tiles = (jnp.sum(active) * tiles).reshape(1)
    in_blk = first_blk + local % nb
    items = (i32(in_blk), i32(blk), i32(o_tile), i32(valid), i32(first), i32(seq), i32(tile_e),
             i32(tile_t), i32(n_tiles))
    return items, dict(blocks_of_expert=nb, local_block=local % nb, weight_tile=w_tile, valid=valid)


def _gather_schedule(in_blk, info, n_valid):
    i32 = lambda a: a.astype(jnp.int32)
    n = in_blk.shape[0]
    cached = (info['blocks_of_expert'] <= X_CACHE_SLOTS) & (info['weight_tile'] > 0)
    fetch = info['valid'] & ~cached
    x_slot = info['local_block'] % X_CACHE_SLOTS
    stage_slot = (jnp.cumsum(fetch) - 1) % 2
    idx = jnp.arange(n, dtype=jnp.int32)
    later = lax.cummin(jnp.where(fetch, idx, n), axis=0, reverse=True)
    next_fetch = jnp.concatenate([later[1:], jnp.full((1,), n, later.dtype)])
    prefetch = fetch & (next_fetch < n)
    pf_blk = jnp.where(prefetch, _lookup(in_blk, jnp.minimum(next_fetch, n - 1)), in_blk)
    return (i32(fetch), i32(x_slot), i32(stage_slot), i32(prefetch), i32(pf_blk),
            i32(_lookup(n_valid, pf_blk)))


def _dispatch_plan(ids, counts):
    counts = counts[0, :N_EXPERTS]
    n_blocks_e = (counts + MOE_ROWS - 1) // MOE_ROWS
    block_ends = jnp.cumsum(n_blocks_e)
    block_start_e = block_ends - n_blocks_e
    first_row_e = block_start_e * MOE_ROWS
    dest = [(_lookup(first_row_e, ids[:, k]) + ids[:, TOP_K + k]).astype(jnp.int32)
            for k in range(TOP_K)]
    token = jnp.arange(SEQ, dtype=jnp.int32)
    row_tok = jnp.zeros((N_MOE_ROWS,), jnp.int32).at[jnp.concatenate(dest)].set(
        jnp.concatenate([token] * TOP_K))
    blk = jnp.arange(N_MOE_BLOCKS, dtype=jnp.int32)
    blk_e = jnp.minimum(_count_le(block_ends, blk), N_EXPERTS - 1)
    n_valid = jnp.clip(_lookup(counts, blk_e) - (blk - _lookup(block_start_e, blk_e)) * MOE_ROWS,
                       0, MOE_ROWS)
    n_valid = jnp.where(blk < block_ends[-1], n_valid, 0).astype(jnp.int32)
    return dest, row_tok, n_valid, n_blocks_e.astype(jnp.int32), block_start_e.astype(jnp.int32)


def kernel(x, norm_mix_w, w_in, q_norm_w, k_norm_w, conv_w, conv_b, conv_ln_w, conv_ln_b, w_out,
           norm_ffn_w, w_group_router, b_group_router, w_expert_router, b_expert_router,
           w_gate, w_up, w_down):
    h = x.reshape(SEQ, D_MODEL)
    slopes = jnp.exp2(-8.0 * jnp.arange(1, N_HEADS + 1, dtype=F32) / N_HEADS)
    for l in range(norm_mix_w.shape[0]):
        hn = _rmsnorm(h, norm_mix_w[l][None, :])
        qk_norm = jnp.concatenate([jnp.tile(q_norm_w[l], N_HEADS), jnp.tile(k_norm_w[l], N_HEADS)])
        z = _inproj(hn, w_in[l], qk_norm[None, :])
        attn = _attention(z, slopes)
        n_ct = CONV_WIDTH // LANES
        conv = _conformer_conv(z, conv_w[l].reshape(CONV_KERNEL, n_ct, LANES),
                               conv_b[l].reshape(n_ct, LANES), conv_ln_w[l][None, :],
                               conv_ln_b[l][None, :])
        h = _outproj(attn, conv, w_out[l], h)

        pad = LANES - N_GROUPS - N_EXPERTS
        w_router = jnp.concatenate([w_group_router[l], w_expert_router[l],
                                    jnp.zeros((D_MODEL, pad), F32)], axis=1)
        b_router = jnp.concatenate([b_group_router[l], b_expert_router[l], jnp.zeros((pad,), F32)])
        w_router_hi = w_router.astype(BF16)
        w_router_lo = (w_router - w_router_hi.astype(F32)).astype(BF16)
        hn_words, ids, gates, counts = _router(h, norm_ffn_w[l][None, :],
                                               jnp.concatenate([w_router_hi, w_router_lo], axis=1),
                                               b_router[None, :])
        dest, row_tok, n_valid, n_blocks_e, block_start_e = _dispatch_plan(ids, counts)
        up_items, up_info = _work_items(n_blocks_e, block_start_e, D_FF // UP_TILE)
        hid = _moe_up(up_items, up_info, n_valid, row_tok, hn_words, w_gate[l], w_up[l])
        down_items, _ = _work_items(n_blocks_e, block_start_e, D_MODEL // DOWN_TILE)
        y = _moe_down(down_items, hid, w_down[l])
        h = _combine(dest, h, gates, y)
    return h.reshape(x.shape)
```

```python
import math

import jax
import jax.numpy as jnp
from jax import lax
from jax.experimental import pallas as pl
from jax.experimental.pallas import tpu as pltpu

F32 = jnp.float32
BF16 = jnp.bfloat16

D_MODEL = 4096
SEQ = 8192
HEAD_DIM = 128
N_HEADS = 16
ATTN_WIDTH = N_HEADS * HEAD_DIM
CONV_WIDTH = D_MODEL - ATTN_WIDTH
CONV_KERNEL = 31
IN_COLS = 3 * ATTN_WIDTH + 2 * CONV_WIDTH
DILATIONS = (1, 4, 16)
ATTN_BLOCK = 128
N_KEYS = 128
N_GROUPS = 8
EXPERTS_PER_GROUP = 8
N_EXPERTS = N_GROUPS * EXPERTS_PER_GROUP
TOP_K = 2
D_FF = 1024
EPS = 1e-6
NEG_INF = -1e30

LANES = 128
VMEM_LIMIT = 56 * 1024 * 1024

SPAN = DILATIONS[-1] * ATTN_BLOCK

MOE_ROWS = 256
N_ASSIGN = SEQ * TOP_K
N_MOE_ROWS = N_ASSIGN + N_EXPERTS * MOE_ROWS
N_MOE_BLOCKS = N_MOE_ROWS // MOE_ROWS
UP_TILE = 512
DOWN_TILE = 4096
WEIGHT_RING = 2
WEIGHT_DMA_PRIORITY = 1
ROUTER_ROWS = 256
COMBINE_ROWS = 128
X_CACHE_SLOTS = 2
X_ROW_TILES = D_MODEL // 2 // LANES
Y_ROW_TILES = D_MODEL // 2 // LANES
COMBINE_PITCH = Y_ROW_TILES + 4
GATHER_PITCH = X_ROW_TILES + 8
DMA_UNROLL = 8
WAIT_ROWS = 32


def _params(n_axes):
    return pltpu.CompilerParams(dimension_semantics=("arbitrary",) * n_axes,
                                vmem_limit_bytes=VMEM_LIMIT)


def _pack_rows(x):
    half = x.shape[1] // 2
    lo = pltpu.bitcast(x[:, :half].astype(BF16).astype(F32), jnp.uint32)
    hi = pltpu.bitcast(x[:, half:].astype(BF16).astype(F32), jnp.uint32)
    return (hi & jnp.uint32(0xFFFF0000)) | (lo >> 16)


def _unpack_words(words):
    return (pltpu.bitcast(words << 16, F32),
            pltpu.bitcast(words & jnp.uint32(0xFFFF0000), F32))


def _rmsnorm_kernel(x_ref, w_ref, o_ref):
    x = x_ref[...]
    ms = jnp.mean(x * x, axis=-1, keepdims=True)
    o_ref[...] = (x * lax.rsqrt(ms + EPS) * w_ref[...]).astype(o_ref.dtype)


def _rmsnorm(x, w, rows=256):
    n, d = x.shape
    return pl.pallas_call(
        _rmsnorm_kernel,
        grid=(n // rows,),
        in_specs=[pl.BlockSpec((rows, d), lambda i: (i, 0)),
                  pl.BlockSpec((1, d), lambda i: (0, 0))],
        out_specs=pl.BlockSpec((rows, d), lambda i: (i, 0)),
        out_shape=jax.ShapeDtypeStruct((n, d), BF16),
        compiler_params=_params(1),
        name="rmsnorm",
    )(x, w)


IN_TM = 1024
IN_TN = 512
IN_CHUNK = 256
QK_TILES = 2 * ATTN_WIDTH // IN_TN


QKV_TILES = 3 * ATTN_WIDTH // IN_TN
WIDE = DILATIONS[-1]
TILES_PER_SPAN = SPAN // IN_TM
CLASS_ROWS_PER_TILE = IN_TM // WIDE
CLASS_ROWS_PER_CHUNK = IN_CHUNK // WIDE


def _inproj_qkv_kernel(hn_ref, w_ref, nw_ref, o_ref, oc_ref, w_bf, slab):
    j = pl.program_id(0)

    @pl.when(pl.program_id(1) == 0)
    def _():
        w_bf[...] = w_ref[...].astype(BF16)

    is_qk = j < QK_TILES
    heads = IN_TN // HEAD_DIM
    for c in range(IN_TM // IN_CHUNK):
        rows = slice(c * IN_CHUNK, (c + 1) * IN_CHUNK)
        acc = jnp.dot(hn_ref[rows, :], w_bf[...], preferred_element_type=F32)
        for hd in range(heads):
            sl = slice(hd * HEAD_DIM, (hd + 1) * HEAD_DIM)
            t = acc[:, sl]
            ms = jnp.mean(t * t, axis=-1, keepdims=True)
            normed = t * lax.rsqrt(ms + EPS) * nw_ref[:, sl]
            val = jnp.where(is_qk, normed, t)
            o_ref[rows, sl] = val.astype(o_ref.dtype)
            part = slab.at[c * heads + hd]
            for g in range(CLASS_ROWS_PER_CHUNK):
                part[pl.ds(g, WIDE, stride=CLASS_ROWS_PER_CHUNK), :] = val[g * WIDE:(g + 1) * WIDE, :]
            crows = slice(c * CLASS_ROWS_PER_CHUNK, (c + 1) * CLASS_ROWS_PER_CHUNK)
            for r in range(WIDE):
                oc_ref[0, r, crows, sl] = part[r * CLASS_ROWS_PER_CHUNK:(r + 1) * CLASS_ROWS_PER_CHUNK,
                                               :].astype(oc_ref.dtype)


def _inproj_qkv(hn, w, qk_norm_w):
    n = hn.shape[0]
    width = 3 * ATTN_WIDTH
    return pl.pallas_call(
        _inproj_qkv_kernel,
        grid=(QKV_TILES, n // IN_TM),
        in_specs=[pl.BlockSpec((IN_TM, D_MODEL), lambda j, i: (i, 0)),
                  pl.BlockSpec((D_MODEL, IN_TN), lambda j, i: (0, j)),
                  pl.BlockSpec((1, IN_TN), lambda j, i: (0, jnp.minimum(j, QK_TILES - 1)))],
        out_specs=[pl.BlockSpec((IN_TM, IN_TN), lambda j, i: (i, j)),
                   pl.BlockSpec((1, WIDE, CLASS_ROWS_PER_TILE, IN_TN),
                                lambda j, i: (i // TILES_PER_SPAN, 0, i % TILES_PER_SPAN, j))],
        out_shape=[jax.ShapeDtypeStruct((n, width), BF16),
                   jax.ShapeDtypeStruct((n // SPAN, WIDE, SPAN // WIDE, width), BF16)],
        scratch_shapes=[pltpu.VMEM((D_MODEL, IN_TN), BF16),
                        pltpu.VMEM((IN_TM // IN_CHUNK * (IN_TN // HEAD_DIM), IN_CHUNK, HEAD_DIM), F32)],
        compiler_params=_params(2),
        name="inproj_qkv",
    )(hn, w, qk_norm_w)


def _inproj_glu_kernel(hn_ref, w_ref, o_ref, w_bf):
    @pl.when(pl.program_id(1) == 0)
    def _():
        w_bf[...] = w_ref[...].astype(BF16)

    for c in range(IN_TM // IN_CHUNK):
        rows = slice(c * IN_CHUNK, (c + 1) * IN_CHUNK)
        o_ref[rows, :] = jnp.dot(hn_ref[rows, :], w_bf[...],
                                 preferred_element_type=F32).astype(o_ref.dtype)


def _inproj_glu(hn, w):
    n = hn.shape[0]
    return pl.pallas_call(
        _inproj_glu_kernel,
        grid=(2 * CONV_WIDTH // IN_TN, n // IN_TM),
        in_specs=[pl.BlockSpec((IN_TM, D_MODEL), lambda j, i: (i, 0)),
                  pl.BlockSpec((D_MODEL, IN_TN), lambda j, i: (0, QKV_TILES + j))],
        out_specs=pl.BlockSpec((IN_TM, IN_TN), lambda j, i: (i, j)),
        out_shape=jax.ShapeDtypeStruct((n, 2 * CONV_WIDTH), BF16),
        scratch_shapes=[pltpu.VMEM((D_MODEL, IN_TN), BF16)],
        compiler_params=_params(2),
        name="inproj_glu",
    )(hn, w)


def _attn_kernel(slopes_ref, q_ref, kc_ref, kp_ref, vc_ref, vp_ref,
                 qw_ref, kwc_ref, kwp_ref, vwc_ref, vwp_ref, o_ref, qf, kf, vf, ob, lb):
    h = pl.program_id(0)
    first_span = pl.program_id(1) == 0
    slope = slopes_ref[h]

    qi = lax.broadcasted_iota(jnp.int32, (ATTN_BLOCK, 2 * ATTN_BLOCK), 0)
    kj = lax.broadcasted_iota(jnp.int32, (ATTN_BLOCK, 2 * ATTN_BLOCK), 1)
    steps = qi + ATTN_BLOCK - kj
    valid = (steps >= 0) & (steps <= N_KEYS)
    stepsf = steps.astype(F32)
    scale = 1.0 / math.sqrt(HEAD_DIM)

    def attend(b, d, bias, q0, q, k, v, no_prev_block):
        s = lax.dot_general(q, k, (((1,), (1,)), ((), ())),
                            preferred_element_type=F32) * scale + bias
        if no_prev_block is not None:
            s = jnp.where(kj >= jnp.where(no_prev_block, ATTN_BLOCK, 0), s, NEG_INF)
        m = jnp.max(s, axis=-1, keepdims=True)
        p = jnp.exp(s - m)
        den = jnp.sum(p, axis=-1, keepdims=True)
        o = jnp.dot(p.astype(BF16), v, preferred_element_type=F32) / den
        lse = m + jnp.log(den)
        ob[b, pl.ds(q0, ATTN_BLOCK, stride=d), :] = o
        lb[b, pl.ds(q0, ATTN_BLOCK, stride=d), :] = jnp.broadcast_to(lse, (ATTN_BLOCK, HEAD_DIM))

    def banded_bias(d):
        return jnp.where(valid, (-slope * float(d)) * stepsf, NEG_INF)

    def key_window(cur, prev, n):
        if n > 0:
            return cur[(n - 1) * ATTN_BLOCK:(n + 1) * ATTN_BLOCK, :]
        last = prev.shape[0] - ATTN_BLOCK
        return jnp.concatenate([prev[last:, :], cur[:ATTN_BLOCK, :]], axis=0)

    b, d = 0, DILATIONS[0]
    bias = banded_bias(d)
    for n in range(SPAN // ATTN_BLOCK):
        attend(b, d, bias, n * ATTN_BLOCK, q_ref[n * ATTN_BLOCK:(n + 1) * ATTN_BLOCK, :],
               key_window(kc_ref, kp_ref, n), key_window(vc_ref, vp_ref, n),
               first_span if n == 0 else None)

    b, d = len(DILATIONS) - 1, WIDE
    bias = banded_bias(d)
    for r in range(d):
        attend(b, d, bias, r, qw_ref[0, r],
               jnp.concatenate([kwp_ref[0, r], kwc_ref[0, r]], axis=0),
               jnp.concatenate([vwp_ref[0, r], vwc_ref[0, r]], axis=0), first_span)

    qf[...] = q_ref[...].astype(F32)
    kf[0:SPAN, :] = kp_ref[...].astype(F32)
    kf[SPAN:2 * SPAN, :] = kc_ref[...].astype(F32)
    vf[0:SPAN, :] = vp_ref[...].astype(F32)
    vf[SPAN:2 * SPAN, :] = vc_ref[...].astype(F32)
    for b, d in list(enumerate(DILATIONS))[1:-1]:
        bias = banded_bias(d)
        for r in range(d):
            for n in range(SPAN // (ATTN_BLOCK * d)):
                q0 = r + d * ATTN_BLOCK * n
                k0 = SPAN + q0 - d * ATTN_BLOCK
                attend(b, d, bias, q0,
                       qf[pl.ds(q0, ATTN_BLOCK, stride=d), :].astype(BF16),
                       kf[pl.ds(k0, 2 * ATTN_BLOCK, stride=d), :].astype(BF16),
                       vf[pl.ds(k0, 2 * ATTN_BLOCK, stride=d), :].astype(BF16),
                       first_span if n == 0 else None)

    chunk = 256
    for c in range(SPAN // chunk):
        rows = slice(c * chunk, (c + 1) * chunk)
        l0, l1, l2 = lb[0, rows, :], lb[1, rows, :], lb[2, rows, :]
        mx = jnp.maximum(jnp.maximum(l0, l1), l2)
        w0, w1, w2 = jnp.exp(l0 - mx), jnp.exp(l1 - mx), jnp.exp(l2 - mx)
        num = w0 * ob[0, rows, :] + w1 * ob[1, rows, :] + w2 * ob[2, rows, :]
        o_ref[rows, :] = (num / (w0 + w1 + w2)).astype(o_ref.dtype)


def _attention(z, z_wide, slopes):
    n_spans = SEQ // SPAN
    k_col, v_col = N_HEADS, 2 * N_HEADS
    blk = (SPAN, HEAD_DIM)
    wblk = (1, WIDE, SPAN // WIDE, HEAD_DIM)
    prev = lambda s: jnp.maximum(s - 1, 0)
    return pl.pallas_call(
        _attn_kernel,
        grid_spec=pltpu.PrefetchScalarGridSpec(
            num_scalar_prefetch=1,
            grid=(N_HEADS, n_spans),
            in_specs=[pl.BlockSpec(blk, lambda h, s, sl: (s, h)),
                      pl.BlockSpec(blk, lambda h, s, sl: (s, k_col + h)),
                      pl.BlockSpec(blk, lambda h, s, sl: (prev(s), k_col + h)),
                      pl.BlockSpec(blk, lambda h, s, sl: (s, v_col + h)),
                      pl.BlockSpec(blk, lambda h, s, sl: (prev(s), v_col + h)),
                      pl.BlockSpec(wblk, lambda h, s, sl: (s, 0, 0, h)),
                      pl.BlockSpec(wblk, lambda h, s, sl: (s, 0, 0, k_col + h)),
                      pl.BlockSpec(wblk, lambda h, s, sl: (prev(s), 0, 0, k_col + h)),
                      pl.BlockSpec(wblk, lambda h, s, sl: (s, 0, 0, v_col + h)),
                      pl.BlockSpec(wblk, lambda h, s, sl: (prev(s), 0, 0, v_col + h))],
            out_specs=pl.BlockSpec(blk, lambda h, s, sl: (s, h)),
            scratch_shapes=[pltpu.VMEM((SPAN, HEAD_DIM), F32),
                            pltpu.VMEM((2 * SPAN, HEAD_DIM), F32),
                            pltpu.VMEM((2 * SPAN, HEAD_DIM), F32),
                            pltpu.VMEM((3, SPAN, HEAD_DIM), F32),
                            pltpu.VMEM((3, SPAN, HEAD_DIM), F32)]),
        out_shape=jax.ShapeDtypeStruct((SEQ, ATTN_WIDTH), BF16),
        compiler_params=_params(2),
        name="dilated_attention",
    )(slopes, z, z, z, z, z, z_wide, z_wide, z_wide, z_wide, z_wide)


CONV_ROWS = 256
CONV_HALO = 32
CONV_CHUNK = 64


SUBLANES = 8
CONV_EXT = CONV_HALO + CONV_ROWS
CONV_SHIFTED = CONV_EXT - SUBLANES


def _conv_kernel(a_ref, g_ref, ah_ref, gh_ref, cw_ref, cb_ref, lw_ref, lb_ref, o_ref,
                 u_ext, shifted, c_scr):
    i = pl.program_id(0)
    n_ct = CONV_WIDTH // LANES
    for ct in range(n_ct):
        cols = slice(ct * LANES, (ct + 1) * LANES)
        u_ext[ct, CONV_HALO:, :] = (a_ref[:, cols].astype(F32)
                                    * jax.nn.sigmoid(g_ref[:, cols].astype(F32)))
        uh = ah_ref[:, cols].astype(F32) * jax.nn.sigmoid(gh_ref[:, cols].astype(F32))
        u_ext[ct, 0:CONV_HALO, :] = jnp.where(i == 0, 0.0, uh)

    first_tap = CONV_HALO - (CONV_KERNEL - 1)

    def slab(ct, carry):
        for k in range(1, SUBLANES):
            shifted[k - 1] = u_ext[ct, k:k + CONV_SHIFTED, :]
        for rc in range(CONV_ROWS // CONV_CHUNK):
            base = rc * CONV_CHUNK
            acc = jnp.broadcast_to(cb_ref[pl.ds(ct, 1), :], (CONV_CHUNK, LANES))
            for j in range(CONV_KERNEL):
                k = (first_tap + j) % SUBLANES
                row = base + (first_tap + j) - k
                if k == 0:
                    src = u_ext[ct, row:row + CONV_CHUNK, :]
                else:
                    src = shifted[k - 1, row:row + CONV_CHUNK, :]
                acc = acc + cw_ref[j, pl.ds(ct, 1), :] * src
            c_scr[ct, base:base + CONV_CHUNK, :] = acc
        return carry

    lax.fori_loop(0, n_ct, slab, 0)

    total = c_scr[0]
    for ct in range(1, n_ct):
        total = total + c_scr[ct]
    mu = jnp.sum(total, axis=-1, keepdims=True) * (1.0 / CONV_WIDTH)
    sq = jnp.square(c_scr[0] - mu)
    for ct in range(1, n_ct):
        sq = sq + jnp.square(c_scr[ct] - mu)
    var = jnp.sum(sq, axis=-1, keepdims=True) * (1.0 / CONV_WIDTH)
    rstd = lax.rsqrt(var + EPS)
    for ct in range(n_ct):
        cols = slice(ct * LANES, (ct + 1) * LANES)
        y = (c_scr[ct] - mu) * rstd * lw_ref[:, cols] + lb_ref[:, cols]
        o_ref[:, cols] = (y * jax.nn.sigmoid(y)).astype(o_ref.dtype)


def _conformer_conv(z, conv_w, conv_b, ln_w, ln_b):
    a_col, g_col = 0, 1
    n_ct = CONV_WIDTH // LANES
    halo_blocks = CONV_ROWS // CONV_HALO
    halo = lambda i: jnp.maximum(i * halo_blocks - 1, 0)
    vec = pl.BlockSpec((1, CONV_WIDTH), lambda i: (0, 0))
    return pl.pallas_call(
        _conv_kernel,
        grid=(SEQ // CONV_ROWS,),
        in_specs=[pl.BlockSpec((CONV_ROWS, CONV_WIDTH), lambda i: (i, a_col)),
                  pl.BlockSpec((CONV_ROWS, CONV_WIDTH), lambda i: (i, g_col)),
                  pl.BlockSpec((CONV_HALO, CONV_WIDTH), lambda i: (halo(i), a_col)),
                  pl.BlockSpec((CONV_HALO, CONV_WIDTH), lambda i: (halo(i), g_col)),
                  pl.BlockSpec((CONV_KERNEL, n_ct, LANES), lambda i: (0, 0, 0)),
                  pl.BlockSpec((n_ct, LANES), lambda i: (0, 0)),
                  vec, vec],
        out_specs=pl.BlockSpec((CONV_ROWS, CONV_WIDTH), lambda i: (i, 0)),
        out_shape=jax.ShapeDtypeStruct((SEQ, CONV_WIDTH), BF16),
        scratch_shapes=[pltpu.VMEM((n_ct, CONV_EXT, LANES), F32),
                        pltpu.VMEM((SUBLANES - 1, CONV_SHIFTED, LANES), F32),
                        pltpu.VMEM((n_ct, CONV_ROWS, LANES), F32)],
        compiler_params=_params(1),
        name="conformer_conv",
    )(z, z, z, z, conv_w, conv_b, ln_w, ln_b)


OUT_TM = 1024
OUT_TN = 512


def _outproj_kernel(attn_ref, conv_ref, wa_ref, wc_ref, x_ref, o_ref, wa_bf, wc_bf):
    @pl.when(pl.program_id(1) == 0)
    def _():
        wa_bf[...] = wa_ref[...].astype(BF16)
        wc_bf[...] = wc_ref[...].astype(BF16)

    acc = jnp.dot(attn_ref[...], wa_bf[...], preferred_element_type=F32)
    acc = acc + jnp.dot(conv_ref[...], wc_bf[...], preferred_element_type=F32)
    o_ref[...] = x_ref[...] + acc


def _outproj(attn, conv, w, x):
    n = x.shape[0]
    conv_blk = ATTN_WIDTH // CONV_WIDTH
    return pl.pallas_call(
        _outproj_kernel,
        grid=(D_MODEL // OUT_TN, n // OUT_TM),
        in_specs=[pl.BlockSpec((OUT_TM, ATTN_WIDTH), lambda j, i: (i, 0)),
                  pl.BlockSpec((OUT_TM, CONV_WIDTH), lambda j, i: (i, 0)),
                  pl.BlockSpec((ATTN_WIDTH, OUT_TN), lambda j, i: (0, j)),
                  pl.BlockSpec((CONV_WIDTH, OUT_TN), lambda j, i: (conv_blk, j)),
                  pl.BlockSpec((OUT_TM, OUT_TN), lambda j, i: (i, j))],
        out_specs=pl.BlockSpec((OUT_TM, OUT_TN), lambda j, i: (i, j)),
        out_shape=jax.ShapeDtypeStruct((n, D_MODEL), F32),
        scratch_shapes=[pltpu.VMEM((ATTN_WIDTH, OUT_TN), BF16),
                        pltpu.VMEM((CONV_WIDTH, OUT_TN), BF16)],
        compiler_params=_params(2),
        name="outproj",
    )(attn, conv, w, w, x)


def _router_kernel(h_ref, nw_ref, wr_ref, br_ref, hn_ref, ids_ref, gates_ref, counts_ref, seen):
    x = h_ref[...]
    ms = jnp.mean(x * x, axis=-1, keepdims=True)
    hn = x * lax.rsqrt(ms + EPS) * nw_ref[...]
    words = _pack_rows(hn)
    for s in range(X_ROW_TILES):
        hn_ref[:, s, :] = words[:, s * LANES:(s + 1) * LANES]

    h_hi = hn.astype(BF16)
    h_lo = (hn - h_hi.astype(F32)).astype(BF16)
    hi_both = jnp.dot(h_hi, wr_ref[...], preferred_element_type=F32)
    lo_hi = jnp.dot(h_lo, wr_ref[:, :LANES], preferred_element_type=F32)
    logits = hi_both[:, :LANES] + (hi_both[:, LANES:] + lo_hi) + br_ref[...]
    lane = lax.broadcasted_iota(jnp.int32, logits.shape, 1)
    big = jnp.int32(1 << 20)

    is_g = lane < N_GROUPS
    gl = jnp.where(is_g, logits, -jnp.inf)
    ge = jnp.exp(gl - jnp.max(gl, axis=-1, keepdims=True))
    g_prob = ge / jnp.sum(ge, axis=-1, keepdims=True)
    g_w = jnp.max(g_prob, axis=-1, keepdims=True)
    g_top = jnp.min(jnp.where(is_g & (g_prob == g_w), lane, big), axis=-1, keepdims=True)

    eidx = lane - N_GROUPS
    grp_shift = EXPERTS_PER_GROUP.bit_length() - 1
    in_grp = (eidx >= 0) & (eidx < N_EXPERTS) & ((eidx >> grp_shift) == g_top)
    el = jnp.where(in_grp, logits, -jnp.inf)
    ee = jnp.exp(el - jnp.max(el, axis=-1, keepdims=True))
    e_prob = ee / jnp.sum(ee, axis=-1, keepdims=True)
    v1 = jnp.max(jnp.where(in_grp, e_prob, -1.0), axis=-1, keepdims=True)
    i1 = jnp.min(jnp.where(in_grp & (e_prob == v1), lane, big), axis=-1, keepdims=True)
    rest = in_grp & (lane != i1)
    v2 = jnp.max(jnp.where(rest, e_prob, -1.0), axis=-1, keepdims=True)
    i2 = jnp.min(jnp.where(rest & (e_prob == v2), lane, big), axis=-1, keepdims=True)
    tot = v1 + v2
    gates_ref[...] = jnp.where(lane == 0, g_w * v1 / tot,
                               jnp.where(lane == 1, g_w * v2 / tot, 0.0))

    e1 = i1 - N_GROUPS
    e2 = i2 - N_GROUPS
    first_choice = lane == e1
    second_choice = lane == e2 + N_EXPERTS
    chosen = jnp.where(first_choice | second_choice, 1.0, 0.0)
    t_row = lax.broadcasted_iota(jnp.int32, (ROUTER_ROWS, ROUTER_ROWS), 0)
    t_col = lax.broadcasted_iota(jnp.int32, (ROUTER_ROWS, ROUTER_ROWS), 1)
    earlier = jnp.where(t_col < t_row, 1.0, 0.0).astype(BF16)
    chosen = chosen + pltpu.roll(chosen, N_EXPERTS, axis=1)
    before = jnp.dot(earlier, chosen.astype(BF16), preferred_element_type=F32)

    @pl.when(pl.program_id(0) == 0)
    def _():
        seen[...] = jnp.zeros_like(seen)

    before = before + seen[0:1, :]
    rank1 = jnp.sum(jnp.where(first_choice, before, 0.0), axis=-1, keepdims=True)
    rank2 = jnp.sum(jnp.where(second_choice, before, 0.0), axis=-1, keepdims=True)
    seen[0:1, :] = seen[0:1, :] + jnp.sum(chosen, axis=0, keepdims=True)
    counts_ref[...] = jnp.broadcast_to(seen[0:1, :], counts_ref.shape).astype(jnp.int32)

    ids_ref[...] = jnp.where(
        lane == 0, e1, jnp.where(
            lane == 1, e2, jnp.where(
                lane == 2, rank1.astype(jnp.int32), jnp.where(
                    lane == 3, rank2.astype(jnp.int32), 0))))


def _router(h, norm_w, w_router, b_router):
    return pl.pallas_call(
        _router_kernel,
        grid=(SEQ // ROUTER_ROWS,),
        in_specs=[pl.BlockSpec((ROUTER_ROWS, D_MODEL), lambda i: (i, 0)),
                  pl.BlockSpec((1, D_MODEL), lambda i: (0, 0)),
                  pl.BlockSpec((D_MODEL, 2 * LANES), lambda i: (0, 0)),
                  pl.BlockSpec((1, LANES), lambda i: (0, 0))],
        out_specs=[pl.BlockSpec((ROUTER_ROWS, X_ROW_TILES, LANES), lambda i: (i, 0, 0)),
                   pl.BlockSpec((ROUTER_ROWS, LANES), lambda i: (i, 0)),
                   pl.BlockSpec((ROUTER_ROWS, LANES), lambda i: (i, 0)),
                   pl.BlockSpec((SUBLANES, LANES), lambda i: (0, 0))],
        out_shape=[jax.ShapeDtypeStruct((SEQ, X_ROW_TILES, LANES), jnp.uint32),
                   jax.ShapeDtypeStruct((SEQ, LANES), jnp.int32),
                   jax.ShapeDtypeStruct((SEQ, LANES), F32),
                   jax.ShapeDtypeStruct((SUBLANES, LANES), jnp.int32)],
        scratch_shapes=[pltpu.VMEM((SUBLANES, LANES), F32)],
        compiler_params=_params(1),
        name="ffn_norm_router",
    )(h, norm_w, w_router, b_router)


def _staged_token(stage, slot, r):
    return stage.at[slot, pl.ds(pl.multiple_of(r * GATHER_PITCH, SUBLANES), X_ROW_TILES)]


def _issue_token_rows(toks, nv, hn_hbm, stage, slot, sem):
    def start(r, carry):
        pltpu.make_async_copy(hn_hbm.at[toks[0, 0, r]], _staged_token(stage, slot, r),
                              sem.at[slot]).start()
        return carry

    def start_group(g, carry):
        for u in range(DMA_UNROLL):
            start(g * DMA_UNROLL + u, carry)
        return carry

    def zero(r, carry):
        _staged_token(stage, slot, r)[...] = jnp.zeros((X_ROW_TILES, LANES), stage.dtype)
        return carry

    groups = nv // DMA_UNROLL
    lax.fori_loop(0, groups, start_group, 0)
    lax.fori_loop(groups * DMA_UNROLL, nv, start, 0)
    lax.fori_loop(nv, MOE_ROWS, zero, 0)


def _wait_token_rows(nv, stage, slot, sem):
    def wait_rows(n_tokens):
        span = stage.at[slot, pl.ds(0, n_tokens * X_ROW_TILES)]
        pltpu.make_async_copy(span, span, sem.at[slot]).wait()

    def wait_many(c, carry):
        wait_rows(WAIT_ROWS)
        return carry

    def wait_one(c, carry):
        wait_rows(1)
        return carry

    lax.fori_loop(0, nv // WAIT_ROWS, wait_many, 0)
    lax.fori_loop(0, nv % WAIT_ROWS, wait_one, 0)


def _staged_block(stage, slot):
    return jnp.concatenate([stage[slot, pl.ds(s, MOE_ROWS, stride=GATHER_PITCH), :]
                            for s in range(X_ROW_TILES)], axis=-1)


def _unpack_rows(words):
    lo, hi = _unpack_words(words)
    return jnp.concatenate([lo.astype(BF16), hi.astype(BF16)], axis=-1)


def _stream_weight_tile(w, first_ref, seq_ref, n_tiles_ref, tile_copies, consume):
    n_tiles = n_tiles_ref[0]

    @pl.when(w == 0)
    def _():
        for k in range(WEIGHT_RING):
            @pl.when(k < n_tiles)
            def _():
                for c in tile_copies(k, k):
                    c.start(priority=WEIGHT_DMA_PRIORITY)

    @pl.when(first_ref[w] == 1)
    def _():
        k = seq_ref[w]
        slot = k % WEIGHT_RING
        for c in tile_copies(k, slot):
            c.wait()
        consume(slot)

        @pl.when(k + WEIGHT_RING < n_tiles)
        def _():
            for c in tile_copies(k + WEIGHT_RING, slot):
                c.start(priority=WEIGHT_DMA_PRIORITY)


def _moe_up_kernel(ib_ref, nv_ref, b_ref, ot_ref, valid_ref, first_ref, seq_ref, te_ref, tt_ref,
                   nt_ref, fetch_ref, xslot_ref, sslot_ref, pf_ref, pfb_ref, pfnv_ref,
                   tok_ref, tok_next_ref, hn_hbm, wg_hbm, wu_hbm, hid_ref,
                   stage, x_cache, ring_g, ring_u, wg_bf, wu_bf, row_sem, sem):
    w = pl.program_id(0)

    @pl.when(w == 0)
    def _():
        _issue_token_rows(tok_ref, nv_ref[0], hn_hbm, stage, 0, row_sem)

    @pl.when(pf_ref[w] == 1)
    def _():
        _issue_token_rows(tok_next_ref, pfnv_ref[w], hn_hbm, stage, 1 - sslot_ref[w], row_sem)

    def tile_copies(k, slot):
        e = te_ref[k]
        col = pl.multiple_of(tt_ref[k] * UP_TILE, UP_TILE)
        return (pltpu.make_async_copy(wg_hbm.at[e, :, pl.ds(col, UP_TILE)], ring_g.at[slot],
                                      sem.at[0, slot]),
                pltpu.make_async_copy(wu_hbm.at[e, :, pl.ds(col, UP_TILE)], ring_u.at[slot],
                                      sem.at[1, slot]))

    def consume(slot):
        wg_bf[...] = ring_g[slot].astype(BF16)
        wu_bf[...] = ring_u[slot].astype(BF16)

    _stream_weight_tile(w, first_ref, seq_ref, nt_ref, tile_copies, consume)

    @pl.when(valid_ref[w] == 1)
    def _():
        x_slot = xslot_ref[w]

        @pl.when(fetch_ref[w] == 1)
        def _():
            slot = sslot_ref[w]
            _wait_token_rows(nv_ref[w], stage, slot, row_sem)
            x_cache[x_slot] = _unpack_rows(_staged_block(stage, slot))

        x = x_cache[x_slot]
        g = jnp.dot(x, wg_bf[...], preferred_element_type=F32)
        u = jnp.dot(x, wu_bf[...], preferred_element_type=F32)
        hid_ref[...] = (g * jax.nn.sigmoid(g) * u).astype(hid_ref.dtype)

    @pl.when(valid_ref[w] == 0)
    def _():
        hid_ref[...] = jnp.zeros_like(hid_ref)


def _moe_up(items, info, n_valid, row_tok, hn_words, w_gate, w_up):
    n_items = N_MOE_BLOCKS * (D_FF // UP_TILE)
    in_blk = items[0]
    schedule = _gather_schedule(in_blk, info, n_valid)
    n_prefetch = 1 + len(items) + len(schedule)
    toks = row_tok.reshape(N_MOE_BLOCKS, 1, MOE_ROWS)
    smem_block = lambda index_map: pl.BlockSpec((1, 1, MOE_ROWS), index_map,
                                                memory_space=pltpu.SMEM)
    any_spec = pl.BlockSpec(memory_space=pl.ANY)
    return pl.pallas_call(
        _moe_up_kernel,
        grid_spec=pltpu.PrefetchScalarGridSpec(
            num_scalar_prefetch=n_prefetch,
            grid=(n_items,),
            in_specs=[smem_block(lambda w, *refs: (refs[0][w], 0, 0)),
                      smem_block(lambda w, *refs: (refs[n_prefetch - 2][w], 0, 0)),
                      any_spec, any_spec, any_spec],
            out_specs=pl.BlockSpec((MOE_ROWS, UP_TILE),
                                   lambda w, ib, nv, b, ot, *_: (b[w], ot[w])),
            scratch_shapes=[pltpu.VMEM((2, MOE_ROWS * GATHER_PITCH, LANES), jnp.uint32),
                            pltpu.VMEM((X_CACHE_SLOTS, MOE_ROWS, D_MODEL), BF16),
                            pltpu.VMEM((WEIGHT_RING, D_MODEL, UP_TILE), F32),
                            pltpu.VMEM((WEIGHT_RING, D_MODEL, UP_TILE), F32),
                            pltpu.VMEM((D_MODEL, UP_TILE), BF16),
                            pltpu.VMEM((D_MODEL, UP_TILE), BF16),
                            pltpu.SemaphoreType.DMA((2,)),
                            pltpu.SemaphoreType.DMA((2, WEIGHT_RING))]),
        out_shape=jax.ShapeDtypeStruct((N_MOE_ROWS, D_FF), BF16),
        compiler_params=_params(1),
        name="moe_gate_up",
    )(in_blk, _lookup(n_valid, in_blk), *items[1:], *schedule, toks, toks, hn_words, w_gate, w_up)


def _moe_down_kernel(ib_ref, b_ref, ot_ref, valid_ref, first_ref, seq_ref, te_ref, tt_ref, nt_ref,
                     hid_ref, wd_hbm, y_ref, ring, wd_bf, sem):
    w = pl.program_id(0)

    def tile_copies(k, slot):
        col = pl.multiple_of(tt_ref[k] * DOWN_TILE, DOWN_TILE)
        return (pltpu.make_async_copy(wd_hbm.at[te_ref[k], :, pl.ds(col, DOWN_TILE)],
                                      ring.at[slot], sem.at[slot]),)

    def consume(slot):
        wd_bf[...] = ring[slot].astype(BF16)

    _stream_weight_tile(w, first_ref, seq_ref, nt_ref, tile_copies, consume)

    @pl.when(valid_ref[w] == 1)
    def _():
        y = jnp.dot(hid_ref[...], wd_bf[...], preferred_element_type=F32)
        words = _pack_rows(y)
        for s in range(DOWN_TILE // 2 // LANES):
            y_ref[:, s, :] = words[:, s * LANES:(s + 1) * LANES]

    @pl.when(valid_ref[w] == 0)
    def _():
        y_ref[...] = jnp.zeros_like(y_ref)


def _moe_down(items, hid, w_down):
    n_items = N_MOE_BLOCKS * (D_MODEL // DOWN_TILE)
    tile_rows = DOWN_TILE // 2 // LANES
    return pl.pallas_call(
        _moe_down_kernel,
        grid_spec=pltpu.PrefetchScalarGridSpec(
            num_scalar_prefetch=9,
            grid=(n_items,),
            in_specs=[pl.BlockSpec((MOE_ROWS, D_FF), lambda w, ib, *_: (ib[w], 0)),
                      pl.BlockSpec(memory_space=pl.ANY)],
            out_specs=pl.BlockSpec((MOE_ROWS, tile_rows, LANES),
                                   lambda w, ib, b, ot, *_: (b[w], ot[w], 0)),
            scratch_shapes=[pltpu.VMEM((WEIGHT_RING, D_FF, DOWN_TILE), F32),
                            pltpu.VMEM((D_FF, DOWN_TILE), BF16),
                            pltpu.SemaphoreType.DMA((WEIGHT_RING,))]),
        out_shape=jax.ShapeDtypeStruct((N_MOE_ROWS, Y_ROW_TILES, LANES), jnp.uint32),
        compiler_params=_params(1),
        name="moe_down",
    )(*items, hid, w_down)


def _combine_kernel(dest_ref, dest_next_ref, h_ref, gate_ref, y_hbm, o_ref, buf, sem):
    i = pl.program_id(0)

    def issue(dests, slot):
        def start(r, carry):
            for k in range(TOP_K):
                dst = buf.at[slot, k, pl.ds(r * COMBINE_PITCH, Y_ROW_TILES)]
                pltpu.make_async_copy(y_hbm.at[dests[0, k, r]], dst, sem.at[slot]).start()
            return carry
        lax.fori_loop(0, COMBINE_ROWS, start, 0, unroll=DMA_UNROLL // TOP_K)

    @pl.when(i == 0)
    def _():
        issue(dest_ref, 0)

    @pl.when(i + 1 < pl.num_programs(0))
    def _():
        issue(dest_next_ref, (i + 1) % 2)

    slot = i % 2

    for k in range(TOP_K):
        for c in range(COMBINE_ROWS // WAIT_ROWS):
            span = buf.at[slot, k, pl.ds(c * WAIT_ROWS * Y_ROW_TILES, WAIT_ROWS * Y_ROW_TILES)]
            pltpu.make_async_copy(span, span, sem.at[slot]).wait()

    g0 = gate_ref[:, 0:1]
    g1 = gate_ref[:, 1:2]
    tiles_per_half = DOWN_TILE // 2 // LANES
    for s in range(Y_ROW_TILES):
        lo_col = (s // tiles_per_half) * DOWN_TILE + (s % tiles_per_half) * LANES
        lo0, hi0 = _unpack_words(buf[slot, 0, pl.ds(s, COMBINE_ROWS, stride=COMBINE_PITCH), :])
        lo1, hi1 = _unpack_words(buf[slot, 1, pl.ds(s, COMBINE_ROWS, stride=COMBINE_PITCH), :])
        for col, y0, y1 in ((lo_col, lo0, lo1), (lo_col + DOWN_TILE // 2, hi0, hi1)):
            cols = slice(col, col + LANES)
            o_ref[:, cols] = h_ref[:, cols] + (y0 * g0 + y1 * g1)


def _combine(dest, h, gates, y):
    n_tiles = SEQ // COMBINE_ROWS
    dests = jnp.stack([d.reshape(n_tiles, COMBINE_ROWS) for d in dest], axis=1)
    blk = (1, TOP_K, COMBINE_ROWS)
    return pl.pallas_call(
        _combine_kernel,
        grid=(n_tiles,),
        in_specs=[pl.BlockSpec(blk, lambda i: (i, 0, 0), memory_space=pltpu.SMEM),
                  pl.BlockSpec(blk, lambda i: (jnp.minimum(i + 1, n_tiles - 1), 0, 0),
                               memory_space=pltpu.SMEM),
                  pl.BlockSpec((COMBINE_ROWS, D_MODEL), lambda i: (i, 0)),
                  pl.BlockSpec((COMBINE_ROWS, LANES), lambda i: (i, 0)),
                  pl.BlockSpec(memory_space=pl.ANY)],
        out_specs=pl.BlockSpec((COMBINE_ROWS, D_MODEL), lambda i: (i, 0)),
        out_shape=jax.ShapeDtypeStruct((SEQ, D_MODEL), F32),
        scratch_shapes=[pltpu.VMEM((2, TOP_K, COMBINE_ROWS * COMBINE_PITCH, LANES), jnp.uint32),
                        pltpu.SemaphoreType.DMA((2,))],
        compiler_params=_params(1),
        name="moe_combine",
    )(dests, dests, h, gates, y)


def _lookup(table, idx):
    pos = jnp.arange(table.shape[0], dtype=jnp.int32)
    return jnp.sum(jnp.where(idx[:, None] == pos[None, :], table[None, :], 0), axis=1)


def _count_le(ends, x):
    return jnp.sum((ends[None, :] <= x[:, None]).astype(jnp.int32), axis=1)


def _work_items(n_blocks_e, block_start_e, tiles):
    i32 = lambda a: a.astype(jnp.int32)
    n_items = N_MOE_BLOCKS * tiles
    per_e = n_blocks_e * tiles
    ends = jnp.cumsum(per_e)
    total = ends[-1]
    idx = jnp.arange(n_items, dtype=jnp.int32)
    valid = idx < total
    w = jnp.minimum(idx, total - 1)
    e = _count_le(ends, w)
    local = w - _lookup(ends - per_e, e)
    nb = _lookup(n_blocks_e, e)
    first_blk = _lookup(block_start_e, e)
    spare = idx - total
    w_tile = local // nb
    o_tile = jnp.where(valid, w_tile, spare % tiles)
    blk = jnp.where(valid, first_blk + local % nb, total // tiles + spare // tiles)
    first = valid & (local % nb == 0)
    active = n_blocks_e > 0
    active_rank = jnp.cumsum(active) - active
    seq = _lookup(active_rank, e) * tiles + w_tile
    k = jnp.arange(N_EXPERTS * tiles, dtype=jnp.int32)
    experts = jnp.arange(N_EXPERTS, dtype=jnp.int32)
    is_kth = active[None, :] & (active_rank[None, :] == (k // tiles)[:, None])
    tile_e = jnp.sum(jnp.where(is_kth, experts[None, :], 0), axis=1)
    tile_t = k % tiles
    n_tiles = (jnp.sum(active) * tiles).reshape(1)
    in_blk = first_blk + local % nb
    items = (i32(in_blk), i32(blk), i32(o_tile), i32(valid), i32(first), i32(seq), i32(tile_e),
             i32(tile_t), i32(n_tiles))
    return items, dict(blocks_of_expert=nb, local_block=local % nb, weight_tile=w_tile, valid=valid)


def _gather_schedule(in_blk, info, n_valid):
    i32 = lambda a: a.astype(jnp.int32)
    n = in_blk.shape[0]
    cached = (info['blocks_of_expert'] <= X_CACHE_SLOTS) & (info['weight_tile'] > 0)
    fetch = info['valid'] & ~cached
    x_slot = info['local_block'] % X_CACHE_SLOTS
    stage_slot = (jnp.cumsum(fetch) - 1) % 2
    idx = jnp.arange(n, dtype=jnp.int32)
    later = lax.cummin(jnp.where(fetch, idx, n), axis=0, reverse=True)
    next_fetch = jnp.concatenate([later[1:], jnp.full((1,), n, later.dtype)])
    prefetch = fetch & (next_fetch < n)
    pf_blk = jnp.where(prefetch, _lookup(in_blk, jnp.minimum(next_fetch, n - 1)), in_blk)
    return (i32(fetch), i32(x_slot), i32(stage_slot), i32(prefetch), i32(pf_blk),
            i32(_lookup(n_valid, pf_blk)))


def _dispatch_plan(ids, counts):
    counts = counts[0, :N_EXPERTS]
    n_blocks_e = (counts + MOE_ROWS - 1) // MOE_ROWS
    block_ends = jnp.cumsum(n_blocks_e)
    block_start_e = block_ends - n_blocks_e
    first_row_e = block_start_e * MOE_ROWS
    dest = [(_lookup(first_row_e, ids[:, k]) + ids[:, TOP_K + k]).astype(jnp.int32)
            for k in range(TOP_K)]
    token = jnp.arange(SEQ, dtype=jnp.int32)
    row_tok = jnp.zeros((N_MOE_ROWS,), jnp.int32).at[jnp.concatenate(dest)].set(
        jnp.concatenate([token] * TOP_K))
    blk = jnp.arange(N_MOE_BLOCKS, dtype=jnp.int32)
    blk_e = jnp.minimum(_count_le(block_ends, blk), N_EXPERTS - 1)
    n_valid = jnp.clip(_lookup(counts, blk_e) - (blk - _lookup(block_start_e, blk_e)) * MOE_ROWS,
                       0, MOE_ROWS)
    n_valid = jnp.where(blk < block_ends[-1], n_valid, 0).astype(jnp.int32)
    return dest, row_tok, n_valid, n_blocks_e.astype(jnp.int32), block_start_e.astype(jnp.int32)


def kernel(x, norm_mix_w, w_in, q_norm_w, k_norm_w, conv_w, conv_b, conv_ln_w, conv_ln_b, w_out,
           norm_ffn_w, w_group_router, b_group_router, w_expert_router, b_expert_router,
           w_gate, w_up, w_down):
    h = x.reshape(SEQ, D_MODEL)
    slopes = jnp.exp2(-8.0 * jnp.arange(1, N_HEADS + 1, dtype=F32) / N_HEADS)
    for l in range(norm_mix_w.shape[0]):
        hn = _rmsnorm(h, norm_mix_w[l][None, :])
        qk_norm = jnp.concatenate([jnp.tile(q_norm_w[l], N_HEADS), jnp.tile(k_norm_w[l], N_HEADS)])
        z_qkv, z_qkv_wide = _inproj_qkv(hn, w_in[l], qk_norm[None, :])
        z_glu = _inproj_glu(hn, w_in[l])
        attn = _attention(z_qkv, z_qkv_wide, slopes)
        n_ct = CONV_WIDTH // LANES
        conv = _conformer_conv(z_glu, conv_w[l].reshape(CONV_KERNEL, n_ct, LANES),
                               conv_b[l].reshape(n_ct, LANES), conv_ln_w[l][None, :],
                               conv_ln_b[l][None, :])
        h = _outproj(attn, conv, w_out[l], h)

        pad = LANES - N_GROUPS - N_EXPERTS
        w_router = jnp.concatenate([w_group_router[l], w_expert_router[l],
                                    jnp.zeros((D_MODEL, pad), F32)], axis=1)
        b_router = jnp.concatenate([b_group_router[l], b_expert_router[l], jnp.zeros((pad,), F32)])
        w_router_hi = w_router.astype(BF16)
        w_router_lo = (w_router - w_router_hi.astype(F32)).astype(BF16)
        hn_words, ids, gates, counts = _router(h, norm_ffn_w[l][None, :],
                                               jnp.concatenate([w_router_hi, w_router_lo], axis=1),
                                               b_router[None, :])
        dest, row_tok, n_valid, n_blocks_e, block_start_e = _dispatch_plan(ids, counts)
        up_items, up_info = _work_items(n_blocks_e, block_start_e, D_FF // UP_TILE)
        hid = _moe_up(up_items, up_info, n_valid, row_tok, hn_words, w_gate[l], w_up[l])
        down_items, _ = _work_items(n_blocks_e, block_start_e, D_MODEL // DOWN_TILE)
        y = _moe_down(down_items, hid, w_down[l])
        h = _combine(dest, h, gates, y)
    return h.reshape(x.shape)
```

```python
import math

import jax
import jax.numpy as jnp
from jax import lax
from jax.experimental import pallas as pl
from jax.experimental.pallas import tpu as pltpu

F32 = jnp.float32
BF16 = jnp.bfloat16

D_MODEL = 4096
SEQ = 8192
HEAD_DIM = 128
N_HEADS = 16
ATTN_WIDTH = N_HEADS * HEAD_DIM
CONV_WIDTH = D_MODEL - ATTN_WIDTH
CONV_KERNEL = 31
IN_COLS = 3 * ATTN_WIDTH + 2 * CONV_WIDTH
DILATIONS = (1, 4, 16)
ATTN_BLOCK = 128
N_KEYS = 128
N_GROUPS = 8
EXPERTS_PER_GROUP = 8
N_EXPERTS = N_GROUPS * EXPERTS_PER_GROUP
TOP_K = 2
D_FF = 1024
EPS = 1e-6
NEG_INF = -1e30

LANES = 128
VMEM_LIMIT = 56 * 1024 * 1024

SPAN = DILATIONS[-1] * ATTN_BLOCK

MOE_ROWS = 256
N_ASSIGN = SEQ * TOP_K
N_MOE_ROWS = N_ASSIGN + N_EXPERTS * MOE_ROWS
N_MOE_BLOCKS = N_MOE_ROWS // MOE_ROWS
UP_TILE = 512
DOWN_TILE = 4096
WEIGHT_RING = 2
WEIGHT_DMA_PRIORITY = 1
ROUTER_ROWS = 256
COMBINE_ROWS = 128
X_CACHE_SLOTS = 2
X_ROW_TILES = D_MODEL // 2 // LANES
Y_ROW_TILES = D_MODEL // 2 // LANES
COMBINE_PITCH = Y_ROW_TILES + 4
GATHER_PITCH = X_ROW_TILES + 8
DMA_UNROLL = 8
WAIT_ROWS = 32


def _params(n_axes):
    return pltpu.CompilerParams(dimension_semantics=("arbitrary",) * n_axes,
                                vmem_limit_bytes=VMEM_LIMIT)


def _pack_rows(x):
    half = x.shape[1] // 2
    lo = pltpu.bitcast(x[:, :half].astype(BF16).astype(F32), jnp.uint32)
    hi = pltpu.bitcast(x[:, half:].astype(BF16).astype(F32), jnp.uint32)
    return (hi & jnp.uint32(0xFFFF0000)) | (lo >> 16)


def _unpack_words(words):
    return (pltpu.bitcast(words << 16, F32),
            pltpu.bitcast(words & jnp.uint32(0xFFFF0000), F32))


def _rmsnorm_kernel(x_ref, w_ref, o_ref):
    x = x_ref[...]
    ms = jnp.mean(x * x, axis=-1, keepdims=True)
    o_ref[...] = (x * lax.rsqrt(ms + EPS) * w_ref[...]).astype(o_ref.dtype)


def _rmsnorm(x, w, rows=256):
    n, d = x.shape
    return pl.pallas_call(
        _rmsnorm_kernel,
        grid=(n // rows,),
        in_specs=[pl.BlockSpec((rows, d), lambda i: (i, 0)),
                  pl.BlockSpec((1, d), lambda i: (0, 0))],
        out_specs=pl.BlockSpec((rows, d), lambda i: (i, 0)),
        out_shape=jax.ShapeDtypeStruct((n, d), BF16),
        compiler_params=_params(1),
        name="rmsnorm",
    )(x, w)


IN_TM = 1024
IN_TN = 512
IN_CHUNKS = (256, 256, 256, 128, 128)
QK_TILES = 2 * ATTN_WIDTH // IN_TN


def _inproj_kernel(hn_ref, w_ref, nw_ref, o_ref, w_bf):
    j = pl.program_id(0)

    @pl.when(pl.program_id(1) == 0)
    def _():
        w_bf[...] = w_ref[...].astype(BF16)

    is_qk = j < QK_TILES
    row0 = 0
    for chunk in IN_CHUNKS:
        rows = slice(row0, row0 + chunk)
        row0 += chunk
        acc = jnp.dot(hn_ref[rows, :], w_bf[...], preferred_element_type=F32)
        for hd in range(IN_TN // HEAD_DIM):
            sl = slice(hd * HEAD_DIM, (hd + 1) * HEAD_DIM)
            t = acc[:, sl]
            ms = jnp.mean(t * t, axis=-1, keepdims=True)
            normed = t * lax.rsqrt(ms + EPS) * nw_ref[:, sl]
            o_ref[rows, sl] = jnp.where(is_qk, normed, t).astype(o_ref.dtype)


def _inproj(hn, w, qk_norm_w):
    n = hn.shape[0]
    return pl.pallas_call(
        _inproj_kernel,
        grid=(IN_COLS // IN_TN, n // IN_TM),
        in_specs=[pl.BlockSpec((IN_TM, D_MODEL), lambda j, i: (i, 0)),
                  pl.BlockSpec((D_MODEL, IN_TN), lambda j, i: (0, j)),
                  pl.BlockSpec((1, IN_TN), lambda j, i: (0, jnp.minimum(j, QK_TILES - 1)))],
        out_specs=pl.BlockSpec((IN_TM, IN_TN), lambda j, i: (i, j)),
        out_shape=jax.ShapeDtypeStruct((n, IN_COLS), BF16),
        scratch_shapes=[pltpu.VMEM((D_MODEL, IN_TN), BF16)],
        compiler_params=_params(2),
        name="inproj",
    )(hn, w, qk_norm_w)


def _attn_kernel(slopes_ref, q_ref, kc_ref, kp_ref, vc_ref, vp_ref, o_ref, qf, kf, vf, ob, lb):
    h = pl.program_id(0)
    first_span = pl.program_id(1) == 0
    slope = slopes_ref[h]

    qi = lax.broadcasted_iota(jnp.int32, (ATTN_BLOCK, 2 * ATTN_BLOCK), 0)
    kj = lax.broadcasted_iota(jnp.int32, (ATTN_BLOCK, 2 * ATTN_BLOCK), 1)
    steps = qi + ATTN_BLOCK - kj
    valid = (steps >= 0) & (steps <= N_KEYS)
    stepsf = steps.astype(F32)
    scale = 1.0 / math.sqrt(HEAD_DIM)

    def attend(b, d, bias, q0, q, k, v, no_prev_block):
        s = lax.dot_general(q, k, (((1,), (1,)), ((), ())),
                            preferred_element_type=F32) * scale + bias
        if no_prev_block is not None:
            s = jnp.where(kj >= jnp.where(no_prev_block, ATTN_BLOCK, 0), s, NEG_INF)
        m = jnp.max(s, axis=-1, keepdims=True)
        p = jnp.exp(s - m)
        den = jnp.sum(p, axis=-1, keepdims=True)
        o = jnp.dot(p.astype(BF16), v, preferred_element_type=F32) / den
        lse = m + jnp.log(den)
        ob[b, pl.ds(q0, ATTN_BLOCK, stride=d), :] = o
        lb[b, pl.ds(q0, ATTN_BLOCK, stride=d), :] = jnp.broadcast_to(lse, (ATTN_BLOCK, HEAD_DIM))

    def banded_bias(d):
        return jnp.where(valid, (-slope * float(d)) * stepsf, NEG_INF)

    def key_window(cur, prev, n):
        if n > 0:
            return cur[(n - 1) * ATTN_BLOCK:(n + 1) * ATTN_BLOCK, :]
        last = prev.shape[0] - ATTN_BLOCK
        return jnp.concatenate([prev[last:, :], cur[:ATTN_BLOCK, :]], axis=0)

    b, d = 0, DILATIONS[0]
    bias = banded_bias(d)
    for n in range(SPAN // ATTN_BLOCK):
        attend(b, d, bias, n * ATTN_BLOCK, q_ref[n * ATTN_BLOCK:(n + 1) * ATTN_BLOCK, :],
               key_window(kc_ref, kp_ref, n), key_window(vc_ref, vp_ref, n),
               first_span if n == 0 else None)

    qf[...] = q_ref[...].astype(F32)
    kf[0:SPAN, :] = kp_ref[...].astype(F32)
    kf[SPAN:2 * SPAN, :] = kc_ref[...].astype(F32)
    vf[0:SPAN, :] = vp_ref[...].astype(F32)
    vf[SPAN:2 * SPAN, :] = vc_ref[...].astype(F32)
    for b, d in list(enumerate(DILATIONS))[1:]:
        bias = banded_bias(d)
        for r in range(d):
            for n in range(SPAN // (ATTN_BLOCK * d)):
                q0 = r + d * ATTN_BLOCK * n
                k0 = SPAN + q0 - d * ATTN_BLOCK
                attend(b, d, bias, q0,
                       qf[pl.ds(q0, ATTN_BLOCK, stride=d), :].astype(BF16),
                       kf[pl.ds(k0, 2 * ATTN_BLOCK, stride=d), :].astype(BF16),
                       vf[pl.ds(k0, 2 * ATTN_BLOCK, stride=d), :].astype(BF16),
                       first_span if n == 0 else None)

    chunk = 256
    for c in range(SPAN // chunk):
        rows = slice(c * chunk, (c + 1) * chunk)
        l0, l1, l2 = lb[0, rows, :], lb[1, rows, :], lb[2, rows, :]
        mx = jnp.maximum(jnp.maximum(l0, l1), l2)
        w0, w1, w2 = jnp.exp(l0 - mx), jnp.exp(l1 - mx), jnp.exp(l2 - mx)
        num = w0 * ob[0, rows, :] + w1 * ob[1, rows, :] + w2 * ob[2, rows, :]
        o_ref[rows, :] = (num / (w0 + w1 + w2)).astype(o_ref.dtype)


def _attention(z, slopes):
    n_spans = SEQ // SPAN
    k_col, v_col = N_HEADS, 2 * N_HEADS
    blk = (SPAN, HEAD_DIM)
    prev = lambda s: jnp.maximum(s - 1, 0)
    return pl.pallas_call(
        _attn_kernel,
        grid_spec=pltpu.PrefetchScalarGridSpec(
            num_scalar_prefetch=1,
            grid=(N_HEADS, n_spans),
            in_specs=[pl.BlockSpec(blk, lambda h, s, sl: (s, h)),
                      pl.BlockSpec(blk, lambda h, s, sl: (s, k_col + h)),
                      pl.BlockSpec(blk, lambda h, s, sl: (prev(s), k_col + h)),
                      pl.BlockSpec(blk, lambda h, s, sl: (s, v_col + h)),
                      pl.BlockSpec(blk, lambda h, s, sl: (prev(s), v_col + h))],
            out_specs=pl.BlockSpec(blk, lambda h, s, sl: (s, h)),
            scratch_shapes=[pltpu.VMEM((SPAN, HEAD_DIM), F32),
                            pltpu.VMEM((2 * SPAN, HEAD_DIM), F32),
                            pltpu.VMEM((2 * SPAN, HEAD_DIM), F32),
                            pltpu.VMEM((3, SPAN, HEAD_DIM), F32),
                            pltpu.VMEM((3, SPAN, HEAD_DIM), F32)]),
        out_shape=jax.ShapeDtypeStruct((SEQ, ATTN_WIDTH), BF16),
        compiler_params=_params(2),
        name="dilated_attention",
    )(slopes, z, z, z, z, z)


CONV_ROWS = 256
CONV_HALO = 32
CONV_CHUNK = 64


SUBLANES = 8
CONV_EXT = CONV_HALO + CONV_ROWS
CONV_SHIFTED = CONV_EXT - SUBLANES


def _conv_kernel(a_ref, g_ref, ah_ref, gh_ref, cw_ref, cb_ref, lw_ref, lb_ref, o_ref,
                 u_ext, shifted, c_scr):
    i = pl.program_id(0)
    n_ct = CONV_WIDTH // LANES
    for ct in range(n_ct):
        cols = slice(ct * LANES, (ct + 1) * LANES)
        u_ext[ct, CONV_HALO:, :] = (a_ref[:, cols].astype(F32)
                                    * jax.nn.sigmoid(g_ref[:, cols].astype(F32)))
        uh = ah_ref[:, cols].astype(F32) * jax.nn.sigmoid(gh_ref[:, cols].astype(F32))
        u_ext[ct, 0:CONV_HALO, :] = jnp.where(i == 0, 0.0, uh)

    first_tap = CONV_HALO - (CONV_KERNEL - 1)

    def slab(ct, carry):
        for k in range(1, SUBLANES):
            shifted[k - 1] = u_ext[ct, k:k + CONV_SHIFTED, :]
        for rc in range(CONV_ROWS // CONV_CHUNK):
            base = rc * CONV_CHUNK
            acc = jnp.broadcast_to(cb_ref[pl.ds(ct, 1), :], (CONV_CHUNK, LANES))
            for j in range(CONV_KERNEL):
                k = (first_tap + j) % SUBLANES
                row = base + (first_tap + j) - k
                if k == 0:
                    src = u_ext[ct, row:row + CONV_CHUNK, :]
                else:
                    src = shifted[k - 1, row:row + CONV_CHUNK, :]
                acc = acc + cw_ref[j, pl.ds(ct, 1), :] * src
            c_scr[ct, base:base + CONV_CHUNK, :] = acc
        return carry

    lax.fori_loop(0, n_ct, slab, 0)

    total = c_scr[0]
    for ct in range(1, n_ct):
        total = total + c_scr[ct]
    mu = jnp.sum(total, axis=-1, keepdims=True) * (1.0 / CONV_WIDTH)
    sq = jnp.square(c_scr[0] - mu)
    for ct in range(1, n_ct):
        sq = sq + jnp.square(c_scr[ct] - mu)
    var = jnp.sum(sq, axis=-1, keepdims=True) * (1.0 / CONV_WIDTH)
    rstd = lax.rsqrt(var + EPS)
    for ct in range(n_ct):
        cols = slice(ct * LANES, (ct + 1) * LANES)
        y = (c_scr[ct] - mu) * rstd * lw_ref[:, cols] + lb_ref[:, cols]
        o_ref[:, cols] = (y * jax.nn.sigmoid(y)).astype(o_ref.dtype)


def _conformer_conv(z, conv_w, conv_b, ln_w, ln_b):
    a_col = 3 * ATTN_WIDTH // CONV_WIDTH
    g_col = a_col + 1
    n_ct = CONV_WIDTH // LANES
    halo_blocks = CONV_ROWS // CONV_HALO
    halo = lambda i: jnp.maximum(i * halo_blocks - 1, 0)
    vec = pl.BlockSpec((1, CONV_WIDTH), lambda i: (0, 0))
    return pl.pallas_call(
        _conv_kernel,
        grid=(SEQ // CONV_ROWS,),
        in_specs=[pl.BlockSpec((CONV_ROWS, CONV_WIDTH), lambda i: (i, a_col)),
                  pl.BlockSpec((CONV_ROWS, CONV_WIDTH), lambda i: (i, g_col)),
                  pl.BlockSpec((CONV_HALO, CONV_WIDTH), lambda i: (halo(i), a_col)),
                  pl.BlockSpec((CONV_HALO, CONV_WIDTH), lambda i: (halo(i), g_col)),
                  pl.BlockSpec((CONV_KERNEL, n_ct, LANES), lambda i: (0, 0, 0)),
                  pl.BlockSpec((n_ct, LANES), lambda i: (0, 0)),
                  vec, vec],
        out_specs=pl.BlockSpec((CONV_ROWS, CONV_WIDTH), lambda i: (i, 0)),
        out_shape=jax.ShapeDtypeStruct((SEQ, CONV_WIDTH), BF16),
        scratch_shapes=[pltpu.VMEM((n_ct, CONV_EXT, LANES), F32),
                        pltpu.VMEM((SUBLANES - 1, CONV_SHIFTED, LANES), F32),
                        pltpu.VMEM((n_ct, CONV_ROWS, LANES), F32)],
        compiler_params=_params(1),
        name="conformer_conv",
    )(z, z, z, z, conv_w, conv_b, ln_w, ln_b)


OUT_TM = 1024
OUT_TN = 512


def _outproj_kernel(attn_ref, conv_ref, wa_ref, wc_ref, x_ref, o_ref, wa_bf, wc_bf):
    @pl.when(pl.program_id(1) == 0)
    def _():
        wa_bf[...] = wa_ref[...].astype(BF16)
        wc_bf[...] = wc_ref[...].astype(BF16)

    acc = jnp.dot(attn_ref[...], wa_bf[...], preferred_element_type=F32)
    acc = acc + jnp.dot(conv_ref[...], wc_bf[...], preferred_element_type=F32)
    o_ref[...] = x_ref[...] + acc


def _outproj(attn, conv, w, x):
    n = x.shape[0]
    conv_blk = ATTN_WIDTH // CONV_WIDTH
    return pl.pallas_call(
        _outproj_kernel,
        grid=(D_MODEL // OUT_TN, n // OUT_TM),
        in_specs=[pl.BlockSpec((OUT_TM, ATTN_WIDTH), lambda j, i: (i, 0)),
                  pl.BlockSpec((OUT_TM, CONV_WIDTH), lambda j, i: (i, 0)),
                  pl.BlockSpec((ATTN_WIDTH, OUT_TN), lambda j, i: (0, j)),
                  pl.BlockSpec((CONV_WIDTH, OUT_TN), lambda j, i: (conv_blk, j)),
                  pl.BlockSpec((OUT_TM, OUT_TN), lambda j, i: (i, j))],
        out_specs=pl.BlockSpec((OUT_TM, OUT_TN), lambda j, i: (i, j)),
        out_shape=jax.ShapeDtypeStruct((n, D_MODEL), F32),
        scratch_shapes=[pltpu.VMEM((ATTN_WIDTH, OUT_TN), BF16),
                        pltpu.VMEM((CONV_WIDTH, OUT_TN), BF16)],
        compiler_params=_params(2),
        name="outproj",
    )(attn, conv, w, w, x)


def _router_kernel(h_ref, nw_ref, wr_ref, br_ref, hn_ref, ids_ref, gates_ref, counts_ref, seen):
    x = h_ref[...]
    ms = jnp.mean(x * x, axis=-1, keepdims=True)
    hn = x * lax.rsqrt(ms + EPS) * nw_ref[...]
    words = _pack_rows(hn)
    for s in range(X_ROW_TILES):
        hn_ref[:, s, :] = words[:, s * LANES:(s + 1) * LANES]

    h_hi = hn.astype(BF16)
    h_lo = (hn - h_hi.astype(F32)).astype(BF16)
    hi_both = jnp.dot(h_hi, wr_ref[...], preferred_element_type=F32)
    lo_hi = jnp.dot(h_lo, wr_ref[:, :LANES], preferred_element_type=F32)
    logits = hi_both[:, :LANES] + (hi_both[:, LANES:] + lo_hi) + br_ref[...]
    lane = lax.broadcasted_iota(jnp.int32, logits.shape, 1)
    big = jnp.int32(1 << 20)

    is_g = lane < N_GROUPS
    gl = jnp.where(is_g, logits, -jnp.inf)
    ge = jnp.exp(gl - jnp.max(gl, axis=-1, keepdims=True))
    g_prob = ge / jnp.sum(ge, axis=-1, keepdims=True)
    g_w = jnp.max(g_prob, axis=-1, keepdims=True)
    g_top = jnp.min(jnp.where(is_g & (g_prob == g_w), lane, big), axis=-1, keepdims=True)

    eidx = lane - N_GROUPS
    grp_shift = EXPERTS_PER_GROUP.bit_length() - 1
    in_grp = (eidx >= 0) & (eidx < N_EXPERTS) & ((eidx >> grp_shift) == g_top)
    el = jnp.where(in_grp, logits, -jnp.inf)
    ee = jnp.exp(el - jnp.max(el, axis=-1, keepdims=True))
    e_prob = ee / jnp.sum(ee, axis=-1, keepdims=True)
    v1 = jnp.max(jnp.where(in_grp, e_prob, -1.0), axis=-1, keepdims=True)
    i1 = jnp.min(jnp.where(in_grp & (e_prob == v1), lane, big), axis=-1, keepdims=True)
    rest = in_grp & (lane != i1)
    v2 = jnp.max(jnp.where(rest, e_prob, -1.0), axis=-1, keepdims=True)
    i2 = jnp.min(jnp.where(rest & (e_prob == v2), lane, big), axis=-1, keepdims=True)
    tot = v1 + v2
    gates_ref[...] = jnp.where(lane == 0, g_w * v1 / tot,
                               jnp.where(lane == 1, g_w * v2 / tot, 0.0))

    e1 = i1 - N_GROUPS
    e2 = i2 - N_GROUPS
    first_choice = lane == e1
    second_choice = lane == e2 + N_EXPERTS
    chosen = jnp.where(first_choice | second_choice, 1.0, 0.0)
    t_row = lax.broadcasted_iota(jnp.int32, (ROUTER_ROWS, ROUTER_ROWS), 0)
    t_col = lax.broadcasted_iota(jnp.int32, (ROUTER_ROWS, ROUTER_ROWS), 1)
    earlier = jnp.where(t_col < t_row, 1.0, 0.0).astype(BF16)
    chosen = chosen + pltpu.roll(chosen, N_EXPERTS, axis=1)
    before = jnp.dot(earlier, chosen.astype(BF16), preferred_element_type=F32)

    @pl.when(pl.program_id(0) == 0)
    def _():
        seen[...] = jnp.zeros_like(seen)

    before = before + seen[0:1, :]
    rank1 = jnp.sum(jnp.where(first_choice, before, 0.0), axis=-1, keepdims=True)
    rank2 = jnp.sum(jnp.where(second_choice, before, 0.0), axis=-1, keepdims=True)
    seen[0:1, :] = seen[0:1, :] + jnp.sum(chosen, axis=0, keepdims=True)
    counts_ref[...] = jnp.broadcast_to(seen[0:1, :], counts_ref.shape).astype(jnp.int32)

    ids_ref[...] = jnp.where(
        lane == 0, e1, jnp.where(
            lane == 1, e2, jnp.where(
                lane == 2, rank1.astype(jnp.int32), jnp.where(
                    lane == 3, rank2.astype(jnp.int32), 0))))


def _router(h, norm_w, w_router, b_router):
    return pl.pallas_call(
        _router_kernel,
        grid=(SEQ // ROUTER_ROWS,),
        in_specs=[pl.BlockSpec((ROUTER_ROWS, D_MODEL), lambda i: (i, 0)),
                  pl.BlockSpec((1, D_MODEL), lambda i: (0, 0)),
                  pl.BlockSpec((D_MODEL, 2 * LANES), lambda i: (0, 0)),
                  pl.BlockSpec((1, LANES), lambda i: (0, 0))],
        out_specs=[pl.BlockSpec((ROUTER_ROWS, X_ROW_TILES, LANES), lambda i: (i, 0, 0)),
                   pl.BlockSpec((ROUTER_ROWS, LANES), lambda i: (i, 0)),
                   pl.BlockSpec((ROUTER_ROWS, LANES), lambda i: (i, 0)),
                   pl.BlockSpec((SUBLANES, LANES), lambda i: (0, 0))],
        out_shape=[jax.ShapeDtypeStruct((SEQ, X_ROW_TILES, LANES), jnp.uint32),
                   jax.ShapeDtypeStruct((SEQ, LANES), jnp.int32),
                   jax.ShapeDtypeStruct((SEQ, LANES), F32),
                   jax.ShapeDtypeStruct((SUBLANES, LANES), jnp.int32)],
        scratch_shapes=[pltpu.VMEM((SUBLANES, LANES), F32)],
        compiler_params=_params(1),
        name="ffn_norm_router",
    )(h, norm_w, w_router, b_router)


def _staged_token(stage, slot, r):
    return stage.at[slot, pl.ds(pl.multiple_of(r * GATHER_PITCH, SUBLANES), X_ROW_TILES)]


def _issue_token_rows(toks, nv, hn_hbm, stage, slot, sem):
    def start(r, carry):
        pltpu.make_async_copy(hn_hbm.at[toks[0, 0, r]], _staged_token(stage, slot, r),
                              sem.at[slot]).start()
        return carry

    def start_group(g, carry):
        for u in range(DMA_UNROLL):
            start(g * DMA_UNROLL + u, carry)
        return carry

    groups = nv // DMA_UNROLL
    lax.fori_loop(0, groups, start_group, 0)
    lax.fori_loop(groups * DMA_UNROLL, nv, start, 0)


def _wait_token_rows(nv, stage, slot, sem):
    def wait_rows(n_tokens):
        span = stage.at[slot, pl.ds(0, n_tokens * X_ROW_TILES)]
        pltpu.make_async_copy(span, span, sem.at[slot]).wait()

    def wait_many(c, carry):
        wait_rows(WAIT_ROWS)
        return carry

    def wait_one(c, carry):
        wait_rows(1)
        return carry

    lax.fori_loop(0, nv // WAIT_ROWS, wait_many, 0)
    lax.fori_loop(0, nv % WAIT_ROWS, wait_one, 0)


def _staged_block(stage, slot):
    return jnp.concatenate([stage[slot, pl.ds(s, MOE_ROWS, stride=GATHER_PITCH), :]
                            for s in range(X_ROW_TILES)], axis=-1)


def _unpack_rows(words):
    lo, hi = _unpack_words(words)
    return jnp.concatenate([lo.astype(BF16), hi.astype(BF16)], axis=-1)


def _stream_weight_tile(w, first_ref, seq_ref, n_tiles_ref, tile_copies, consume):
    n_tiles = n_tiles_ref[0]

    @pl.when(w == 0)
    def _():
        for k in range(WEIGHT_RING):
            @pl.when(k < n_tiles)
            def _():
                for c in tile_copies(k, k):
                    c.start(priority=WEIGHT_DMA_PRIORITY)

    @pl.when(first_ref[w] == 1)
    def _():
        k = seq_ref[w]
        slot = k % WEIGHT_RING
        for c in tile_copies(k, slot):
            c.wait()
        consume(slot)

        @pl.when(k + WEIGHT_RING < n_tiles)
        def _():
            for c in tile_copies(k + WEIGHT_RING, slot):
                c.start(priority=WEIGHT_DMA_PRIORITY)


def _moe_up_kernel(ib_ref, nv_ref, b_ref, ot_ref, valid_ref, first_ref, seq_ref, te_ref, tt_ref,
                   nt_ref, fetch_ref, xslot_ref, sslot_ref, pf_ref, pfb_ref, pfnv_ref,
                   tok_ref, tok_next_ref, hn_hbm, wg_hbm, wu_hbm, hid_ref,
                   stage, x_cache, ring_g, ring_u, wg_bf, wu_bf, row_sem, sem):
    w = pl.program_id(0)

    @pl.when(w == 0)
    def _():
        stage[...] = jnp.zeros_like(stage)
        _issue_token_rows(tok_ref, nv_ref[0], hn_hbm, stage, 0, row_sem)

    @pl.when(pf_ref[w] == 1)
    def _():
        _issue_token_rows(tok_next_ref, pfnv_ref[w], hn_hbm, stage, 1 - sslot_ref[w], row_sem)

    def tile_copies(k, slot):
        e = te_ref[k]
        col = pl.multiple_of(tt_ref[k] * UP_TILE, UP_TILE)
        return (pltpu.make_async_copy(wg_hbm.at[e, :, pl.ds(col, UP_TILE)], ring_g.at[slot],
                                      sem.at[0, slot]),
                pltpu.make_async_copy(wu_hbm.at[e, :, pl.ds(col, UP_TILE)], ring_u.at[slot],
                                      sem.at[1, slot]))

    def consume(slot):
        wg_bf[...] = ring_g[slot].astype(BF16)
        wu_bf[...] = ring_u[slot].astype(BF16)

    _stream_weight_tile(w, first_ref, seq_ref, nt_ref, tile_copies, consume)

    @pl.when(valid_ref[w] == 1)
    def _():
        x_slot = xslot_ref[w]

        @pl.when(fetch_ref[w] == 1)
        def _():
            slot = sslot_ref[w]
            _wait_token_rows(nv_ref[w], stage, slot, row_sem)
            x_cache[x_slot] = _unpack_rows(_staged_block(stage, slot))

        x = x_cache[x_slot]
        g = jnp.dot(x, wg_bf[...], preferred_element_type=F32)
        u = jnp.dot(x, wu_bf[...], preferred_element_type=F32)
        hid_ref[...] = (g * jax.nn.sigmoid(g) * u).astype(hid_ref.dtype)

    @pl.when(valid_ref[w] == 0)
    def _():
        hid_ref[...] = jnp.zeros_like(hid_ref)


def _moe_up(items, info, n_valid, row_tok, hn_words, w_gate, w_up):
    n_items = N_MOE_BLOCKS * (D_FF // UP_TILE)
    in_blk = items[0]
    schedule = _gather_schedule(in_blk, info, n_valid)
    n_prefetch = 1 + len(items) + len(schedule)
    toks = row_tok.reshape(N_MOE_BLOCKS, 1, MOE_ROWS)
    smem_block = lambda index_map: pl.BlockSpec((1, 1, MOE_ROWS), index_map,
                                                memory_space=pltpu.SMEM)
    any_spec = pl.BlockSpec(memory_space=pl.ANY)
    return pl.pallas_call(
        _moe_up_kernel,
        grid_spec=pltpu.PrefetchScalarGridSpec(
            num_scalar_prefetch=n_prefetch,
            grid=(n_items,),
            in_specs=[smem_block(lambda w, *refs: (refs[0][w], 0, 0)),
                      smem_block(lambda w, *refs: (refs[n_prefetch - 2][w], 0, 0)),
                      any_spec, any_spec, any_spec],
            out_specs=pl.BlockSpec((MOE_ROWS, UP_TILE),
                                   lambda w, ib, nv, b, ot, *_: (b[w], ot[w])),
            scratch_shapes=[pltpu.VMEM((2, MOE_ROWS * GATHER_PITCH, LANES), jnp.uint32),
                            pltpu.VMEM((X_CACHE_SLOTS, MOE_ROWS, D_MODEL), BF16),
                            pltpu.VMEM((WEIGHT_RING, D_MODEL, UP_TILE), F32),
                            pltpu.VMEM((WEIGHT_RING, D_MODEL, UP_TILE), F32),
                            pltpu.VMEM((D_MODEL, UP_TILE), BF16),
                            pltpu.VMEM((D_MODEL, UP_TILE), BF16),
                            pltpu.SemaphoreType.DMA((2,)),
                            pltpu.SemaphoreType.DMA((2, WEIGHT_RING))]),
        out_shape=jax.ShapeDtypeStruct((N_MOE_ROWS, D_FF), BF16),
        compiler_params=_params(1),
        name="moe_gate_up",
    )(in_blk, _lookup(n_valid, in_blk), *items[1:], *schedule, toks, toks, hn_words, w_gate, w_up)


def _moe_down_kernel(ib_ref, b_ref, ot_ref, valid_ref, first_ref, seq_ref, te_ref, tt_ref, nt_ref,
                     hid_ref, wd_hbm, y_ref, ring, wd_bf, sem):
    w = pl.program_id(0)

    def tile_copies(k, slot):
        col = pl.multiple_of(tt_ref[k] * DOWN_TILE, DOWN_TILE)
        return (pltpu.make_async_copy(wd_hbm.at[te_ref[k], :, pl.ds(col, DOWN_TILE)],
                                      ring.at[slot], sem.at[slot]),)

    def consume(slot):
        wd_bf[...] = ring[slot].astype(BF16)

    _stream_weight_tile(w, first_ref, seq_ref, nt_ref, tile_copies, consume)

    @pl.when(valid_ref[w] == 1)
    def _():
        y = jnp.dot(hid_ref[...], wd_bf[...], preferred_element_type=F32)
        words = _pack_rows(y)
        for s in range(DOWN_TILE // 2 // LANES):
            y_ref[:, s, :] = words[:, s * LANES:(s + 1) * LANES]

    @pl.when(valid_ref[w] == 0)
    def _():
        y_ref[...] = jnp.zeros_like(y_ref)


def _moe_down(items, hid, w_down):
    n_items = N_MOE_BLOCKS * (D_MODEL // DOWN_TILE)
    tile_rows = DOWN_TILE // 2 // LANES
    return pl.pallas_call(
        _moe_down_kernel,
        grid_spec=pltpu.PrefetchScalarGridSpec(
            num_scalar_prefetch=9,
            grid=(n_items,),
            in_specs=[pl.BlockSpec((MOE_ROWS, D_FF), lambda w, ib, *_: (ib[w], 0)),
                      pl.BlockSpec(memory_space=pl.ANY)],
            out_specs=pl.BlockSpec((MOE_ROWS, tile_rows, LANES),
                                   lambda w, ib, b, ot, *_: (b[w], ot[w], 0)),
            scratch_shapes=[pltpu.VMEM((WEIGHT_RING, D_FF, DOWN_TILE), F32),
                            pltpu.VMEM((D_FF, DOWN_TILE), BF16),
                            pltpu.SemaphoreType.DMA((WEIGHT_RING,))]),
        out_shape=jax.ShapeDtypeStruct((N_MOE_ROWS, Y_ROW_TILES, LANES), jnp.uint32),
        compiler_params=_params(1),
        name="moe_down",
    )(*items, hid, w_down)


def _combine_kernel(dest_ref, dest_next_ref, h_ref, gate_ref, y_hbm, o_ref, buf, sem):
    i = pl.program_id(0)

    def issue(dests, slot):
        def start(r, carry):
            for k in range(TOP_K):
                dst = buf.at[slot, k, pl.ds(r * COMBINE_PITCH, Y_ROW_TILES)]
                pltpu.make_async_copy(y_hbm.at[dests[0, k, r]], dst, sem.at[slot]).start()
            return carry
        lax.fori_loop(0, COMBINE_ROWS, start, 0, unroll=DMA_UNROLL // TOP_K)

    @pl.when(i == 0)
    def _():
        issue(dest_ref, 0)

    @pl.when(i + 1 < pl.num_programs(0))
    def _():
        issue(dest_next_ref, (i + 1) % 2)

    slot = i % 2

    for k in range(TOP_K):
        for c in range(COMBINE_ROWS // WAIT_ROWS):
            span = buf.at[slot, k, pl.ds(c * WAIT_ROWS * Y_ROW_TILES, WAIT_ROWS * Y_ROW_TILES)]
            pltpu.make_async_copy(span, span, sem.at[slot]).wait()

    g0 = gate_ref[:, 0:1]
    g1 = gate_ref[:, 1:2]
    tiles_per_half = DOWN_TILE // 2 // LANES
    for s in range(Y_ROW_TILES):
        lo_col = (s // tiles_per_half) * DOWN_TILE + (s % tiles_per_half) * LANES
        lo0, hi0 = _unpack_words(buf[slot, 0, pl.ds(s, COMBINE_ROWS, stride=COMBINE_PITCH), :])
        lo1, hi1 = _unpack_words(buf[slot, 1, pl.ds(s, COMBINE_ROWS, stride=COMBINE_PITCH), :])
        for col, y0, y1 in ((lo_col, lo0, lo1), (lo_col + DOWN_TILE // 2, hi0, hi1)):
            cols = slice(col, col + LANES)
            o_ref[:, cols] = h_ref[:, cols] + (y0 * g0 + y1 * g1)


def _combine(dest, h, gates, y):
    n_tiles = SEQ // COMBINE_ROWS
    dests = jnp.stack([d.reshape(n_tiles, COMBINE_ROWS) for d in dest], axis=1)
    blk = (1, TOP_K, COMBINE_ROWS)
    return pl.pallas_call(
        _combine_kernel,
        grid=(n_tiles,),
        in_specs=[pl.BlockSpec(blk, lambda i: (i, 0, 0), memory_space=pltpu.SMEM),
                  pl.BlockSpec(blk, lambda i: (jnp.minimum(i + 1, n_tiles - 1), 0, 0),
                               memory_space=pltpu.SMEM),
                  pl.BlockSpec((COMBINE_ROWS, D_MODEL), lambda i: (i, 0)),
                  pl.BlockSpec((COMBINE_ROWS, LANES), lambda i: (i, 0)),
                  pl.BlockSpec(memory_space=pl.ANY)],
        out_specs=pl.BlockSpec((COMBINE_ROWS, D_MODEL), lambda i: (i, 0)),
        out_shape=jax.ShapeDtypeStruct((SEQ, D_MODEL), F32),
        scratch_shapes=[pltpu.VMEM((2, TOP_K, COMBINE_ROWS * COMBINE_PITCH, LANES), jnp.uint32),
                        pltpu.SemaphoreType.DMA((2,))],
        compiler_params=_params(1),
        name="moe_combine",
    )(dests, dests, h, gates, y)


def _lookup(table, idx):
    pos = jnp.arange(table.shape[0], dtype=jnp.int32)
    return jnp.sum(jnp.where(idx[:, None] == pos[None, :], table[None, :], 0), axis=1)


def _count_le(ends, x):
    return jnp.sum((ends[None, :] <= x[:, None]).astype(jnp.int32), axis=1)


def _work_items(n_blocks_e, block_start_e, tiles):
    i32 = lambda a: a.astype(jnp.int32)
    n_items = N_MOE_BLOCKS * tiles
    per_e = n_blocks_e * tiles
    ends = jnp.cumsum(per_e)
    total = ends[-1]
    idx = jnp.arange(n_items, dtype=jnp.int32)
    valid = idx < total
    w = jnp.minimum(idx, total - 1)
    e = _count_le(ends, w)
    local = w - _lookup(ends - per_e, e)
    nb = _lookup(n_blocks_e, e)
    first_blk = _lookup(block_start_e, e)
    spare = idx - total
    w_tile = local // nb
    o_tile = jnp.where(valid, w_tile, spare % tiles)
    blk = jnp.where(valid, first_blk + local % nb, total // tiles + spare // tiles)
    first = valid & (local % nb == 0)
    active = n_blocks_e > 0
    active_rank = jnp.cumsum(active) - active
    seq = _lookup(active_rank, e) * tiles + w_tile
    k = jnp.arange(N_EXPERTS * tiles, dtype=jnp.int32)
    experts = jnp.arange(N_EXPERTS, dtype=jnp.int32)
    is_kth = active[None, :] & (active_rank[None, :] == (k // tiles)[:, None])
    tile_e = jnp.sum(jnp.where(is_kth, experts[None, :], 0), axis=1)
    tile_t = k % tiles
    n_tiles = (jnp.sum(active) * tiles).reshape(1)
    in_blk = first_blk + local % nb
    items = (i32(in_blk), i32(blk), i32(o_tile), i32(valid), i32(first), i32(seq), i32(tile_e),
             i32(tile_t), i32(n_tiles))
    return items, dict(blocks_of_expert=nb, local_block=local % nb, weight_tile=w_tile, valid=valid)


def _gather_schedule(in_blk, info, n_valid):
    i32 = lambda a: a.astype(jnp.int32)
    n = in_blk.shape[0]
    cached = (info['blocks_of_expert'] <= X_CACHE_SLOTS) & (info['weight_tile'] > 0)
    fetch = info['valid'] & ~cached
    x_slot = info['local_block'] % X_CACHE_SLOTS
    stage_slot = (jnp.cumsum(fetch) - 1) % 2
    idx = jnp.arange(n, dtype=jnp.int32)
    later = lax.cummin(jnp.where(fetch, idx, n), axis=0, reverse=True)
    next_fetch = jnp.concatenate([later[1:], jnp.full((1,), n, later.dtype)])
    prefetch = fetch & (next_fetch < n)
    pf_blk = jnp.where(prefetch, _lookup(in_blk, jnp.minimum(next_fetch, n - 1)), in_blk)
    return (i32(fetch), i32(x_slot), i32(stage_slot), i32(prefetch), i32(pf_blk),
            i32(_lookup(n_valid, pf_blk)))


def _dispatch_plan(ids, counts):
    counts = counts[0, :N_EXPERTS]
    n_blocks_e = (counts + MOE_ROWS - 1) // MOE_ROWS
    block_ends = jnp.cumsum(n_blocks_e)
    block_start_e = block_ends - n_blocks_e
    first_row_e = block_start_e * MOE_ROWS
    dest = [(_lookup(first_row_e, ids[:, k]) + ids[:, TOP_K + k]).astype(jnp.int32)
            for k in range(TOP_K)]
    token = jnp.arange(SEQ, dtype=jnp.int32)
    row_tok = jnp.zeros((N_MOE_ROWS,), jnp.int32).at[jnp.concatenate(dest)].set(
        jnp.concatenate([token] * TOP_K))
    blk = jnp.arange(N_MOE_BLOCKS, dtype=jnp.int32)
    blk_e = jnp.minimum(_count_le(block_ends, blk), N_EXPERTS - 1)
    n_valid = jnp.clip(_lookup(counts, blk_e) - (blk - _lookup(block_start_e, blk_e)) * MOE_ROWS,
                       0, MOE_ROWS)
    n_valid = jnp.where(blk < block_ends[-1], n_valid, 0).astype(jnp.int32)
    return dest, row_tok, n_valid, n_blocks_e.astype(jnp.int32), block_start_e.astype(jnp.int32)


def kernel(x, norm_mix_w, w_in, q_norm_w, k_norm_w, conv_w, conv_b, conv_ln_w, conv_ln_b, w_out,
           norm_ffn_w, w_group_router, b_group_router, w_expert_router, b_expert_router,
           w_gate, w_up, w_down):
    h = x.reshape(SEQ, D_MODEL)
    slopes = jnp.exp2(-8.0 * jnp.arange(1, N_HEADS + 1, dtype=F32) / N_HEADS)
    for l in range(norm_mix_w.shape[0]):
        hn = _rmsnorm(h, norm_mix_w[l][None, :])
        qk_norm = jnp.concatenate([jnp.tile(q_norm_w[l], N_HEADS), jnp.tile(k_norm_w[l], N_HEADS)])
        z = _inproj(hn, w_in[l], qk_norm[None, :])
        attn = _attention(z, slopes)
        n_ct = CONV_WIDTH // LANES
        conv = _conformer_conv(z, conv_w[l].reshape(CONV_KERNEL, n_ct, LANES),
                               conv_b[l].reshape(n_ct, LANES), conv_ln_w[l][None, :],
                               conv_ln_b[l][None, :])
        h = _outproj(attn, conv, w_out[l], h)

        pad = LANES - N_GROUPS - N_EXPERTS
        w_router = jnp.concatenate([w_group_router[l], w_expert_router[l],
                                    jnp.zeros((D_MODEL, pad), F32)], axis=1)
        b_router = jnp.concatenate([b_group_router[l], b_expert_router[l], jnp.zeros((pad,), F32)])
        w_router_hi = w_router.astype(BF16)
        w_router_lo = (w_router - w_router_hi.astype(F32)).astype(BF16)
        hn_words, ids, gates, counts = _router(h, norm_ffn_w[l][None, :],
                                               jnp.concatenate([w_router_hi, w_router_lo], axis=1),
                                               b_router[None, :])
        dest, row_tok, n_valid, n_blocks_e, block_start_e = _dispatch_plan(ids, counts)
        up_items, up_info = _work_items(n_blocks_e, block_start_e, D_FF // UP_TILE)
        hid = _moe_up(up_items, up_info, n_valid, row_tok, hn_words, w_gate[l], w_up[l])
        down_items, _ = _work_items(n_blocks_e, block_start_e, D_MODEL // DOWN_TILE)
        y = _moe_down(down_items, hid, w_down[l])
        h = _combine(dest, h, gates, y)
    return h.reshape(x.shape)
```

```python
import math

import jax
import jax.numpy as jnp
from jax import lax
from jax.experimental import pallas as pl
from jax.experimental.pallas import tpu as pltpu

F32 = jnp.float32
BF16 = jnp.bfloat16

D_MODEL = 4096
SEQ = 8192
HEAD_DIM = 128
N_HEADS = 16
ATTN_WIDTH = N_HEADS * HEAD_DIM
CONV_WIDTH = D_MODEL - ATTN_WIDTH
CONV_KERNEL = 31
IN_COLS = 3 * ATTN_WIDTH + 2 * CONV_WIDTH
DILATIONS = (1, 4, 16)
ATTN_BLOCK = 128
N_KEYS = 128
N_GROUPS = 8
EXPERTS_PER_GROUP = 8
N_EXPERTS = N_GROUPS * EXPERTS_PER_GROUP
TOP_K = 2
D_FF = 1024
EPS = 1e-6
NEG_INF = -1e30

LANES = 128
VMEM_LIMIT = 56 * 1024 * 1024

SPAN = DILATIONS[-1] * ATTN_BLOCK

MOE_ROWS = 256
N_ASSIGN = SEQ * TOP_K
N_MOE_ROWS = N_ASSIGN + N_EXPERTS * MOE_ROWS
N_MOE_BLOCKS = N_MOE_ROWS // MOE_ROWS
UP_TILE = 512
DOWN_TILE = 4096
WEIGHT_RING = 2
WEIGHT_DMA_PRIORITY = 1
ROUTER_ROWS = 256
COMBINE_ROWS = 128
X_CACHE_SLOTS = 2
X_ROW_TILES = D_MODEL // 2 // LANES
Y_ROW_TILES = D_MODEL // 2 // LANES
COMBINE_PITCH = Y_ROW_TILES + 4
GATHER_PITCH = X_ROW_TILES + 8
DMA_UNROLL = 8
WAIT_ROWS = 32


def _params(n_axes):
    return pltpu.CompilerParams(dimension_semantics=("arbitrary",) * n_axes,
                                vmem_limit_bytes=VMEM_LIMIT)


def _pack_rows(x):
    half = x.shape[1] // 2
    lo = pltpu.bitcast(x[:, :half].astype(BF16).astype(F32), jnp.uint32)
    hi = pltpu.bitcast(x[:, half:].astype(BF16).astype(F32), jnp.uint32)
    return (hi & jnp.uint32(0xFFFF0000)) | (lo >> 16)


def _unpack_words(words):
    return (pltpu.bitcast(words << 16, F32),
            pltpu.bitcast(words & jnp.uint32(0xFFFF0000), F32))


def _rmsnorm_kernel(x_ref, w_ref, o_ref):
    x = x_ref[...]
    ms = jnp.mean(x * x, axis=-1, keepdims=True)
    o_ref[...] = (x * lax.rsqrt(ms + EPS) * w_ref[...]).astype(o_ref.dtype)


def _rmsnorm(x, w, rows=256):
    n, d = x.shape
    return pl.pallas_call(
        _rmsnorm_kernel,
        grid=(n // rows,),
        in_specs=[pl.BlockSpec((rows, d), lambda i: (i, 0)),
                  pl.BlockSpec((1, d), lambda i: (0, 0))],
        out_specs=pl.BlockSpec((rows, d), lambda i: (i, 0)),
        out_shape=jax.ShapeDtypeStruct((n, d), BF16),
        compiler_params=_params(1),
        name="rmsnorm",
    )(x, w)


IN_TM = 1024
IN_TN = 512
IN_CHUNKS = (256, 256, 256, 128, 128)
QK_TILES = 2 * ATTN_WIDTH // IN_TN


def _inproj_kernel(hn_ref, w_ref, nw_ref, o_ref, w_bf):
    j = pl.program_id(0)

    @pl.when(pl.program_id(1) == 0)
    def _():
        w_bf[...] = w_ref[...].astype(BF16)

    is_qk = j < QK_TILES
    row0 = 0
    for chunk in IN_CHUNKS:
        rows = slice(row0, row0 + chunk)
        row0 += chunk
        acc = jnp.dot(hn_ref[rows, :], w_bf[...], preferred_element_type=F32)
        for hd in range(IN_TN // HEAD_DIM):
            sl = slice(hd * HEAD_DIM, (hd + 1) * HEAD_DIM)
            t = acc[:, sl]
            ms = jnp.mean(t * t, axis=-1, keepdims=True)
            normed = t * lax.rsqrt(ms + EPS) * nw_ref[:, sl]
            o_ref[rows, sl] = jnp.where(is_qk, normed, t).astype(o_ref.dtype)


def _inproj(hn, w, qk_norm_w):
    n = hn.shape[0]
    return pl.pallas_call(
        _inproj_kernel,
        grid=(IN_COLS // IN_TN, n // IN_TM),
        in_specs=[pl.BlockSpec((IN_TM, D_MODEL), lambda j, i: (i, 0)),
                  pl.BlockSpec((D_MODEL, IN_TN), lambda j, i: (0, j)),
                  pl.BlockSpec((1, IN_TN), lambda j, i: (0, jnp.minimum(j, QK_TILES - 1)))],
        out_specs=pl.BlockSpec((IN_TM, IN_TN), lambda j, i: (i, j)),
        out_shape=jax.ShapeDtypeStruct((n, IN_COLS), BF16),
        scratch_shapes=[pltpu.VMEM((D_MODEL, IN_TN), BF16)],
        compiler_params=_params(2),
        name="inproj",
    )(hn, w, qk_norm_w)


def _attn_kernel(slopes_ref, q_ref, kc_ref, kp_ref, vc_ref, vp_ref, o_ref, qf, kf, vf, ob, lb):
    h = pl.program_id(0)
    first_span = pl.program_id(1) == 0
    slope = slopes_ref[h]

    qi = lax.broadcasted_iota(jnp.int32, (ATTN_BLOCK, 2 * ATTN_BLOCK), 0)
    kj = lax.broadcasted_iota(jnp.int32, (ATTN_BLOCK, 2 * ATTN_BLOCK), 1)
    steps = qi + ATTN_BLOCK - kj
    valid = (steps >= 0) & (steps <= N_KEYS)
    stepsf = steps.astype(F32)
    scale = 1.0 / math.sqrt(HEAD_DIM)

    def attend(b, d, bias, q0, q, k, v, no_prev_block):
        s = lax.dot_general(q, k, (((1,), (1,)), ((), ())),
                            preferred_element_type=F32) * scale + bias
        if no_prev_block is not None:
            s = jnp.where(kj >= jnp.where(no_prev_block, ATTN_BLOCK, 0), s, NEG_INF)
        m = jnp.max(s, axis=-1, keepdims=True)
        p = jnp.exp(s - m)
        den = jnp.sum(p, axis=-1, keepdims=True)
        o = jnp.dot(p.astype(BF16), v, preferred_element_type=F32) / den
        lse = m + jnp.log(den)
        ob[b, pl.ds(q0, ATTN_BLOCK, stride=d), :] = o
        lb[b, pl.ds(q0, ATTN_BLOCK, stride=d), :] = jnp.broadcast_to(lse, (ATTN_BLOCK, HEAD_DIM))

    def banded_bias(d):
        return jnp.where(valid, (-slope * float(d)) * stepsf, NEG_INF)

    def key_window(cur, prev, n):
        if n > 0:
            return cur[(n - 1) * ATTN_BLOCK:(n + 1) * ATTN_BLOCK, :]
        last = prev.shape[0] - ATTN_BLOCK
        return jnp.concatenate([prev[last:, :], cur[:ATTN_BLOCK, :]], axis=0)

    b, d = 0, DILATIONS[0]
    bias = banded_bias(d)
    for n in range(SPAN // ATTN_BLOCK):
        attend(b, d, bias, n * ATTN_BLOCK, q_ref[n * ATTN_BLOCK:(n + 1) * ATTN_BLOCK, :],
               key_window(kc_ref, kp_ref, n), key_window(vc_ref, vp_ref, n),
               first_span if n == 0 else None)

    qf[...] = q_ref[...].astype(F32)
    kf[0:SPAN, :] = kp_ref[...].astype(F32)
    kf[SPAN:2 * SPAN, :] = kc_ref[...].astype(F32)
    vf[0:SPAN, :] = vp_ref[...].astype(F32)
    vf[SPAN:2 * SPAN, :] = vc_ref[...].astype(F32)
    for b, d in list(enumerate(DILATIONS))[1:]:
        bias = banded_bias(d)
        for r in range(d):
            for n in range(SPAN // (ATTN_BLOCK * d)):
                q0 = r + d * ATTN_BLOCK * n
                k0 = SPAN + q0 - d * ATTN_BLOCK
                attend(b, d, bias, q0,
                       qf[pl.ds(q0, ATTN_BLOCK, stride=d), :].astype(BF16),
                       kf[pl.ds(k0, 2 * ATTN_BLOCK, stride=d), :].astype(BF16),
                       vf[pl.ds(k0, 2 * ATTN_BLOCK, stride=d), :].astype(BF16),
                       first_span if n == 0 else None)

    chunk = 256
    for c in range(SPAN // chunk):
        rows = slice(c * chunk, (c + 1) * chunk)
        l0, l1, l2 = lb[0, rows, :], lb[1, rows, :], lb[2, rows, :]
        mx = jnp.maximum(jnp.maximum(l0, l1), l2)
        w0, w1, w2 = jnp.exp(l0 - mx), jnp.exp(l1 - mx), jnp.exp(l2 - mx)
        num = w0 * ob[0, rows, :] + w1 * ob[1, rows, :] + w2 * ob[2, rows, :]
        o_ref[rows, :] = (num / (w0 + w1 + w2)).astype(o_ref.dtype)


def _attention(z, slopes):
    n_spans = SEQ // SPAN
    k_col, v_col = N_HEADS, 2 * N_HEADS
    blk = (SPAN, HEAD_DIM)
    prev = lambda s: jnp.maximum(s - 1, 0)
    return pl.pallas_call(
        _attn_kernel,
        grid_spec=pltpu.PrefetchScalarGridSpec(
            num_scalar_prefetch=1,
            grid=(N_HEADS, n_spans),
            in_specs=[pl.BlockSpec(blk, lambda h, s, sl: (s, h)),
                      pl.BlockSpec(blk, lambda h, s, sl: (s, k_col + h)),
                      pl.BlockSpec(blk, lambda h, s, sl: (prev(s), k_col + h)),
                      pl.BlockSpec(blk, lambda h, s, sl: (s, v_col + h)),
                      pl.BlockSpec(blk, lambda h, s, sl: (prev(s), v_col + h))],
            out_specs=pl.BlockSpec(blk, lambda h, s, sl: (s, h)),
            scratch_shapes=[pltpu.VMEM((SPAN, HEAD_DIM), F32),
                            pltpu.VMEM((2 * SPAN, HEAD_DIM), F32),
                            pltpu.VMEM((2 * SPAN, HEAD_DIM), F32),
                            pltpu.VMEM((3, SPAN, HEAD_DIM), F32),
                            pltpu.VMEM((3, SPAN, HEAD_DIM), F32)]),
        out_shape=jax.ShapeDtypeStruct((SEQ, ATTN_WIDTH), BF16),
        compiler_params=_params(2),
        name="dilated_attention",
    )(slopes, z, z, z, z, z)


CONV_ROWS = 256
CONV_HALO = 32
CONV_CHUNK = 64


SUBLANES = 8
CONV_EXT = CONV_HALO + CONV_ROWS
CONV_SHIFTED = CONV_EXT - SUBLANES


def _conv_kernel(a_ref, g_ref, ah_ref, gh_ref, cw_ref, cb_ref, lw_ref, lb_ref, o_ref,
                 u_ext, shifted, c_scr):
    i = pl.program_id(0)
    n_ct = CONV_WIDTH // LANES
    for ct in range(n_ct):
        cols = slice(ct * LANES, (ct + 1) * LANES)
        u_ext[ct, CONV_HALO:, :] = (a_ref[:, cols].astype(F32)
                                    * jax.nn.sigmoid(g_ref[:, cols].astype(F32)))
        uh = ah_ref[:, cols].astype(F32) * jax.nn.sigmoid(gh_ref[:, cols].astype(F32))
        u_ext[ct, 0:CONV_HALO, :] = jnp.where(i == 0, 0.0, uh)

    first_tap = CONV_HALO - (CONV_KERNEL - 1)

    def slab(ct, carry):
        for k in range(1, SUBLANES):
            shifted[k - 1] = u_ext[ct, k:k + CONV_SHIFTED, :]
        for rc in range(CONV_ROWS // CONV_CHUNK):
            base = rc * CONV_CHUNK
            acc = jnp.broadcast_to(cb_ref[pl.ds(ct, 1), :], (CONV_CHUNK, LANES))
            for j in range(CONV_KERNEL):
                k = (first_tap + j) % SUBLANES
                row = base + (first_tap + j) - k
                if k == 0:
                    src = u_ext[ct, row:row + CONV_CHUNK, :]
                else:
                    src = shifted[k - 1, row:row + CONV_CHUNK, :]
                acc = acc + cw_ref[j, pl.ds(ct, 1), :] * src
            c_scr[ct, base:base + CONV_CHUNK, :] = acc
        return carry

    lax.fori_loop(0, n_ct, slab, 0)

    total = c_scr[0]
    for ct in range(1, n_ct):
        total = total + c_scr[ct]
    mu = jnp.sum(total, axis=-1, keepdims=True) * (1.0 / CONV_WIDTH)
    sq = jnp.square(c_scr[0] - mu)
    for ct in range(1, n_ct):
        sq = sq + jnp.square(c_scr[ct] - mu)
    var = jnp.sum(sq, axis=-1, keepdims=True) * (1.0 / CONV_WIDTH)
    rstd = lax.rsqrt(var + EPS)
    for ct in range(n_ct):
        cols = slice(ct * LANES, (ct + 1) * LANES)
        y = (c_scr[ct] - mu) * rstd * lw_ref[:, cols] + lb_ref[:, cols]
        o_ref[:, cols] = (y * jax.nn.sigmoid(y)).astype(o_ref.dtype)


def _conformer_conv(z, conv_w, conv_b, ln_w, ln_b):
    a_col = 3 * ATTN_WIDTH // CONV_WIDTH
    g_col = a_col + 1
    n_ct = CONV_WIDTH // LANES
    halo_blocks = CONV_ROWS // CONV_HALO
    halo = lambda i: jnp.maximum(i * halo_blocks - 1, 0)
    vec = pl.BlockSpec((1, CONV_WIDTH), lambda i: (0, 0))
    return pl.pallas_call(
        _conv_kernel,
        grid=(SEQ // CONV_ROWS,),
        in_specs=[pl.BlockSpec((CONV_ROWS, CONV_WIDTH), lambda i: (i, a_col)),
                  pl.BlockSpec((CONV_ROWS, CONV_WIDTH), lambda i: (i, g_col)),
                  pl.BlockSpec((CONV_HALO, CONV_WIDTH), lambda i: (halo(i), a_col)),
                  pl.BlockSpec((CONV_HALO, CONV_WIDTH), lambda i: (halo(i), g_col)),
                  pl.BlockSpec((CONV_KERNEL, n_ct, LANES), lambda i: (0, 0, 0)),
                  pl.BlockSpec((n_ct, LANES), lambda i: (0, 0)),
                  vec, vec],
        out_specs=pl.BlockSpec((CONV_ROWS, CONV_WIDTH), lambda i: (i, 0)),
        out_shape=jax.ShapeDtypeStruct((SEQ, CONV_WIDTH), BF16),
        scratch_shapes=[pltpu.VMEM((n_ct, CONV_EXT, LANES), F32),
                        pltpu.VMEM((SUBLANES - 1, CONV_SHIFTED, LANES), F32),
                        pltpu.VMEM((n_ct, CONV_ROWS, LANES), F32)],
        compiler_params=_params(1),
        name="conformer_conv",
    )(z, z, z, z, conv_w, conv_b, ln_w, ln_b)


OUT_TM = 1024
OUT_TN = 512


def _outproj_kernel(attn_ref, conv_ref, wa_ref, wc_ref, x_ref, o_ref, wa_bf, wc_bf):
    @pl.when(pl.program_id(1) == 0)
    def _():
        wa_bf[...] = wa_ref[...].astype(BF16)
        wc_bf[...] = wc_ref[...].astype(BF16)

    acc = jnp.dot(attn_ref[...], wa_bf[...], preferred_element_type=F32)
    acc = acc + jnp.dot(conv_ref[...], wc_bf[...], preferred_element_type=F32)
    o_ref[...] = x_ref[...] + acc


def _outproj(attn, conv, w, x):
    n = x.shape[0]
    conv_blk = ATTN_WIDTH // CONV_WIDTH
    return pl.pallas_call(
        _outproj_kernel,
        grid=(D_MODEL // OUT_TN, n // OUT_TM),
        in_specs=[pl.BlockSpec((OUT_TM, ATTN_WIDTH), lambda j, i: (i, 0)),
                  pl.BlockSpec((OUT_TM, CONV_WIDTH), lambda j, i: (i, 0)),
                  pl.BlockSpec((ATTN_WIDTH, OUT_TN), lambda j, i: (0, j)),
                  pl.BlockSpec((CONV_WIDTH, OUT_TN), lambda j, i: (conv_blk, j)),
                  pl.BlockSpec((OUT_TM, OUT_TN), lambda j, i: (i, j))],
        out_specs=pl.BlockSpec((OUT_TM, OUT_TN), lambda j, i: (i, j)),
        out_shape=jax.ShapeDtypeStruct((n, D_MODEL), F32),
        scratch_shapes=[pltpu.VMEM((ATTN_WIDTH, OUT_TN), BF16),
                        pltpu.VMEM((CONV_WIDTH, OUT_TN), BF16)],
        compiler_params=_params(2),
        name="outproj",
    )(attn, conv, w, w, x)


def _router_kernel(h_ref, nw_ref, wr_ref, br_ref, hn_ref, ids_ref, gates_ref, counts_ref, seen):
    x = h_ref[...]
    ms = jnp.mean(x * x, axis=-1, keepdims=True)
    hn = x * lax.rsqrt(ms + EPS) * nw_ref[...]
    words = _pack_rows(hn)
    for s in range(X_ROW_TILES):
        hn_ref[:, s, :] = words[:, s * LANES:(s + 1) * LANES]

    h_hi = hn.astype(BF16)
    h_lo = (hn - h_hi.astype(F32)).astype(BF16)
    hi_both = jnp.dot(h_hi, wr_ref[...], preferred_element_type=F32)
    lo_hi = jnp.dot(h_lo, wr_ref[:, :LANES], preferred_element_type=F32)
    logits = hi_both[:, :LANES] + (hi_both[:, LANES:] + lo_hi) + br_ref[...]
    lane = lax.broadcasted_iota(jnp.int32, logits.shape, 1)
    big = jnp.int32(1 << 20)

    is_g = lane < N_GROUPS
    gl = jnp.where(is_g, logits, -jnp.inf)
    ge = jnp.exp(gl - jnp.max(gl, axis=-1, keepdims=True))
    g_prob = ge / jnp.sum(ge, axis=-1, keepdims=True)
    g_w = jnp.max(g_prob, axis=-1, keepdims=True)
    g_top = jnp.min(jnp.where(is_g & (g_prob == g_w), lane, big), axis=-1, keepdims=True)

    eidx = lane - N_GROUPS
    grp_shift = EXPERTS_PER_GROUP.bit_length() - 1
    in_grp = (eidx >= 0) & (eidx < N_EXPERTS) & ((eidx >> grp_shift) == g_top)
    el = jnp.where(in_grp, logits, -jnp.inf)
    ee = jnp.exp(el - jnp.max(el, axis=-1, keepdims=True))
    e_prob = ee / jnp.sum(ee, axis=-1, keepdims=True)
    v1 = jnp.max(jnp.where(in_grp, e_prob, -1.0), axis=-1, keepdims=True)
    i1 = jnp.min(jnp.where(in_grp & (e_prob == v1), lane, big), axis=-1, keepdims=True)
    rest = in_grp & (lane != i1)
    v2 = jnp.max(jnp.where(rest, e_prob, -1.0), axis=-1, keepdims=True)
    i2 = jnp.min(jnp.where(rest & (e_prob == v2), lane, big), axis=-1, keepdims=True)
    tot = v1 + v2
    gates_ref[...] = jnp.where(lane == 0, g_w * v1 / tot,
                               jnp.where(lane == 1, g_w * v2 / tot, 0.0))

    e1 = i1 - N_GROUPS
    e2 = i2 - N_GROUPS
    first_choice = lane == e1
    second_choice = lane == e2 + N_EXPERTS
    chosen = jnp.where(first_choice | second_choice, 1.0, 0.0)
    t_row = lax.broadcasted_iota(jnp.int32, (ROUTER_ROWS, ROUTER_ROWS), 0)
    t_col = lax.broadcasted_iota(jnp.int32, (ROUTER_ROWS, ROUTER_ROWS), 1)
    earlier = jnp.where(t_col < t_row, 1.0, 0.0).astype(BF16)
    chosen = chosen + pltpu.roll(chosen, N_EXPERTS, axis=1)
    before = jnp.dot(earlier, chosen.astype(BF16), preferred_element_type=F32)

    @pl.when(pl.program_id(0) == 0)
    def _():
        seen[...] = jnp.zeros_like(seen)

    before = before + seen[0:1, :]
    rank1 = jnp.sum(jnp.where(first_choice, before, 0.0), axis=-1, keepdims=True)
    rank2 = jnp.sum(jnp.where(second_choice, before, 0.0), axis=-1, keepdims=True)
    seen[0:1, :] = seen[0:1, :] + jnp.sum(chosen, axis=0, keepdims=True)
    counts_ref[...] = jnp.broadcast_to(seen[0:1, :], counts_ref.shape).astype(jnp.int32)

    per_token = jnp.where(
        lane == 0, e1, jnp.where(
            lane == 1, e2, jnp.where(
                lane == 2, rank1.astype(jnp.int32), jnp.where(
                    lane == 3, rank2.astype(jnp.int32), 0))))
    ids_ref[...] = jnp.transpose(per_token)[:SUBLANES, :]


def _router(h, norm_w, w_router, b_router):
    return pl.pallas_call(
        _router_kernel,
        grid=(SEQ // ROUTER_ROWS,),
        in_specs=[pl.BlockSpec((ROUTER_ROWS, D_MODEL), lambda i: (i, 0)),
                  pl.BlockSpec((1, D_MODEL), lambda i: (0, 0)),
                  pl.BlockSpec((D_MODEL, 2 * LANES), lambda i: (0, 0)),
                  pl.BlockSpec((1, LANES), lambda i: (0, 0))],
        out_specs=[pl.BlockSpec((ROUTER_ROWS, X_ROW_TILES, LANES), lambda i: (i, 0, 0)),
                   pl.BlockSpec((SUBLANES, ROUTER_ROWS), lambda i: (0, i)),
                   pl.BlockSpec((ROUTER_ROWS, LANES), lambda i: (i, 0)),
                   pl.BlockSpec((SUBLANES, LANES), lambda i: (0, 0))],
        out_shape=[jax.ShapeDtypeStruct((SEQ, X_ROW_TILES, LANES), jnp.uint32),
                   jax.ShapeDtypeStruct((SUBLANES, SEQ), jnp.int32),
                   jax.ShapeDtypeStruct((SEQ, LANES), F32),
                   jax.ShapeDtypeStruct((SUBLANES, LANES), jnp.int32)],
        scratch_shapes=[pltpu.VMEM((SUBLANES, LANES), F32)],
        compiler_params=_params(1),
        name="ffn_norm_router",
    )(h, norm_w, w_router, b_router)


def _staged_token(stage, slot, r):
    return stage.at[slot, pl.ds(pl.multiple_of(r * GATHER_PITCH, SUBLANES), X_ROW_TILES)]


def _issue_token_rows(toks, nv, hn_hbm, stage, slot, sem):
    def start(r, carry):
        pltpu.make_async_copy(hn_hbm.at[toks[0, 0, r]], _staged_token(stage, slot, r),
                              sem.at[slot]).start()
        return carry

    def start_group(g, carry):
        for u in range(DMA_UNROLL):
            start(g * DMA_UNROLL + u, carry)
        return carry

    groups = nv // DMA_UNROLL
    lax.fori_loop(0, groups, start_group, 0)
    lax.fori_loop(groups * DMA_UNROLL, nv, start, 0)


def _wait_token_rows(nv, stage, slot, sem):
    def wait_rows(n_tokens):
        span = stage.at[slot, pl.ds(0, n_tokens * X_ROW_TILES)]
        pltpu.make_async_copy(span, span, sem.at[slot]).wait()

    def wait_many(c, carry):
        wait_rows(WAIT_ROWS)
        return carry

    def wait_one(c, carry):
        wait_rows(1)
        return carry

    lax.fori_loop(0, nv // WAIT_ROWS, wait_many, 0)
    lax.fori_loop(0, nv % WAIT_ROWS, wait_one, 0)


def _staged_block(stage, slot):
    return jnp.concatenate([stage[slot, pl.ds(s, MOE_ROWS, stride=GATHER_PITCH), :]
                            for s in range(X_ROW_TILES)], axis=-1)


def _unpack_rows(words):
    lo, hi = _unpack_words(words)
    return jnp.concatenate([lo.astype(BF16), hi.astype(BF16)], axis=-1)


def _stream_weight_tile(w, first_ref, seq_ref, n_tiles_ref, tile_copies, consume):
    n_tiles = n_tiles_ref[0]

    @pl.when(w == 0)
    def _():
        for k in range(WEIGHT_RING):
            @pl.when(k < n_tiles)
            def _():
                for c in tile_copies(k, k):
                    c.start(priority=WEIGHT_DMA_PRIORITY)

    @pl.when(first_ref[w] == 1)
    def _():
        k = seq_ref[w]
        slot = k % WEIGHT_RING
        for c in tile_copies(k, slot):
            c.wait()
        consume(slot)

        @pl.when(k + WEIGHT_RING < n_tiles)
        def _():
            for c in tile_copies(k + WEIGHT_RING, slot):
                c.start(priority=WEIGHT_DMA_PRIORITY)


def _moe_up_kernel(ib_ref, nv_ref, b_ref, ot_ref, valid_ref, first_ref, seq_ref, te_ref, tt_ref,
                   nt_ref, fetch_ref, xslot_ref, sslot_ref, pf_ref, pfb_ref, pfnv_ref,
                   tok_ref, tok_next_ref, hn_hbm, wg_hbm, wu_hbm, hid_ref,
                   stage, x_cache, ring_g, ring_u, wg_bf, wu_bf, row_sem, sem):
    w = pl.program_id(0)

    @pl.when(w == 0)
    def _():
        stage[...] = jnp.zeros_like(stage)
        _issue_token_rows(tok_ref, nv_ref[0], hn_hbm, stage, 0, row_sem)

    @pl.when(pf_ref[w] == 1)
    def _():
        _issue_token_rows(tok_next_ref, pfnv_ref[w], hn_hbm, stage, 1 - sslot_ref[w], row_sem)

    def tile_copies(k, slot):
        e = te_ref[k]
        col = pl.multiple_of(tt_ref[k] * UP_TILE, UP_TILE)
        return (pltpu.make_async_copy(wg_hbm.at[e, :, pl.ds(col, UP_TILE)], ring_g.at[slot],
                                      sem.at[0, slot]),
                pltpu.make_async_copy(wu_hbm.at[e, :, pl.ds(col, UP_TILE)], ring_u.at[slot],
                                      sem.at[1, slot]))

    def consume(slot):
        wg_bf[...] = ring_g[slot].astype(BF16)
        wu_bf[...] = ring_u[slot].astype(BF16)

    _stream_weight_tile(w, first_ref, seq_ref, nt_ref, tile_copies, consume)

    @pl.when(valid_ref[w] == 1)
    def _():
        x_slot = xslot_ref[w]

        @pl.when(fetch_ref[w] == 1)
        def _():
            slot = sslot_ref[w]
            _wait_token_rows(nv_ref[w], stage, slot, row_sem)
            x_cache[x_slot] = _unpack_rows(_staged_block(stage, slot))

        x = x_cache[x_slot]
        g = jnp.dot(x, wg_bf[...], preferred_element_type=F32)
        u = jnp.dot(x, wu_bf[...], preferred_element_type=F32)
        hid_ref[...] = (g * jax.nn.sigmoid(g) * u).astype(hid_ref.dtype)

    @pl.when(valid_ref[w] == 0)
    def _():
        hid_ref[...] = jnp.zeros_like(hid_ref)


def _moe_up(items, info, n_valid, row_tok, hn_words, w_gate, w_up):
    n_items = N_MOE_BLOCKS * (D_FF // UP_TILE)
    in_blk = items[0]
    schedule = _gather_schedule(in_blk, info, n_valid)
    n_prefetch = 1 + len(items) + len(schedule)
    toks = row_tok.reshape(N_MOE_BLOCKS, 1, MOE_ROWS)
    smem_block = lambda index_map: pl.BlockSpec((1, 1, MOE_ROWS), index_map,
                                                memory_space=pltpu.SMEM)
    any_spec = pl.BlockSpec(memory_space=pl.ANY)
    return pl.pallas_call(
        _moe_up_kernel,
        grid_spec=pltpu.PrefetchScalarGridSpec(
            num_scalar_prefetch=n_prefetch,
            grid=(n_items,),
            in_specs=[smem_block(lambda w, *refs: (refs[0][w], 0, 0)),
                      smem_block(lambda w, *refs: (refs[n_prefetch - 2][w], 0, 0)),
                      any_spec, any_spec, any_spec],
            out_specs=pl.BlockSpec((MOE_ROWS, UP_TILE),
                                   lambda w, ib, nv, b, ot, *_: (b[w], ot[w])),
            scratch_shapes=[pltpu.VMEM((2, MOE_ROWS * GATHER_PITCH, LANES), jnp.uint32),
                            pltpu.VMEM((X_CACHE_SLOTS, MOE_ROWS, D_MODEL), BF16),
                            pltpu.VMEM((WEIGHT_RING, D_MODEL, UP_TILE), F32),
                            pltpu.VMEM((WEIGHT_RING, D_MODEL, UP_TILE), F32),
                            pltpu.VMEM((D_MODEL, UP_TILE), BF16),
                            pltpu.VMEM((D_MODEL, UP_TILE), BF16),
                            pltpu.SemaphoreType.DMA((2,)),
                            pltpu.SemaphoreType.DMA((2, WEIGHT_RING))]),
        out_shape=jax.ShapeDtypeStruct((N_MOE_ROWS, D_FF), BF16),
        compiler_params=_params(1),
        name="moe_gate_up",
    )(in_blk, _lookup(n_valid, in_blk), *items[1:], *schedule, toks, toks, hn_words, w_gate, w_up)


def _moe_down_kernel(ib_ref, b_ref, ot_ref, valid_ref, first_ref, seq_ref, te_ref, tt_ref, nt_ref,
                     hid_ref, wd_hbm, y_ref, ring, wd_bf, sem):
    w = pl.program_id(0)

    def tile_copies(k, slot):
        col = pl.multiple_of(tt_ref[k] * DOWN_TILE, DOWN_TILE)
        return (pltpu.make_async_copy(wd_hbm.at[te_ref[k], :, pl.ds(col, DOWN_TILE)],
                                      ring.at[slot], sem.at[slot]),)

    def consume(slot):
        wd_bf[...] = ring[slot].astype(BF16)

    _stream_weight_tile(w, first_ref, seq_ref, nt_ref, tile_copies, consume)

    @pl.when(valid_ref[w] == 1)
    def _():
        y = jnp.dot(hid_ref[...], wd_bf[...], preferred_element_type=F32)
        words = _pack_rows(y)
        for s in range(DOWN_TILE // 2 // LANES):
            y_ref[:, s, :] = words[:, s * LANES:(s + 1) * LANES]

    @pl.when(valid_ref[w] == 0)
    def _():
        y_ref[...] = jnp.zeros_like(y_ref)


def _moe_down(items, hid, w_down):
    n_items = N_MOE_BLOCKS * (D_MODEL // DOWN_TILE)
    tile_rows = DOWN_TILE // 2 // LANES
    return pl.pallas_call(
        _moe_down_kernel,
        grid_spec=pltpu.PrefetchScalarGridSpec(
            num_scalar_prefetch=9,
            grid=(n_items,),
            in_specs=[pl.BlockSpec((MOE_ROWS, D_FF), lambda w, ib, *_: (ib[w], 0)),
                      pl.BlockSpec(memory_space=pl.ANY)],
            out_specs=pl.BlockSpec((MOE_ROWS, tile_rows, LANES),
                                   lambda w, ib, b, ot, *_: (b[w], ot[w], 0)),
            scratch_shapes=[pltpu.VMEM((WEIGHT_RING, D_FF, DOWN_TILE), F32),
                            pltpu.VMEM((D_FF, DOWN_TILE), BF16),
                            pltpu.SemaphoreType.DMA((WEIGHT_RING,))]),
        out_shape=jax.ShapeDtypeStruct((N_MOE_ROWS, Y_ROW_TILES, LANES), jnp.uint32),
        compiler_params=_params(1),
        name="moe_down",
    )(*items, hid, w_down)


def _combine_kernel(dest_ref, dest_next_ref, h_ref, gate_ref, y_hbm, o_ref, buf, sem):
    i = pl.program_id(0)

    def issue(dests, slot):
        def start(r, carry):
            for k in range(TOP_K):
                dst = buf.at[slot, k, pl.ds(r * COMBINE_PITCH, Y_ROW_TILES)]
                pltpu.make_async_copy(y_hbm.at[dests[0, k, r]], dst, sem.at[slot]).start()
            return carry
        lax.fori_loop(0, COMBINE_ROWS, start, 0, unroll=DMA_UNROLL // TOP_K)

    @pl.when(i == 0)
    def _():
        issue(dest_ref, 0)

    @pl.when(i + 1 < pl.num_programs(0))
    def _():
        issue(dest_next_ref, (i + 1) % 2)

    slot = i % 2

    for k in range(TOP_K):
        for c in range(COMBINE_ROWS // WAIT_ROWS):
            span = buf.at[slot, k, pl.ds(c * WAIT_ROWS * Y_ROW_TILES, WAIT_ROWS * Y_ROW_TILES)]
            pltpu.make_async_copy(span, span, sem.at[slot]).wait()

    g0 = gate_ref[:, 0:1]
    g1 = gate_ref[:, 1:2]
    tiles_per_half = DOWN_TILE // 2 // LANES
    for s in range(Y_ROW_TILES):
        lo_col = (s // tiles_per_half) * DOWN_TILE + (s % tiles_per_half) * LANES
        lo0, hi0 = _unpack_words(buf[slot, 0, pl.ds(s, COMBINE_ROWS, stride=COMBINE_PITCH), :])
        lo1, hi1 = _unpack_words(buf[slot, 1, pl.ds(s, COMBINE_ROWS, stride=COMBINE_PITCH), :])
        for col, y0, y1 in ((lo_col, lo0, lo1), (lo_col + DOWN_TILE // 2, hi0, hi1)):
            cols = slice(col, col + LANES)
            o_ref[:, cols] = h_ref[:, cols] + (y0 * g0 + y1 * g1)


def _combine(dest, h, gates, y):
    n_tiles = SEQ // COMBINE_ROWS
    dests = jnp.stack([d.reshape(n_tiles, COMBINE_ROWS) for d in dest], axis=1)
    blk = (1, TOP_K, COMBINE_ROWS)
    return pl.pallas_call(
        _combine_kernel,
        grid=(n_tiles,),
        in_specs=[pl.BlockSpec(blk, lambda i: (i, 0, 0), memory_space=pltpu.SMEM),
                  pl.BlockSpec(blk, lambda i: (jnp.minimum(i + 1, n_tiles - 1), 0, 0),
                               memory_space=pltpu.SMEM),
                  pl.BlockSpec((COMBINE_ROWS, D_MODEL), lambda i: (i, 0)),
                  pl.BlockSpec((COMBINE_ROWS, LANES), lambda i: (i, 0)),
                  pl.BlockSpec(memory_space=pl.ANY)],
        out_specs=pl.BlockSpec((COMBINE_ROWS, D_MODEL), lambda i: (i, 0)),
        out_shape=jax.ShapeDtypeStruct((SEQ, D_MODEL), F32),
        scratch_shapes=[pltpu.VMEM((2, TOP_K, COMBINE_ROWS * COMBINE_PITCH, LANES), jnp.uint32),
                        pltpu.SemaphoreType.DMA((2,))],
        compiler_params=_params(1),
        name="moe_combine",
    )(dests, dests, h, gates, y)


def _lookup(table, idx):
    pos = jnp.arange(table.shape[0], dtype=jnp.int32)
    return jnp.sum(jnp.where(idx[:, None] == pos[None, :], table[None, :], 0), axis=1)


def _count_le(ends, x):
    return jnp.sum((ends[None, :] <= x[:, None]).astype(jnp.int32), axis=1)


def _work_items(n_blocks_e, block_start_e, tiles):
    i32 = lambda a: a.astype(jnp.int32)
    n_items = N_MOE_BLOCKS * tiles
    per_e = n_blocks_e * tiles
    ends = jnp.cumsum(per_e)
    total = ends[-1]
    idx = jnp.arange(n_items, dtype=jnp.int32)
    valid = idx < total
    w = jnp.minimum(idx, total - 1)
    e = _count_le(ends, w)
    local = w - _lookup(ends - per_e, e)
    nb = _lookup(n_blocks_e, e)
    first_blk = _lookup(block_start_e, e)
    spare = idx - total
    w_tile = local // nb
    o_tile = jnp.where(valid, w_tile, spare % tiles)
    blk = jnp.where(valid, first_blk + local % nb, total // tiles + spare // tiles)
    first = valid & (local % nb == 0)
    active = n_blocks_e > 0
    active_rank = jnp.cumsum(active) - active
    seq = _lookup(active_rank, e) * tiles + w_tile
    k = jnp.arange(N_EXPERTS * tiles, dtype=jnp.int32)
    experts = jnp.arange(N_EXPERTS, dtype=jnp.int32)
    is_kth = active[None, :] & (active_rank[None, :] == (k // tiles)[:, None])
    tile_e = jnp.sum(jnp.where(is_kth, experts[None, :], 0), axis=1)
    tile_t = k % tiles
    n_tiles = (jnp.sum(active) * tiles).reshape(1)
    in_blk = first_blk + local % nb
    items = (i32(in_blk), i32(blk), i32(o_tile), i32(valid), i32(first), i32(seq), i32(tile_e),
             i32(tile_t), i32(n_tiles))
    return items, dict(blocks_of_expert=nb, local_block=local % nb, weight_tile=w_tile, valid=valid)


def _gather_schedule(in_blk, info, n_valid):
    i32 = lambda a: a.astype(jnp.int32)
    n = in_blk.shape[0]
    cached = (info['blocks_of_expert'] <= X_CACHE_SLOTS) & (info['weight_tile'] > 0)
    fetch = info['valid'] & ~cached
    x_slot = info['local_block'] % X_CACHE_SLOTS
    stage_slot = (jnp.cumsum(fetch) - 1) % 2
    idx = jnp.arange(n, dtype=jnp.int32)
    later = lax.cummin(jnp.where(fetch, idx, n), axis=0, reverse=True)
    next_fetch = jnp.concatenate([later[1:], jnp.full((1,), n, later.dtype)])
    prefetch = fetch & (next_fetch < n)
    pf_blk = jnp.where(prefetch, _lookup(in_blk, jnp.minimum(next_fetch, n - 1)), in_blk)
    return (i32(fetch), i32(x_slot), i32(stage_slot), i32(prefetch), i32(pf_blk),
            i32(_lookup(n_valid, pf_blk)))


def _dispatch_plan(ids, counts):
    counts = counts[0, :N_EXPERTS]
    n_blocks_e = (counts + MOE_ROWS - 1) // MOE_ROWS
    block_ends = jnp.cumsum(n_blocks_e)
    block_start_e = block_ends - n_blocks_e
    first_row_e = block_start_e * MOE_ROWS
    dest = [(_lookup(first_row_e, ids[k]) + ids[TOP_K + k]).astype(jnp.int32)
            for k in range(TOP_K)]
    token = jnp.arange(SEQ, dtype=jnp.int32)
    row_tok = jnp.zeros((N_MOE_ROWS,), jnp.int32).at[jnp.concatenate(dest)].set(
        jnp.concatenate([token] * TOP_K), unique_indices=True, mode='promise_in_bounds')
    blk = jnp.arange(N_MOE_BLOCKS, dtype=jnp.int32)
    blk_e = jnp.minimum(_count_le(block_ends, blk), N_EXPERTS - 1)
    n_valid = jnp.clip(_lookup(counts, blk_e) - (blk - _lookup(block_start_e, blk_e)) * MOE_ROWS,
                       0, MOE_ROWS)
    n_valid = jnp.where(blk < block_ends[-1], n_valid, 0).astype(jnp.int32)
    return dest, row_tok, n_valid, n_blocks_e.astype(jnp.int32), block_start_e.astype(jnp.int32)


def kernel(x, norm_mix_w, w_in, q_norm_w, k_norm_w, conv_w, conv_b, conv_ln_w, conv_ln_b, w_out,
           norm_ffn_w, w_group_router, b_group_router, w_expert_router, b_expert_router,
           w_gate, w_up, w_down):
    h = x.reshape(SEQ, D_MODEL)
    slopes = jnp.exp2(-8.0 * jnp.arange(1, N_HEADS + 1, dtype=F32) / N_HEADS)
    for l in range(norm_mix_w.shape[0]):
        hn = _rmsnorm(h, norm_mix_w[l][None, :])
        qk_norm = jnp.concatenate([jnp.tile(q_norm_w[l], N_HEADS), jnp.tile(k_norm_w[l], N_HEADS)])
        z = _inproj(hn, w_in[l], qk_norm[None, :])
        attn = _attention(z, slopes)
        n_ct = CONV_WIDTH // LANES
        conv = _conformer_conv(z, conv_w[l].reshape(CONV_KERNEL, n_ct, LANES),
                               conv_b[l].reshape(n_ct, LANES), conv_ln_w[l][None, :],
                               conv_ln_b[l][None, :])
        h = _outproj(attn, conv, w_out[l], h)

        pad = LANES - N_GROUPS - N_EXPERTS
        w_router = jnp.concatenate([w_group_router[l], w_expert_router[l],
                                    jnp.zeros((D_MODEL, pad), F32)], axis=1)
        b_router = jnp.concatenate([b_group_router[l], b_expert_router[l], jnp.zeros((pad,), F32)])
        w_router_hi = w_router.astype(BF16)
        w_router_lo = (w_router - w_router_hi.astype(F32)).astype(BF16)
        hn_words, ids, gates, counts = _router(h, norm_ffn_w[l][None, :],
                                               jnp.concatenate([w_router_hi, w_router_lo], axis=1),
                                               b_router[None, :])
        dest, row_tok, n_valid, n_blocks_e, block_start_e = _dispatch_plan(ids, counts)
        up_items, up_info = _work_items(n_blocks_e, block_start_e, D_FF // UP_TILE)
        hid = _moe_up(up_items, up_info, n_valid, row_tok, hn_words, w_gate[l], w_up[l])
        down_items, _ = _work_items(n_blocks_e, block_start_e, D_MODEL // DOWN_TILE)
        y = _moe_down(down_items, hid, w_down[l])
        h = _combine(dest, h, gates, y)
    return h.reshape(x.shape)
```

```python
import math

import jax
import jax.numpy as jnp
from jax import lax
from jax.experimental import pallas as pl
from jax.experimental.pallas import tpu as pltpu

F32 = jnp.float32
BF16 = jnp.bfloat16

D_MODEL = 4096
SEQ = 8192
HEAD_DIM = 128
N_HEADS = 16
ATTN_WIDTH = N_HEADS * HEAD_DIM
CONV_WIDTH = D_MODEL - ATTN_WIDTH
CONV_KERNEL = 31
IN_COLS = 3 * ATTN_WIDTH + 2 * CONV_WIDTH
DILATIONS = (1, 4, 16)
ATTN_BLOCK = 128
N_KEYS = 128
N_GROUPS = 8
EXPERTS_PER_GROUP = 8
N_EXPERTS = N_GROUPS * EXPERTS_PER_GROUP
TOP_K = 2
D_FF = 1024
EPS = 1e-6
NEG_INF = -1e30

LANES = 128
SUBLANES = 8
VMEM_LIMIT = 56 * 1024 * 1024

SPAN = DILATIONS[-1] * ATTN_BLOCK

MOE_ROWS = 256
N_ASSIGN = SEQ * TOP_K
N_MOE_ROWS = N_ASSIGN + N_EXPERTS * MOE_ROWS
N_MOE_BLOCKS = N_MOE_ROWS // MOE_ROWS
UP_TILE = 512
DOWN_TILE = 4096
WEIGHT_RING = 2
WEIGHT_DMA_PRIORITY = 1
ROUTER_ROWS = 256
COMBINE_ROWS = 128
X_CACHE_SLOTS = 2
X_ROW_TILES = D_MODEL // 2 // LANES
Y_ROW_TILES = D_MODEL // 2 // LANES
COMBINE_PITCH = Y_ROW_TILES + 4
GATHER_PITCH = X_ROW_TILES + 8
DMA_UNROLL = 8
WAIT_ROWS = 32


def _params(n_axes):
    return pltpu.CompilerParams(dimension_semantics=("arbitrary",) * n_axes,
                                vmem_limit_bytes=VMEM_LIMIT)


def _pack_rows(x):
    half = x.shape[1] // 2
    lo = pltpu.bitcast(x[:, :half].astype(BF16).astype(F32), jnp.uint32)
    hi = pltpu.bitcast(x[:, half:].astype(BF16).astype(F32), jnp.uint32)
    return (hi & jnp.uint32(0xFFFF0000)) | (lo >> 16)


def _unpack_words(words):
    return (pltpu.bitcast(words << 16, F32),
            pltpu.bitcast(words & jnp.uint32(0xFFFF0000), F32))


def _rmsnorm_kernel(x_ref, w_ref, o_ref):
    x = x_ref[...]
    ms = jnp.mean(x * x, axis=-1, keepdims=True)
    o_ref[...] = (x * lax.rsqrt(ms + EPS) * w_ref[...]).astype(o_ref.dtype)


def _rmsnorm(x, w, rows=512):
    n, d = x.shape
    return pl.pallas_call(
        _rmsnorm_kernel,
        grid=(n // rows,),
        in_specs=[pl.BlockSpec((rows, d), lambda i: (i, 0)),
                  pl.BlockSpec((1, d), lambda i: (0, 0))],
        out_specs=pl.BlockSpec((rows, d), lambda i: (i, 0)),
        out_shape=jax.ShapeDtypeStruct((n, d), BF16),
        compiler_params=_params(1),
        name="rmsnorm",
    )(x, w)


IN_TM = 1024
IN_TN = 512
IN_CHUNKS = (256, 256, 256, 128, 128)
QK_TILES = 2 * ATTN_WIDTH // IN_TN


def _inproj_kernel(hn_ref, w_ref, nw_ref, o_ref, w_bf):
    j = pl.program_id(0)

    @pl.when(pl.program_id(1) == 0)
    def _():
        w_bf[...] = w_ref[...].astype(BF16)

    is_qk = j < QK_TILES
    row0 = 0
    for chunk in IN_CHUNKS:
        rows = slice(row0, row0 + chunk)
        row0 += chunk
        acc = jnp.dot(hn_ref[rows, :], w_bf[...], preferred_element_type=F32)
        for hd in range(IN_TN // HEAD_DIM):
            sl = slice(hd * HEAD_DIM, (hd + 1) * HEAD_DIM)
            t = acc[:, sl]
            ms = jnp.mean(t * t, axis=-1, keepdims=True)
            normed = t * lax.rsqrt(ms + EPS) * nw_ref[:, sl]
            o_ref[rows, sl] = jnp.where(is_qk, normed, t).astype(o_ref.dtype)


def _inproj(hn, w, qk_norm_w):
    n = hn.shape[0]
    return pl.pallas_call(
        _inproj_kernel,
        grid=(IN_COLS // IN_TN, n // IN_TM),
        in_specs=[pl.BlockSpec((IN_TM, D_MODEL), lambda j, i: (i, 0)),
                  pl.BlockSpec((D_MODEL, IN_TN), lambda j, i: (0, j)),
                  pl.BlockSpec((1, IN_TN), lambda j, i: (0, jnp.minimum(j, QK_TILES - 1)))],
        out_specs=pl.BlockSpec((IN_TM, IN_TN), lambda j, i: (i, j)),
        out_shape=jax.ShapeDtypeStruct((n, IN_COLS), BF16),
        scratch_shapes=[pltpu.VMEM((D_MODEL, IN_TN), BF16)],
        compiler_params=_params(2),
        name="inproj",
    )(hn, w, qk_norm_w)


def _attn_kernel(slopes_ref, q_ref, kc_ref, kp_ref, vc_ref, vp_ref, o_ref, qf, kf, vf, ob, lb):
    h = pl.program_id(0)
    first_span = pl.program_id(1) == 0
    slope = slopes_ref[h]

    qi = lax.broadcasted_iota(jnp.int32, (ATTN_BLOCK, 2 * ATTN_BLOCK), 0)
    kj = lax.broadcasted_iota(jnp.int32, (ATTN_BLOCK, 2 * ATTN_BLOCK), 1)
    steps = qi + ATTN_BLOCK - kj
    valid = (steps >= 0) & (steps <= N_KEYS)
    stepsf = steps.astype(F32)
    scale = 1.0 / math.sqrt(HEAD_DIM)

    def attend(b, d, bias, q0, q, k, v, no_prev_block):
        s = lax.dot_general(q, k, (((1,), (1,)), ((), ())),
                            preferred_element_type=F32) * scale + bias
        if no_prev_block is not None:
            s = jnp.where(kj >= jnp.where(no_prev_block, ATTN_BLOCK, 0), s, NEG_INF)
        m = jnp.max(s, axis=-1, keepdims=True)
        p = jnp.exp(s - m)
        den = jnp.sum(p, axis=-1, keepdims=True)
        o = jnp.dot(p.astype(BF16), v, preferred_element_type=F32) / den
        lse = m + jnp.log(den)
        ob[b, pl.ds(q0, ATTN_BLOCK, stride=d), :] = o
        lb[b, pl.ds(q0, ATTN_BLOCK, stride=d), :] = jnp.broadcast_to(lse, (ATTN_BLOCK, HEAD_DIM))

    def banded_bias(d):
        return jnp.where(valid, (-slope * float(d)) * stepsf, NEG_INF)

    def key_window(cur, prev, n):
        if n > 0:
            return cur[(n - 1) * ATTN_BLOCK:(n + 1) * ATTN_BLOCK, :]
        last = prev.shape[0] - ATTN_BLOCK
        return jnp.concatenate([prev[last:, :], cur[:ATTN_BLOCK, :]], axis=0)

    b, d = 0, DILATIONS[0]
    bias = banded_bias(d)
    for n in range(SPAN // ATTN_BLOCK):
        attend(b, d, bias, n * ATTN_BLOCK, q_ref[n * ATTN_BLOCK:(n + 1) * ATTN_BLOCK, :],
               key_window(kc_ref, kp_ref, n), key_window(vc_ref, vp_ref, n),
               first_span if n == 0 else None)

    qf[...] = q_ref[...].astype(F32)
    kf[0:SPAN, :] = kp_ref[...].astype(F32)
    kf[SPAN:2 * SPAN, :] = kc_ref[...].astype(F32)
    vf[0:SPAN, :] = vp_ref[...].astype(F32)
    vf[SPAN:2 * SPAN, :] = vc_ref[...].astype(F32)
    for b, d in list(enumerate(DILATIONS))[1:]:
        bias = banded_bias(d)
        for r in range(d):
            for n in range(SPAN // (ATTN_BLOCK * d)):
                q0 = r + d * ATTN_BLOCK * n
                k0 = SPAN + q0 - d * ATTN_BLOCK
                attend(b, d, bias, q0,
                       qf[pl.ds(q0, ATTN_BLOCK, stride=d), :].astype(BF16),
                       kf[pl.ds(k0, 2 * ATTN_BLOCK, stride=d), :].astype(BF16),
                       vf[pl.ds(k0, 2 * ATTN_BLOCK, stride=d), :].astype(BF16),
                       first_span if n == 0 else None)

    chunk = 256
    for c in range(SPAN // chunk):
        rows = slice(c * chunk, (c + 1) * chunk)
        l0, l1, l2 = lb[0, rows, :], lb[1, rows, :], lb[2, rows, :]
        mx = jnp.maximum(jnp.maximum(l0, l1), l2)
        w0, w1, w2 = jnp.exp(l0 - mx), jnp.exp(l1 - mx), jnp.exp(l2 - mx)
        num = w0 * ob[0, rows, :] + w1 * ob[1, rows, :] + w2 * ob[2, rows, :]
        o_ref[rows, :] = (num / (w0 + w1 + w2)).astype(o_ref.dtype)


def _attention(z, slopes):
    n_spans = SEQ // SPAN
    k_col, v_col = N_HEADS, 2 * N_HEADS
    blk = (SPAN, HEAD_DIM)
    prev = lambda s: jnp.maximum(s - 1, 0)
    return pl.pallas_call(
        _attn_kernel,
        grid_spec=pltpu.PrefetchScalarGridSpec(
            num_scalar_prefetch=1,
            grid=(N_HEADS, n_spans),
            in_specs=[pl.BlockSpec(blk, lambda h, s, sl: (s, h)),
                      pl.BlockSpec(blk, lambda h, s, sl: (s, k_col + h)),
                      pl.BlockSpec(blk, lambda h, s, sl: (prev(s), k_col + h)),
                      pl.BlockSpec(blk, lambda h, s, sl: (s, v_col + h)),
                      pl.BlockSpec(blk, lambda h, s, sl: (prev(s), v_col + h))],
            out_specs=pl.BlockSpec(blk, lambda h, s, sl: (s, h)),
            scratch_shapes=[pltpu.VMEM((SPAN, HEAD_DIM), F32),
                            pltpu.VMEM((2 * SPAN, HEAD_DIM), F32),
                            pltpu.VMEM((2 * SPAN, HEAD_DIM), F32),
                            pltpu.VMEM((3, SPAN, HEAD_DIM), F32),
                            pltpu.VMEM((3, SPAN, HEAD_DIM), F32)]),
        out_shape=jax.ShapeDtypeStruct((SEQ, ATTN_WIDTH), BF16),
        compiler_params=_params(2),
        name="dilated_attention",
    )(slopes, z, z, z, z, z)


CONV_ROWS = 256
CONV_HALO = 32
CONV_CHUNK = 64

CONV_EXT = CONV_HALO + CONV_ROWS
CONV_SHIFTED = CONV_EXT - SUBLANES


def _conv_kernel(a_ref, g_ref, ah_ref, gh_ref, cw_ref, cb_ref, lw_ref, lb_ref, o_ref,
                 u_ext, shifted, c_scr):
    i = pl.program_id(0)
    n_ct = CONV_WIDTH // LANES
    for ct in range(n_ct):
        cols = slice(ct * LANES, (ct + 1) * LANES)
        u_ext[ct, CONV_HALO:, :] = (a_ref[:, cols].astype(F32)
                                    * jax.nn.sigmoid(g_ref[:, cols].astype(F32)))
        uh = ah_ref[:, cols].astype(F32) * jax.nn.sigmoid(gh_ref[:, cols].astype(F32))
        u_ext[ct, 0:CONV_HALO, :] = jnp.where(i == 0, 0.0, uh)

    first_tap = CONV_HALO - (CONV_KERNEL - 1)

    def slab(ct, carry):
        for k in range(1, SUBLANES):
            shifted[k - 1] = u_ext[ct, k:k + CONV_SHIFTED, :]
        for rc in range(CONV_ROWS // CONV_CHUNK):
            base = rc * CONV_CHUNK
            acc = jnp.broadcast_to(cb_ref[pl.ds(ct, 1), :], (CONV_CHUNK, LANES))
            for j in range(CONV_KERNEL):
                k = (first_tap + j) % SUBLANES
                row = base + (first_tap + j) - k
                if k == 0:
                    src = u_ext[ct, row:row + CONV_CHUNK, :]
                else:
                    src = shifted[k - 1, row:row + CONV_CHUNK, :]
                acc = acc + cw_ref[j, pl.ds(ct, 1), :] * src
            c_scr[ct, base:base + CONV_CHUNK, :] = acc
        return carry

    lax.fori_loop(0, n_ct, slab, 0)

    total = c_scr[0]
    for ct in range(1, n_ct):
        total = total + c_scr[ct]
    mu = jnp.sum(total, axis=-1, keepdims=True) * (1.0 / CONV_WIDTH)
    sq = jnp.square(c_scr[0] - mu)
    for ct in range(1, n_ct):
        sq = sq + jnp.square(c_scr[ct] - mu)
    var = jnp.sum(sq, axis=-1, keepdims=True) * (1.0 / CONV_WIDTH)
    rstd = lax.rsqrt(var + EPS)
    for ct in range(n_ct):
        cols = slice(ct * LANES, (ct + 1) * LANES)
        y = (c_scr[ct] - mu) * rstd * lw_ref[:, cols] + lb_ref[:, cols]
        o_ref[:, cols] = (y * jax.nn.sigmoid(y)).astype(o_ref.dtype)


def _conformer_conv(z, conv_w, conv_b, ln_w, ln_b):
    a_col = 3 * ATTN_WIDTH // CONV_WIDTH
    g_col = a_col + 1
    n_ct = CONV_WIDTH // LANES
    halo_blocks = CONV_ROWS // CONV_HALO
    halo = lambda i: jnp.maximum(i * halo_blocks - 1, 0)
    vec = pl.BlockSpec((1, CONV_WIDTH), lambda i: (0, 0))
    return pl.pallas_call(
        _conv_kernel,
        grid=(SEQ // CONV_ROWS,),
        in_specs=[pl.BlockSpec((CONV_ROWS, CONV_WIDTH), lambda i: (i, a_col)),
                  pl.BlockSpec((CONV_ROWS, CONV_WIDTH), lambda i: (i, g_col)),
                  pl.BlockSpec((CONV_HALO, CONV_WIDTH), lambda i: (halo(i), a_col)),
                  pl.BlockSpec((CONV_HALO, CONV_WIDTH), lambda i: (halo(i), g_col)),
                  pl.BlockSpec((CONV_KERNEL, n_ct, LANES), lambda i: (0, 0, 0)),
                  pl.BlockSpec((n_ct, LANES), lambda i: (0, 0)),
                  vec, vec],
        out_specs=pl.BlockSpec((CONV_ROWS, CONV_WIDTH), lambda i: (i, 0)),
        out_shape=jax.ShapeDtypeStruct((SEQ, CONV_WIDTH), BF16),
        scratch_shapes=[pltpu.VMEM((n_ct, CONV_EXT, LANES), F32),
                        pltpu.VMEM((SUBLANES - 1, CONV_SHIFTED, LANES), F32),
                        pltpu.VMEM((n_ct, CONV_ROWS, LANES), F32)],
        compiler_params=_params(1),
        name="conformer_conv",
    )(z, z, z, z, conv_w, conv_b, ln_w, ln_b)


OUT_TM = 1024
OUT_TN = 512


def _outproj_kernel(attn_ref, conv_ref, wa_ref, wc_ref, x_ref, o_ref, wa_bf, wc_bf):
    @pl.when(pl.program_id(1) == 0)
    def _():
        wa_bf[...] = wa_ref[...].astype(BF16)
        wc_bf[...] = wc_ref[...].astype(BF16)

    acc = jnp.dot(attn_ref[...], wa_bf[...], preferred_element_type=F32)
    acc = acc + jnp.dot(conv_ref[...], wc_bf[...], preferred_element_type=F32)
    o_ref[...] = x_ref[...] + acc


def _outproj(attn, conv, w, x):
    n = x.shape[0]
    conv_blk = ATTN_WIDTH // CONV_WIDTH
    return pl.pallas_call(
        _outproj_kernel,
        grid=(D_MODEL // OUT_TN, n // OUT_TM),
        in_specs=[pl.BlockSpec((OUT_TM, ATTN_WIDTH), lambda j, i: (i, 0)),
                  pl.BlockSpec((OUT_TM, CONV_WIDTH), lambda j, i: (i, 0)),
                  pl.BlockSpec((ATTN_WIDTH, OUT_TN), lambda j, i: (0, j)),
                  pl.BlockSpec((CONV_WIDTH, OUT_TN), lambda j, i: (conv_blk, j)),
                  pl.BlockSpec((OUT_TM, OUT_TN), lambda j, i: (i, j))],
        out_specs=pl.BlockSpec((OUT_TM, OUT_TN), lambda j, i: (i, j)),
        out_shape=jax.ShapeDtypeStruct((n, D_MODEL), F32),
        scratch_shapes=[pltpu.VMEM((ATTN_WIDTH, OUT_TN), BF16),
                        pltpu.VMEM((CONV_WIDTH, OUT_TN), BF16)],
        compiler_params=_params(2),
        name="outproj",
    )(attn, conv, w, w, x)


def _router_kernel(h_ref, nw_ref, wr_ref, br_ref, hn_ref, ids_ref, gates_ref, counts_ref, seen):
    x = h_ref[...]
    ms = jnp.mean(x * x, axis=-1, keepdims=True)
    hn = x * lax.rsqrt(ms + EPS) * nw_ref[...]
    words = _pack_rows(hn)
    for s in range(X_ROW_TILES):
        hn_ref[:, s, :] = words[:, s * LANES:(s + 1) * LANES]

    h_hi = hn.astype(BF16)
    h_lo = (hn - h_hi.astype(F32)).astype(BF16)
    hi_both = jnp.dot(h_hi, wr_ref[...], preferred_element_type=F32)
    lo_hi = jnp.dot(h_lo, wr_ref[:, :LANES], preferred_element_type=F32)
    logits = hi_both[:, :LANES] + (hi_both[:, LANES:] + lo_hi) + br_ref[...]
    lane = lax.broadcasted_iota(jnp.int32, logits.shape, 1)
    big = jnp.int32(1 << 20)

    is_g = lane < N_GROUPS
    gl = jnp.where(is_g, logits, -jnp.inf)
    ge = jnp.exp(gl - jnp.max(gl, axis=-1, keepdims=True))
    g_prob = ge / jnp.sum(ge, axis=-1, keepdims=True)
    g_w = jnp.max(g_prob, axis=-1, keepdims=True)
    g_top = jnp.min(jnp.where(is_g & (g_prob == g_w), lane, big), axis=-1, keepdims=True)

    eidx = lane - N_GROUPS
    grp_shift = EXPERTS_PER_GROUP.bit_length() - 1
    in_grp = (eidx >= 0) & (eidx < N_EXPERTS) & ((eidx >> grp_shift) == g_top)
    el = jnp.where(in_grp, logits, -jnp.inf)
    ee = jnp.exp(el - jnp.max(el, axis=-1, keepdims=True))
    e_prob = ee / jnp.sum(ee, axis=-1, keepdims=True)
    v1 = jnp.max(jnp.where(in_grp, e_prob, -1.0), axis=-1, keepdims=True)
    i1 = jnp.min(jnp.where(in_grp & (e_prob == v1), lane, big), axis=-1, keepdims=True)
    rest = in_grp & (lane != i1)
    v2 = jnp.max(jnp.where(rest, e_prob, -1.0), axis=-1, keepdims=True)
    i2 = jnp.min(jnp.where(rest & (e_prob == v2), lane, big), axis=-1, keepdims=True)
    tot = v1 + v2
    gates_ref[...] = jnp.where(lane == 0, g_w * v1 / tot,
                               jnp.where(lane == 1, g_w * v2 / tot, 0.0))

    e1 = i1 - N_GROUPS
    e2 = i2 - N_GROUPS
    first_choice = lane == e1
    second_choice = lane == e2 + N_EXPERTS
    chosen = jnp.where(first_choice | second_choice, 1.0, 0.0)
    t_row = lax.broadcasted_iota(jnp.int32, (ROUTER_ROWS, ROUTER_ROWS), 0)
    t_col = lax.broadcasted_iota(jnp.int32, (ROUTER_ROWS, ROUTER_ROWS), 1)
    earlier = jnp.where(t_col < t_row, 1.0, 0.0).astype(BF16)
    chosen = chosen + pltpu.roll(chosen, N_EXPERTS, axis=1)
    before = jnp.dot(earlier, chosen.astype(BF16), preferred_element_type=F32)

    @pl.when(pl.program_id(0) == 0)
    def _():
        seen[...] = jnp.zeros_like(seen)

    before = before + seen[0:1, :]
    rank1 = jnp.sum(jnp.where(first_choice, before, 0.0), axis=-1, keepdims=True)
    rank2 = jnp.sum(jnp.where(second_choice, before, 0.0), axis=-1, keepdims=True)
    seen[0:1, :] = seen[0:1, :] + jnp.sum(chosen, axis=0, keepdims=True)
    counts_ref[...] = jnp.broadcast_to(seen[0:1, :], counts_ref.shape).astype(jnp.int32)

    per_token = jnp.where(
        lane == 0, e1, jnp.where(
            lane == 1, e2, jnp.where(
                lane == 2, rank1.astype(jnp.int32), jnp.where(
                    lane == 3, rank2.astype(jnp.int32), 0))))
    ids_ref[...] = jnp.transpose(per_token)[:SUBLANES, :]


def _router(h, norm_w, w_router, b_router):
    return pl.pallas_call(
        _router_kernel,
        grid=(SEQ // ROUTER_ROWS,),
        in_specs=[pl.BlockSpec((ROUTER_ROWS, D_MODEL), lambda i: (i, 0)),
                  pl.BlockSpec((1, D_MODEL), lambda i: (0, 0)),
                  pl.BlockSpec((D_MODEL, 2 * LANES), lambda i: (0, 0)),
                  pl.BlockSpec((1, LANES), lambda i: (0, 0))],
        out_specs=[pl.BlockSpec((ROUTER_ROWS, X_ROW_TILES, LANES), lambda i: (i, 0, 0)),
                   pl.BlockSpec((SUBLANES, ROUTER_ROWS), lambda i: (0, i)),
                   pl.BlockSpec((ROUTER_ROWS, LANES), lambda i: (i, 0)),
                   pl.BlockSpec((SUBLANES, LANES), lambda i: (0, 0))],
        out_shape=[jax.ShapeDtypeStruct((SEQ, X_ROW_TILES, LANES), jnp.uint32),
                   jax.ShapeDtypeStruct((SUBLANES, SEQ), jnp.int32),
                   jax.ShapeDtypeStruct((SEQ, LANES), F32),
                   jax.ShapeDtypeStruct((SUBLANES, LANES), jnp.int32)],
        scratch_shapes=[pltpu.VMEM((SUBLANES, LANES), F32)],
        compiler_params=_params(1),
        name="ffn_norm_router",
    )(h, norm_w, w_router, b_router)


def _staged_token(stage, slot, r):
    return stage.at[slot, pl.ds(pl.multiple_of(r * GATHER_PITCH, SUBLANES), X_ROW_TILES)]


def _issue_token_rows(toks, nv, hn_hbm, stage, slot, sem):
    def start(r, carry):
        pltpu.make_async_copy(hn_hbm.at[toks[0, 0, r]], _staged_token(stage, slot, r),
                              sem.at[slot]).start()
        return carry

    def start_group(g, carry):
        for u in range(DMA_UNROLL):
            start(g * DMA_UNROLL + u, carry)
        return carry

    groups = nv // DMA_UNROLL
    lax.fori_loop(0, groups, start_group, 0)
    lax.fori_loop(groups * DMA_UNROLL, nv, start, 0)


def _wait_token_rows(nv, stage, slot, sem):
    def wait_rows(n_tokens):
        span = stage.at[slot, pl.ds(0, n_tokens * X_ROW_TILES)]
        pltpu.make_async_copy(span, span, sem.at[slot]).wait()

    def wait_many(c, carry):
        wait_rows(WAIT_ROWS)
        return carry

    def wait_one(c, carry):
        wait_rows(1)
        return carry

    lax.fori_loop(0, nv // WAIT_ROWS, wait_many, 0)
    lax.fori_loop(0, nv % WAIT_ROWS, wait_one, 0)


def _staged_block(stage, slot):
    return jnp.concatenate([stage[slot, pl.ds(s, MOE_ROWS, stride=GATHER_PITCH), :]
                            for s in range(X_ROW_TILES)], axis=-1)


def _unpack_rows(words):
    lo, hi = _unpack_words(words)
    return jnp.concatenate([lo.astype(BF16), hi.astype(BF16)], axis=-1)


def _stream_weight_tile(w, first_ref, seq_ref, n_tiles_ref, tile_copies, consume):
    n_tiles = n_tiles_ref[0]

    @pl.when(w == 0)
    def _():
        for k in range(WEIGHT_RING):
            @pl.when(k < n_tiles)
            def _():
                for c in tile_copies(k, k):
                    c.start(priority=WEIGHT_DMA_PRIORITY)

    @pl.when(first_ref[w] == 1)
    def _():
        k = seq_ref[w]
        slot = k % WEIGHT_RING
        for c in tile_copies(k, slot):
            c.wait()
        consume(slot)

        @pl.when(k + WEIGHT_RING < n_tiles)
        def _():
            for c in tile_copies(k + WEIGHT_RING, slot):
                c.start(priority=WEIGHT_DMA_PRIORITY)


def _moe_up_kernel(ib_ref, nv_ref, b_ref, ot_ref, valid_ref, first_ref, seq_ref, te_ref, tt_ref,
                   nt_ref, fetch_ref, xslot_ref, sslot_ref, pf_ref, pfb_ref, pfnv_ref,
                   tok_ref, tok_next_ref, hn_hbm, wg_hbm, wu_hbm, hid_ref,
                   stage, x_cache, ring_g, ring_u, wg_bf, wu_bf, row_sem, sem):
    w = pl.program_id(0)

    @pl.when(w == 0)
    def _():
        stage[...] = jnp.zeros_like(stage)
        _issue_token_rows(tok_ref, nv_ref[0], hn_hbm, stage, 0, row_sem)

    @pl.when(pf_ref[w] == 1)
    def _():
        _issue_token_rows(tok_next_ref, pfnv_ref[w], hn_hbm, stage, 1 - sslot_ref[w], row_sem)

    def tile_copies(k, slot):
        e = te_ref[k]
        col = pl.multiple_of(tt_ref[k] * UP_TILE, UP_TILE)
        return (pltpu.make_async_copy(wg_hbm.at[e, :, pl.ds(col, UP_TILE)], ring_g.at[slot],
                                      sem.at[0, slot]),
                pltpu.make_async_copy(wu_hbm.at[e, :, pl.ds(col, UP_TILE)], ring_u.at[slot],
                                      sem.at[1, slot]))

    def consume(slot):
        wg_bf[...] = ring_g[slot].astype(BF16)
        wu_bf[...] = ring_u[slot].astype(BF16)

    _stream_weight_tile(w, first_ref, seq_ref, nt_ref, tile_copies, consume)

    @pl.when(valid_ref[w] == 1)
    def _():
        x_slot = xslot_ref[w]

        @pl.when(fetch_ref[w] == 1)
        def _():
            slot = sslot_ref[w]
            _wait_token_rows(nv_ref[w], stage, slot, row_sem)
            x_cache[x_slot] = _unpack_rows(_staged_block(stage, slot))

        x = x_cache[x_slot]
        g = jnp.dot(x, wg_bf[...], preferred_element_type=F32)
        u = jnp.dot(x, wu_bf[...], preferred_element_type=F32)
        hid_ref[...] = (g * jax.nn.sigmoid(g) * u).astype(hid_ref.dtype)

    @pl.when(valid_ref[w] == 0)
    def _():
        hid_ref[...] = jnp.zeros_like(hid_ref)


def _moe_up(items, info, n_valid, row_tok, hn_words, w_gate, w_up):
    n_items = N_MOE_BLOCKS * (D_FF // UP_TILE)
    in_blk = items[0]
    schedule = _gather_schedule(in_blk, info, n_valid)
    n_prefetch = 1 + len(items) + len(schedule)
    toks = row_tok.reshape(N_MOE_BLOCKS, 1, MOE_ROWS)
    smem_block = lambda index_map: pl.BlockSpec((1, 1, MOE_ROWS), index_map,
                                                memory_space=pltpu.SMEM)
    any_spec = pl.BlockSpec(memory_space=pl.ANY)
    return pl.pallas_call(
        _moe_up_kernel,
        grid_spec=pltpu.PrefetchScalarGridSpec(
            num_scalar_prefetch=n_prefetch,
            grid=(n_items,),
            in_specs=[smem_block(lambda w, *refs: (refs[0][w], 0, 0)),
                      smem_block(lambda w, *refs: (refs[n_prefetch - 2][w], 0, 0)),
                      any_spec, any_spec, any_spec],
            out_specs=pl.BlockSpec((MOE_ROWS, UP_TILE),
                                   lambda w, ib, nv, b, ot, *_: (b[w], ot[w])),
            scratch_shapes=[pltpu.VMEM((2, MOE_ROWS * GATHER_PITCH, LANES), jnp.uint32),
                            pltpu.VMEM((X_CACHE_SLOTS, MOE_ROWS, D_MODEL), BF16),
                            pltpu.VMEM((WEIGHT_RING, D_MODEL, UP_TILE), F32),
                            pltpu.VMEM((WEIGHT_RING, D_MODEL, UP_TILE), F32),
                            pltpu.VMEM((D_MODEL, UP_TILE), BF16),
                            pltpu.VMEM((D_MODEL, UP_TILE), BF16),
                            pltpu.SemaphoreType.DMA((2,)),
                            pltpu.SemaphoreType.DMA((2, WEIGHT_RING))]),
        out_shape=jax.ShapeDtypeStruct((N_MOE_ROWS, D_FF), BF16),
        compiler_params=_params(1),
        name="moe_gate_up",
    )(in_blk, _lookup(n_valid, in_blk), *items[1:], *schedule, toks, toks, hn_words, w_gate, w_up)


def _moe_down_kernel(ib_ref, b_ref, ot_ref, valid_ref, first_ref, seq_ref, te_ref, tt_ref, nt_ref,
                     hid_ref, wd_hbm, y_ref, ring, wd_bf, sem):
    w = pl.program_id(0)

    def tile_copies(k, slot):
        col = pl.multiple_of(tt_ref[k] * DOWN_TILE, DOWN_TILE)
        return (pltpu.make_async_copy(wd_hbm.at[te_ref[k], :, pl.ds(col, DOWN_TILE)],
                                      ring.at[slot], sem.at[slot]),)

    def consume(slot):
        wd_bf[...] = ring[slot].astype(BF16)

    _stream_weight_tile(w, first_ref, seq_ref, nt_ref, tile_copies, consume)

    @pl.when(valid_ref[w] == 1)
    def _():
        y = jnp.dot(hid_ref[...], wd_bf[...], preferred_element_type=F32)
        words = _pack_rows(y)
        for s in range(DOWN_TILE // 2 // LANES):
            y_ref[:, s, :] = words[:, s * LANES:(s + 1) * LANES]

    @pl.when(valid_ref[w] == 0)
    def _():
        y_ref[...] = jnp.zeros_like(y_ref)


def _moe_down(items, hid, w_down):
    n_items = N_MOE_BLOCKS * (D_MODEL // DOWN_TILE)
    tile_rows = DOWN_TILE // 2 // LANES
    return pl.pallas_call(
        _moe_down_kernel,
        grid_spec=pltpu.PrefetchScalarGridSpec(
            num_scalar_prefetch=9,
            grid=(n_items,),
            in_specs=[pl.BlockSpec((MOE_ROWS, D_FF), lambda w, ib, *_: (ib[w], 0)),
                      pl.BlockSpec(memory_space=pl.ANY)],
            out_specs=pl.BlockSpec((MOE_ROWS, tile_rows, LANES),
                                   lambda w, ib, b, ot, *_: (b[w], ot[w], 0)),
            scratch_shapes=[pltpu.VMEM((WEIGHT_RING, D_FF, DOWN_TILE), F32),
                            pltpu.VMEM((D_FF, DOWN_TILE), BF16),
                            pltpu.SemaphoreType.DMA((WEIGHT_RING,))]),
        out_shape=jax.ShapeDtypeStruct((N_MOE_ROWS, Y_ROW_TILES, LANES), jnp.uint32),
        compiler_params=_params(1),
        name="moe_down",
    )(*items, hid, w_down)


def _combine_kernel(dest_ref, dest_next_ref, h_ref, gate_ref, y_hbm, o_ref, buf, sem):
    i = pl.program_id(0)

    def issue(dests, slot):
        def start(r, carry):
            for k in range(TOP_K):
                dst = buf.at[slot, k, pl.ds(r * COMBINE_PITCH, Y_ROW_TILES)]
                pltpu.make_async_copy(y_hbm.at[dests[0, k, r]], dst, sem.at[slot]).start()
            return carry
        lax.fori_loop(0, COMBINE_ROWS, start, 0, unroll=DMA_UNROLL // TOP_K)

    @pl.when(i == 0)
    def _():
        issue(dest_ref, 0)

    @pl.when(i + 1 < pl.num_programs(0))
    def _():
        issue(dest_next_ref, (i + 1) % 2)

    slot = i % 2

    for k in range(TOP_K):
        for c in range(COMBINE_ROWS // WAIT_ROWS):
            span = buf.at[slot, k, pl.ds(c * WAIT_ROWS * Y_ROW_TILES, WAIT_ROWS * Y_ROW_TILES)]
            pltpu.make_async_copy(span, span, sem.at[slot]).wait()

    g0 = gate_ref[:, 0:1]
    g1 = gate_ref[:, 1:2]
    tiles_per_half = DOWN_TILE // 2 // LANES
    for s in range(Y_ROW_TILES):
        lo_col = (s // tiles_per_half) * DOWN_TILE + (s % tiles_per_half) * LANES
        lo0, hi0 = _unpack_words(buf[slot, 0, pl.ds(s, COMBINE_ROWS, stride=COMBINE_PITCH), :])
        lo1, hi1 = _unpack_words(buf[slot, 1, pl.ds(s, COMBINE_ROWS, stride=COMBINE_PITCH), :])
        for col, y0, y1 in ((lo_col, lo0, lo1), (lo_col + DOWN_TILE // 2, hi0, hi1)):
            cols = slice(col, col + LANES)
            o_ref[:, cols] = h_ref[:, cols] + (y0 * g0 + y1 * g1)


def _combine(dest, h, gates, y):
    n_tiles = SEQ // COMBINE_ROWS
    dests = jnp.stack([d.reshape(n_tiles, COMBINE_ROWS) for d in dest], axis=1)
    blk = (1, TOP_K, COMBINE_ROWS)
    return pl.pallas_call(
        _combine_kernel,
        grid=(n_tiles,),
        in_specs=[pl.BlockSpec(blk, lambda i: (i, 0, 0), memory_space=pltpu.SMEM),
                  pl.BlockSpec(blk, lambda i: (jnp.minimum(i + 1, n_tiles - 1), 0, 0),
                               memory_space=pltpu.SMEM),
                  pl.BlockSpec((COMBINE_ROWS, D_MODEL), lambda i: (i, 0)),
                  pl.BlockSpec((COMBINE_ROWS, LANES), lambda i: (i, 0)),
                  pl.BlockSpec(memory_space=pl.ANY)],
        out_specs=pl.BlockSpec((COMBINE_ROWS, D_MODEL), lambda i: (i, 0)),
        out_shape=jax.ShapeDtypeStruct((SEQ, D_MODEL), F32),
        scratch_shapes=[pltpu.VMEM((2, TOP_K, COMBINE_ROWS * COMBINE_PITCH, LANES), jnp.uint32),
                        pltpu.SemaphoreType.DMA((2,))],
        compiler_params=_params(1),
        name="moe_combine",
    )(dests, dests, h, gates, y)


def _lookup(table, idx):
    pos = jnp.arange(table.shape[0], dtype=jnp.int32)
    return jnp.sum(jnp.where(idx[:, None] == pos[None, :], table[None, :], 0), axis=1)


def _count_le(ends, x):
    return jnp.sum((ends[None, :] <= x[:, None]).astype(jnp.int32), axis=1)


def _work_items(n_blocks_e, block_start_e, tiles):
    i32 = lambda a: a.astype(jnp.int32)
    n_items = N_MOE_BLOCKS * tiles
    per_e = n_blocks_e * tiles
    ends = jnp.cumsum(per_e)
    total = ends[-1]
    idx = jnp.arange(n_items, dtype=jnp.int32)
    valid = idx < total
    w = jnp.minimum(idx, total - 1)
    e = _count_le(ends, w)
    local = w - _lookup(ends - per_e, e)
    nb = _lookup(n_blocks_e, e)
    first_blk = _lookup(block_start_e, e)
    spare = idx - total
    w_tile = local // nb
    o_tile = jnp.where(valid, w_tile, spare % tiles)
    blk = jnp.where(valid, first_blk + local % nb, total // tiles + spare // tiles)
    first = valid & (local % nb == 0)
    active = n_blocks_e > 0
    active_rank = jnp.cumsum(active) - active
    seq = _lookup(active_rank, e) * tiles + w_tile
    k = jnp.arange(N_EXPERTS * tiles, dtype=jnp.int32)
    experts = jnp.arange(N_EXPERTS, dtype=jnp.int32)
    is_kth = active[None, :] & (active_rank[None, :] == (k // tiles)[:, None])
    tile_e = jnp.sum(jnp.where(is_kth, experts[None, :], 0), axis=1)
    tile_t = k % tiles
    n_tiles = (jnp.sum(active) * tiles).reshape(1)
    in_blk = first_blk + local % nb
    items = (i32(in_blk), i32(blk), i32(o_tile), i32(valid), i32(first), i32(seq), i32(tile_e),
             i32(tile_t), i32(n_tiles))
    return items, dict(blocks_of_expert=nb, local_block=local % nb, weight_tile=w_tile, valid=valid)


def _gather_schedule(in_blk, info, n_valid):
    i32 = lambda a: a.astype(jnp.int32)
    n = in_blk.shape[0]
    cached = (info['blocks_of_expert'] <= X_CACHE_SLOTS) & (info['weight_tile'] > 0)
    fetch = info['valid'] & ~cached
    x_slot = info['local_block'] % X_CACHE_SLOTS
    stage_slot = (jnp.cumsum(fetch) - 1) % 2
    idx = jnp.arange(n, dtype=jnp.int32)
    later = lax.cummin(jnp.where(fetch, idx, n), axis=0, reverse=True)
    next_fetch = jnp.concatenate([later[1:], jnp.full((1,), n, later.dtype)])
    prefetch = fetch & (next_fetch < n)
    pf_blk = jnp.where(prefetch, _lookup(in_blk, jnp.minimum(next_fetch, n - 1)), in_blk)
    return (i32(fetch), i32(x_slot), i32(stage_slot), i32(prefetch), i32(pf_blk),
            i32(_lookup(n_valid, pf_blk)))


def _dispatch_plan(ids, counts):
    counts = counts[0, :N_EXPERTS]
    n_blocks_e = (counts + MOE_ROWS - 1) // MOE_ROWS
    block_ends = jnp.cumsum(n_blocks_e)
    block_start_e = block_ends - n_blocks_e
    first_row_e = block_start_e * MOE_ROWS
    dest = [(_lookup(first_row_e, ids[k]) + ids[TOP_K + k]).astype(jnp.int32)
            for k in range(TOP_K)]
    token = jnp.arange(SEQ, dtype=jnp.int32)
    row_tok = jnp.zeros((N_MOE_ROWS,), jnp.int32).at[jnp.concatenate(dest)].set(
        jnp.concatenate([token] * TOP_K), unique_indices=True, mode='promise_in_bounds')
    blk = jnp.arange(N_MOE_BLOCKS, dtype=jnp.int32)
    blk_e = jnp.minimum(_count_le(block_ends, blk), N_EXPERTS - 1)
    n_valid = jnp.clip(_lookup(counts, blk_e) - (blk - _lookup(block_start_e, blk_e)) * MOE_ROWS,
                       0, MOE_ROWS)
    n_valid = jnp.where(blk < block_ends[-1], n_valid, 0).astype(jnp.int32)
    return dest, row_tok, n_valid, n_blocks_e.astype(jnp.int32), block_start_e.astype(jnp.int32)


def kernel(x, norm_mix_w, w_in, q_norm_w, k_norm_w, conv_w, conv_b, conv_ln_w, conv_ln_b, w_out,
           norm_ffn_w, w_group_router, b_group_router, w_expert_router, b_expert_router,
           w_gate, w_up, w_down):
    h = x.reshape(SEQ, D_MODEL)
    slopes = jnp.exp2(-8.0 * jnp.arange(1, N_HEADS + 1, dtype=F32) / N_HEADS)
    for l in range(norm_mix_w.shape[0]):
        hn = _rmsnorm(h, norm_mix_w[l][None, :])
        qk_norm = jnp.concatenate([jnp.tile(q_norm_w[l], N_HEADS), jnp.tile(k_norm_w[l], N_HEADS)])
        z = _inproj(hn, w_in[l], qk_norm[None, :])
        attn = _attention(z, slopes)
        n_ct = CONV_WIDTH // LANES
        conv = _conformer_conv(z, conv_w[l].reshape(CONV_KERNEL, n_ct, LANES),
                               conv_b[l].reshape(n_ct, LANES), conv_ln_w[l][None, :],
                               conv_ln_b[l][None, :])
        h = _outproj(attn, conv, w_out[l], h)

        pad = LANES - N_GROUPS - N_EXPERTS
        w_router = jnp.concatenate([w_group_router[l], w_expert_router[l],
                                    jnp.zeros((D_MODEL, pad), F32)], axis=1)
        b_router = jnp.concatenate([b_group_router[l], b_expert_router[l], jnp.zeros((pad,), F32)])
        w_router_hi = w_router.astype(BF16)
        w_router_lo = (w_router - w_router_hi.astype(F32)).astype(BF16)
        hn_words, ids, gates, counts = _router(h, norm_ffn_w[l][None, :],
                                               jnp.concatenate([w_router_hi, w_router_lo], axis=1),
                                               b_router[None, :])
        dest, row_tok, n_valid, n_blocks_e, block_start_e = _dispatch_plan(ids, counts)
        up_items, up_info = _work_items(n_blocks_e, block_start_e, D_FF // UP_TILE)
        hid = _moe_up(up_items, up_info, n_valid, row_tok, hn_words, w_gate[l], w_up[l])
        down_items, _ = _work_items(n_blocks_e, block_start_e, D_MODEL // DOWN_TILE)
        y = _moe_down(down_items, hid, w_down[l])
        h = _combine(dest, h, gates, y)
    return h.reshape(x.shape)
```

```python
import math

import jax
import jax.numpy as jnp
from jax import lax
from jax.experimental import pallas as pl
from jax.experimental.pallas import tpu as pltpu

F32 = jnp.float32
BF16 = jnp.bfloat16

D_MODEL = 4096
SEQ = 8192
HEAD_DIM = 128
N_HEADS = 16
ATTN_WIDTH = N_HEADS * HEAD_DIM
CONV_WIDTH = D_MODEL - ATTN_WIDTH
CONV_KERNEL = 31
IN_COLS = 3 * ATTN_WIDTH + 2 * CONV_WIDTH
DILATIONS = (1, 4, 16)
ATTN_BLOCK = 128
N_KEYS = 128
N_GROUPS = 8
EXPERTS_PER_GROUP = 8
N_EXPERTS = N_GROUPS * EXPERTS_PER_GROUP
TOP_K = 2
D_FF = 1024
EPS = 1e-6
NEG_INF = -1e30

LANES = 128
SUBLANES = 8
VMEM_LIMIT = 56 * 1024 * 1024

SPAN = DILATIONS[-1] * ATTN_BLOCK

MOE_ROWS = 128
N_ASSIGN = SEQ * TOP_K
N_MOE_ROWS = N_ASSIGN + N_EXPERTS * MOE_ROWS
N_MOE_BLOCKS = N_MOE_ROWS // MOE_ROWS
UP_TILE = 512
DOWN_TILE = 4096
WEIGHT_RING = 2
WEIGHT_DMA_PRIORITY = 1
ROUTER_ROWS = 256
COMBINE_ROWS = 128
X_CACHE_SLOTS = 4
X_ROW_TILES = D_MODEL // 2 // LANES
Y_ROW_TILES = D_MODEL // 2 // LANES
COMBINE_PITCH = Y_ROW_TILES + 4
GATHER_PITCH = X_ROW_TILES + 8
DMA_UNROLL = 8
WAIT_ROWS = 32


def _params(n_axes):
    return pltpu.CompilerParams(dimension_semantics=("arbitrary",) * n_axes,
                                vmem_limit_bytes=VMEM_LIMIT)


def _pack_rows(x):
    half = x.shape[1] // 2
    lo = pltpu.bitcast(x[:, :half].astype(BF16).astype(F32), jnp.uint32)
    hi = pltpu.bitcast(x[:, half:].astype(BF16).astype(F32), jnp.uint32)
    return (hi & jnp.uint32(0xFFFF0000)) | (lo >> 16)


def _unpack_words(words):
    return (pltpu.bitcast(words << 16, F32),
            pltpu.bitcast(words & jnp.uint32(0xFFFF0000), F32))


def _rmsnorm_kernel(x_ref, w_ref, o_ref):
    x = x_ref[...]
    ms = jnp.mean(x * x, axis=-1, keepdims=True)
    o_ref[...] = (x * lax.rsqrt(ms + EPS) * w_ref[...]).astype(o_ref.dtype)


def _rmsnorm(x, w, rows=512):
    n, d = x.shape
    return pl.pallas_call(
        _rmsnorm_kernel,
        grid=(n // rows,),
        in_specs=[pl.BlockSpec((rows, d), lambda i: (i, 0)),
                  pl.BlockSpec((1, d), lambda i: (0, 0))],
        out_specs=pl.BlockSpec((rows, d), lambda i: (i, 0)),
        out_shape=jax.ShapeDtypeStruct((n, d), BF16),
        compiler_params=_params(1),
        name="rmsnorm",
    )(x, w)


IN_TM = 1024
IN_TN = 512
IN_CHUNKS = (256, 256, 256, 128, 128)
QK_TILES = 2 * ATTN_WIDTH // IN_TN


def _inproj_kernel(hn_ref, w_ref, nw_ref, o_ref, w_bf):
    j = pl.program_id(0)

    @pl.when(pl.program_id(1) == 0)
    def _():
        w_bf[...] = w_ref[...].astype(BF16)

    is_qk = j < QK_TILES
    row0 = 0
    for chunk in IN_CHUNKS:
        rows = slice(row0, row0 + chunk)
        row0 += chunk
        acc = jnp.dot(hn_ref[rows, :], w_bf[...], preferred_element_type=F32)
        for hd in range(IN_TN // HEAD_DIM):
            sl = slice(hd * HEAD_DIM, (hd + 1) * HEAD_DIM)
            t = acc[:, sl]
            ms = jnp.mean(t * t, axis=-1, keepdims=True)
            normed = t * lax.rsqrt(ms + EPS) * nw_ref[:, sl]
            o_ref[rows, sl] = jnp.where(is_qk, normed, t).astype(o_ref.dtype)


def _inproj(hn, w, qk_norm_w):
    n = hn.shape[0]
    return pl.pallas_call(
        _inproj_kernel,
        grid=(IN_COLS // IN_TN, n // IN_TM),
        in_specs=[pl.BlockSpec((IN_TM, D_MODEL), lambda j, i: (i, 0)),
                  pl.BlockSpec((D_MODEL, IN_TN), lambda j, i: (0, j)),
                  pl.BlockSpec((1, IN_TN), lambda j, i: (0, jnp.minimum(j, QK_TILES - 1)))],
        out_specs=pl.BlockSpec((IN_TM, IN_TN), lambda j, i: (i, j)),
        out_shape=jax.ShapeDtypeStruct((n, IN_COLS), BF16),
        scratch_shapes=[pltpu.VMEM((D_MODEL, IN_TN), BF16)],
        compiler_params=_params(2),
        name="inproj",
    )(hn, w, qk_norm_w)


def _attn_kernel(slopes_ref, q_ref, kc_ref, kp_ref, vc_ref, vp_ref, o_ref, qf, kf, vf, ob, lb):
    h = pl.program_id(0)
    first_span = pl.program_id(1) == 0
    slope = slopes_ref[h]

    qi = lax.broadcasted_iota(jnp.int32, (ATTN_BLOCK, 2 * ATTN_BLOCK), 0)
    kj = lax.broadcasted_iota(jnp.int32, (ATTN_BLOCK, 2 * ATTN_BLOCK), 1)
    steps = qi + ATTN_BLOCK - kj
    valid = (steps >= 0) & (steps <= N_KEYS)
    stepsf = steps.astype(F32)
    scale = 1.0 / math.sqrt(HEAD_DIM)

    def attend(b, d, bias, q0, q, k, v, no_prev_block):
        s = lax.dot_general(q, k, (((1,), (1,)), ((), ())),
                            preferred_element_type=F32) * scale + bias
        if no_prev_block is not None:
            s = jnp.where(kj >= jnp.where(no_prev_block, ATTN_BLOCK, 0), s, NEG_INF)
        m = jnp.max(s, axis=-1, keepdims=True)
        p = jnp.exp(s - m)
        den = jnp.sum(p, axis=-1, keepdims=True)
        o = jnp.dot(p.astype(BF16), v, preferred_element_type=F32) / den
        lse = m + jnp.log(den)
        ob[b, pl.ds(q0, ATTN_BLOCK, stride=d), :] = o
        lb[b, pl.ds(q0, ATTN_BLOCK, stride=d), :] = jnp.broadcast_to(lse, (ATTN_BLOCK, HEAD_DIM))

    def banded_bias(d):
        return jnp.where(valid, (-slope * float(d)) * stepsf, NEG_INF)

    def key_window(cur, prev, n):
        if n > 0:
            return cur[(n - 1) * ATTN_BLOCK:(n + 1) * ATTN_BLOCK, :]
        last = prev.shape[0] - ATTN_BLOCK
        return jnp.concatenate([prev[last:, :], cur[:ATTN_BLOCK, :]], axis=0)

    b, d = 0, DILATIONS[0]
    bias = banded_bias(d)
    for n in range(SPAN // ATTN_BLOCK):
        attend(b, d, bias, n * ATTN_BLOCK, q_ref[n * ATTN_BLOCK:(n + 1) * ATTN_BLOCK, :],
               key_window(kc_ref, kp_ref, n), key_window(vc_ref, vp_ref, n),
               first_span if n == 0 else None)

    qf[...] = q_ref[...].astype(F32)
    kf[0:SPAN, :] = kp_ref[...].astype(F32)
    kf[SPAN:2 * SPAN, :] = kc_ref[...].astype(F32)
    vf[0:SPAN, :] = vp_ref[...].astype(F32)
    vf[SPAN:2 * SPAN, :] = vc_ref[...].astype(F32)
    for b, d in list(enumerate(DILATIONS))[1:]:
        bias = banded_bias(d)
        for r in range(d):
            for n in range(SPAN // (ATTN_BLOCK * d)):
                q0 = r + d * ATTN_BLOCK * n
                k0 = SPAN + q0 - d * ATTN_BLOCK
                attend(b, d, bias, q0,
                       qf[pl.ds(q0, ATTN_BLOCK, stride=d), :].astype(BF16),
                       kf[pl.ds(k0, 2 * ATTN_BLOCK, stride=d), :].astype(BF16),
                       vf[pl.ds(k0, 2 * ATTN_BLOCK, stride=d), :].astype(BF16),
                       first_span if n == 0 else None)

    chunk = 256
    for c in range(SPAN // chunk):
        rows = slice(c * chunk, (c + 1) * chunk)
        l0, l1, l2 = lb[0, rows, :], lb[1, rows, :], lb[2, rows, :]
        mx = jnp.maximum(jnp.maximum(l0, l1), l2)
        w0, w1, w2 = jnp.exp(l0 - mx), jnp.exp(l1 - mx), jnp.exp(l2 - mx)
        num = w0 * ob[0, rows, :] + w1 * ob[1, rows, :] + w2 * ob[2, rows, :]
        o_ref[rows, :] = (num / (w0 + w1 + w2)).astype(o_ref.dtype)


def _attention(z, slopes):
    n_spans = SEQ // SPAN
    k_col, v_col = N_HEADS, 2 * N_HEADS
    blk = (SPAN, HEAD_DIM)
    prev = lambda s: jnp.maximum(s - 1, 0)
    return pl.pallas_call(
        _attn_kernel,
        grid_spec=pltpu.PrefetchScalarGridSpec(
            num_scalar_prefetch=1,
            grid=(N_HEADS, n_spans),
            in_specs=[pl.BlockSpec(blk, lambda h, s, sl: (s, h)),
                      pl.BlockSpec(blk, lambda h, s, sl: (s, k_col + h)),
                      pl.BlockSpec(blk, lambda h, s, sl: (prev(s), k_col + h)),
                      pl.BlockSpec(blk, lambda h, s, sl: (s, v_col + h)),
                      pl.BlockSpec(blk, lambda h, s, sl: (prev(s), v_col + h))],
            out_specs=pl.BlockSpec(blk, lambda h, s, sl: (s, h)),
            scratch_shapes=[pltpu.VMEM((SPAN, HEAD_DIM), F32),
                            pltpu.VMEM((2 * SPAN, HEAD_DIM), F32),
                            pltpu.VMEM((2 * SPAN, HEAD_DIM), F32),
                            pltpu.VMEM((3, SPAN, HEAD_DIM), F32),
                            pltpu.VMEM((3, SPAN, HEAD_DIM), F32)]),
        out_shape=jax.ShapeDtypeStruct((SEQ, ATTN_WIDTH), BF16),
        compiler_params=_params(2),
        name="dilated_attention",
    )(slopes, z, z, z, z, z)


CONV_ROWS = 256
CONV_HALO = 32
CONV_CHUNK = 64

CONV_EXT = CONV_HALO + CONV_ROWS
CONV_SHIFTED = CONV_EXT - SUBLANES


def _conv_kernel(a_ref, g_ref, ah_ref, gh_ref, cw_ref, cb_ref, lw_ref, lb_ref, o_ref,
                 u_ext, shifted, c_scr):
    i = pl.program_id(0)
    n_ct = CONV_WIDTH // LANES
    for ct in range(n_ct):
        cols = slice(ct * LANES, (ct + 1) * LANES)
        u_ext[ct, CONV_HALO:, :] = (a_ref[:, cols].astype(F32)
                                    * jax.nn.sigmoid(g_ref[:, cols].astype(F32)))
        uh = ah_ref[:, cols].astype(F32) * jax.nn.sigmoid(gh_ref[:, cols].astype(F32))
        u_ext[ct, 0:CONV_HALO, :] = jnp.where(i == 0, 0.0, uh)

    first_tap = CONV_HALO - (CONV_KERNEL - 1)

    def slab(ct, carry):
        for k in range(1, SUBLANES):
            shifted[k - 1] = u_ext[ct, k:k + CONV_SHIFTED, :]
        for rc in range(CONV_ROWS // CONV_CHUNK):
            base = rc * CONV_CHUNK
            acc = jnp.broadcast_to(cb_ref[pl.ds(ct, 1), :], (CONV_CHUNK, LANES))
            for j in range(CONV_KERNEL):
                k = (first_tap + j) % SUBLANES
                row = base + (first_tap + j) - k
                if k == 0:
                    src = u_ext[ct, row:row + CONV_CHUNK, :]
                else:
                    src = shifted[k - 1, row:row + CONV_CHUNK, :]
                acc = acc + cw_ref[j, pl.ds(ct, 1), :] * src
            c_scr[ct, base:base + CONV_CHUNK, :] = acc
        return carry

    lax.fori_loop(0, n_ct, slab, 0)

    total = c_scr[0]
    for ct in range(1, n_ct):
        total = total + c_scr[ct]
    mu = jnp.sum(total, axis=-1, keepdims=True) * (1.0 / CONV_WIDTH)
    sq = jnp.square(c_scr[0] - mu)
    for ct in range(1, n_ct):
        sq = sq + jnp.square(c_scr[ct] - mu)
    var = jnp.sum(sq, axis=-1, keepdims=True) * (1.0 / CONV_WIDTH)
    rstd = lax.rsqrt(var + EPS)
    for ct in range(n_ct):
        cols = slice(ct * LANES, (ct + 1) * LANES)
        y = (c_scr[ct] - mu) * rstd * lw_ref[:, cols] + lb_ref[:, cols]
        o_ref[:, cols] = (y * jax.nn.sigmoid(y)).astype(o_ref.dtype)


def _conformer_conv(z, conv_w, conv_b, ln_w, ln_b):
    a_col = 3 * ATTN_WIDTH // CONV_WIDTH
    g_col = a_col + 1
    n_ct = CONV_WIDTH // LANES
    halo_blocks = CONV_ROWS // CONV_HALO
    halo = lambda i: jnp.maximum(i * halo_blocks - 1, 0)
    vec = pl.BlockSpec((1, CONV_WIDTH), lambda i: (0, 0))
    return pl.pallas_call(
        _conv_kernel,
        grid=(SEQ // CONV_ROWS,),
        in_specs=[pl.BlockSpec((CONV_ROWS, CONV_WIDTH), lambda i: (i, a_col)),
                  pl.BlockSpec((CONV_ROWS, CONV_WIDTH), lambda i: (i, g_col)),
                  pl.BlockSpec((CONV_HALO, CONV_WIDTH), lambda i: (halo(i), a_col)),
                  pl.BlockSpec((CONV_HALO, CONV_WIDTH), lambda i: (halo(i), g_col)),
                  pl.BlockSpec((CONV_KERNEL, n_ct, LANES), lambda i: (0, 0, 0)),
                  pl.BlockSpec((n_ct, LANES), lambda i: (0, 0)),
                  vec, vec],
        out_specs=pl.BlockSpec((CONV_ROWS, CONV_WIDTH), lambda i: (i, 0)),
        out_shape=jax.ShapeDtypeStruct((SEQ, CONV_WIDTH), BF16),
        scratch_shapes=[pltpu.VMEM((n_ct, CONV_EXT, LANES), F32),
                        pltpu.VMEM((SUBLANES - 1, CONV_SHIFTED, LANES), F32),
                        pltpu.VMEM((n_ct, CONV_ROWS, LANES), F32)],
        compiler_params=_params(1),
        name="conformer_conv",
    )(z, z, z, z, conv_w, conv_b, ln_w, ln_b)


OUT_TM = 1024
OUT_TN = 512


def _outproj_kernel(attn_ref, conv_ref, wa_ref, wc_ref, x_ref, o_ref, wa_bf, wc_bf):
    @pl.when(pl.program_id(1) == 0)
    def _():
        wa_bf[...] = wa_ref[...].astype(BF16)
        wc_bf[...] = wc_ref[...].astype(BF16)

    acc = jnp.dot(attn_ref[...], wa_bf[...], preferred_element_type=F32)
    acc = acc + jnp.dot(conv_ref[...], wc_bf[...], preferred_element_type=F32)
    o_ref[...] = x_ref[...] + acc


def _outproj(attn, conv, w, x):
    n = x.shape[0]
    conv_blk = ATTN_WIDTH // CONV_WIDTH
    return pl.pallas_call(
        _outproj_kernel,
        grid=(D_MODEL // OUT_TN, n // OUT_TM),
        in_specs=[pl.BlockSpec((OUT_TM, ATTN_WIDTH), lambda j, i: (i, 0)),
                  pl.BlockSpec((OUT_TM, CONV_WIDTH), lambda j, i: (i, 0)),
                  pl.BlockSpec((ATTN_WIDTH, OUT_TN), lambda j, i: (0, j)),
                  pl.BlockSpec((CONV_WIDTH, OUT_TN), lambda j, i: (conv_blk, j)),
                  pl.BlockSpec((OUT_TM, OUT_TN), lambda j, i: (i, j))],
        out_specs=pl.BlockSpec((OUT_TM, OUT_TN), lambda j, i: (i, j)),
        out_shape=jax.ShapeDtypeStruct((n, D_MODEL), F32),
        scratch_shapes=[pltpu.VMEM((ATTN_WIDTH, OUT_TN), BF16),
                        pltpu.VMEM((CONV_WIDTH, OUT_TN), BF16)],
        compiler_params=_params(2),
        name="outproj",
    )(attn, conv, w, w, x)


def _router_kernel(h_ref, nw_ref, wr_ref, br_ref, hn_ref, ids_ref, gates_ref, counts_ref, seen):
    x = h_ref[...]
    ms = jnp.mean(x * x, axis=-1, keepdims=True)
    hn = x * lax.rsqrt(ms + EPS) * nw_ref[...]
    words = _pack_rows(hn)
    for s in range(X_ROW_TILES):
        hn_ref[:, s, :] = words[:, s * LANES:(s + 1) * LANES]

    h_hi = hn.astype(BF16)
    h_lo = (hn - h_hi.astype(F32)).astype(BF16)
    hi_both = jnp.dot(h_hi, wr_ref[...], preferred_element_type=F32)
    lo_hi = jnp.dot(h_lo, wr_ref[:, :LANES], preferred_element_type=F32)
    logits = hi_both[:, :LANES] + (hi_both[:, LANES:] + lo_hi) + br_ref[...]
    lane = lax.broadcasted_iota(jnp.int32, logits.shape, 1)
    big = jnp.int32(1 << 20)

    is_g = lane < N_GROUPS
    gl = jnp.where(is_g, logits, -jnp.inf)
    ge = jnp.exp(gl - jnp.max(gl, axis=-1, keepdims=True))
    g_prob = ge / jnp.sum(ge, axis=-1, keepdims=True)
    g_w = jnp.max(g_prob, axis=-1, keepdims=True)
    g_top = jnp.min(jnp.where(is_g & (g_prob == g_w), lane, big), axis=-1, keepdims=True)

    eidx = lane - N_GROUPS
    grp_shift = EXPERTS_PER_GROUP.bit_length() - 1
    in_grp = (eidx >= 0) & (eidx < N_EXPERTS) & ((eidx >> grp_shift) == g_top)
    el = jnp.where(in_grp, logits, -jnp.inf)
    ee = jnp.exp(el - jnp.max(el, axis=-1, keepdims=True))
    e_prob = ee / jnp.sum(ee, axis=-1, keepdims=True)
    v1 = jnp.max(jnp.where(in_grp, e_prob, -1.0), axis=-1, keepdims=True)
    i1 = jnp.min(jnp.where(in_grp & (e_prob == v1), lane, big), axis=-1, keepdims=True)
    rest = in_grp & (lane != i1)
    v2 = jnp.max(jnp.where(rest, e_prob, -1.0), axis=-1, keepdims=True)
    i2 = jnp.min(jnp.where(rest & (e_prob == v2), lane, big), axis=-1, keepdims=True)
    tot = v1 + v2
    gates_ref[...] = jnp.where(lane == 0, g_w * v1 / tot,
                               jnp.where(lane == 1, g_w * v2 / tot, 0.0))

    e1 = i1 - N_GROUPS
    e2 = i2 - N_GROUPS
    first_choice = lane == e1
    second_choice = lane == e2 + N_EXPERTS
    chosen = jnp.where(first_choice | second_choice, 1.0, 0.0)
    t_row = lax.broadcasted_iota(jnp.int32, (ROUTER_ROWS, ROUTER_ROWS), 0)
    t_col = lax.broadcasted_iota(jnp.int32, (ROUTER_ROWS, ROUTER_ROWS), 1)
    earlier = jnp.where(t_col < t_row, 1.0, 0.0).astype(BF16)
    chosen = chosen + pltpu.roll(chosen, N_EXPERTS, axis=1)
    before = jnp.dot(earlier, chosen.astype(BF16), preferred_element_type=F32)

    @pl.when(pl.program_id(0) == 0)
    def _():
        seen[...] = jnp.zeros_like(seen)

    before = before + seen[0:1, :]
    rank1 = jnp.sum(jnp.where(first_choice, before, 0.0), axis=-1, keepdims=True)
    rank2 = jnp.sum(jnp.where(second_choice, before, 0.0), axis=-1, keepdims=True)
    seen[0:1, :] = seen[0:1, :] + jnp.sum(chosen, axis=0, keepdims=True)
    counts_ref[...] = jnp.broadcast_to(seen[0:1, :], counts_ref.shape).astype(jnp.int32)

    per_token = jnp.where(
        lane == 0, e1, jnp.where(
            lane == 1, e2, jnp.where(
                lane == 2, rank1.astype(jnp.int32), jnp.where(
                    lane == 3, rank2.astype(jnp.int32), 0))))
    ids_ref[...] = jnp.transpose(per_token)[:SUBLANES, :]


def _router(h, norm_w, w_router, b_router):
    return pl.pallas_call(
        _router_kernel,
        grid=(SEQ // ROUTER_ROWS,),
        in_specs=[pl.BlockSpec((ROUTER_ROWS, D_MODEL), lambda i: (i, 0)),
                  pl.BlockSpec((1, D_MODEL), lambda i: (0, 0)),
                  pl.BlockSpec((D_MODEL, 2 * LANES), lambda i: (0, 0)),
                  pl.BlockSpec((1, LANES), lambda i: (0, 0))],
        out_specs=[pl.BlockSpec((ROUTER_ROWS, X_ROW_TILES, LANES), lambda i: (i, 0, 0)),
                   pl.BlockSpec((SUBLANES, ROUTER_ROWS), lambda i: (0, i)),
                   pl.BlockSpec((ROUTER_ROWS, LANES), lambda i: (i, 0)),
                   pl.BlockSpec((SUBLANES, LANES), lambda i: (0, 0))],
        out_shape=[jax.ShapeDtypeStruct((SEQ, X_ROW_TILES, LANES), jnp.uint32),
                   jax.ShapeDtypeStruct((SUBLANES, SEQ), jnp.int32),
                   jax.ShapeDtypeStruct((SEQ, LANES), F32),
                   jax.ShapeDtypeStruct((SUBLANES, LANES), jnp.int32)],
        scratch_shapes=[pltpu.VMEM((SUBLANES, LANES), F32)],
        compiler_params=_params(1),
        name="ffn_norm_router",
    )(h, norm_w, w_router, b_router)


def _staged_token(stage, slot, r):
    return stage.at[slot, pl.ds(pl.multiple_of(r * GATHER_PITCH, SUBLANES), X_ROW_TILES)]


def _issue_token_rows(toks, nv, hn_hbm, stage, slot, sem):
    def start(r, carry):
        pltpu.make_async_copy(hn_hbm.at[toks[0, 0, r]], _staged_token(stage, slot, r),
                              sem.at[slot]).start()
        return carry

    def start_group(g, carry):
        for u in range(DMA_UNROLL):
            start(g * DMA_UNROLL + u, carry)
        return carry

    groups = nv // DMA_UNROLL
    lax.fori_loop(0, groups, start_group, 0)
    lax.fori_loop(groups * DMA_UNROLL, nv, start, 0)


def _wait_token_rows(nv, stage, slot, sem):
    def wait_rows(n_tokens):
        span = stage.at[slot, pl.ds(0, n_tokens * X_ROW_TILES)]
        pltpu.make_async_copy(span, span, sem.at[slot]).wait()

    def wait_many(c, carry):
        wait_rows(WAIT_ROWS)
        return carry

    def wait_one(c, carry):
        wait_rows(1)
        return carry

    lax.fori_loop(0, nv // WAIT_ROWS, wait_many, 0)
    lax.fori_loop(0, nv % WAIT_ROWS, wait_one, 0)


def _staged_block(stage, slot):
    return jnp.concatenate([stage[slot, pl.ds(s, MOE_ROWS, stride=GATHER_PITCH), :]
                            for s in range(X_ROW_TILES)], axis=-1)


def _unpack_rows(words):
    lo, hi = _unpack_words(words)
    return jnp.concatenate([lo.astype(BF16), hi.astype(BF16)], axis=-1)


def _stream_weight_tile(w, first_ref, seq_ref, n_tiles_ref, tile_copies, consume):
    n_tiles = n_tiles_ref[0]

    @pl.when(w == 0)
    def _():
        for k in range(WEIGHT_RING):
            @pl.when(k < n_tiles)
            def _():
                for c in tile_copies(k, k):
                    c.start(priority=WEIGHT_DMA_PRIORITY)

    @pl.when(first_ref[w] == 1)
    def _():
        k = seq_ref[w]
        slot = k % WEIGHT_RING
        for c in tile_copies(k, slot):
            c.wait()
        consume(slot)

        @pl.when(k + WEIGHT_RING < n_tiles)
        def _():
            for c in tile_copies(k + WEIGHT_RING, slot):
                c.start(priority=WEIGHT_DMA_PRIORITY)


def _moe_up_kernel(ib_ref, nv_ref, b_ref, ot_ref, valid_ref, first_ref, seq_ref, te_ref, tt_ref,
                   nt_ref, fetch_ref, xslot_ref, sslot_ref, pf_ref, pfb_ref, pfnv_ref,
                   tok_ref, tok_next_ref, hn_hbm, wg_hbm, wu_hbm, hid_ref,
                   stage, x_cache, ring_g, ring_u, wg_bf, wu_bf, row_sem, sem):
    w = pl.program_id(0)

    @pl.when(w == 0)
    def _():
        stage[...] = jnp.zeros_like(stage)
        _issue_token_rows(tok_ref, nv_ref[0], hn_hbm, stage, 0, row_sem)

    @pl.when(pf_ref[w] == 1)
    def _():
        _issue_token_rows(tok_next_ref, pfnv_ref[w], hn_hbm, stage, 1 - sslot_ref[w], row_sem)

    def tile_copies(k, slot):
        e = te_ref[k]
        col = pl.multiple_of(tt_ref[k] * UP_TILE, UP_TILE)
        return (pltpu.make_async_copy(wg_hbm.at[e, :, pl.ds(col, UP_TILE)], ring_g.at[slot],
                                      sem.at[0, slot]),
                pltpu.make_async_copy(wu_hbm.at[e, :, pl.ds(col, UP_TILE)], ring_u.at[slot],
                                      sem.at[1, slot]))

    def consume(slot):
        wg_bf[...] = ring_g[slot].astype(BF16)
        wu_bf[...] = ring_u[slot].astype(BF16)

    _stream_weight_tile(w, first_ref, seq_ref, nt_ref, tile_copies, consume)

    @pl.when(valid_ref[w] == 1)
    def _():
        x_slot = xslot_ref[w]

        @pl.when(fetch_ref[w] == 1)
        def _():
            slot = sslot_ref[w]
            _wait_token_rows(nv_ref[w], stage, slot, row_sem)
            x_cache[x_slot] = _unpack_rows(_staged_block(stage, slot))

        x = x_cache[x_slot]
        g = jnp.dot(x, wg_bf[...], preferred_element_type=F32)
        u = jnp.dot(x, wu_bf[...], preferred_element_type=F32)
        hid_ref[...] = (g * jax.nn.sigmoid(g) * u).astype(hid_ref.dtype)

    @pl.when(valid_ref[w] == 0)
    def _():
        hid_ref[...] = jnp.zeros_like(hid_ref)


def _moe_up(items, info, n_valid, row_tok, hn_words, w_gate, w_up):
    n_items = N_MOE_BLOCKS * (D_FF // UP_TILE)
    in_blk = items[0]
    schedule = _gather_schedule(in_blk, info, n_valid)
    n_prefetch = 1 + len(items) + len(schedule)
    toks = row_tok.reshape(N_MOE_BLOCKS, 1, MOE_ROWS)
    smem_block = lambda index_map: pl.BlockSpec((1, 1, MOE_ROWS), index_map,
                                                memory_space=pltpu.SMEM)
    any_spec = pl.BlockSpec(memory_space=pl.ANY)
    return pl.pallas_call(
        _moe_up_kernel,
        grid_spec=pltpu.PrefetchScalarGridSpec(
            num_scalar_prefetch=n_prefetch,
            grid=(n_items,),
            in_specs=[smem_block(lambda w, *refs: (refs[0][w], 0, 0)),
                      smem_block(lambda w, *refs: (refs[n_prefetch - 2][w], 0, 0)),
                      any_spec, any_spec, any_spec],
            out_specs=pl.BlockSpec((MOE_ROWS, UP_TILE),
                                   lambda w, ib, nv, b, ot, *_: (b[w], ot[w])),
            scratch_shapes=[pltpu.VMEM((2, MOE_ROWS * GATHER_PITCH, LANES), jnp.uint32),
                            pltpu.VMEM((X_CACHE_SLOTS, MOE_ROWS, D_MODEL), BF16),
                            pltpu.VMEM((WEIGHT_RING, D_MODEL, UP_TILE), F32),
                            pltpu.VMEM((WEIGHT_RING, D_MODEL, UP_TILE), F32),
                            pltpu.VMEM((D_MODEL, UP_TILE), BF16),
                            pltpu.VMEM((D_MODEL, UP_TILE), BF16),
                            pltpu.SemaphoreType.DMA((2,)),
                            pltpu.SemaphoreType.DMA((2, WEIGHT_RING))]),
        out_shape=jax.ShapeDtypeStruct((N_MOE_ROWS, D_FF), BF16),
        compiler_params=_params(1),
        name="moe_gate_up",
    )(in_blk, _lookup(n_valid, in_blk), *items[1:], *schedule, toks, toks, hn_words, w_gate, w_up)


def _moe_down_kernel(ib_ref, b_ref, ot_ref, valid_ref, first_ref, seq_ref, te_ref, tt_ref, nt_ref,
                     hid_ref, wd_hbm, y_ref, ring, wd_bf, sem):
    w = pl.program_id(0)

    def tile_copies(k, slot):
        col = pl.multiple_of(tt_ref[k] * DOWN_TILE, DOWN_TILE)
        return (pltpu.make_async_copy(wd_hbm.at[te_ref[k], :, pl.ds(col, DOWN_TILE)],
                                      ring.at[slot], sem.at[slot]),)

    def consume(slot):
        wd_bf[...] = ring[slot].astype(BF16)

    _stream_weight_tile(w, first_ref, seq_ref, nt_ref, tile_copies, consume)

    @pl.when(valid_ref[w] == 1)
    def _():
        y = jnp.dot(hid_ref[...], wd_bf[...], preferred_element_type=F32)
        words = _pack_rows(y)
        for s in range(DOWN_TILE // 2 // LANES):
            y_ref[:, s, :] = words[:, s * LANES:(s + 1) * LANES]

    @pl.when(valid_ref[w] == 0)
    def _():
        y_ref[...] = jnp.zeros_like(y_ref)


def _moe_down(items, hid, w_down):
    n_items = N_MOE_BLOCKS * (D_MODEL // DOWN_TILE)
    tile_rows = DOWN_TILE // 2 // LANES
    return pl.pallas_call(
        _moe_down_kernel,
        grid_spec=pltpu.PrefetchScalarGridSpec(
            num_scalar_prefetch=9,
            grid=(n_items,),
            in_specs=[pl.BlockSpec((MOE_ROWS, D_FF), lambda w, ib, *_: (ib[w], 0)),
                      pl.BlockSpec(memory_space=pl.ANY)],
            out_specs=pl.BlockSpec((MOE_ROWS, tile_rows, LANES),
                                   lambda w, ib, b, ot, *_: (b[w], ot[w], 0)),
            scratch_shapes=[pltpu.VMEM((WEIGHT_RING, D_FF, DOWN_TILE), F32),
                            pltpu.VMEM((D_FF, DOWN_TILE), BF16),
                            pltpu.SemaphoreType.DMA((WEIGHT_RING,))]),
        out_shape=jax.ShapeDtypeStruct((N_MOE_ROWS, Y_ROW_TILES, LANES), jnp.uint32),
        compiler_params=_params(1),
        name="moe_down",
    )(*items, hid, w_down)


def _combine_kernel(dest_ref, dest_next_ref, h_ref, gate_ref, y_hbm, o_ref, buf, sem):
    i = pl.program_id(0)

    def issue(dests, slot):
        def start(r, carry):
            for k in range(TOP_K):
                dst = buf.at[slot, k, pl.ds(r * COMBINE_PITCH, Y_ROW_TILES)]
                pltpu.make_async_copy(y_hbm.at[dests[0, k, r]], dst, sem.at[slot]).start()
            return carry
        lax.fori_loop(0, COMBINE_ROWS, start, 0, unroll=DMA_UNROLL // TOP_K)

    @pl.when(i == 0)
    def _():
        issue(dest_ref, 0)

    @pl.when(i + 1 < pl.num_programs(0))
    def _():
        issue(dest_next_ref, (i + 1) % 2)

    slot = i % 2

    for k in range(TOP_K):
        for c in range(COMBINE_ROWS // WAIT_ROWS):
            span = buf.at[slot, k, pl.ds(c * WAIT_ROWS * Y_ROW_TILES, WAIT_ROWS * Y_ROW_TILES)]
            pltpu.make_async_copy(span, span, sem.at[slot]).wait()

    g0 = gate_ref[:, 0:1]
    g1 = gate_ref[:, 1:2]
    tiles_per_half = DOWN_TILE // 2 // LANES
    for s in range(Y_ROW_TILES):
        lo_col = (s // tiles_per_half) * DOWN_TILE + (s % tiles_per_half) * LANES
        lo0, hi0 = _unpack_words(buf[slot, 0, pl.ds(s, COMBINE_ROWS, stride=COMBINE_PITCH), :])
        lo1, hi1 = _unpack_words(buf[slot, 1, pl.ds(s, COMBINE_ROWS, stride=COMBINE_PITCH), :])
        for col, y0, y1 in ((lo_col, lo0, lo1), (lo_col + DOWN_TILE // 2, hi0, hi1)):
            cols = slice(col, col + LANES)
            o_ref[:, cols] = h_ref[:, cols] + (y0 * g0 + y1 * g1)


def _combine(dest, h, gates, y):
    n_tiles = SEQ // COMBINE_ROWS
    dests = jnp.stack([d.reshape(n_tiles, COMBINE_ROWS) for d in dest], axis=1)
    blk = (1, TOP_K, COMBINE_ROWS)
    return pl.pallas_call(
        _combine_kernel,
        grid=(n_tiles,),
        in_specs=[pl.BlockSpec(blk, lambda i: (i, 0, 0), memory_space=pltpu.SMEM),
                  pl.BlockSpec(blk, lambda i: (jnp.minimum(i + 1, n_tiles - 1), 0, 0),
                               memory_space=pltpu.SMEM),
                  pl.BlockSpec((COMBINE_ROWS, D_MODEL), lambda i: (i, 0)),
                  pl.BlockSpec((COMBINE_ROWS, LANES), lambda i: (i, 0)),
                  pl.BlockSpec(memory_space=pl.ANY)],
        out_specs=pl.BlockSpec((COMBINE_ROWS, D_MODEL), lambda i: (i, 0)),
        out_shape=jax.ShapeDtypeStruct((SEQ, D_MODEL), F32),
        scratch_shapes=[pltpu.VMEM((2, TOP_K, COMBINE_ROWS * COMBINE_PITCH, LANES), jnp.uint32),
                        pltpu.SemaphoreType.DMA((2,))],
        compiler_params=_params(1),
        name="moe_combine",
    )(dests, dests, h, gates, y)


def _lookup(table, idx):
    pos = jnp.arange(table.shape[0], dtype=jnp.int32)
    return jnp.sum(jnp.where(idx[:, None] == pos[None, :], table[None, :], 0), axis=1)


def _count_le(ends, x):
    return jnp.sum((ends[None, :] <= x[:, None]).astype(jnp.int32), axis=1)


def _work_items(n_blocks_e, block_start_e, tiles):
    i32 = lambda a: a.astype(jnp.int32)
    n_items = N_MOE_BLOCKS * tiles
    per_e = n_blocks_e * tiles
    ends = jnp.cumsum(per_e)
    total = ends[-1]
    idx = jnp.arange(n_items, dtype=jnp.int32)
    valid = idx < total
    w = jnp.minimum(idx, total - 1)
    e = _count_le(ends, w)
    local = w - _lookup(ends - per_e, e)
    nb = _lookup(n_blocks_e, e)
    first_blk = _lookup(block_start_e, e)
    spare = idx - total
    w_tile = local // nb
    o_tile = jnp.where(valid, w_tile, spare % tiles)
    blk = jnp.where(valid, first_blk + local % nb, total // tiles + spare // tiles)
    first = valid & (local % nb == 0)
    active = n_blocks_e > 0
    active_rank = jnp.cumsum(active) - active
    seq = _lookup(active_rank, e) * tiles + w_tile
    k = jnp.arange(N_EXPERTS * tiles, dtype=jnp.int32)
    experts = jnp.arange(N_EXPERTS, dtype=jnp.int32)
    is_kth = active[None, :] & (active_rank[None, :] == (k // tiles)[:, None])
    tile_e = jnp.sum(jnp.where(is_kth, experts[None, :], 0), axis=1)
    tile_t = k % tiles
    n_tiles = (jnp.sum(active) * tiles).reshape(1)
    in_blk = first_blk + local % nb
    items = (i32(in_blk), i32(blk), i32(o_tile), i32(valid), i32(first), i32(seq), i32(tile_e),
             i32(tile_t), i32(n_tiles))
    return items, dict(blocks_of_expert=nb, local_block=local % nb, weight_tile=w_tile, valid=valid)


def _gather_schedule(in_blk, info, n_valid):
    i32 = lambda a: a.astype(jnp.int32)
    n = in_blk.shape[0]
    cached = (info['blocks_of_expert'] <= X_CACHE_SLOTS) & (info['weight_tile'] > 0)
    fetch = info['valid'] & ~cached
    x_slot = info['local_block'] % X_CACHE_SLOTS
    stage_slot = (jnp.cumsum(fetch) - 1) % 2
    idx = jnp.arange(n, dtype=jnp.int32)
    later = lax.cummin(jnp.where(fetch, idx, n), axis=0, reverse=True)
    next_fetch = jnp.concatenate([later[1:], jnp.full((1,), n, later.dtype)])
    prefetch = fetch & (next_fetch < n)
    pf_blk = jnp.where(prefetch, _lookup(in_blk, jnp.minimum(next_fetch, n - 1)), in_blk)
    return (i32(fetch), i32(x_slot), i32(stage_slot), i32(prefetch), i32(pf_blk),
            i32(_lookup(n_valid, pf_blk)))


def _dispatch_plan(ids, counts):
    counts = counts[0, :N_EXPERTS]
    n_blocks_e = (counts + MOE_ROWS - 1) // MOE_ROWS
    block_ends = jnp.cumsum(n_blocks_e)
    block_start_e = block_ends - n_blocks_e
    first_row_e = block_start_e * MOE_ROWS
    dest = [(_lookup(first_row_e, ids[k]) + ids[TOP_K + k]).astype(jnp.int32)
            for k in range(TOP_K)]
    token = jnp.arange(SEQ, dtype=jnp.int32)
    row_tok = jnp.zeros((N_MOE_ROWS,), jnp.int32).at[jnp.concatenate(dest)].set(
        jnp.concatenate([token] * TOP_K), unique_indices=True, mode='promise_in_bounds')
    blk = jnp.arange(N_MOE_BLOCKS, dtype=jnp.int32)
    blk_e = jnp.minimum(_count_le(block_ends, blk), N_EXPERTS - 1)
    n_valid = jnp.clip(_lookup(counts, blk_e) - (blk - _lookup(block_start_e, blk_e)) * MOE_ROWS,
                       0, MOE_ROWS)
    n_valid = jnp.where(blk < block_ends[-1], n_valid, 0).astype(jnp.int32)
    return dest, row_tok, n_valid, n_blocks_e.astype(jnp.int32), block_start_e.astype(jnp.int32)


def kernel(x, norm_mix_w, w_in, q_norm_w, k_norm_w, conv_w, conv_b, conv_ln_w, conv_ln_b, w_out,
           norm_ffn_w, w_group_router, b_group_router, w_expert_router, b_expert_router,
           w_gate, w_up, w_down):
    h = x.reshape(SEQ, D_MODEL)
    slopes = jnp.exp2(-8.0 * jnp.arange(1, N_HEADS + 1, dtype=F32) / N_HEADS)
    for l in range(norm_mix_w.shape[0]):
        hn = _rmsnorm(h, norm_mix_w[l][None, :])
        qk_norm = jnp.concatenate([jnp.tile(q_norm_w[l], N_HEADS), jnp.tile(k_norm_w[l], N_HEADS)])
        z = _inproj(hn, w_in[l], qk_norm[None, :])
        attn = _attention(z, slopes)
        n_ct = CONV_WIDTH // LANES
        conv = _conformer_conv(z, conv_w[l].reshape(CONV_KERNEL, n_ct, LANES),
                               conv_b[l].reshape(n_ct, LANES), conv_ln_w[l][None, :],
                               conv_ln_b[l][None, :])
        h = _outproj(attn, conv, w_out[l], h)

        pad = LANES - N_GROUPS - N_EXPERTS
        w_router = jnp.concatenate([w_group_router[l], w_expert_router[l],
                                    jnp.zeros((D_MODEL, pad), F32)], axis=1)
        b_router = jnp.concatenate([b_group_router[l], b_expert_router[l], jnp.zeros((pad,), F32)])
        w_router_hi = w_router.astype(BF16)
        w_router_lo = (w_router - w_router_hi.astype(F32)).astype(BF16)
        hn_words, ids, gates, counts = _router(h, norm_ffn_w[l][None, :],
                                               jnp.concatenate([w_router_hi, w_router_lo], axis=1),
                                               b_router[None, :])
        dest, row_tok, n_valid, n_blocks_e, block_start_e = _dispatch_plan(ids, counts)
        up_items, up_info = _work_items(n_blocks_e, block_start_e, D_FF // UP_TILE)
        hid = _moe_up(up_items, up_info, n_valid, row_tok, hn_words, w_gate[l], w_up[l])
        down_items, _ = _work_items(n_blocks_e, block_start_e, D_MODEL // DOWN_TILE)
        y = _moe_down(down_items, hid, w_down[l])
        h = _combine(dest, h, gates, y)
    return h.reshape(x.shape)
```

```python
import math

import jax
import jax.numpy as jnp
from jax import lax
from jax.experimental import pallas as pl
from jax.experimental.pallas import tpu as pltpu

F32 = jnp.float32
BF16 = jnp.bfloat16

D_MODEL = 4096
SEQ = 8192
HEAD_DIM = 128
N_HEADS = 16
ATTN_WIDTH = N_HEADS * HEAD_DIM
CONV_WIDTH = D_MODEL - ATTN_WIDTH
CONV_KERNEL = 31
IN_COLS = 3 * ATTN_WIDTH + 2 * CONV_WIDTH
DILATIONS = (1, 4, 16)
ATTN_BLOCK = 128
N_KEYS = 128
N_GROUPS = 8
EXPERTS_PER_GROUP = 8
N_EXPERTS = N_GROUPS * EXPERTS_PER_GROUP
TOP_K = 2
D_FF = 1024
EPS = 1e-6
NEG_INF = -1e30

LANES = 128
SUBLANES = 8
VMEM_LIMIT = 56 * 1024 * 1024

SPAN = DILATIONS[-1] * ATTN_BLOCK

MOE_ROWS = 128
N_ASSIGN = SEQ * TOP_K
N_MOE_ROWS = N_ASSIGN + N_EXPERTS * MOE_ROWS
N_MOE_BLOCKS = N_MOE_ROWS // MOE_ROWS
UP_TILE = 512
DOWN_TILE = 4096
WEIGHT_RING = 2
WEIGHT_DMA_PRIORITY = 1
ROUTER_ROWS = 512
COMBINE_ROWS = 128
X_CACHE_SLOTS = 4
X_ROW_TILES = D_MODEL // 2 // LANES
Y_ROW_TILES = D_MODEL // 2 // LANES
COMBINE_PITCH = Y_ROW_TILES + 4
GATHER_PITCH = X_ROW_TILES + 8
DMA_UNROLL = 16
WAIT_ROWS = 32


def _params(n_axes):
    return pltpu.CompilerParams(dimension_semantics=("arbitrary",) * n_axes,
                                vmem_limit_bytes=VMEM_LIMIT)


def _pack_rows(x):
    half = x.shape[1] // 2
    lo = pltpu.bitcast(x[:, :half].astype(BF16).astype(F32), jnp.uint32)
    hi = pltpu.bitcast(x[:, half:].astype(BF16).astype(F32), jnp.uint32)
    return (hi & jnp.uint32(0xFFFF0000)) | (lo >> 16)


def _unpack_words(words):
    return (pltpu.bitcast(words << 16, F32),
            pltpu.bitcast(words & jnp.uint32(0xFFFF0000), F32))


def _rmsnorm_kernel(x_ref, w_ref, o_ref):
    x = x_ref[...]
    ms = jnp.mean(x * x, axis=-1, keepdims=True)
    o_ref[...] = (x * lax.rsqrt(ms + EPS) * w_ref[...]).astype(o_ref.dtype)


def _rmsnorm(x, w, rows=512):
    n, d = x.shape
    return pl.pallas_call(
        _rmsnorm_kernel,
        grid=(n // rows,),
        in_specs=[pl.BlockSpec((rows, d), lambda i: (i, 0)),
                  pl.BlockSpec((1, d), lambda i: (0, 0))],
        out_specs=pl.BlockSpec((rows, d), lambda i: (i, 0)),
        out_shape=jax.ShapeDtypeStruct((n, d), BF16),
        compiler_params=_params(1),
        name="rmsnorm",
    )(x, w)


IN_TM = 1024
IN_TN = 512
IN_CHUNKS = (256, 256, 256, 128, 128)
QK_TILES = 2 * ATTN_WIDTH // IN_TN


def _inproj_kernel(hn_ref, w_ref, nw_ref, o_ref, w_bf):
    j = pl.program_id(0)

    @pl.when(pl.program_id(1) == 0)
    def _():
        w_bf[...] = w_ref[...].astype(BF16)

    is_qk = j < QK_TILES
    row0 = 0
    for chunk in IN_CHUNKS:
        rows = slice(row0, row0 + chunk)
        row0 += chunk
        acc = jnp.dot(hn_ref[rows, :], w_bf[...], preferred_element_type=F32)
        for hd in range(IN_TN // HEAD_DIM):
            sl = slice(hd * HEAD_DIM, (hd + 1) * HEAD_DIM)
            t = acc[:, sl]
            ms = jnp.mean(t * t, axis=-1, keepdims=True)
            normed = t * lax.rsqrt(ms + EPS) * nw_ref[:, sl]
            o_ref[rows, sl] = jnp.where(is_qk, normed, t).astype(o_ref.dtype)


def _inproj(hn, w, qk_norm_w):
    n = hn.shape[0]
    return pl.pallas_call(
        _inproj_kernel,
        grid=(IN_COLS // IN_TN, n // IN_TM),
        in_specs=[pl.BlockSpec((IN_TM, D_MODEL), lambda j, i: (i, 0)),
                  pl.BlockSpec((D_MODEL, IN_TN), lambda j, i: (0, j)),
                  pl.BlockSpec((1, IN_TN), lambda j, i: (0, jnp.minimum(j, QK_TILES - 1)))],
        out_specs=pl.BlockSpec((IN_TM, IN_TN), lambda j, i: (i, j)),
        out_shape=jax.ShapeDtypeStruct((n, IN_COLS), BF16),
        scratch_shapes=[pltpu.VMEM((D_MODEL, IN_TN), BF16)],
        compiler_params=_params(2),
        name="inproj",
    )(hn, w, qk_norm_w)


def _attn_kernel(slopes_ref, q_ref, kc_ref, kp_ref, vc_ref, vp_ref, o_ref, qf, kf, vf, ob, lb):
    h = pl.program_id(0)
    first_span = pl.program_id(1) == 0
    slope = slopes_ref[h]

    qi = lax.broadcasted_iota(jnp.int32, (ATTN_BLOCK, 2 * ATTN_BLOCK), 0)
    kj = lax.broadcasted_iota(jnp.int32, (ATTN_BLOCK, 2 * ATTN_BLOCK), 1)
    steps = qi + ATTN_BLOCK - kj
    valid = (steps >= 0) & (steps <= N_KEYS)
    stepsf = steps.astype(F32)
    scale = 1.0 / math.sqrt(HEAD_DIM)

    def attend(b, d, bias, q0, q, k, v, no_prev_block):
        s = lax.dot_general(q, k, (((1,), (1,)), ((), ())),
                            preferred_element_type=F32) * scale + bias
        if no_prev_block is not None:
            s = jnp.where(kj >= jnp.where(no_prev_block, ATTN_BLOCK, 0), s, NEG_INF)
        m = jnp.max(s, axis=-1, keepdims=True)
        p = jnp.exp(s - m)
        den = jnp.sum(p, axis=-1, keepdims=True)
        o = jnp.dot(p.astype(BF16), v, preferred_element_type=F32) / den
        lse = m + jnp.log(den)
        ob[b, pl.ds(q0, ATTN_BLOCK, stride=d), :] = o
        lb[b, pl.ds(q0, ATTN_BLOCK, stride=d), :] = jnp.broadcast_to(lse, (ATTN_BLOCK, HEAD_DIM))

    def banded_bias(d):
        return jnp.where(valid, (-slope * float(d)) * stepsf, NEG_INF)

    def key_window(cur, prev, n):
        if n > 0:
            return cur[(n - 1) * ATTN_BLOCK:(n + 1) * ATTN_BLOCK, :]
        last = prev.shape[0] - ATTN_BLOCK
        return jnp.concatenate([prev[last:, :], cur[:ATTN_BLOCK, :]], axis=0)

    b, d = 0, DILATIONS[0]
    bias = banded_bias(d)
    for n in range(SPAN // ATTN_BLOCK):
        attend(b, d, bias, n * ATTN_BLOCK, q_ref[n * ATTN_BLOCK:(n + 1) * ATTN_BLOCK, :],
               key_window(kc_ref, kp_ref, n), key_window(vc_ref, vp_ref, n),
               first_span if n == 0 else None)

    qf[...] = q_ref[...].astype(F32)
    kf[0:SPAN, :] = kp_ref[...].astype(F32)
    kf[SPAN:2 * SPAN, :] = kc_ref[...].astype(F32)
    vf[0:SPAN, :] = vp_ref[...].astype(F32)
    vf[SPAN:2 * SPAN, :] = vc_ref[...].astype(F32)
    for b, d in list(enumerate(DILATIONS))[1:]:
        bias = banded_bias(d)
        for r in range(d):
            for n in range(SPAN // (ATTN_BLOCK * d)):
                q0 = r + d * ATTN_BLOCK * n
                k0 = SPAN + q0 - d * ATTN_BLOCK
                attend(b, d, bias, q0,
                       qf[pl.ds(q0, ATTN_BLOCK, stride=d), :].astype(BF16),
                       kf[pl.ds(k0, 2 * ATTN_BLOCK, stride=d), :].astype(BF16),
                       vf[pl.ds(k0, 2 * ATTN_BLOCK, stride=d), :].astype(BF16),
                       first_span if n == 0 else None)

    chunk = 256
    for c in range(SPAN // chunk):
        rows = slice(c * chunk, (c + 1) * chunk)
        l0, l1, l2 = lb[0, rows, :], lb[1, rows, :], lb[2, rows, :]
        mx = jnp.maximum(jnp.maximum(l0, l1), l2)
        w0, w1, w2 = jnp.exp(l0 - mx), jnp.exp(l1 - mx), jnp.exp(l2 - mx)
        num = w0 * ob[0, rows, :] + w1 * ob[1, rows, :] + w2 * ob[2, rows, :]
        o_ref[rows, :] = (num / (w0 + w1 + w2)).astype(o_ref.dtype)


def _attention(z, slopes):
    n_spans = SEQ // SPAN
    k_col, v_col = N_HEADS, 2 * N_HEADS
    blk = (SPAN, HEAD_DIM)
    prev = lambda s: jnp.maximum(s - 1, 0)
    return pl.pallas_call(
        _attn_kernel,
        grid_spec=pltpu.PrefetchScalarGridSpec(
            num_scalar_prefetch=1,
            grid=(N_HEADS, n_spans),
            in_specs=[pl.BlockSpec(blk, lambda h, s, sl: (s, h)),
                      pl.BlockSpec(blk, lambda h, s, sl: (s, k_col + h)),
                      pl.BlockSpec(blk, lambda h, s, sl: (prev(s), k_col + h)),
                      pl.BlockSpec(blk, lambda h, s, sl: (s, v_col + h)),
                      pl.BlockSpec(blk, lambda h, s, sl: (prev(s), v_col + h))],
            out_specs=pl.BlockSpec(blk, lambda h, s, sl: (s, h)),
            scratch_shapes=[pltpu.VMEM((SPAN, HEAD_DIM), F32),
                            pltpu.VMEM((2 * SPAN, HEAD_DIM), F32),
                            pltpu.VMEM((2 * SPAN, HEAD_DIM), F32),
                            pltpu.VMEM((3, SPAN, HEAD_DIM), F32),
                            pltpu.VMEM((3, SPAN, HEAD_DIM), F32)]),
        out_shape=jax.ShapeDtypeStruct((SEQ, ATTN_WIDTH), BF16),
        compiler_params=_params(2),
        name="dilated_attention",
    )(slopes, z, z, z, z, z)


CONV_ROWS = 256
CONV_HALO = 32
CONV_CHUNK = 64

CONV_EXT = CONV_HALO + CONV_ROWS
CONV_SHIFTED = CONV_EXT - SUBLANES


def _conv_kernel(a_ref, g_ref, ah_ref, gh_ref, cw_ref, cb_ref, lw_ref, lb_ref, o_ref,
                 u_ext, shifted, c_scr):
    i = pl.program_id(0)
    n_ct = CONV_WIDTH // LANES
    for ct in range(n_ct):
        cols = slice(ct * LANES, (ct + 1) * LANES)
        u_ext[ct, CONV_HALO:, :] = (a_ref[:, cols].astype(F32)
                                    * jax.nn.sigmoid(g_ref[:, cols].astype(F32)))
        uh = ah_ref[:, cols].astype(F32) * jax.nn.sigmoid(gh_ref[:, cols].astype(F32))
        u_ext[ct, 0:CONV_HALO, :] = jnp.where(i == 0, 0.0, uh)

    first_tap = CONV_HALO - (CONV_KERNEL - 1)

    def slab(ct, carry):
        for k in range(1, SUBLANES):
            shifted[k - 1] = u_ext[ct, k:k + CONV_SHIFTED, :]
        for rc in range(CONV_ROWS // CONV_CHUNK):
            base = rc * CONV_CHUNK
            acc = jnp.broadcast_to(cb_ref[pl.ds(ct, 1), :], (CONV_CHUNK, LANES))
            for j in range(CONV_KERNEL):
                k = (first_tap + j) % SUBLANES
                row = base + (first_tap + j) - k
                if k == 0:
                    src = u_ext[ct, row:row + CONV_CHUNK, :]
                else:
                    src = shifted[k - 1, row:row + CONV_CHUNK, :]
                acc = acc + cw_ref[j, pl.ds(ct, 1), :] * src
            c_scr[ct, base:base + CONV_CHUNK, :] = acc
        return carry

    lax.fori_loop(0, n_ct, slab, 0)

    total = c_scr[0]
    for ct in range(1, n_ct):
        total = total + c_scr[ct]
    mu = jnp.sum(total, axis=-1, keepdims=True) * (1.0 / CONV_WIDTH)
    sq = jnp.square(c_scr[0] - mu)
    for ct in range(1, n_ct):
        sq = sq + jnp.square(c_scr[ct] - mu)
    var = jnp.sum(sq, axis=-1, keepdims=True) * (1.0 / CONV_WIDTH)
    rstd = lax.rsqrt(var + EPS)
    for ct in range(n_ct):
        cols = slice(ct * LANES, (ct + 1) * LANES)
        y = (c_scr[ct] - mu) * rstd * lw_ref[:, cols] + lb_ref[:, cols]
        o_ref[:, cols] = (y * jax.nn.sigmoid(y)).astype(o_ref.dtype)


def _conformer_conv(z, conv_w, conv_b, ln_w, ln_b):
    a_col = 3 * ATTN_WIDTH // CONV_WIDTH
    g_col = a_col + 1
    n_ct = CONV_WIDTH // LANES
    halo_blocks = CONV_ROWS // CONV_HALO
    halo = lambda i: jnp.maximum(i * halo_blocks - 1, 0)
    vec = pl.BlockSpec((1, CONV_WIDTH), lambda i: (0, 0))
    return pl.pallas_call(
        _conv_kernel,
        grid=(SEQ // CONV_ROWS,),
        in_specs=[pl.BlockSpec((CONV_ROWS, CONV_WIDTH), lambda i: (i, a_col)),
                  pl.BlockSpec((CONV_ROWS, CONV_WIDTH), lambda i: (i, g_col)),
                  pl.BlockSpec((CONV_HALO, CONV_WIDTH), lambda i: (halo(i), a_col)),
                  pl.BlockSpec((CONV_HALO, CONV_WIDTH), lambda i: (halo(i), g_col)),
                  pl.BlockSpec((CONV_KERNEL, n_ct, LANES), lambda i: (0, 0, 0)),
                  pl.BlockSpec((n_ct, LANES), lambda i: (0, 0)),
                  vec, vec],
        out_specs=pl.BlockSpec((CONV_ROWS, CONV_WIDTH), lambda i: (i, 0)),
        out_shape=jax.ShapeDtypeStruct((SEQ, CONV_WIDTH), BF16),
        scratch_shapes=[pltpu.VMEM((n_ct, CONV_EXT, LANES), F32),
                        pltpu.VMEM((SUBLANES - 1, CONV_SHIFTED, LANES), F32),
                        pltpu.VMEM((n_ct, CONV_ROWS, LANES), F32)],
        compiler_params=_params(1),
        name="conformer_conv",
    )(z, z, z, z, conv_w, conv_b, ln_w, ln_b)


OUT_TM = 1024
OUT_TN = 512


def _outproj_kernel(attn_ref, conv_ref, wa_ref, wc_ref, x_ref, o_ref, wa_bf, wc_bf):
    @pl.when(pl.program_id(1) == 0)
    def _():
        wa_bf[...] = wa_ref[...].astype(BF16)
        wc_bf[...] = wc_ref[...].astype(BF16)

    acc = jnp.dot(attn_ref[...], wa_bf[...], preferred_element_type=F32)
    acc = acc + jnp.dot(conv_ref[...], wc_bf[...], preferred_element_type=F32)
    o_ref[...] = x_ref[...] + acc


def _outproj(attn, conv, w, x):
    n = x.shape[0]
    conv_blk = ATTN_WIDTH // CONV_WIDTH
    return pl.pallas_call(
        _outproj_kernel,
        grid=(D_MODEL // OUT_TN, n // OUT_TM),
        in_specs=[pl.BlockSpec((OUT_TM, ATTN_WIDTH), lambda j, i: (i, 0)),
                  pl.BlockSpec((OUT_TM, CONV_WIDTH), lambda j, i: (i, 0)),
                  pl.BlockSpec((ATTN_WIDTH, OUT_TN), lambda j, i: (0, j)),
                  pl.BlockSpec((CONV_WIDTH, OUT_TN), lambda j, i: (conv_blk, j)),
                  pl.BlockSpec((OUT_TM, OUT_TN), lambda j, i: (i, j))],
        out_specs=pl.BlockSpec((OUT_TM, OUT_TN), lambda j, i: (i, j)),
        out_shape=jax.ShapeDtypeStruct((n, D_MODEL), F32),
        scratch_shapes=[pltpu.VMEM((ATTN_WIDTH, OUT_TN), BF16),
                        pltpu.VMEM((CONV_WIDTH, OUT_TN), BF16)],
        compiler_params=_params(2),
        name="outproj",
    )(attn, conv, w, w, x)


def _router_kernel(h_ref, nw_ref, wr_ref, br_ref, hn_ref, ids_ref, gates_ref, counts_ref, seen):
    x = h_ref[...]
    ms = jnp.mean(x * x, axis=-1, keepdims=True)
    hn = x * lax.rsqrt(ms + EPS) * nw_ref[...]
    words = _pack_rows(hn)
    for s in range(X_ROW_TILES):
        hn_ref[:, s, :] = words[:, s * LANES:(s + 1) * LANES]

    h_hi = hn.astype(BF16)
    h_lo = (hn - h_hi.astype(F32)).astype(BF16)
    hi_both = jnp.dot(h_hi, wr_ref[...], preferred_element_type=F32)
    lo_hi = jnp.dot(h_lo, wr_ref[:, :LANES], preferred_element_type=F32)
    logits = hi_both[:, :LANES] + (hi_both[:, LANES:] + lo_hi) + br_ref[...]
    lane = lax.broadcasted_iota(jnp.int32, logits.shape, 1)
    big = jnp.int32(1 << 20)

    is_g = lane < N_GROUPS
    gl = jnp.where(is_g, logits, -jnp.inf)
    ge = jnp.exp(gl - jnp.max(gl, axis=-1, keepdims=True))
    g_prob = ge / jnp.sum(ge, axis=-1, keepdims=True)
    g_w = jnp.max(g_prob, axis=-1, keepdims=True)
    g_top = jnp.min(jnp.where(is_g & (g_prob == g_w), lane, big), axis=-1, keepdims=True)

    eidx = lane - N_GROUPS
    grp_shift = EXPERTS_PER_GROUP.bit_length() - 1
    in_grp = (eidx >= 0) & (eidx < N_EXPERTS) & ((eidx >> grp_shift) == g_top)
    el = jnp.where(in_grp, logits, -jnp.inf)
    ee = jnp.exp(el - jnp.max(el, axis=-1, keepdims=True))
    e_prob = ee / jnp.sum(ee, axis=-1, keepdims=True)
    v1 = jnp.max(jnp.where(in_grp, e_prob, -1.0), axis=-1, keepdims=True)
    i1 = jnp.min(jnp.where(in_grp & (e_prob == v1), lane, big), axis=-1, keepdims=True)
    rest = in_grp & (lane != i1)
    v2 = jnp.max(jnp.where(rest, e_prob, -1.0), axis=-1, keepdims=True)
    i2 = jnp.min(jnp.where(rest & (e_prob == v2), lane, big), axis=-1, keepdims=True)
    tot = v1 + v2
    gates_ref[...] = jnp.where(lane == 0, g_w * v1 / tot,
                               jnp.where(lane == 1, g_w * v2 / tot, 0.0))

    e1 = i1 - N_GROUPS
    e2 = i2 - N_GROUPS
    first_choice = lane == e1
    second_choice = lane == e2 + N_EXPERTS
    chosen = jnp.where(first_choice | second_choice, 1.0, 0.0)
    t_row = lax.broadcasted_iota(jnp.int32, (ROUTER_ROWS, ROUTER_ROWS), 0)
    t_col = lax.broadcasted_iota(jnp.int32, (ROUTER_ROWS, ROUTER_ROWS), 1)
    earlier = jnp.where(t_col < t_row, 1.0, 0.0).astype(BF16)
    chosen = chosen + pltpu.roll(chosen, N_EXPERTS, axis=1)
    before = jnp.dot(earlier, chosen.astype(BF16), preferred_element_type=F32)

    @pl.when(pl.program_id(0) == 0)
    def _():
        seen[...] = jnp.zeros_like(seen)

    before = before + seen[0:1, :]
    rank1 = jnp.sum(jnp.where(first_choice, before, 0.0), axis=-1, keepdims=True)
    rank2 = jnp.sum(jnp.where(second_choice, before, 0.0), axis=-1, keepdims=True)
    seen[0:1, :] = seen[0:1, :] + jnp.sum(chosen, axis=0, keepdims=True)
    counts_ref[...] = jnp.broadcast_to(seen[0:1, :], counts_ref.shape).astype(jnp.int32)

    per_token = jnp.where(
        lane == 0, e1, jnp.where(
            lane == 1, e2, jnp.where(
                lane == 2, rank1.astype(jnp.int32), jnp.where(
                    lane == 3, rank2.astype(jnp.int32), 0))))
    ids_ref[...] = jnp.transpose(per_token)[:SUBLANES, :]


def _router(h, norm_w, w_router, b_router):
    return pl.pallas_call(
        _router_kernel,
        grid=(SEQ // ROUTER_ROWS,),
        in_specs=[pl.BlockSpec((ROUTER_ROWS, D_MODEL), lambda i: (i, 0)),
                  pl.BlockSpec((1, D_MODEL), lambda i: (0, 0)),
                  pl.BlockSpec((D_MODEL, 2 * LANES), lambda i: (0, 0)),
                  pl.BlockSpec((1, LANES), lambda i: (0, 0))],
        out_specs=[pl.BlockSpec((ROUTER_ROWS, X_ROW_TILES, LANES), lambda i: (i, 0, 0)),
                   pl.BlockSpec((SUBLANES, ROUTER_ROWS), lambda i: (0, i)),
                   pl.BlockSpec((ROUTER_ROWS, LANES), lambda i: (i, 0)),
                   pl.BlockSpec((SUBLANES, LANES), lambda i: (0, 0))],
        out_shape=[jax.ShapeDtypeStruct((SEQ, X_ROW_TILES, LANES), jnp.uint32),
                   jax.ShapeDtypeStruct((SUBLANES, SEQ), jnp.int32),
                   jax.ShapeDtypeStruct((SEQ, LANES), F32),
                   jax.ShapeDtypeStruct((SUBLANES, LANES), jnp.int32)],
        scratch_shapes=[pltpu.VMEM((SUBLANES, LANES), F32)],
        compiler_params=_params(1),
        name="ffn_norm_router",
    )(h, norm_w, w_router, b_router)


def _staged_token(stage, slot, r):
    return stage.at[slot, pl.ds(pl.multiple_of(r * GATHER_PITCH, SUBLANES), X_ROW_TILES)]


def _issue_token_rows(toks, nv, hn_hbm, stage, slot, sem):
    def start(r, carry):
        pltpu.make_async_copy(hn_hbm.at[toks[0, 0, r]], _staged_token(stage, slot, r),
                              sem.at[slot]).start()
        return carry

    def start_group(g, carry):
        for u in range(DMA_UNROLL):
            start(g * DMA_UNROLL + u, carry)
        return carry

    groups = nv // DMA_UNROLL
    lax.fori_loop(0, groups, start_group, 0)
    lax.fori_loop(groups * DMA_UNROLL, nv, start, 0)


def _wait_token_rows(nv, stage, slot, sem):
    def wait_rows(n_tokens):
        span = stage.at[slot, pl.ds(0, n_tokens * X_ROW_TILES)]
        pltpu.make_async_copy(span, span, sem.at[slot]).wait()

    def wait_many(c, carry):
        wait_rows(WAIT_ROWS)
        return carry

    def wait_one(c, carry):
        wait_rows(1)
        return carry

    lax.fori_loop(0, nv // WAIT_ROWS, wait_many, 0)
    lax.fori_loop(0, nv % WAIT_ROWS, wait_one, 0)


def _staged_block(stage, slot):
    return jnp.concatenate([stage[slot, pl.ds(s, MOE_ROWS, stride=GATHER_PITCH), :]
                            for s in range(X_ROW_TILES)], axis=-1)


def _unpack_rows(words):
    lo, hi = _unpack_words(words)
    return jnp.concatenate([lo.astype(BF16), hi.astype(BF16)], axis=-1)


def _stream_weight_tile(w, first_ref, seq_ref, n_tiles_ref, tile_copies, consume):
    n_tiles = n_tiles_ref[0]

    @pl.when(w == 0)
    def _():
        for k in range(WEIGHT_RING):
            @pl.when(k < n_tiles)
            def _():
                for c in tile_copies(k, k):
                    c.start(priority=WEIGHT_DMA_PRIORITY)

    @pl.when(first_ref[w] == 1)
    def _():
        k = seq_ref[w]
        slot = k % WEIGHT_RING
        for c in tile_copies(k, slot):
            c.wait()
        consume(slot)

        @pl.when(k + WEIGHT_RING < n_tiles)
        def _():
            for c in tile_copies(k + WEIGHT_RING, slot):
                c.start(priority=WEIGHT_DMA_PRIORITY)


def _moe_up_kernel(ib_ref, nv_ref, b_ref, ot_ref, valid_ref, first_ref, seq_ref, te_ref, tt_ref,
                   nt_ref, fetch_ref, xslot_ref, sslot_ref, pf_ref, pfb_ref, pfnv_ref,
                   tok_ref, tok_next_ref, hn_hbm, wg_hbm, wu_hbm, hid_ref,
                   stage, x_cache, ring_g, ring_u, wg_bf, wu_bf, row_sem, sem):
    w = pl.program_id(0)

    @pl.when(w == 0)
    def _():
        stage[...] = jnp.zeros_like(stage)
        _issue_token_rows(tok_ref, nv_ref[0], hn_hbm, stage, 0, row_sem)

    @pl.when(pf_ref[w] == 1)
    def _():
        _issue_token_rows(tok_next_ref, pfnv_ref[w], hn_hbm, stage, 1 - sslot_ref[w], row_sem)

    def tile_copies(k, slot):
        e = te_ref[k]
        col = pl.multiple_of(tt_ref[k] * UP_TILE, UP_TILE)
        return (pltpu.make_async_copy(wg_hbm.at[e, :, pl.ds(col, UP_TILE)], ring_g.at[slot],
                                      sem.at[0, slot]),
                pltpu.make_async_copy(wu_hbm.at[e, :, pl.ds(col, UP_TILE)], ring_u.at[slot],
                                      sem.at[1, slot]))

    def consume(slot):
        wg_bf[...] = ring_g[slot].astype(BF16)
        wu_bf[...] = ring_u[slot].astype(BF16)

    _stream_weight_tile(w, first_ref, seq_ref, nt_ref, tile_copies, consume)

    @pl.when(valid_ref[w] == 1)
    def _():
        x_slot = xslot_ref[w]

        @pl.when(fetch_ref[w] == 1)
        def _():
            slot = sslot_ref[w]
            _wait_token_rows(nv_ref[w], stage, slot, row_sem)
            x_cache[x_slot] = _unpack_rows(_staged_block(stage, slot))

        x = x_cache[x_slot]
        g = jnp.dot(x, wg_bf[...], preferred_element_type=F32)
        u = jnp.dot(x, wu_bf[...], preferred_element_type=F32)
        hid_ref[...] = (g * jax.nn.sigmoid(g) * u).astype(hid_ref.dtype)

    @pl.when(valid_ref[w] == 0)
    def _():
        hid_ref[...] = jnp.zeros_like(hid_ref)


def _moe_up(items, info, n_valid, row_tok, hn_words, w_gate, w_up):
    n_items = N_MOE_BLOCKS * (D_FF // UP_TILE)
    in_blk = items[0]
    schedule = _gather_schedule(in_blk, info, n_valid)
    n_prefetch = 1 + len(items) + len(schedule)
    toks = row_tok.reshape(N_MOE_BLOCKS, 1, MOE_ROWS)
    smem_block = lambda index_map: pl.BlockSpec((1, 1, MOE_ROWS), index_map,
                                                memory_space=pltpu.SMEM)
    any_spec = pl.BlockSpec(memory_space=pl.ANY)
    return pl.pallas_call(
        _moe_up_kernel,
        grid_spec=pltpu.PrefetchScalarGridSpec(
            num_scalar_prefetch=n_prefetch,
            grid=(n_items,),
            in_specs=[smem_block(lambda w, *refs: (refs[0][w], 0, 0)),
                      smem_block(lambda w, *refs: (refs[n_prefetch - 2][w], 0, 0)),
                      any_spec, any_spec, any_spec],
            out_specs=pl.BlockSpec((MOE_ROWS, UP_TILE),
                                   lambda w, ib, nv, b, ot, *_: (b[w], ot[w])),
            scratch_shapes=[pltpu.VMEM((2, MOE_ROWS * GATHER_PITCH, LANES), jnp.uint32),
                            pltpu.VMEM((X_CACHE_SLOTS, MOE_ROWS, D_MODEL), BF16),
                            pltpu.VMEM((WEIGHT_RING, D_MODEL, UP_TILE), F32),
                            pltpu.VMEM((WEIGHT_RING, D_MODEL, UP_TILE), F32),
                            pltpu.VMEM((D_MODEL, UP_TILE), BF16),
                            pltpu.VMEM((D_MODEL, UP_TILE), BF16),
                            pltpu.SemaphoreType.DMA((2,)),
                            pltpu.SemaphoreType.DMA((2, WEIGHT_RING))]),
        out_shape=jax.ShapeDtypeStruct((N_MOE_ROWS, D_FF), BF16),
        compiler_params=_params(1),
        name="moe_gate_up",
    )(in_blk, _lookup(n_valid, in_blk), *items[1:], *schedule, toks, toks, hn_words, w_gate, w_up)


def _moe_down_kernel(ib_ref, b_ref, ot_ref, valid_ref, first_ref, seq_ref, te_ref, tt_ref, nt_ref,
                     hid_ref, wd_hbm, y_ref, ring, wd_bf, sem):
    w = pl.program_id(0)

    def tile_copies(k, slot):
        col = pl.multiple_of(tt_ref[k] * DOWN_TILE, DOWN_TILE)
        return (pltpu.make_async_copy(wd_hbm.at[te_ref[k], :, pl.ds(col, DOWN_TILE)],
                                      ring.at[slot], sem.at[slot]),)

    def consume(slot):
        wd_bf[...] = ring[slot].astype(BF16)

    _stream_weight_tile(w, first_ref, seq_ref, nt_ref, tile_copies, consume)

    @pl.when(valid_ref[w] == 1)
    def _():
        y = jnp.dot(hid_ref[...], wd_bf[...], preferred_element_type=F32)
        words = _pack_rows(y)
        for s in range(DOWN_TILE // 2 // LANES):
            y_ref[:, s, :] = words[:, s * LANES:(s + 1) * LANES]

    @pl.when(valid_ref[w] == 0)
    def _():
        y_ref[...] = jnp.zeros_like(y_ref)


def _moe_down(items, hid, w_down):
    n_items = N_MOE_BLOCKS * (D_MODEL // DOWN_TILE)
    tile_rows = DOWN_TILE // 2 // LANES
    return pl.pallas_call(
        _moe_down_kernel,
        grid_spec=pltpu.PrefetchScalarGridSpec(
            num_scalar_prefetch=9,
            grid=(n_items,),
            in_specs=[pl.BlockSpec((MOE_ROWS, D_FF), lambda w, ib, *_: (ib[w], 0)),
                      pl.BlockSpec(memory_space=pl.ANY)],
            out_specs=pl.BlockSpec((MOE_ROWS, tile_rows, LANES),
                                   lambda w, ib, b, ot, *_: (b[w], ot[w], 0)),
            scratch_shapes=[pltpu.VMEM((WEIGHT_RING, D_FF, DOWN_TILE), F32),
                            pltpu.VMEM((D_FF, DOWN_TILE), BF16),
                            pltpu.SemaphoreType.DMA((WEIGHT_RING,))]),
        out_shape=jax.ShapeDtypeStruct((N_MOE_ROWS, Y_ROW_TILES, LANES), jnp.uint32),
        compiler_params=_params(1),
        name="moe_down",
    )(*items, hid, w_down)


def _combine_kernel(dest_ref, dest_next_ref, h_ref, gate_ref, y_hbm, o_ref, buf, sem):
    i = pl.program_id(0)

    def issue(dests, slot):
        def start(r, carry):
            for k in range(TOP_K):
                dst = buf.at[slot, k, pl.ds(r * COMBINE_PITCH, Y_ROW_TILES)]
                pltpu.make_async_copy(y_hbm.at[dests[0, k, r]], dst, sem.at[slot]).start()
            return carry
        lax.fori_loop(0, COMBINE_ROWS, start, 0, unroll=DMA_UNROLL // TOP_K)

    @pl.when(i == 0)
    def _():
        issue(dest_ref, 0)

    @pl.when(i + 1 < pl.num_programs(0))
    def _():
        issue(dest_next_ref, (i + 1) % 2)

    slot = i % 2

    for k in range(TOP_K):
        for c in range(COMBINE_ROWS // WAIT_ROWS):
            span = buf.at[slot, k, pl.ds(c * WAIT_ROWS * Y_ROW_TILES, WAIT_ROWS * Y_ROW_TILES)]
            pltpu.make_async_copy(span, span, sem.at[slot]).wait()

    g0 = gate_ref[:, 0:1]
    g1 = gate_ref[:, 1:2]
    tiles_per_half = DOWN_TILE // 2 // LANES
    for s in range(Y_ROW_TILES):
        lo_col = (s // tiles_per_half) * DOWN_TILE + (s % tiles_per_half) * LANES
        lo0, hi0 = _unpack_words(buf[slot, 0, pl.ds(s, COMBINE_ROWS, stride=COMBINE_PITCH), :])
        lo1, hi1 = _unpack_words(buf[slot, 1, pl.ds(s, COMBINE_ROWS, stride=COMBINE_PITCH), :])
        for col, y0, y1 in ((lo_col, lo0, lo1), (lo_col + DOWN_TILE // 2, hi0, hi1)):
            cols = slice(col, col + LANES)
            o_ref[:, cols] = h_ref[:, cols] + (y0 * g0 + y1 * g1)


def _combine(dest, h, gates, y):
    n_tiles = SEQ // COMBINE_ROWS
    dests = jnp.stack([d.reshape(n_tiles, COMBINE_ROWS) for d in dest], axis=1)
    blk = (1, TOP_K, COMBINE_ROWS)
    return pl.pallas_call(
        _combine_kernel,
        grid=(n_tiles,),
        in_specs=[pl.BlockSpec(blk, lambda i: (i, 0, 0), memory_space=pltpu.SMEM),
                  pl.BlockSpec(blk, lambda i: (jnp.minimum(i + 1, n_tiles - 1), 0, 0),
                               memory_space=pltpu.SMEM),
                  pl.BlockSpec((COMBINE_ROWS, D_MODEL), lambda i: (i, 0)),
                  pl.BlockSpec((COMBINE_ROWS, LANES), lambda i: (i, 0)),
                  pl.BlockSpec(memory_space=pl.ANY)],
        out_specs=pl.BlockSpec((COMBINE_ROWS, D_MODEL), lambda i: (i, 0)),
        out_shape=jax.ShapeDtypeStruct((SEQ, D_MODEL), F32),
        scratch_shapes=[pltpu.VMEM((2, TOP_K, COMBINE_ROWS * COMBINE_PITCH, LANES), jnp.uint32),
                        pltpu.SemaphoreType.DMA((2,))],
        compiler_params=_params(1),
        name="moe_combine",
    )(dests, dests, h, gates, y)


def _lookup(table, idx):
    pos = jnp.arange(table.shape[0], dtype=jnp.int32)
    return jnp.sum(jnp.where(idx[:, None] == pos[None, :], table[None, :], 0), axis=1)


def _count_le(ends, x):
    return jnp.sum((ends[None, :] <= x[:, None]).astype(jnp.int32), axis=1)


def _work_items(n_blocks_e, block_start_e, tiles):
    i32 = lambda a: a.astype(jnp.int32)
    n_items = N_MOE_BLOCKS * tiles
    per_e = n_blocks_e * tiles
    ends = jnp.cumsum(per_e)
    total = ends[-1]
    idx = jnp.arange(n_items, dtype=jnp.int32)
    valid = idx < total
    w = jnp.minimum(idx, total - 1)
    e = _count_le(ends, w)
    local = w - _lookup(ends - per_e, e)
    nb = _lookup(n_blocks_e, e)
    first_blk = _lookup(block_start_e, e)
    spare = idx - total
    w_tile = local // nb
    o_tile = jnp.where(valid, w_tile, spare % tiles)
    blk = jnp.where(valid, first_blk + local % nb, total // tiles + spare // tiles)
    first = valid & (local % nb == 0)
    active = n_blocks_e > 0
    active_rank = jnp.cumsum(active) - active
    seq = _lookup(active_rank, e) * tiles + w_tile
    k = jnp.arange(N_EXPERTS * tiles, dtype=jnp.int32)
    experts = jnp.arange(N_EXPERTS, dtype=jnp.int32)
    is_kth = active[None, :] & (active_rank[None, :] == (k // tiles)[:, None])
    tile_e = jnp.sum(jnp.where(is_kth, experts[None, :], 0), axis=1)
    tile_t = k % tiles
    n_tiles = (jnp.sum(active) * tiles).reshape(1)
    in_blk = first_blk + local % nb
    items = (i32(in_blk), i32(blk), i32(o_tile), i32(valid), i32(first), i32(seq), i32(tile_e),
             i32(tile_t), i32(n_tiles))
    return items, dict(blocks_of_expert=nb, local_block=local % nb, weight_tile=w_tile, valid=valid)


def _gather_schedule(in_blk, info, n_valid):
    i32 = lambda a: a.astype(jnp.int32)
    n = in_blk.shape[0]
    cached = (info['blocks_of_expert'] <= X_CACHE_SLOTS) & (info['weight_tile'] > 0)
    fetch = info['valid'] & ~cached
    x_slot = info['local_block'] % X_CACHE_SLOTS
    stage_slot = (jnp.cumsum(fetch) - 1) % 2
    idx = jnp.arange(n, dtype=jnp.int32)
    later = lax.cummin(jnp.where(fetch, idx, n), axis=0, reverse=True)
    next_fetch = jnp.concatenate([later[1:], jnp.full((1,), n, later.dtype)])
    prefetch = fetch & (next_fetch < n)
    pf_blk = jnp.where(prefetch, _lookup(in_blk, jnp.minimum(next_fetch, n - 1)), in_blk)
    return (i32(fetch), i32(x_slot), i32(stage_slot), i32(prefetch), i32(pf_blk),
            i32(_lookup(n_valid, pf_blk)))


def _dispatch_plan(ids, counts):
    counts = counts[0, :N_EXPERTS]
    n_blocks_e = (counts + MOE_ROWS - 1) // MOE_ROWS
    block_ends = jnp.cumsum(n_blocks_e)
    block_start_e = block_ends - n_blocks_e
    first_row_e = block_start_e * MOE_ROWS
    dest = [(_lookup(first_row_e, ids[k]) + ids[TOP_K + k]).astype(jnp.int32)
            for k in range(TOP_K)]
    token = jnp.arange(SEQ, dtype=jnp.int32)
    row_tok = jnp.zeros((N_MOE_ROWS,), jnp.int32).at[jnp.concatenate(dest)].set(
        jnp.concatenate([token] * TOP_K), unique_indices=True, mode='promise_in_bounds')
    blk = jnp.arange(N_MOE_BLOCKS, dtype=jnp.int32)
    blk_e = jnp.minimum(_count_le(block_ends, blk), N_EXPERTS - 1)
    n_valid = jnp.clip(_lookup(counts, blk_e) - (blk - _lookup(block_start_e, blk_e)) * MOE_ROWS,
                       0, MOE_ROWS)
    n_valid = jnp.where(blk < block_ends[-1], n_valid, 0).astype(jnp.int32)
    return dest, row_tok, n_valid, n_blocks_e.astype(jnp.int32), block_start_e.astype(jnp.int32)


def kernel(x, norm_mix_w, w_in, q_norm_w, k_norm_w, conv_w, conv_b, conv_ln_w, conv_ln_b, w_out,
           norm_ffn_w, w_group_router, b_group_router, w_expert_router, b_expert_router,
           w_gate, w_up, w_down):
    h = x.reshape(SEQ, D_MODEL)
    slopes = jnp.exp2(-8.0 * jnp.arange(1, N_HEADS + 1, dtype=F32) / N_HEADS)
    for l in range(norm_mix_w.shape[0]):
        hn = _rmsnorm(h, norm_mix_w[l][None, :])
        qk_norm = jnp.concatenate([jnp.tile(q_norm_w[l], N_HEADS), jnp.tile(k_norm_w[l], N_HEADS)])
        z = _inproj(hn, w_in[l], qk_norm[None, :])
        attn = _attention(z, slopes)
        n_ct = CONV_WIDTH // LANES
        conv = _conformer_conv(z, conv_w[l].reshape(CONV_KERNEL, n_ct, LANES),
                               conv_b[l].reshape(n_ct, LANES), conv_ln_w[l][None, :],
                               conv_ln_b[l][None, :])
        h = _outproj(attn, conv, w_out[l], h)

        pad = LANES - N_GROUPS - N_EXPERTS
        w_router = jnp.concatenate([w_group_router[l], w_expert_router[l],
                                    jnp.zeros((D_MODEL, pad), F32)], axis=1)
        b_router = jnp.concatenate([b_group_router[l], b_expert_router[l], jnp.zeros((pad,), F32)])
        w_router_hi = w_router.astype(BF16)
        w_router_lo = (w_router - w_router_hi.astype(F32)).astype(BF16)
        hn_words, ids, gates, counts = _router(h, norm_ffn_w[l][None, :],
                                               jnp.concatenate([w_router_hi, w_router_lo], axis=1),
                                               b_router[None, :])
        dest, row_tok, n_valid, n_blocks_e, block_start_e = _dispatch_plan(ids, counts)
        up_items, up_info = _work_items(n_blocks_e, block_start_e, D_FF // UP_TILE)
        hid = _moe_up(up_items, up_info, n_valid, row_tok, hn_words, w_gate[l], w_up[l])
        down_items, _ = _work_items(n_blocks_e, block_start_e, D_MODEL // DOWN_TILE)
        y = _moe_down(down_items, hid, w_down[l])
        h = _combine(dest, h, gates, y)
    return h.reshape(x.shape)
```

```python
import math

import jax
import jax.numpy as jnp
from jax import lax
from jax.experimental import pallas as pl
from jax.experimental.pallas import tpu as pltpu

F32 = jnp.float32
BF16 = jnp.bfloat16

D_MODEL = 4096
SEQ = 8192
HEAD_DIM = 128
N_HEADS = 16
ATTN_WIDTH = N_HEADS * HEAD_DIM
CONV_WIDTH = D_MODEL - ATTN_WIDTH
CONV_KERNEL = 31
IN_COLS = 3 * ATTN_WIDTH + 2 * CONV_WIDTH
DILATIONS = (1, 4, 16)
ATTN_BLOCK = 128
N_KEYS = 128
N_GROUPS = 8
EXPERTS_PER_GROUP = 8
N_EXPERTS = N_GROUPS * EXPERTS_PER_GROUP
TOP_K = 2
D_FF = 1024
EPS = 1e-6
NEG_INF = -1e30

LANES = 128
SUBLANES = 8
VMEM_LIMIT = 56 * 1024 * 1024

SPAN = DILATIONS[-1] * ATTN_BLOCK

MOE_ROWS = 128
N_ASSIGN = SEQ * TOP_K
N_MOE_ROWS = N_ASSIGN + N_EXPERTS * MOE_ROWS
N_MOE_BLOCKS = N_MOE_ROWS // MOE_ROWS
UP_TILE = 512
DOWN_TILE = 4096
WEIGHT_RING = 2
WEIGHT_DMA_PRIORITY = 1
ROUTER_ROWS = 512
COMBINE_ROWS = 128
X_CACHE_SLOTS = 4
X_ROW_TILES = D_MODEL // 2 // LANES
Y_ROW_TILES = D_MODEL // 2 // LANES
COMBINE_PITCH = Y_ROW_TILES + 4
GATHER_PITCH = X_ROW_TILES + 8
DMA_UNROLL = 16
WAIT_ROWS = 32


def _params(n_axes):
    return pltpu.CompilerParams(dimension_semantics=("arbitrary",) * n_axes,
                                vmem_limit_bytes=VMEM_LIMIT)


def _pack_rows(x):
    half = x.shape[1] // 2
    lo = pltpu.bitcast(x[:, :half].astype(BF16).astype(F32), jnp.uint32)
    hi = pltpu.bitcast(x[:, half:].astype(BF16).astype(F32), jnp.uint32)
    return (hi & jnp.uint32(0xFFFF0000)) | (lo >> 16)


def _unpack_words(words):
    return (pltpu.bitcast(words << 16, F32),
            pltpu.bitcast(words & jnp.uint32(0xFFFF0000), F32))


def _rmsnorm_kernel(x_ref, w_ref, o_ref):
    x = x_ref[...]
    ms = jnp.mean(x * x, axis=-1, keepdims=True)
    o_ref[...] = (x * lax.rsqrt(ms + EPS) * w_ref[...]).astype(o_ref.dtype)


def _rmsnorm(x, w, rows=512):
    n, d = x.shape
    return pl.pallas_call(
        _rmsnorm_kernel,
        grid=(n // rows,),
        in_specs=[pl.BlockSpec((rows, d), lambda i: (i, 0)),
                  pl.BlockSpec((1, d), lambda i: (0, 0))],
        out_specs=pl.BlockSpec((rows, d), lambda i: (i, 0)),
        out_shape=jax.ShapeDtypeStruct((n, d), BF16),
        compiler_params=_params(1),
        name="rmsnorm",
    )(x, w)


IN_TM = 1024
IN_TN = 512
IN_CHUNKS = (256, 256, 256, 128, 128)
QK_TILES = 2 * ATTN_WIDTH // IN_TN


def _inproj_kernel(hn_ref, w_ref, nw_ref, o_ref, w_bf):
    j = pl.program_id(0)

    @pl.when(pl.program_id(1) == 0)
    def _():
        w_bf[...] = w_ref[...].astype(BF16)

    is_qk = j < QK_TILES
    row0 = 0
    for chunk in IN_CHUNKS:
        rows = slice(row0, row0 + chunk)
        row0 += chunk
        acc = jnp.dot(hn_ref[rows, :], w_bf[...], preferred_element_type=F32)
        for hd in range(IN_TN // HEAD_DIM):
            sl = slice(hd * HEAD_DIM, (hd + 1) * HEAD_DIM)
            t = acc[:, sl]
            ms = jnp.mean(t * t, axis=-1, keepdims=True)
            normed = t * lax.rsqrt(ms + EPS) * nw_ref[:, sl]
            o_ref[rows, sl] = jnp.where(is_qk, normed, t).astype(o_ref.dtype)


def _inproj(hn, w, qk_norm_w):
    n = hn.shape[0]
    return pl.pallas_call(
        _inproj_kernel,
        grid=(IN_COLS // IN_TN, n // IN_TM),
        in_specs=[pl.BlockSpec((IN_TM, D_MODEL), lambda j, i: (i, 0)),
                  pl.BlockSpec((D_MODEL, IN_TN), lambda j, i: (0, j)),
                  pl.BlockSpec((1, IN_TN), lambda j, i: (0, jnp.minimum(j, QK_TILES - 1)))],
        out_specs=pl.BlockSpec((IN_TM, IN_TN), lambda j, i: (i, j)),
        out_shape=jax.ShapeDtypeStruct((n, IN_COLS), BF16),
        scratch_shapes=[pltpu.VMEM((D_MODEL, IN_TN), BF16)],
        compiler_params=_params(2),
        name="inproj",
    )(hn, w, qk_norm_w)


def _attn_kernel(slopes_ref, q_ref, kc_ref, kp_ref, vc_ref, vp_ref, o_ref, qf, kf, vf, ob, lb):
    h = pl.program_id(0)
    first_span = pl.program_id(1) == 0
    slope = slopes_ref[h]

    qi = lax.broadcasted_iota(jnp.int32, (ATTN_BLOCK, 2 * ATTN_BLOCK), 0)
    kj = lax.broadcasted_iota(jnp.int32, (ATTN_BLOCK, 2 * ATTN_BLOCK), 1)
    steps = qi + ATTN_BLOCK - kj
    valid = (steps >= 0) & (steps <= N_KEYS)
    stepsf = steps.astype(F32)
    scale = 1.0 / math.sqrt(HEAD_DIM)

    def attend(b, d, bias, q0, q, k, v, no_prev_block):
        s = lax.dot_general(q, k, (((1,), (1,)), ((), ())),
                            preferred_element_type=F32) * scale + bias
        if no_prev_block is not None:
            s = jnp.where(kj >= jnp.where(no_prev_block, ATTN_BLOCK, 0), s, NEG_INF)
        m = jnp.max(s, axis=-1, keepdims=True)
        p = jnp.exp(s - m)
        den = jnp.sum(p, axis=-1, keepdims=True)
        o = jnp.dot(p.astype(BF16), v, preferred_element_type=F32) / den
        lse = m + jnp.log(den)
        ob[b, pl.ds(q0, ATTN_BLOCK, stride=d), :] = o
        lb[b, pl.ds(q0, ATTN_BLOCK, stride=d), :] = jnp.broadcast_to(lse, (ATTN_BLOCK, HEAD_DIM))

    def banded_bias(d):
        return jnp.where(valid, (-slope * float(d)) * stepsf, NEG_INF)

    def key_window(cur, prev, n):
        if n > 0:
            return cur[(n - 1) * ATTN_BLOCK:(n + 1) * ATTN_BLOCK, :]
        last = prev.shape[0] - ATTN_BLOCK
        return jnp.concatenate([prev[last:, :], cur[:ATTN_BLOCK, :]], axis=0)

    b, d = 0, DILATIONS[0]
    bias = banded_bias(d)
    for n in range(SPAN // ATTN_BLOCK):
        attend(b, d, bias, n * ATTN_BLOCK, q_ref[n * ATTN_BLOCK:(n + 1) * ATTN_BLOCK, :],
               key_window(kc_ref, kp_ref, n), key_window(vc_ref, vp_ref, n),
               first_span if n == 0 else None)

    qf[...] = q_ref[...].astype(F32)
    kf[0:SPAN, :] = kp_ref[...].astype(F32)
    kf[SPAN:2 * SPAN, :] = kc_ref[...].astype(F32)
    vf[0:SPAN, :] = vp_ref[...].astype(F32)
    vf[SPAN:2 * SPAN, :] = vc_ref[...].astype(F32)
    for b, d in list(enumerate(DILATIONS))[1:]:
        bias = banded_bias(d)
        for r in range(d):
            for n in range(SPAN // (ATTN_BLOCK * d)):
                q0 = r + d * ATTN_BLOCK * n
                k0 = SPAN + q0 - d * ATTN_BLOCK
                attend(b, d, bias, q0,
                       qf[pl.ds(q0, ATTN_BLOCK, stride=d), :].astype(BF16),
                       kf[pl.ds(k0, 2 * ATTN_BLOCK, stride=d), :].astype(BF16),
                       vf[pl.ds(k0, 2 * ATTN_BLOCK, stride=d), :].astype(BF16),
                       first_span if n == 0 else None)

    chunk = 256
    for c in range(SPAN // chunk):
        rows = slice(c * chunk, (c + 1) * chunk)
        l0, l1, l2 = lb[0, rows, :], lb[1, rows, :], lb[2, rows, :]
        mx = jnp.maximum(jnp.maximum(l0, l1), l2)
        w0, w1, w2 = jnp.exp(l0 - mx), jnp.exp(l1 - mx), jnp.exp(l2 - mx)
        num = w0 * ob[0, rows, :] + w1 * ob[1, rows, :] + w2 * ob[2, rows, :]
        o_ref[rows, :] = (num / (w0 + w1 + w2)).astype(o_ref.dtype)


def _attention(z, slopes):
    n_spans = SEQ // SPAN
    k_col, v_col = N_HEADS, 2 * N_HEADS
    blk = (SPAN, HEAD_DIM)
    prev = lambda s: jnp.maximum(s - 1, 0)
    return pl.pallas_call(
        _attn_kernel,
        grid_spec=pltpu.PrefetchScalarGridSpec(
            num_scalar_prefetch=1,
            grid=(N_HEADS, n_spans),
            in_specs=[pl.BlockSpec(blk, lambda h, s, sl: (s, h)),
                      pl.BlockSpec(blk, lambda h, s, sl: (s, k_col + h)),
                      pl.BlockSpec(blk, lambda h, s, sl: (prev(s), k_col + h)),
                      pl.BlockSpec(blk, lambda h, s, sl: (s, v_col + h)),
                      pl.BlockSpec(blk, lambda h, s, sl: (prev(s), v_col + h))],
            out_specs=pl.BlockSpec(blk, lambda h, s, sl: (s, h)),
            scratch_shapes=[pltpu.VMEM((SPAN, HEAD_DIM), F32),
                            pltpu.VMEM((2 * SPAN, HEAD_DIM), F32),
                            pltpu.VMEM((2 * SPAN, HEAD_DIM), F32),
                            pltpu.VMEM((3, SPAN, HEAD_DIM), F32),
                            pltpu.VMEM((3, SPAN, HEAD_DIM), F32)]),
        out_shape=jax.ShapeDtypeStruct((SEQ, ATTN_WIDTH), BF16),
        compiler_params=_params(2),
        name="dilated_attention",
    )(slopes, z, z, z, z, z)


CONV_ROWS = 512
CONV_HALO = 32
CONV_CHUNK = 64

CONV_EXT = CONV_HALO + CONV_ROWS
CONV_SHIFTED = CONV_EXT - SUBLANES


def _conv_kernel(a_ref, g_ref, ah_ref, gh_ref, cw_ref, cb_ref, lw_ref, lb_ref, o_ref,
                 u_ext, shifted, c_scr):
    i = pl.program_id(0)
    n_ct = CONV_WIDTH // LANES
    for ct in range(n_ct):
        cols = slice(ct * LANES, (ct + 1) * LANES)
        u_ext[ct, CONV_HALO:, :] = (a_ref[:, cols].astype(F32)
                                    * jax.nn.sigmoid(g_ref[:, cols].astype(F32)))
        uh = ah_ref[:, cols].astype(F32) * jax.nn.sigmoid(gh_ref[:, cols].astype(F32))
        u_ext[ct, 0:CONV_HALO, :] = jnp.where(i == 0, 0.0, uh)

    first_tap = CONV_HALO - (CONV_KERNEL - 1)

    def slab(ct, carry):
        for k in range(1, SUBLANES):
            shifted[k - 1] = u_ext[ct, k:k + CONV_SHIFTED, :]
        for rc in range(CONV_ROWS // CONV_CHUNK):
            base = rc * CONV_CHUNK
            acc = jnp.broadcast_to(cb_ref[pl.ds(ct, 1), :], (CONV_CHUNK, LANES))
            for j in range(CONV_KERNEL):
                k = (first_tap + j) % SUBLANES
                row = base + (first_tap + j) - k
                if k == 0:
                    src = u_ext[ct, row:row + CONV_CHUNK, :]
                else:
                    src = shifted[k - 1, row:row + CONV_CHUNK, :]
                acc = acc + cw_ref[j, pl.ds(ct, 1), :] * src
            c_scr[ct, base:base + CONV_CHUNK, :] = acc
        return carry

    lax.fori_loop(0, n_ct, slab, 0)

    total = c_scr[0]
    for ct in range(1, n_ct):
        total = total + c_scr[ct]
    mu = jnp.sum(total, axis=-1, keepdims=True) * (1.0 / CONV_WIDTH)
    sq = jnp.square(c_scr[0] - mu)
    for ct in range(1, n_ct):
        sq = sq + jnp.square(c_scr[ct] - mu)
    var = jnp.sum(sq, axis=-1, keepdims=True) * (1.0 / CONV_WIDTH)
    rstd = lax.rsqrt(var + EPS)
    for ct in range(n_ct):
        cols = slice(ct * LANES, (ct + 1) * LANES)
        y = (c_scr[ct] - mu) * rstd * lw_ref[:, cols] + lb_ref[:, cols]
        o_ref[:, cols] = (y * jax.nn.sigmoid(y)).astype(o_ref.dtype)


def _conformer_conv(z, conv_w, conv_b, ln_w, ln_b):
    a_col = 3 * ATTN_WIDTH // CONV_WIDTH
    g_col = a_col + 1
    n_ct = CONV_WIDTH // LANES
    halo_blocks = CONV_ROWS // CONV_HALO
    halo = lambda i: jnp.maximum(i * halo_blocks - 1, 0)
    vec = pl.BlockSpec((1, CONV_WIDTH), lambda i: (0, 0))
    return pl.pallas_call(
        _conv_kernel,
        grid=(SEQ // CONV_ROWS,),
        in_specs=[pl.BlockSpec((CONV_ROWS, CONV_WIDTH), lambda i: (i, a_col)),
                  pl.BlockSpec((CONV_ROWS, CONV_WIDTH), lambda i: (i, g_col)),
                  pl.BlockSpec((CONV_HALO, CONV_WIDTH), lambda i: (halo(i), a_col)),
                  pl.BlockSpec((CONV_HALO, CONV_WIDTH), lambda i: (halo(i), g_col)),
                  pl.BlockSpec((CONV_KERNEL, n_ct, LANES), lambda i: (0, 0, 0)),
                  pl.BlockSpec((n_ct, LANES), lambda i: (0, 0)),
                  vec, vec],
        out_specs=pl.BlockSpec((CONV_ROWS, CONV_WIDTH), lambda i: (i, 0)),
        out_shape=jax.ShapeDtypeStruct((SEQ, CONV_WIDTH), BF16),
        scratch_shapes=[pltpu.VMEM((n_ct, CONV_EXT, LANES), F32),
                        pltpu.VMEM((SUBLANES - 1, CONV_SHIFTED, LANES), F32),
                        pltpu.VMEM((n_ct, CONV_ROWS, LANES), F32)],
        compiler_params=_params(1),
        name="conformer_conv",
    )(z, z, z, z, conv_w, conv_b, ln_w, ln_b)


OUT_TM = 1024
OUT_TN = 512


def _outproj_kernel(attn_ref, conv_ref, wa_ref, wc_ref, x_ref, o_ref, wa_bf, wc_bf):
    @pl.when(pl.program_id(1) == 0)
    def _():
        wa_bf[...] = wa_ref[...].astype(BF16)
        wc_bf[...] = wc_ref[...].astype(BF16)

    acc = jnp.dot(attn_ref[...], wa_bf[...], preferred_element_type=F32)
    acc = acc + jnp.dot(conv_ref[...], wc_bf[...], preferred_element_type=F32)
    o_ref[...] = x_ref[...] + acc


def _outproj(attn, conv, w, x):
    n = x.shape[0]
    conv_blk = ATTN_WIDTH // CONV_WIDTH
    return pl.pallas_call(
        _outproj_kernel,
        grid=(D_MODEL // OUT_TN, n // OUT_TM),
        in_specs=[pl.BlockSpec((OUT_TM, ATTN_WIDTH), lambda j, i: (i, 0)),
                  pl.BlockSpec((OUT_TM, CONV_WIDTH), lambda j, i: (i, 0)),
                  pl.BlockSpec((ATTN_WIDTH, OUT_TN), lambda j, i: (0, j)),
                  pl.BlockSpec((CONV_WIDTH, OUT_TN), lambda j, i: (conv_blk, j)),
                  pl.BlockSpec((OUT_TM, OUT_TN), lambda j, i: (i, j))],
        out_specs=pl.BlockSpec((OUT_TM, OUT_TN), lambda j, i: (i, j)),
        out_shape=jax.ShapeDtypeStruct((n, D_MODEL), F32),
        scratch_shapes=[pltpu.VMEM((ATTN_WIDTH, OUT_TN), BF16),
                        pltpu.VMEM((CONV_WIDTH, OUT_TN), BF16)],
        compiler_params=_params(2),
        name="outproj",
    )(attn, conv, w, w, x)


def _router_kernel(h_ref, nw_ref, wr_ref, br_ref, hn_ref, ids_ref, gates_ref, counts_ref, seen):
    x = h_ref[...]
    ms = jnp.mean(x * x, axis=-1, keepdims=True)
    hn = x * lax.rsqrt(ms + EPS) * nw_ref[...]
    words = _pack_rows(hn)
    for s in range(X_ROW_TILES):
        hn_ref[:, s, :] = words[:, s * LANES:(s + 1) * LANES]

    h_hi = hn.astype(BF16)
    h_lo = (hn - h_hi.astype(F32)).astype(BF16)
    hi_both = jnp.dot(h_hi, wr_ref[...], preferred_element_type=F32)
    lo_hi = jnp.dot(h_lo, wr_ref[:, :LANES], preferred_element_type=F32)
    logits = hi_both[:, :LANES] + (hi_both[:, LANES:] + lo_hi) + br_ref[...]
    lane = lax.broadcasted_iota(jnp.int32, logits.shape, 1)
    big = jnp.int32(1 << 20)

    is_g = lane < N_GROUPS
    gl = jnp.where(is_g, logits, -jnp.inf)
    ge = jnp.exp(gl - jnp.max(gl, axis=-1, keepdims=True))
    g_prob = ge / jnp.sum(ge, axis=-1, keepdims=True)
    g_w = jnp.max(g_prob, axis=-1, keepdims=True)
    g_top = jnp.min(jnp.where(is_g & (g_prob == g_w), lane, big), axis=-1, keepdims=True)

    eidx = lane - N_GROUPS
    grp_shift = EXPERTS_PER_GROUP.bit_length() - 1
    in_grp = (eidx >= 0) & (eidx < N_EXPERTS) & ((eidx >> grp_shift) == g_top)
    el = jnp.where(in_grp, logits, -jnp.inf)
    ee = jnp.exp(el - jnp.max(el, axis=-1, keepdims=True))
    e_prob = ee / jnp.sum(ee, axis=-1, keepdims=True)
    v1 = jnp.max(jnp.where(in_grp, e_prob, -1.0), axis=-1, keepdims=True)
    i1 = jnp.min(jnp.where(in_grp & (e_prob == v1), lane, big), axis=-1, keepdims=True)
    rest = in_grp & (lane != i1)
    v2 = jnp.max(jnp.where(rest, e_prob, -1.0), axis=-1, keepdims=True)
    i2 = jnp.min(jnp.where(rest & (e_prob == v2), lane, big), axis=-1, keepdims=True)
    tot = v1 + v2
    gates_ref[...] = jnp.where(lane == 0, g_w * v1 / tot,
                               jnp.where(lane == 1, g_w * v2 / tot, 0.0))

    e1 = i1 - N_GROUPS
    e2 = i2 - N_GROUPS
    first_choice = lane == e1
    second_choice = lane == e2 + N_EXPERTS
    chosen = jnp.where(first_choice | second_choice, 1.0, 0.0)
    t_row = lax.broadcasted_iota(jnp.int32, (ROUTER_ROWS, ROUTER_ROWS), 0)
    t_col = lax.broadcasted_iota(jnp.int32, (ROUTER_ROWS, ROUTER_ROWS), 1)
    earlier = jnp.where(t_col < t_row, 1.0, 0.0).astype(BF16)
    chosen = chosen + pltpu.roll(chosen, N_EXPERTS, axis=1)
    before = jnp.dot(earlier, chosen.astype(BF16), preferred_element_type=F32)

    @pl.when(pl.program_id(0) == 0)
    def _():
        seen[...] = jnp.zeros_like(seen)

    before = before + seen[0:1, :]
    rank1 = jnp.sum(jnp.where(first_choice, before, 0.0), axis=-1, keepdims=True)
    rank2 = jnp.sum(jnp.where(second_choice, before, 0.0), axis=-1, keepdims=True)
    seen[0:1, :] = seen[0:1, :] + jnp.sum(chosen, axis=0, keepdims=True)
    counts_ref[...] = jnp.broadcast_to(seen[0:1, :], counts_ref.shape).astype(jnp.int32)

    per_token = jnp.where(
        lane == 0, e1, jnp.where(
            lane == 1, e2, jnp.where(
                lane == 2, rank1.astype(jnp.int32), jnp.where(
                    lane == 3, rank2.astype(jnp.int32), 0))))
    ids_ref[...] = jnp.transpose(per_token)[:SUBLANES, :]


def _router(h, norm_w, w_router, b_router):
    return pl.pallas_call(
        _router_kernel,
        grid=(SEQ // ROUTER_ROWS,),
        in_specs=[pl.BlockSpec((ROUTER_ROWS, D_MODEL), lambda i: (i, 0)),
                  pl.BlockSpec((1, D_MODEL), lambda i: (0, 0)),
                  pl.BlockSpec((D_MODEL, 2 * LANES), lambda i: (0, 0)),
                  pl.BlockSpec((1, LANES), lambda i: (0, 0))],
        out_specs=[pl.BlockSpec((ROUTER_ROWS, X_ROW_TILES, LANES), lambda i: (i, 0, 0)),
                   pl.BlockSpec((SUBLANES, ROUTER_ROWS), lambda i: (0, i)),
                   pl.BlockSpec((ROUTER_ROWS, LANES), lambda i: (i, 0)),
                   pl.BlockSpec((SUBLANES, LANES), lambda i: (0, 0))],
        out_shape=[jax.ShapeDtypeStruct((SEQ, X_ROW_TILES, LANES), jnp.uint32),
                   jax.ShapeDtypeStruct((SUBLANES, SEQ), jnp.int32),
                   jax.ShapeDtypeStruct((SEQ, LANES), F32),
                   jax.ShapeDtypeStruct((SUBLANES, LANES), jnp.int32)],
        scratch_shapes=[pltpu.VMEM((SUBLANES, LANES), F32)],
        compiler_params=_params(1),
        name="ffn_norm_router",
    )(h, norm_w, w_router, b_router)


def _staged_token(stage, slot, r):
    return stage.at[slot, pl.ds(pl.multiple_of(r * GATHER_PITCH, SUBLANES), X_ROW_TILES)]


def _issue_token_rows(toks, nv, hn_hbm, stage, slot, sem):
    def start(r, carry):
        pltpu.make_async_copy(hn_hbm.at[toks[0, 0, r]], _staged_token(stage, slot, r),
                              sem.at[slot]).start()
        return carry

    def start_group(g, carry):
        for u in range(DMA_UNROLL):
            start(g * DMA_UNROLL + u, carry)
        return carry

    groups = nv // DMA_UNROLL
    lax.fori_loop(0, groups, start_group, 0)
    lax.fori_loop(groups * DMA_UNROLL, nv, start, 0)


def _wait_token_rows(nv, stage, slot, sem):
    def wait_rows(n_tokens):
        span = stage.at[slot, pl.ds(0, n_tokens * X_ROW_TILES)]
        pltpu.make_async_copy(span, span, sem.at[slot]).wait()

    def wait_many(c, carry):
        wait_rows(WAIT_ROWS)
        return carry

    def wait_one(c, carry):
        wait_rows(1)
        return carry

    lax.fori_loop(0, nv // WAIT_ROWS, wait_many, 0)
    lax.fori_loop(0, nv % WAIT_ROWS, wait_one, 0)


def _staged_block(stage, slot):
    return jnp.concatenate([stage[slot, pl.ds(s, MOE_ROWS, stride=GATHER_PITCH), :]
                            for s in range(X_ROW_TILES)], axis=-1)


def _unpack_rows(words):
    lo, hi = _unpack_words(words)
    return jnp.concatenate([lo.astype(BF16), hi.astype(BF16)], axis=-1)


def _stream_weight_tile(w, first_ref, seq_ref, n_tiles_ref, tile_copies, consume):
    n_tiles = n_tiles_ref[0]

    @pl.when(w == 0)
    def _():
        for k in range(WEIGHT_RING):
            @pl.when(k < n_tiles)
            def _():
                for c in tile_copies(k, k):
                    c.start(priority=WEIGHT_DMA_PRIORITY)

    @pl.when(first_ref[w] == 1)
    def _():
        k = seq_ref[w]
        slot = k % WEIGHT_RING
        for c in tile_copies(k, slot):
            c.wait()
        consume(slot)

        @pl.when(k + WEIGHT_RING < n_tiles)
        def _():
            for c in tile_copies(k + WEIGHT_RING, slot):
                c.start(priority=WEIGHT_DMA_PRIORITY)


def _moe_up_kernel(ib_ref, nv_ref, b_ref, ot_ref, valid_ref, first_ref, seq_ref, te_ref, tt_ref,
                   nt_ref, fetch_ref, xslot_ref, sslot_ref, pf_ref, pfb_ref, pfnv_ref,
                   tok_ref, tok_next_ref, hn_hbm, wg_hbm, wu_hbm, hid_ref,
                   stage, x_cache, ring_g, ring_u, wg_bf, wu_bf, row_sem, sem):
    w = pl.program_id(0)

    @pl.when(w == 0)
    def _():
        stage[...] = jnp.zeros_like(stage)
        _issue_token_rows(tok_ref, nv_ref[0], hn_hbm, stage, 0, row_sem)

    @pl.when(pf_ref[w] == 1)
    def _():
        _issue_token_rows(tok_next_ref, pfnv_ref[w], hn_hbm, stage, 1 - sslot_ref[w], row_sem)

    def tile_copies(k, slot):
        e = te_ref[k]
        col = pl.multiple_of(tt_ref[k] * UP_TILE, UP_TILE)
        return (pltpu.make_async_copy(wg_hbm.at[e, :, pl.ds(col, UP_TILE)], ring_g.at[slot],
                                      sem.at[0, slot]),
                pltpu.make_async_copy(wu_hbm.at[e, :, pl.ds(col, UP_TILE)], ring_u.at[slot],
                                      sem.at[1, slot]))

    def consume(slot):
        wg_bf[...] = ring_g[slot].astype(BF16)
        wu_bf[...] = ring_u[slot].astype(BF16)

    _stream_weight_tile(w, first_ref, seq_ref, nt_ref, tile_copies, consume)

    @pl.when(valid_ref[w] == 1)
    def _():
        x_slot = xslot_ref[w]

        @pl.when(fetch_ref[w] == 1)
        def _():
            slot = sslot_ref[w]
            _wait_token_rows(nv_ref[w], stage, slot, row_sem)
            x_cache[x_slot] = _unpack_rows(_staged_block(stage, slot))

        x = x_cache[x_slot]
        g = jnp.dot(x, wg_bf[...], preferred_element_type=F32)
        u = jnp.dot(x, wu_bf[...], preferred_element_type=F32)
        hid_ref[...] = (g * jax.nn.sigmoid(g) * u).astype(hid_ref.dtype)

    @pl.when(valid_ref[w] == 0)
    def _():
        hid_ref[...] = jnp.zeros_like(hid_ref)


def _moe_up(items, info, n_valid, row_tok, hn_words, w_gate, w_up):
    n_items = N_MOE_BLOCKS * (D_FF // UP_TILE)
    in_blk = items[0]
    schedule = _gather_schedule(in_blk, info, n_valid)
    n_prefetch = 1 + len(items) + len(schedule)
    toks = row_tok.reshape(N_MOE_BLOCKS, 1, MOE_ROWS)
    smem_block = lambda index_map: pl.BlockSpec((1, 1, MOE_ROWS), index_map,
                                                memory_space=pltpu.SMEM)
    any_spec = pl.BlockSpec(memory_space=pl.ANY)
    return pl.pallas_call(
        _moe_up_kernel,
        grid_spec=pltpu.PrefetchScalarGridSpec(
            num_scalar_prefetch=n_prefetch,
            grid=(n_items,),
            in_specs=[smem_block(lambda w, *refs: (refs[0][w], 0, 0)),
                      smem_block(lambda w, *refs: (refs[n_prefetch - 2][w], 0, 0)),
                      any_spec, any_spec, any_spec],
            out_specs=pl.BlockSpec((MOE_ROWS, UP_TILE),
                                   lambda w, ib, nv, b, ot, *_: (b[w], ot[w])),
            scratch_shapes=[pltpu.VMEM((2, MOE_ROWS * GATHER_PITCH, LANES), jnp.uint32),
                            pltpu.VMEM((X_CACHE_SLOTS, MOE_ROWS, D_MODEL), BF16),
                            pltpu.VMEM((WEIGHT_RING, D_MODEL, UP_TILE), F32),
                            pltpu.VMEM((WEIGHT_RING, D_MODEL, UP_TILE), F32),
                            pltpu.VMEM((D_MODEL, UP_TILE), BF16),
                            pltpu.VMEM((D_MODEL, UP_TILE), BF16),
                            pltpu.SemaphoreType.DMA((2,)),
                            pltpu.SemaphoreType.DMA((2, WEIGHT_RING))]),
        out_shape=jax.ShapeDtypeStruct((N_MOE_ROWS, D_FF), BF16),
        compiler_params=_params(1),
        name="moe_gate_up",
    )(in_blk, _lookup(n_valid, in_blk), *items[1:], *schedule, toks, toks, hn_words, w_gate, w_up)


def _moe_down_kernel(ib_ref, b_ref, ot_ref, valid_ref, first_ref, seq_ref, te_ref, tt_ref, nt_ref,
                     hid_ref, wd_hbm, y_ref, ring, wd_bf, sem):
    w = pl.program_id(0)

    def tile_copies(k, slot):
        col = pl.multiple_of(tt_ref[k] * DOWN_TILE, DOWN_TILE)
        return (pltpu.make_async_copy(wd_hbm.at[te_ref[k], :, pl.ds(col, DOWN_TILE)],
                                      ring.at[slot], sem.at[slot]),)

    def consume(slot):
        wd_bf[...] = ring[slot].astype(BF16)

    _stream_weight_tile(w, first_ref, seq_ref, nt_ref, tile_copies, consume)

    @pl.when(valid_ref[w] == 1)
    def _():
        y = jnp.dot(hid_ref[...], wd_bf[...], preferred_element_type=F32)
        words = _pack_rows(y)
        for s in range(DOWN_TILE // 2 // LANES):
            y_ref[:, s, :] = words[:, s * LANES:(s + 1) * LANES]

    @pl.when(valid_ref[w] == 0)
    def _():
        y_ref[...] = jnp.zeros_like(y_ref)


def _moe_down(items, hid, w_down):
    n_items = N_MOE_BLOCKS * (D_MODEL // DOWN_TILE)
    tile_rows = DOWN_TILE // 2 // LANES
    return pl.pallas_call(
        _moe_down_kernel,
        grid_spec=pltpu.PrefetchScalarGridSpec(
            num_scalar_prefetch=9,
            grid=(n_items,),
            in_specs=[pl.BlockSpec((MOE_ROWS, D_FF), lambda w, ib, *_: (ib[w], 0)),
                      pl.BlockSpec(memory_space=pl.ANY)],
            out_specs=pl.BlockSpec((MOE_ROWS, tile_rows, LANES),
                                   lambda w, ib, b, ot, *_: (b[w], ot[w], 0)),
            scratch_shapes=[pltpu.VMEM((WEIGHT_RING, D_FF, DOWN_TILE), F32),
                            pltpu.VMEM((D_FF, DOWN_TILE), BF16),
                            pltpu.SemaphoreType.DMA((WEIGHT_RING,))]),
        out_shape=jax.ShapeDtypeStruct((N_MOE_ROWS, Y_ROW_TILES, LANES), jnp.uint32),
        compiler_params=_params(1),
        name="moe_down",
    )(*items, hid, w_down)


def _combine_kernel(dest_ref, dest_next_ref, h_ref, gate_ref, y_hbm, o_ref, buf, sem):
    i = pl.program_id(0)

    def issue(dests, slot):
        def start(r, carry):
            for k in range(TOP_K):
                dst = buf.at[slot, k, pl.ds(r * COMBINE_PITCH, Y_ROW_TILES)]
                pltpu.make_async_copy(y_hbm.at[dests[0, k, r]], dst, sem.at[slot]).start()
            return carry
        lax.fori_loop(0, COMBINE_ROWS, start, 0, unroll=DMA_UNROLL // TOP_K)

    @pl.when(i == 0)
    def _():
        issue(dest_ref, 0)

    @pl.when(i + 1 < pl.num_programs(0))
    def _():
        issue(dest_next_ref, (i + 1) % 2)

    slot = i % 2

    for k in range(TOP_K):
        for c in range(COMBINE_ROWS // WAIT_ROWS):
            span = buf.at[slot, k, pl.ds(c * WAIT_ROWS * Y_ROW_TILES, WAIT_ROWS * Y_ROW_TILES)]
            pltpu.make_async_copy(span, span, sem.at[slot]).wait()

    g0 = gate_ref[:, 0:1]
    g1 = gate_ref[:, 1:2]
    tiles_per_half = DOWN_TILE // 2 // LANES
    for s in range(Y_ROW_TILES):
        lo_col = (s // tiles_per_half) * DOWN_TILE + (s % tiles_per_half) * LANES
        lo0, hi0 = _unpack_words(buf[slot, 0, pl.ds(s, COMBINE_ROWS, stride=COMBINE_PITCH), :])
        lo1, hi1 = _unpack_words(buf[slot, 1, pl.ds(s, COMBINE_ROWS, stride=COMBINE_PITCH), :])
        for col, y0, y1 in ((lo_col, lo0, lo1), (lo_col + DOWN_TILE // 2, hi0, hi1)):
            cols = slice(col, col + LANES)
            o_ref[:, cols] = h_ref[:, cols] + (y0 * g0 + y1 * g1)


def _combine(dest, h, gates, y):
    n_tiles = SEQ // COMBINE_ROWS
    dests = jnp.stack([d.reshape(n_tiles, COMBINE_ROWS) for d in dest], axis=1)
    blk = (1, TOP_K, COMBINE_ROWS)
    return pl.pallas_call(
        _combine_kernel,
        grid=(n_tiles,),
        in_specs=[pl.BlockSpec(blk, lambda i: (i, 0, 0), memory_space=pltpu.SMEM),
                  pl.BlockSpec(blk, lambda i: (jnp.minimum(i + 1, n_tiles - 1), 0, 0),
                               memory_space=pltpu.SMEM),
                  pl.BlockSpec((COMBINE_ROWS, D_MODEL), lambda i: (i, 0)),
                  pl.BlockSpec((COMBINE_ROWS, LANES), lambda i: (i, 0)),
                  pl.BlockSpec(memory_space=pl.ANY)],
        out_specs=pl.BlockSpec((COMBINE_ROWS, D_MODEL), lambda i: (i, 0)),
        out_shape=jax.ShapeDtypeStruct((SEQ, D_MODEL), F32),
        scratch_shapes=[pltpu.VMEM((2, TOP_K, COMBINE_ROWS * COMBINE_PITCH, LANES), jnp.uint32),
                        pltpu.SemaphoreType.DMA((2,))],
        compiler_params=_params(1),
        name="moe_combine",
    )(dests, dests, h, gates, y)


def _lookup(table, idx):
    pos = jnp.arange(table.shape[0], dtype=jnp.int32)
    return jnp.sum(jnp.where(idx[:, None] == pos[None, :], table[None, :], 0), axis=1)


def _count_le(ends, x):
    return jnp.sum((ends[None, :] <= x[:, None]).astype(jnp.int32), axis=1)


def _work_items(n_blocks_e, block_start_e, tiles):
    i32 = lambda a: a.astype(jnp.int32)
    n_items = N_MOE_BLOCKS * tiles
    per_e = n_blocks_e * tiles
    ends = jnp.cumsum(per_e)
    total = ends[-1]
    idx = jnp.arange(n_items, dtype=jnp.int32)
    valid = idx < total
    w = jnp.minimum(idx, total - 1)
    e = _count_le(ends, w)
    local = w - _lookup(ends - per_e, e)
    nb = _lookup(n_blocks_e, e)
    first_blk = _lookup(block_start_e, e)
    spare = idx - total
    w_tile = local // nb
    o_tile = jnp.where(valid, w_tile, spare % tiles)
    blk = jnp.where(valid, first_blk + local % nb, total // tiles + spare // tiles)
    first = valid & (local % nb == 0)
    active = n_blocks_e > 0
    active_rank = jnp.cumsum(active) - active
    seq = _lookup(active_rank, e) * tiles + w_tile
    k = jnp.arange(N_EXPERTS * tiles, dtype=jnp.int32)
    experts = jnp.arange(N_EXPERTS, dtype=jnp.int32)
    is_kth = active[None, :] & (active_rank[None, :] == (k // tiles)[:, None])
    tile_e = jnp.sum(jnp.where(is_kth, experts[None, :], 0), axis=1)
    tile_t = k % tiles
    n_tiles = (jnp.sum(active) * tiles).reshape(1)
    in_blk = first_blk + local % nb
    items = (i32(in_blk), i32(blk), i32(o_tile), i32(valid), i32(first), i32(seq), i32(tile_e),
             i32(tile_t), i32(n_tiles))
    return items, dict(blocks_of_expert=nb, local_block=local % nb, weight_tile=w_tile, valid=valid)


def _gather_schedule(in_blk, info, n_valid):
    i32 = lambda a: a.astype(jnp.int32)
    n = in_blk.shape[0]
    cached = (info['blocks_of_expert'] <= X_CACHE_SLOTS) & (info['weight_tile'] > 0)
    fetch = info['valid'] & ~cached
    x_slot = info['local_block'] % X_CACHE_SLOTS
    stage_slot = (jnp.cumsum(fetch) - 1) % 2
    idx = jnp.arange(n, dtype=jnp.int32)
    later = lax.cummin(jnp.where(fetch, idx, n), axis=0, reverse=True)
    next_fetch = jnp.concatenate([later[1:], jnp.full((1,), n, later.dtype)])
    prefetch = fetch & (next_fetch < n)
    pf_blk = jnp.where(prefetch, _lookup(in_blk, jnp.minimum(next_fetch, n - 1)), in_blk)
    return (i32(fetch), i32(x_slot), i32(stage_slot), i32(prefetch), i32(pf_blk),
            i32(_lookup(n_valid, pf_blk)))


def _dispatch_plan(ids, counts):
    counts = counts[0, :N_EXPERTS]
    n_blocks_e = (counts + MOE_ROWS - 1) // MOE_ROWS
    block_ends = jnp.cumsum(n_blocks_e)
    block_start_e = block_ends - n_blocks_e
    first_row_e = block_start_e * MOE_ROWS
    dest = [(_lookup(first_row_e, ids[k]) + ids[TOP_K + k]).astype(jnp.int32)
            for k in range(TOP_K)]
    token = jnp.arange(SEQ, dtype=jnp.int32)
    row_tok = jnp.zeros((N_MOE_ROWS,), jnp.int32).at[jnp.concatenate(dest)].set(
        jnp.concatenate([token] * TOP_K), unique_indices=True, mode='promise_in_bounds')
    blk = jnp.arange(N_MOE_BLOCKS, dtype=jnp.int32)
    blk_e = jnp.minimum(_count_le(block_ends, blk), N_EXPERTS - 1)
    n_valid = jnp.clip(_lookup(counts, blk_e) - (blk - _lookup(block_start_e, blk_e)) * MOE_ROWS,
                       0, MOE_ROWS)
    n_valid = jnp.where(blk < block_ends[-1], n_valid, 0).astype(jnp.int32)
    return dest, row_tok, n_valid, n_blocks_e.astype(jnp.int32), block_start_e.astype(jnp.int32)


def kernel(x, norm_mix_w, w_in, q_norm_w, k_norm_w, conv_w, conv_b, conv_ln_w, conv_ln_b, w_out,
           norm_ffn_w, w_group_router, b_group_router, w_expert_router, b_expert_router,
           w_gate, w_up, w_down):
    h = x.reshape(SEQ, D_MODEL)
    slopes = jnp.exp2(-8.0 * jnp.arange(1, N_HEADS + 1, dtype=F32) / N_HEADS)
    for l in range(norm_mix_w.shape[0]):
        hn = _rmsnorm(h, norm_mix_w[l][None, :])
        qk_norm = jnp.concatenate([jnp.tile(q_norm_w[l], N_HEADS), jnp.tile(k_norm_w[l], N_HEADS)])
        z = _inproj(hn, w_in[l], qk_norm[None, :])
        attn = _attention(z, slopes)
        n_ct = CONV_WIDTH // LANES
        conv = _conformer_conv(z, conv_w[l].reshape(CONV_KERNEL, n_ct, LANES),
                               conv_b[l].reshape(n_ct, LANES), conv_ln_w[l][None, :],
                               conv_ln_b[l][None, :])
        h = _outproj(attn, conv, w_out[l], h)

        pad = LANES - N_GROUPS - N_EXPERTS
        w_router = jnp.concatenate([w_group_router[l], w_expert_router[l],
                                    jnp.zeros((D_MODEL, pad), F32)], axis=1)
        b_router = jnp.concatenate([b_group_router[l], b_expert_router[l], jnp.zeros((pad,), F32)])
        w_router_hi = w_router.astype(BF16)
        w_router_lo = (w_router - w_router_hi.astype(F32)).astype(BF16)
        hn_words, ids, gates, counts = _router(h, norm_ffn_w[l][None, :],
                                               jnp.concatenate([w_router_hi, w_router_lo], axis=1),
                                               b_router[None, :])
        dest, row_tok, n_valid, n_blocks_e, block_start_e = _dispatch_plan(ids, counts)
        up_items, up_info = _work_items(n_blocks_e, block_start_e, D_FF // UP_TILE)
        hid = _moe_up(up_items, up_info, n_valid, row_tok, hn_words, w_gate[l], w_up[l])
        down_items, _ = _work_items(n_blocks_e, block_start_e, D_MODEL // DOWN_TILE)
        y = _moe_down(down_items, hid, w_down[l])
        h = _combine(dest, h, gates, y)
    return h.reshape(x.shape)
```

```python
import math

import jax
import jax.numpy as jnp
from jax import lax
from jax.experimental import pallas as pl
from jax.experimental.pallas import tpu as pltpu

F32 = jnp.float32
BF16 = jnp.bfloat16

D_MODEL = 4096
SEQ = 8192
HEAD_DIM = 128
N_HEADS = 16
ATTN_WIDTH = N_HEADS * HEAD_DIM
CONV_WIDTH = D_MODEL - ATTN_WIDTH
CONV_KERNEL = 31
IN_COLS = 3 * ATTN_WIDTH + 2 * CONV_WIDTH
DILATIONS = (1, 4, 16)
ATTN_BLOCK = 128
N_KEYS = 128
N_GROUPS = 8
EXPERTS_PER_GROUP = 8
N_EXPERTS = N_GROUPS * EXPERTS_PER_GROUP
TOP_K = 2
D_FF = 1024
EPS = 1e-6
NEG_INF = -1e30

LANES = 128
SUBLANES = 8
VMEM_LIMIT = 56 * 1024 * 1024

SPAN = DILATIONS[-1] * ATTN_BLOCK

MOE_ROWS = 128
N_ASSIGN = SEQ * TOP_K
N_MOE_ROWS = N_ASSIGN + N_EXPERTS * MOE_ROWS
N_MOE_BLOCKS = N_MOE_ROWS // MOE_ROWS
UP_TILE = 512
DOWN_TILE = 4096
WEIGHT_RING = 2
WEIGHT_DMA_PRIORITY = 1
ROUTER_ROWS = 512
COMBINE_ROWS = 128
X_CACHE_SLOTS = 4
X_ROW_TILES = D_MODEL // 2 // LANES
Y_ROW_TILES = D_MODEL // 2 // LANES
COMBINE_PITCH = Y_ROW_TILES + 4
GATHER_PITCH = X_ROW_TILES + 8
DMA_UNROLL = 16
WAIT_ROWS = 32


def _params(n_axes):
    return pltpu.CompilerParams(dimension_semantics=("arbitrary",) * n_axes,
                                vmem_limit_bytes=VMEM_LIMIT)


def _pack_rows(x):
    half = x.shape[1] // 2
    lo = pltpu.bitcast(x[:, :half].astype(BF16).astype(F32), jnp.uint32)
    hi = pltpu.bitcast(x[:, half:].astype(BF16).astype(F32), jnp.uint32)
    return (hi & jnp.uint32(0xFFFF0000)) | (lo >> 16)


def _unpack_words(words):
    return (pltpu.bitcast(words << 16, F32),
            pltpu.bitcast(words & jnp.uint32(0xFFFF0000), F32))


def _rmsnorm_kernel(x_ref, w_ref, o_ref):
    x = x_ref[...]
    ms = jnp.mean(x * x, axis=-1, keepdims=True)
    o_ref[...] = (x * lax.rsqrt(ms + EPS) * w_ref[...]).astype(o_ref.dtype)


def _rmsnorm(x, w, rows=512):
    n, d = x.shape
    return pl.pallas_call(
        _rmsnorm_kernel,
        grid=(n // rows,),
        in_specs=[pl.BlockSpec((rows, d), lambda i: (i, 0)),
                  pl.BlockSpec((1, d), lambda i: (0, 0))],
        out_specs=pl.BlockSpec((rows, d), lambda i: (i, 0)),
        out_shape=jax.ShapeDtypeStruct((n, d), BF16),
        compiler_params=_params(1),
        name="rmsnorm",
    )(x, w)


IN_TM = 1024
IN_TN = 512
IN_CHUNKS = (256, 256, 256, 128, 128)
QK_TILES = 2 * ATTN_WIDTH // IN_TN


def _inproj_kernel(hn_ref, w_ref, nw_ref, o_ref, w_bf):
    j = pl.program_id(0)

    @pl.when(pl.program_id(1) == 0)
    def _():
        w_bf[...] = w_ref[...].astype(BF16)

    is_qk = j < QK_TILES
    row0 = 0
    for chunk in IN_CHUNKS:
        rows = slice(row0, row0 + chunk)
        row0 += chunk
        acc = jnp.dot(hn_ref[rows, :], w_bf[...], preferred_element_type=F32)
        for hd in range(IN_TN // HEAD_DIM):
            sl = slice(hd * HEAD_DIM, (hd + 1) * HEAD_DIM)
            t = acc[:, sl]
            ms = jnp.mean(t * t, axis=-1, keepdims=True)
            normed = t * lax.rsqrt(ms + EPS) * nw_ref[:, sl]
            o_ref[rows, sl] = jnp.where(is_qk, normed, t).astype(o_ref.dtype)


def _inproj(hn, w, qk_norm_w):
    n = hn.shape[0]
    return pl.pallas_call(
        _inproj_kernel,
        grid=(IN_COLS // IN_TN, n // IN_TM),
        in_specs=[pl.BlockSpec((IN_TM, D_MODEL), lambda j, i: (i, 0)),
                  pl.BlockSpec((D_MODEL, IN_TN), lambda j, i: (0, j)),
                  pl.BlockSpec((1, IN_TN), lambda j, i: (0, jnp.minimum(j, QK_TILES - 1)))],
        out_specs=pl.BlockSpec((IN_TM, IN_TN), lambda j, i: (i, j)),
        out_shape=jax.ShapeDtypeStruct((n, IN_COLS), BF16),
        scratch_shapes=[pltpu.VMEM((D_MODEL, IN_TN), BF16)],
        compiler_params=_params(2),
        name="inproj",
    )(hn, w, qk_norm_w)


def _attn_kernel(slopes_ref, q_ref, kc_ref, kp_ref, vc_ref, vp_ref, o_ref, qf, kf, vf, ob, lb):
    h = pl.program_id(0)
    first_span = pl.program_id(1) == 0
    slope = slopes_ref[h]

    qi = lax.broadcasted_iota(jnp.int32, (ATTN_BLOCK, 2 * ATTN_BLOCK), 0)
    kj = lax.broadcasted_iota(jnp.int32, (ATTN_BLOCK, 2 * ATTN_BLOCK), 1)
    steps = qi + ATTN_BLOCK - kj
    valid = (steps >= 0) & (steps <= N_KEYS)
    stepsf = steps.astype(F32)
    scale = 1.0 / math.sqrt(HEAD_DIM)
    ones_kv = jnp.ones((2 * ATTN_BLOCK, HEAD_DIM), BF16)

    def attend(b, d, bias, q0, q, k, v, no_prev_block):
        s = lax.dot_general(q, k, (((1,), (1,)), ((), ())),
                            preferred_element_type=F32) * scale + bias
        if no_prev_block is not None:
            s = jnp.where(kj >= jnp.where(no_prev_block, ATTN_BLOCK, 0), s, NEG_INF)
        m = jnp.max(s, axis=-1, keepdims=True)
        p = jnp.exp(s - m).astype(BF16)
        both = jnp.dot(p, jnp.concatenate([v, ones_kv], axis=1), preferred_element_type=F32)
        den = both[:, HEAD_DIM:]
        o = both[:, :HEAD_DIM] / den
        ob[b, pl.ds(q0, ATTN_BLOCK, stride=d), :] = o
        lb[b, pl.ds(q0, ATTN_BLOCK, stride=d), :] = m + jnp.log(den)

    def banded_bias(d):
        return jnp.where(valid, (-slope * float(d)) * stepsf, NEG_INF)

    def key_window(cur, prev, n):
        if n > 0:
            return cur[(n - 1) * ATTN_BLOCK:(n + 1) * ATTN_BLOCK, :]
        last = prev.shape[0] - ATTN_BLOCK
        return jnp.concatenate([prev[last:, :], cur[:ATTN_BLOCK, :]], axis=0)

    b, d = 0, DILATIONS[0]
    bias = banded_bias(d)
    for n in range(SPAN // ATTN_BLOCK):
        attend(b, d, bias, n * ATTN_BLOCK, q_ref[n * ATTN_BLOCK:(n + 1) * ATTN_BLOCK, :],
               key_window(kc_ref, kp_ref, n), key_window(vc_ref, vp_ref, n),
               first_span if n == 0 else None)

    qf[...] = q_ref[...].astype(F32)
    kf[0:SPAN, :] = kp_ref[...].astype(F32)
    kf[SPAN:2 * SPAN, :] = kc_ref[...].astype(F32)
    vf[0:SPAN, :] = vp_ref[...].astype(F32)
    vf[SPAN:2 * SPAN, :] = vc_ref[...].astype(F32)
    for b, d in list(enumerate(DILATIONS))[1:]:
        bias = banded_bias(d)
        for r in range(d):
            for n in range(SPAN // (ATTN_BLOCK * d)):
                q0 = r + d * ATTN_BLOCK * n
                k0 = SPAN + q0 - d * ATTN_BLOCK
                attend(b, d, bias, q0,
                       qf[pl.ds(q0, ATTN_BLOCK, stride=d), :].astype(BF16),
                       kf[pl.ds(k0, 2 * ATTN_BLOCK, stride=d), :].astype(BF16),
                       vf[pl.ds(k0, 2 * ATTN_BLOCK, stride=d), :].astype(BF16),
                       first_span if n == 0 else None)

    chunk = 256
    for c in range(SPAN // chunk):
        rows = slice(c * chunk, (c + 1) * chunk)
        l0, l1, l2 = lb[0, rows, :], lb[1, rows, :], lb[2, rows, :]
        mx = jnp.maximum(jnp.maximum(l0, l1), l2)
        w0, w1, w2 = jnp.exp(l0 - mx), jnp.exp(l1 - mx), jnp.exp(l2 - mx)
        num = w0 * ob[0, rows, :] + w1 * ob[1, rows, :] + w2 * ob[2, rows, :]
        o_ref[rows, :] = (num / (w0 + w1 + w2)).astype(o_ref.dtype)


def _attention(z, slopes):
    n_spans = SEQ // SPAN
    k_col, v_col = N_HEADS, 2 * N_HEADS
    blk = (SPAN, HEAD_DIM)
    prev = lambda s: jnp.maximum(s - 1, 0)
    return pl.pallas_call(
        _attn_kernel,
        grid_spec=pltpu.PrefetchScalarGridSpec(
            num_scalar_prefetch=1,
            grid=(N_HEADS, n_spans),
            in_specs=[pl.BlockSpec(blk, lambda h, s, sl: (s, h)),
                      pl.BlockSpec(blk, lambda h, s, sl: (s, k_col + h)),
                      pl.BlockSpec(blk, lambda h, s, sl: (prev(s), k_col + h)),
                      pl.BlockSpec(blk, lambda h, s, sl: (s, v_col + h)),
                      pl.BlockSpec(blk, lambda h, s, sl: (prev(s), v_col + h))],
            out_specs=pl.BlockSpec(blk, lambda h, s, sl: (s, h)),
            scratch_shapes=[pltpu.VMEM((SPAN, HEAD_DIM), F32),
                            pltpu.VMEM((2 * SPAN, HEAD_DIM), F32),
                            pltpu.VMEM((2 * SPAN, HEAD_DIM), F32),
                            pltpu.VMEM((3, SPAN, HEAD_DIM), F32),
                            pltpu.VMEM((3, SPAN, HEAD_DIM), F32)]),
        out_shape=jax.ShapeDtypeStruct((SEQ, ATTN_WIDTH), BF16),
        compiler_params=_params(2),
        name="dilated_attention",
    )(slopes, z, z, z, z, z)


CONV_ROWS = 256
CONV_HALO = 32
CONV_CHUNK = 64

CONV_EXT = CONV_HALO + CONV_ROWS
CONV_SHIFTED = CONV_EXT - SUBLANES


def _conv_kernel(a_ref, g_ref, ah_ref, gh_ref, cw_ref, cb_ref, lw_ref, lb_ref, o_ref,
                 u_ext, shifted, c_scr):
    i = pl.program_id(0)
    n_ct = CONV_WIDTH // LANES
    for ct in range(n_ct):
        cols = slice(ct * LANES, (ct + 1) * LANES)
        u_ext[ct, CONV_HALO:, :] = (a_ref[:, cols].astype(F32)
                                    * jax.nn.sigmoid(g_ref[:, cols].astype(F32)))
        uh = ah_ref[:, cols].astype(F32) * jax.nn.sigmoid(gh_ref[:, cols].astype(F32))
        u_ext[ct, 0:CONV_HALO, :] = jnp.where(i == 0, 0.0, uh)

    first_tap = CONV_HALO - (CONV_KERNEL - 1)

    def slab(ct, carry):
        for k in range(1, SUBLANES):
            shifted[k - 1] = u_ext[ct, k:k + CONV_SHIFTED, :]
        for rc in range(CONV_ROWS // CONV_CHUNK):
            base = rc * CONV_CHUNK
            acc = jnp.broadcast_to(cb_ref[pl.ds(ct, 1), :], (CONV_CHUNK, LANES))
            for j in range(CONV_KERNEL):
                k = (first_tap + j) % SUBLANES
                row = base + (first_tap + j) - k
                if k == 0:
                    src = u_ext[ct, row:row + CONV_CHUNK, :]
                else:
                    src = shifted[k - 1, row:row + CONV_CHUNK, :]
                acc = acc + cw_ref[j, pl.ds(ct, 1), :] * src
            c_scr[ct, base:base + CONV_CHUNK, :] = acc
        return carry

    lax.fori_loop(0, n_ct, slab, 0)

    total = c_scr[0]
    for ct in range(1, n_ct):
        total = total + c_scr[ct]
    mu = jnp.sum(total, axis=-1, keepdims=True) * (1.0 / CONV_WIDTH)
    sq = jnp.square(c_scr[0] - mu)
    for ct in range(1, n_ct):
        sq = sq + jnp.square(c_scr[ct] - mu)
    var = jnp.sum(sq, axis=-1, keepdims=True) * (1.0 / CONV_WIDTH)
    rstd = lax.rsqrt(var + EPS)
    for ct in range(n_ct):
        cols = slice(ct * LANES, (ct + 1) * LANES)
        y = (c_scr[ct] - mu) * rstd * lw_ref[:, cols] + lb_ref[:, cols]
        o_ref[:, cols] = (y * jax.nn.sigmoid(y)).astype(o_ref.dtype)


def _conformer_conv(z, conv_w, conv_b, ln_w, ln_b):
    a_col = 3 * ATTN_WIDTH // CONV_WIDTH
    g_col = a_col + 1
    n_ct = CONV_WIDTH // LANES
    halo_blocks = CONV_ROWS // CONV_HALO
    halo = lambda i: jnp.maximum(i * halo_blocks - 1, 0)
    vec = pl.BlockSpec((1, CONV_WIDTH), lambda i: (0, 0))
    return pl.pallas_call(
        _conv_kernel,
        grid=(SEQ // CONV_ROWS,),
        in_specs=[pl.BlockSpec((CONV_ROWS, CONV_WIDTH), lambda i: (i, a_col)),
                  pl.BlockSpec((CONV_ROWS, CONV_WIDTH), lambda i: (i, g_col)),
                  pl.BlockSpec((CONV_HALO, CONV_WIDTH), lambda i: (halo(i), a_col)),
                  pl.BlockSpec((CONV_HALO, CONV_WIDTH), lambda i: (halo(i), g_col)),
                  pl.BlockSpec((CONV_KERNEL, n_ct, LANES), lambda i: (0, 0, 0)),
                  pl.BlockSpec((n_ct, LANES), lambda i: (0, 0)),
                  vec, vec],
        out_specs=pl.BlockSpec((CONV_ROWS, CONV_WIDTH), lambda i: (i, 0)),
        out_shape=jax.ShapeDtypeStruct((SEQ, CONV_WIDTH), BF16),
        scratch_shapes=[pltpu.VMEM((n_ct, CONV_EXT, LANES), F32),
                        pltpu.VMEM((SUBLANES - 1, CONV_SHIFTED, LANES), F32),
                        pltpu.VMEM((n_ct, CONV_ROWS, LANES), F32)],
        compiler_params=_params(1),
        name="conformer_conv",
    )(z, z, z, z, conv_w, conv_b, ln_w, ln_b)


OUT_TM = 1024
OUT_TN = 512


def _outproj_kernel(attn_ref, conv_ref, wa_ref, wc_ref, x_ref, o_ref, wa_bf, wc_bf):
    @pl.when(pl.program_id(1) == 0)
    def _():
        wa_bf[...] = wa_ref[...].astype(BF16)
        wc_bf[...] = wc_ref[...].astype(BF16)

    acc = jnp.dot(attn_ref[...], wa_bf[...], preferred_element_type=F32)
    acc = acc + jnp.dot(conv_ref[...], wc_bf[...], preferred_element_type=F32)
    o_ref[...] = x_ref[...] + acc


def _outproj(attn, conv, w, x):
    n = x.shape[0]
    conv_blk = ATTN_WIDTH // CONV_WIDTH
    return pl.pallas_call(
        _outproj_kernel,
        grid=(D_MODEL // OUT_TN, n // OUT_TM),
        in_specs=[pl.BlockSpec((OUT_TM, ATTN_WIDTH), lambda j, i: (i, 0)),
                  pl.BlockSpec((OUT_TM, CONV_WIDTH), lambda j, i: (i, 0)),
                  pl.BlockSpec((ATTN_WIDTH, OUT_TN), lambda j, i: (0, j)),
                  pl.BlockSpec((CONV_WIDTH, OUT_TN), lambda j, i: (conv_blk, j)),
                  pl.BlockSpec((OUT_TM, OUT_TN), lambda j, i: (i, j))],
        out_specs=pl.BlockSpec((OUT_TM, OUT_TN), lambda j, i: (i, j)),
        out_shape=jax.ShapeDtypeStruct((n, D_MODEL), F32),
        scratch_shapes=[pltpu.VMEM((ATTN_WIDTH, OUT_TN), BF16),
                        pltpu.VMEM((CONV_WIDTH, OUT_TN), BF16)],
        compiler_params=_params(2),
        name="outproj",
    )(attn, conv, w, w, x)


def _router_kernel(h_ref, nw_ref, wr_ref, br_ref, hn_ref, ids_ref, gates_ref, counts_ref, seen):
    x = h_ref[...]
    ms = jnp.mean(x * x, axis=-1, keepdims=True)
    hn = x * lax.rsqrt(ms + EPS) * nw_ref[...]
    words = _pack_rows(hn)
    for s in range(X_ROW_TILES):
        hn_ref[:, s, :] = words[:, s * LANES:(s + 1) * LANES]

    h_hi = hn.astype(BF16)
    h_lo = (hn - h_hi.astype(F32)).astype(BF16)
    hi_both = jnp.dot(h_hi, wr_ref[...], preferred_element_type=F32)
    lo_hi = jnp.dot(h_lo, wr_ref[:, :LANES], preferred_element_type=F32)
    logits = hi_both[:, :LANES] + (hi_both[:, LANES:] + lo_hi) + br_ref[...]
    lane = lax.broadcasted_iota(jnp.int32, logits.shape, 1)
    big = jnp.int32(1 << 20)

    is_g = lane < N_GROUPS
    gl = jnp.where(is_g, logits, -jnp.inf)
    ge = jnp.exp(gl - jnp.max(gl, axis=-1, keepdims=True))
    g_prob = ge / jnp.sum(ge, axis=-1, keepdims=True)
    g_w = jnp.max(g_prob, axis=-1, keepdims=True)
    g_top = jnp.min(jnp.where(is_g & (g_prob == g_w), lane, big), axis=-1, keepdims=True)

    eidx = lane - N_GROUPS
    grp_shift = EXPERTS_PER_GROUP.bit_length() - 1
    in_grp = (eidx >= 0) & (eidx < N_EXPERTS) & ((eidx >> grp_shift) == g_top)
    el = jnp.where(in_grp, logits, -jnp.inf)
    ee = jnp.exp(el - jnp.max(el, axis=-1, keepdims=True))
    e_prob = ee / jnp.sum(ee, axis=-1, keepdims=True)
    v1 = jnp.max(jnp.where(in_grp, e_prob, -1.0), axis=-1, keepdims=True)
    i1 = jnp.min(jnp.where(in_grp & (e_prob == v1), lane, big), axis=-1, keepdims=True)
    rest = in_grp & (lane != i1)
    v2 = jnp.max(jnp.where(rest, e_prob, -1.0), axis=-1, keepdims=True)
    i2 = jnp.min(jnp.where(rest & (e_prob == v2), lane, big), axis=-1, keepdims=True)
    tot = v1 + v2
    gates_ref[...] = jnp.where(lane == 0, g_w * v1 / tot,
                               jnp.where(lane == 1, g_w * v2 / tot, 0.0))

    e1 = i1 - N_GROUPS
    e2 = i2 - N_GROUPS
    first_choice = lane == e1
    second_choice = lane == e2 + N_EXPERTS
    chosen = jnp.where(first_choice | second_choice, 1.0, 0.0)
    t_row = lax.broadcasted_iota(jnp.int32, (ROUTER_ROWS, ROUTER_ROWS), 0)
    t_col = lax.broadcasted_iota(jnp.int32, (ROUTER_ROWS, ROUTER_ROWS), 1)
    earlier = jnp.where(t_col < t_row, 1.0, 0.0).astype(BF16)
    chosen = chosen + pltpu.roll(chosen, N_EXPERTS, axis=1)
    before = jnp.dot(earlier, chosen.astype(BF16), preferred_element_type=F32)

    @pl.when(pl.program_id(0) == 0)
    def _():
        seen[...] = jnp.zeros_like(seen)

    before = before + seen[0:1, :]
    rank1 = jnp.sum(jnp.where(first_choice, before, 0.0), axis=-1, keepdims=True)
    rank2 = jnp.sum(jnp.where(second_choice, before, 0.0), axis=-1, keepdims=True)
    seen[0:1, :] = seen[0:1, :] + jnp.sum(chosen, axis=0, keepdims=True)
    counts_ref[...] = jnp.broadcast_to(seen[0:1, :], counts_ref.shape).astype(jnp.int32)

    per_token = jnp.where(
        lane == 0, e1, jnp.where(
            lane == 1, e2, jnp.where(
                lane == 2, rank1.astype(jnp.int32), jnp.where(
                    lane == 3, rank2.astype(jnp.int32), 0))))
    ids_ref[...] = jnp.transpose(per_token)[:SUBLANES, :]


def _router(h, norm_w, w_router, b_router):
    return pl.pallas_call(
        _router_kernel,
        grid=(SEQ // ROUTER_ROWS,),
        in_specs=[pl.BlockSpec((ROUTER_ROWS, D_MODEL), lambda i: (i, 0)),
                  pl.BlockSpec((1, D_MODEL), lambda i: (0, 0)),
                  pl.BlockSpec((D_MODEL, 2 * LANES), lambda i: (0, 0)),
                  pl.BlockSpec((1, LANES), lambda i: (0, 0))],
        out_specs=[pl.BlockSpec((ROUTER_ROWS, X_ROW_TILES, LANES), lambda i: (i, 0, 0)),
                   pl.BlockSpec((SUBLANES, ROUTER_ROWS), lambda i: (0, i)),
                   pl.BlockSpec((ROUTER_ROWS, LANES), lambda i: (i, 0)),
                   pl.BlockSpec((SUBLANES, LANES), lambda i: (0, 0))],
        out_shape=[jax.ShapeDtypeStruct((SEQ, X_ROW_TILES, LANES), jnp.uint32),
                   jax.ShapeDtypeStruct((SUBLANES, SEQ), jnp.int32),
                   jax.ShapeDtypeStruct((SEQ, LANES), F32),
                   jax.ShapeDtypeStruct((SUBLANES, LANES), jnp.int32)],
        scratch_shapes=[pltpu.VMEM((SUBLANES, LANES), F32)],
        compiler_params=_params(1),
        name="ffn_norm_router",
    )(h, norm_w, w_router, b_router)


def _staged_token(stage, slot, r):
    return stage.at[slot, pl.ds(pl.multiple_of(r * GATHER_PITCH, SUBLANES), X_ROW_TILES)]


def _issue_token_rows(toks, nv, hn_hbm, stage, slot, sem):
    def start(r, carry):
        pltpu.make_async_copy(hn_hbm.at[toks[0, 0, r]], _staged_token(stage, slot, r),
                              sem.at[slot]).start()
        return carry

    def start_group(g, carry):
        for u in range(DMA_UNROLL):
            start(g * DMA_UNROLL + u, carry)
        return carry

    groups = nv // DMA_UNROLL
    lax.fori_loop(0, groups, start_group, 0)
    lax.fori_loop(groups * DMA_UNROLL, nv, start, 0)


def _wait_token_rows(nv, stage, slot, sem):
    def wait_rows(n_tokens):
        span = stage.at[slot, pl.ds(0, n_tokens * X_ROW_TILES)]
        pltpu.make_async_copy(span, span, sem.at[slot]).wait()

    def wait_many(c, carry):
        wait_rows(WAIT_ROWS)
        return carry

    def wait_one(c, carry):
        wait_rows(1)
        return carry

    lax.fori_loop(0, nv // WAIT_ROWS, wait_many, 0)
    lax.fori_loop(0, nv % WAIT_ROWS, wait_one, 0)


def _staged_block(stage, slot):
    return jnp.concatenate([stage[slot, pl.ds(s, MOE_ROWS, stride=GATHER_PITCH), :]
                            for s in range(X_ROW_TILES)], axis=-1)


def _unpack_rows(words):
    lo, hi = _unpack_words(words)
    return jnp.concatenate([lo.astype(BF16), hi.astype(BF16)], axis=-1)


def _stream_weight_tile(w, first_ref, seq_ref, n_tiles_ref, tile_copies, consume):
    n_tiles = n_tiles_ref[0]

    @pl.when(w == 0)
    def _():
        for k in range(WEIGHT_RING):
            @pl.when(k < n_tiles)
            def _():
                for c in tile_copies(k, k):
                    c.start(priority=WEIGHT_DMA_PRIORITY)

    @pl.when(first_ref[w] == 1)
    def _():
        k = seq_ref[w]
        slot = k % WEIGHT_RING
        for c in tile_copies(k, slot):
            c.wait()
        consume(slot)

        @pl.when(k + WEIGHT_RING < n_tiles)
        def _():
            for c in tile_copies(k + WEIGHT_RING, slot):
                c.start(priority=WEIGHT_DMA_PRIORITY)


def _moe_up_kernel(ib_ref, nv_ref, b_ref, ot_ref, valid_ref, first_ref, seq_ref, te_ref, tt_ref,
                   nt_ref, fetch_ref, xslot_ref, sslot_ref, pf_ref, pfb_ref, pfnv_ref,
                   tok_ref, tok_next_ref, hn_hbm, wg_hbm, wu_hbm, hid_ref,
                   stage, x_cache, ring_g, ring_u, wg_bf, wu_bf, row_sem, sem):
    w = pl.program_id(0)

    @pl.when(w == 0)
    def _():
        stage[...] = jnp.zeros_like(stage)
        _issue_token_rows(tok_ref, nv_ref[0], hn_hbm, stage, 0, row_sem)

    @pl.when(pf_ref[w] == 1)
    def _():
        _issue_token_rows(tok_next_ref, pfnv_ref[w], hn_hbm, stage, 1 - sslot_ref[w], row_sem)

    def tile_copies(k, slot):
        e = te_ref[k]
        col = pl.multiple_of(tt_ref[k] * UP_TILE, UP_TILE)
        return (pltpu.make_async_copy(wg_hbm.at[e, :, pl.ds(col, UP_TILE)], ring_g.at[slot],
                                      sem.at[0, slot]),
                pltpu.make_async_copy(wu_hbm.at[e, :, pl.ds(col, UP_TILE)], ring_u.at[slot],
                                      sem.at[1, slot]))

    def consume(slot):
        wg_bf[...] = ring_g[slot].astype(BF16)
        wu_bf[...] = ring_u[slot].astype(BF16)

    _stream_weight_tile(w, first_ref, seq_ref, nt_ref, tile_copies, consume)

    @pl.when(valid_ref[w] == 1)
    def _():
        x_slot = xslot_ref[w]

        @pl.when(fetch_ref[w] == 1)
        def _():
            slot = sslot_ref[w]
            _wait_token_rows(nv_ref[w], stage, slot, row_sem)
            x_cache[x_slot] = _unpack_rows(_staged_block(stage, slot))

        x = x_cache[x_slot]
        g = jnp.dot(x, wg_bf[...], preferred_element_type=F32)
        u = jnp.dot(x, wu_bf[...], preferred_element_type=F32)
        hid_ref[...] = (g * jax.nn.sigmoid(g) * u).astype(hid_ref.dtype)

    @pl.when(valid_ref[w] == 0)
    def _():
        hid_ref[...] = jnp.zeros_like(hid_ref)


def _moe_up(items, info, n_valid, row_tok, hn_words, w_gate, w_up):
    n_items = N_MOE_BLOCKS * (D_FF // UP_TILE)
    in_blk = items[0]
    schedule = _gather_schedule(in_blk, info, n_valid)
    n_prefetch = 1 + len(items) + len(schedule)
    toks = row_tok.reshape(N_MOE_BLOCKS, 1, MOE_ROWS)
    smem_block = lambda index_map: pl.BlockSpec((1, 1, MOE_ROWS), index_map,
                                                memory_space=pltpu.SMEM)
    any_spec = pl.BlockSpec(memory_space=pl.ANY)
    return pl.pallas_call(
        _moe_up_kernel,
        grid_spec=pltpu.PrefetchScalarGridSpec(
            num_scalar_prefetch=n_prefetch,
            grid=(n_items,),
            in_specs=[smem_block(lambda w, *refs: (refs[0][w], 0, 0)),
                      smem_block(lambda w, *refs: (refs[n_prefetch - 2][w], 0, 0)),
                      any_spec, any_spec, any_spec],
            out_specs=pl.BlockSpec((MOE_ROWS, UP_TILE),
                                   lambda w, ib, nv, b, ot, *_: (b[w], ot[w])),
            scratch_shapes=[pltpu.VMEM((2, MOE_ROWS * GATHER_PITCH, LANES), jnp.uint32),
                            pltpu.VMEM((X_CACHE_SLOTS, MOE_ROWS, D_MODEL), BF16),
                            pltpu.VMEM((WEIGHT_RING, D_MODEL, UP_TILE), F32),
                            pltpu.VMEM((WEIGHT_RING, D_MODEL, UP_TILE), F32),
                            pltpu.VMEM((D_MODEL, UP_TILE), BF16),
                            pltpu.VMEM((D_MODEL, UP_TILE), BF16),
                            pltpu.SemaphoreType.DMA((2,)),
                            pltpu.SemaphoreType.DMA((2, WEIGHT_RING))]),
        out_shape=jax.ShapeDtypeStruct((N_MOE_ROWS, D_FF), BF16),
        compiler_params=_params(1),
        name="moe_gate_up",
    )(in_blk, _lookup(n_valid, in_blk), *items[1:], *schedule, toks, toks, hn_words, w_gate, w_up)


def _moe_down_kernel(ib_ref, b_ref, ot_ref, valid_ref, first_ref, seq_ref, te_ref, tt_ref, nt_ref,
                     hid_ref, wd_hbm, y_ref, ring, wd_bf, sem):
    w = pl.program_id(0)

    def tile_copies(k, slot):
        col = pl.multiple_of(tt_ref[k] * DOWN_TILE, DOWN_TILE)
        return (pltpu.make_async_copy(wd_hbm.at[te_ref[k], :, pl.ds(col, DOWN_TILE)],
                                      ring.at[slot], sem.at[slot]),)

    def consume(slot):
        wd_bf[...] = ring[slot].astype(BF16)

    _stream_weight_tile(w, first_ref, seq_ref, nt_ref, tile_copies, consume)

    @pl.when(valid_ref[w] == 1)
    def _():
        y = jnp.dot(hid_ref[...], wd_bf[...], preferred_element_type=F32)
        words = _pack_rows(y)
        for s in range(DOWN_TILE // 2 // LANES):
            y_ref[:, s, :] = words[:, s * LANES:(s + 1) * LANES]

    @pl.when(valid_ref[w] == 0)
    def _():
        y_ref[...] = jnp.zeros_like(y_ref)


def _moe_down(items, hid, w_down):
    n_items = N_MOE_BLOCKS * (D_MODEL // DOWN_TILE)
    tile_rows = DOWN_TILE // 2 // LANES
    return pl.pallas_call(
        _moe_down_kernel,
        grid_spec=pltpu.PrefetchScalarGridSpec(
            num_scalar_prefetch=9,
            grid=(n_items,),
            in_specs=[pl.BlockSpec((MOE_ROWS, D_FF), lambda w, ib, *_: (ib[w], 0)),
                      pl.BlockSpec(memory_space=pl.ANY)],
            out_specs=pl.BlockSpec((MOE_ROWS, tile_rows, LANES),
                                   lambda w, ib, b, ot, *_: (b[w], ot[w], 0)),
            scratch_shapes=[pltpu.VMEM((WEIGHT_RING, D_FF, DOWN_TILE), F32),
                            pltpu.VMEM((D_FF, DOWN_TILE), BF16),
                            pltpu.SemaphoreType.DMA((WEIGHT_RING,))]),
        out_shape=jax.ShapeDtypeStruct((N_MOE_ROWS, Y_ROW_TILES, LANES), jnp.uint32),
        compiler_params=_params(1),
        name="moe_down",
    )(*items, hid, w_down)


def _combine_kernel(dest_ref, dest_next_ref, h_ref, gate_ref, y_hbm, o_ref, buf, sem):
    i = pl.program_id(0)

    def issue(dests, slot):
        def start(r, carry):
            for k in range(TOP_K):
                dst = buf.at[slot, k, pl.ds(r * COMBINE_PITCH, Y_ROW_TILES)]
                pltpu.make_async_copy(y_hbm.at[dests[0, k, r]], dst, sem.at[slot]).start()
            return carry
        lax.fori_loop(0, COMBINE_ROWS, start, 0, unroll=DMA_UNROLL // TOP_K)

    @pl.when(i == 0)
    def _():
        issue(dest_ref, 0)

    @pl.when(i + 1 < pl.num_programs(0))
    def _():
        issue(dest_next_ref, (i + 1) % 2)

    slot = i % 2

    for k in range(TOP_K):
        for c in range(COMBINE_ROWS // WAIT_ROWS):
            span = buf.at[slot, k, pl.ds(c * WAIT_ROWS * Y_ROW_TILES, WAIT_ROWS * Y_ROW_TILES)]
            pltpu.make_async_copy(span, span, sem.at[slot]).wait()

    g0 = gate_ref[:, 0:1]
    g1 = gate_ref[:, 1:2]
    tiles_per_half = DOWN_TILE // 2 // LANES
    for s in range(Y_ROW_TILES):
        lo_col = (s // tiles_per_half) * DOWN_TILE + (s % tiles_per_half) * LANES
        lo0, hi0 = _unpack_words(buf[slot, 0, pl.ds(s, COMBINE_ROWS, stride=COMBINE_PITCH), :])
        lo1, hi1 = _unpack_words(buf[slot, 1, pl.ds(s, COMBINE_ROWS, stride=COMBINE_PITCH), :])
        for col, y0, y1 in ((lo_col, lo0, lo1), (lo_col + DOWN_TILE // 2, hi0, hi1)):
            cols = slice(col, col + LANES)
            o_ref[:, cols] = h_ref[:, cols] + (y0 * g0 + y1 * g1)


def _combine(dest, h, gates, y):
    n_tiles = SEQ // COMBINE_ROWS
    dests = jnp.stack([d.reshape(n_tiles, COMBINE_ROWS) for d in dest], axis=1)
    blk = (1, TOP_K, COMBINE_ROWS)
    return pl.pallas_call(
        _combine_kernel,
        grid=(n_tiles,),
        in_specs=[pl.BlockSpec(blk, lambda i: (i, 0, 0), memory_space=pltpu.SMEM),
                  pl.BlockSpec(blk, lambda i: (jnp.minimum(i + 1, n_tiles - 1), 0, 0),
                               memory_space=pltpu.SMEM),
                  pl.BlockSpec((COMBINE_ROWS, D_MODEL), lambda i: (i, 0)),
                  pl.BlockSpec((COMBINE_ROWS, LANES), lambda i: (i, 0)),
                  pl.BlockSpec(memory_space=pl.ANY)],
        out_specs=pl.BlockSpec((COMBINE_ROWS, D_MODEL), lambda i: (i, 0)),
        out_shape=jax.ShapeDtypeStruct((SEQ, D_MODEL), F32),
        scratch_shapes=[pltpu.VMEM((2, TOP_K, COMBINE_ROWS * COMBINE_PITCH, LANES), jnp.uint32),
                        pltpu.SemaphoreType.DMA((2,))],
        compiler_params=_params(1),
        name="moe_combine",
    )(dests, dests, h, gates, y)


def _lookup(table, idx):
    pos = jnp.arange(table.shape[0], dtype=jnp.int32)
    return jnp.sum(jnp.where(idx[:, None] == pos[None, :], table[None, :], 0), axis=1)


def _count_le(ends, x):
    return jnp.sum((ends[None, :] <= x[:, None]).astype(jnp.int32), axis=1)


def _work_items(n_blocks_e, block_start_e, tiles):
    i32 = lambda a: a.astype(jnp.int32)
    n_items = N_MOE_BLOCKS * tiles
    per_e = n_blocks_e * tiles
    ends = jnp.cumsum(per_e)
    total = ends[-1]
    idx = jnp.arange(n_items, dtype=jnp.int32)
    valid = idx < total
    w = jnp.minimum(idx, total - 1)
    e = _count_le(ends, w)
    local = w - _lookup(ends - per_e, e)
    nb = _lookup(n_blocks_e, e)
    first_blk = _lookup(block_start_e, e)
    spare = idx - total
    w_tile = local // nb
    o_tile = jnp.where(valid, w_tile, spare % tiles)
    blk = jnp.where(valid, first_blk + local % nb, total // tiles + spare // tiles)
    first = valid & (local % nb == 0)
    active = n_blocks_e > 0
    active_rank = jnp.cumsum(active) - active
    seq = _lookup(active_rank, e) * tiles + w_tile
    k = jnp.arange(N_EXPERTS * tiles, dtype=jnp.int32)
    experts = jnp.arange(N_EXPERTS, dtype=jnp.int32)
    is_kth = active[None, :] & (active_rank[None, :] == (k // tiles)[:, None])
    tile_e = jnp.sum(jnp.where(is_kth, experts[None, :], 0), axis=1)
    tile_t = k % tiles
    n_tiles = (jnp.sum(active) * tiles).reshape(1)
    in_blk = first_blk + local % nb
    items = (i32(in_blk), i32(blk), i32(o_tile), i32(valid), i32(first), i32(seq), i32(tile_e),
             i32(tile_t), i32(n_tiles))
    return items, dict(blocks_of_expert=nb, local_block=local % nb, weight_tile=w_tile, valid=valid)


def _gather_schedule(in_blk, info, n_valid):
    i32 = lambda a: a.astype(jnp.int32)
    n = in_blk.shape[0]
    cached = (info['blocks_of_expert'] <= X_CACHE_SLOTS) & (info['weight_tile'] > 0)
    fetch = info['valid'] & ~cached
    x_slot = info['local_block'] % X_CACHE_SLOTS
    stage_slot = (jnp.cumsum(fetch) - 1) % 2
    idx = jnp.arange(n, dtype=jnp.int32)
    later = lax.cummin(jnp.where(fetch, idx, n), axis=0, reverse=True)
    next_fetch = jnp.concatenate([later[1:], jnp.full((1,), n, later.dtype)])
    prefetch = fetch & (next_fetch < n)
    pf_blk = jnp.where(prefetch, _lookup(in_blk, jnp.minimum(next_fetch, n - 1)), in_blk)
    return (i32(fetch), i32(x_slot), i32(stage_slot), i32(prefetch), i32(pf_blk),
            i32(_lookup(n_valid, pf_blk)))


def _dispatch_plan(ids, counts):
    counts = counts[0, :N_EXPERTS]
    n_blocks_e = (counts + MOE_ROWS - 1) // MOE_ROWS
    block_ends = jnp.cumsum(n_blocks_e)
    block_start_e = block_ends - n_blocks_e
    first_row_e = block_start_e * MOE_ROWS
    dest = [(_lookup(first_row_e, ids[k]) + ids[TOP_K + k]).astype(jnp.int32)
            for k in range(TOP_K)]
    token = jnp.arange(SEQ, dtype=jnp.int32)
    row_tok = jnp.zeros((N_MOE_ROWS,), jnp.int32).at[jnp.concatenate(dest)].set(
        jnp.concatenate([token] * TOP_K), unique_indices=True, mode='promise_in_bounds')
    blk = jnp.arange(N_MOE_BLOCKS, dtype=jnp.int32)
    blk_e = jnp.minimum(_count_le(block_ends, blk), N_EXPERTS - 1)
    n_valid = jnp.clip(_lookup(counts, blk_e) - (blk - _lookup(block_start_e, blk_e)) * MOE_ROWS,
                       0, MOE_ROWS)
    n_valid = jnp.where(blk < block_ends[-1], n_valid, 0).astype(jnp.int32)
    return dest, row_tok, n_valid, n_blocks_e.astype(jnp.int32), block_start_e.astype(jnp.int32)


def kernel(x, norm_mix_w, w_in, q_norm_w, k_norm_w, conv_w, conv_b, conv_ln_w, conv_ln_b, w_out,
           norm_ffn_w, w_group_router, b_group_router, w_expert_router, b_expert_router,
           w_gate, w_up, w_down):
    h = x.reshape(SEQ, D_MODEL)
    slopes = jnp.exp2(-8.0 * jnp.arange(1, N_HEADS + 1, dtype=F32) / N_HEADS)
    for l in range(norm_mix_w.shape[0]):
        hn = _rmsnorm(h, norm_mix_w[l][None, :])
        qk_norm = jnp.concatenate([jnp.tile(q_norm_w[l], N_HEADS), jnp.tile(k_norm_w[l], N_HEADS)])
        z = _inproj(hn, w_in[l], qk_norm[None, :])
        attn = _attention(z, slopes)
        n_ct = CONV_WIDTH // LANES
        conv = _conformer_conv(z, conv_w[l].reshape(CONV_KERNEL, n_ct, LANES),
                               conv_b[l].reshape(n_ct, LANES), conv_ln_w[l][None, :],
                               conv_ln_b[l][None, :])
        h = _outproj(attn, conv, w_out[l], h)

        pad = LANES - N_GROUPS - N_EXPERTS
        w_router = jnp.concatenate([w_group_router[l], w_expert_router[l],
                                    jnp.zeros((D_MODEL, pad), F32)], axis=1)
        b_router = jnp.concatenate([b_group_router[l], b_expert_router[l], jnp.zeros((pad,), F32)])
        w_router_hi = w_router.astype(BF16)
        w_router_lo = (w_router - w_router_hi.astype(F32)).astype(BF16)
        hn_words, ids, gates, counts = _router(h, norm_ffn_w[l][None, :],
                                               jnp.concatenate([w_router_hi, w_router_lo], axis=1),
                                               b_router[None, :])
        dest, row_tok, n_valid, n_blocks_e, block_start_e = _dispatch_plan(ids, counts)
        up_items, up_info = _work_items(n_blocks_e, block_start_e, D_FF // UP_TILE)
        hid = _moe_up(up_items, up_info, n_valid, row_tok, hn_words, w_gate[l], w_up[l])
        down_items, _ = _work_items(n_blocks_e, block_start_e, D_MODEL // DOWN_TILE)
        y = _moe_down(down_items, hid, w_down[l])
        h = _combine(dest, h, gates, y)
    return h.reshape(x.shape)
```
